```python
import math
import jax
import jax.numpy as jnp
from jax import lax
import numpy as np

D_MODEL = 1024
BATCH = 1
SEQ = 16384
DEPTH = 2

HEAD_DIM = 64
HEADS_PER_MIXER = 4
N_MIXERS = 4
N_HEADS = HEADS_PER_MIXER * N_MIXERS
GROUP_WIDTH = HEADS_PER_MIXER * HEAD_DIM
MIX_WIDTH = N_HEADS * HEAD_DIM
MLA_Q_RANK = 256
MLA_KV_RANK = 128
MLA_NOPE = 64
MLA_ROPE = 32
MLA_V = HEAD_DIM
ROPE_THETA = 10000.0
QBLOCK = 128
MOBA_BLOCK = 256
MOBA_TOPK = 3
DIL_PATTERNS = ((128, 1), (512, 4), (2048, 16))
N_EXPERTS = 32
TOP_K = 4
D_EXPERT = 1024
SWIGLU_LIMIT = 7.0
SWIGLU_ALPHA = 1.702
EXPERT_BLOCK = 128
NORM_EPS = 1e-6
NEG_INF = -1e30
IN_COLS = MLA_Q_RANK + MLA_KV_RANK + MLA_ROPE + 3 * GROUP_WIDTH + HEADS_PER_MIXER + 6 * GROUP_WIDTH

kernel_name = 'hybrid_parallel_heads_mla_fox_moba_dilated_moe'


def rms_norm(x, g):
    xf = x.astype(jnp.float32)
    y = xf * lax.rsqrt(jnp.mean(xf * xf, axis=-1, keepdims=True) + NORM_EPS)
    return (y * g.astype(jnp.float32)).astype(x.dtype)


def alibi_slopes(n):
    return jnp.asarray(2.0 ** (-8.0 * np.arange(1, n + 1) / n), dtype=jnp.float32)


def rope(x, pos):
    half = x.shape[-1] // 2
    inv = 1.0 / (ROPE_THETA ** (jnp.arange(half, dtype=jnp.float32) / half))
    ang = pos.astype(jnp.float32)[:, None] * inv[None, :]
    cos = jnp.cos(ang)[None, :, None, :]
    sin = jnp.sin(ang)[None, :, None, :]
    xf = x.astype(jnp.float32)
    x1, x2 = xf[..., :half], xf[..., half:]
    return jnp.concatenate([x1 * cos - x2 * sin, x2 * cos + x1 * sin], axis=-1).astype(x.dtype)


def pad_seq(a, mult, axis):
    n = a.shape[axis]
    target = -(-n // mult) * mult
    if target == n:
        return a
    widths = [(0, 0)] * a.ndim
    widths[axis] = (0, target - n)
    return jnp.pad(a, widths)


def causal_block_attention(q, k, v, scale, decay_cum=None):
    B, H, S, dq = q.shape
    nb = S // QBLOCK
    qb = q.reshape(B, H, nb, QBLOCK, dq).transpose(2, 0, 1, 3, 4)
    k_pos = jnp.arange(S)

    def one_block(args):
        qi, bi = args
        s = jnp.einsum('bhqd,bhkd->bhqk', qi, k).astype(jnp.float32) * scale
        q_pos = bi * QBLOCK + jnp.arange(QBLOCK)
        if decay_cum is not None:
            fq = lax.dynamic_slice_in_dim(decay_cum, bi * QBLOCK, QBLOCK, axis=2)
            s = s + fq[..., :, None] - decay_cum[..., None, :]
        s = jnp.where(k_pos[None, :] <= q_pos[:, None], s, NEG_INF)
        p = jax.nn.softmax(s, axis=-1)
        return jnp.einsum('bhqk,bhkd->bhqd', p.astype(v.dtype), v)

    out = lax.map(one_block, (qb, jnp.arange(nb)))
    return out.transpose(1, 2, 0, 3, 4).reshape(B, H, S, v.shape[-1])


def moba_attention(q, k, v, slopes):
    B, H, S, d = q.shape
    scale = d ** -0.5
    q, k, v = (pad_seq(a, MOBA_BLOCK, 2) for a in (q, k, v))
    Sp = q.shape[2]
    nblk = Sp // MOBA_BLOCK
    topk = min(MOBA_TOPK, nblk)
    nq = Sp // QBLOCK
    kb = k.reshape(B, H, nblk, MOBA_BLOCK, d)
    vb = v.reshape(B, H, nblk, MOBA_BLOCK, d)
    k_mean = jnp.mean(kb.astype(jnp.float32), axis=3)
    own = jnp.arange(Sp) // MOBA_BLOCK
    past = jnp.arange(nblk)[None, :] < own[:, None]
    gate = jnp.einsum('bhsd,bhnd->bhsn', q.astype(jnp.float32), k_mean)
    gate = jnp.where(past, gate, NEG_INF)
    _, idx = lax.top_k(gate, topk)
    valid = idx < own[:, None]
    chunk = lambda a: a.reshape(B, H, nq, QBLOCK, a.shape[-1]).transpose(2, 0, 1, 3, 4)
    b_ix = jnp.arange(B)[:, None, None, None]
    h_ix = jnp.arange(H)[None, :, None, None]
    rel = jnp.arange(MOBA_BLOCK)
    sl = slopes.astype(jnp.float32)
    n_sel = topk * MOBA_BLOCK

    def one_chunk(args):
        qc, idxc, validc, ci = args
        t = ci * QBLOCK + jnp.arange(QBLOCK)
        kg = kb[b_ix, h_ix, idxc]
        vg = vb[b_ix, h_ix, idxc]
        kpos = idxc[..., None] * MOBA_BLOCK + rel
        s_sel = jnp.einsum('bhqd,bhqnjd->bhqnj', qc, kg).astype(jnp.float32) * scale
        s_sel = s_sel - sl[None, :, None, None, None] * (t[:, None, None] - kpos).astype(jnp.float32)
        s_sel = jnp.where(validc[..., None], s_sel, NEG_INF)
        start = (ci * QBLOCK) // MOBA_BLOCK * MOBA_BLOCK
        ko = lax.dynamic_slice_in_dim(k, start, MOBA_BLOCK, axis=2)
        vo = lax.dynamic_slice_in_dim(v, start, MOBA_BLOCK, axis=2)
        kpos_o = start + rel
        s_own = jnp.einsum('bhqd,bhjd->bhqj', qc, ko).astype(jnp.float32) * scale
        s_own = s_own - sl[None, :, None, None] * (t[:, None] - kpos_o[None, :]).astype(jnp.float32)
        s_own = jnp.where(kpos_o[None, :] <= t[:, None], s_own, NEG_INF)
        s_all = jnp.concatenate([s_sel.reshape(B, H, QBLOCK, n_sel), s_own], axis=-1)
        p = jax.nn.softmax(s_all, axis=-1).astype(v.dtype)
        p_sel = p[..., :n_sel].reshape(B, H, QBLOCK, topk, MOBA_BLOCK)
        return (jnp.einsum('bhqnj,bhqnjd->bhqd', p_sel, vg)
                + jnp.einsum('bhqj,bhjd->bhqd', p[..., n_sel:], vo))

    out = lax.map(one_chunk, (chunk(q), chunk(idx), chunk(valid), jnp.arange(nq)))
    return out.transpose(1, 2, 0, 3, 4).reshape(B, H, Sp, d)[:, :, :S]


def dilated_attention(q, k, v, slopes):
    B, H, S, d = q.shape
    scale = d ** -0.5
    outs, lses = [], []
    for window, r in DIL_PATTERNS:
        span = window // r
        Sp = -(-S // (r * span)) * (r * span)
        L = Sp // r
        nb = L // span

        def to_sub(a):
            dd = a.shape[-1]
            a = pad_seq(a, r * span, 2)
            return a.reshape(B, H, L, r, dd).transpose(0, 1, 3, 2, 4).reshape(B, H, r, nb, span, dd)

        def with_prev(a):
            prev = jnp.concatenate([jnp.zeros_like(a[:, :, :, :1]), a[:, :, :, :-1]], axis=3)
            return jnp.concatenate([prev, a], axis=4)

        qs = to_sub(q)
        kc = with_prev(to_sub(k))
        vc = with_prev(to_sub(v))
        i = jnp.arange(span)
        j = jnp.arange(2 * span)
        dist = i[:, None] + span - j[None, :]
        valid = (dist >= 0) & (dist <= span)
        valid = valid[None] & ((jnp.arange(nb)[:, None, None] > 0) | (j[None, None, :] >= span))
        s = jnp.einsum('bhrnqd,bhrnjd->bhrnqj', qs, kc).astype(jnp.float32) * scale
        s = s - slopes.astype(jnp.float32)[None, :, None, None, None, None] * (dist * r).astype(jnp.float32)
        s = jnp.where(valid, s, NEG_INF)
        lse = jax.nn.logsumexp(s, axis=-1)
        p = jnp.exp(s - lse[..., None])
        o = jnp.einsum('bhrnqj,bhrnjd->bhrnqd', p.astype(v.dtype), vc)
        dv = v.shape[-1]
        outs.append(o.reshape(B, H, r, L, dv).transpose(0, 1, 3, 2, 4).reshape(B, H, Sp, dv)[:, :, :S])
        lses.append(lse.reshape(B, H, r, L).transpose(0, 1, 3, 2).reshape(B, H, Sp)[:, :, :S])
    w = jax.nn.softmax(jnp.stack(lses), axis=0)
    o = jnp.sum(w[..., None] * jnp.stack(outs).astype(jnp.float32), axis=0)
    return o.astype(q.dtype)


def token_mixers(h, w_in, mla_cq_g, mla_w_uq, mla_ckv_g, mla_w_ukv, mla_q_g, mla_k_g,
                 fox_q_g, fox_k_g, fox_b_f, moba_q_g, moba_k_g, dil_q_g, dil_k_g, w_out):
    B, S, _ = h.shape
    Hm = HEADS_PER_MIXER
    proj = h @ w_in
    sizes = [MLA_Q_RANK, MLA_KV_RANK, MLA_ROPE,
             GROUP_WIDTH, GROUP_WIDTH, GROUP_WIDTH, Hm,
             GROUP_WIDTH, GROUP_WIDTH, GROUP_WIDTH,
             GROUP_WIDTH, GROUP_WIDTH, GROUP_WIDTH]
    splits = [int(s) for s in np.cumsum(sizes)[:-1]]
    cq, ckv, kr, fq, fk, fv, flog, mq, mk, mv, dq, dk, dv = jnp.split(proj, splits, axis=-1)
    pos = jnp.arange(S)
    heads = lambda a: a.reshape(B, S, Hm, -1)
    bhsd = lambda a: a.transpose(0, 2, 1, 3)

    q_a = heads(rms_norm(cq, mla_cq_g) @ mla_w_uq)
    kv_a = heads(rms_norm(ckv, mla_ckv_g) @ mla_w_ukv)
    k_a = jnp.concatenate([kv_a[..., :MLA_NOPE],
                           jnp.broadcast_to(kr[:, :, None, :], (B, S, Hm, MLA_ROPE))], axis=-1)
    v_a = kv_a[..., MLA_NOPE:]
    q_a = rms_norm(q_a, mla_q_g)
    k_a = rms_norm(k_a, mla_k_g)
    q_a = jnp.concatenate([q_a[..., :MLA_NOPE], rope(q_a[..., MLA_NOPE:], pos)], axis=-1)
    k_a = jnp.concatenate([k_a[..., :MLA_NOPE], rope(k_a[..., MLA_NOPE:], pos)], axis=-1)
    o_a = causal_block_attention(bhsd(q_a), bhsd(k_a), bhsd(v_a), (MLA_NOPE + MLA_ROPE) ** -0.5)

    log_f = jax.nn.log_sigmoid((flog + fox_b_f).astype(jnp.float32))
    decay_cum = jnp.cumsum(log_f, axis=1).transpose(0, 2, 1)
    o_b = causal_block_attention(bhsd(rms_norm(heads(fq), fox_q_g)), bhsd(rms_norm(heads(fk), fox_k_g)),
                                 bhsd(heads(fv)), HEAD_DIM ** -0.5, decay_cum)

    slopes = alibi_slopes(2 * Hm)
    o_c = moba_attention(bhsd(rms_norm(heads(mq), moba_q_g)), bhsd(rms_norm(heads(mk), moba_k_g)),
                         bhsd(heads(mv)), slopes[1::2])
    o_d = dilated_attention(bhsd(rms_norm(heads(dq), dil_q_g)), bhsd(rms_norm(heads(dk), dil_k_g)),
                            bhsd(heads(dv)), slopes[0::2])

    o = jnp.concatenate([o_a, o_b.astype(o_a.dtype), o_c.astype(o_a.dtype), o_d.astype(o_a.dtype)], axis=1)
    o = o.transpose(0, 2, 1, 3).reshape(B, S, MIX_WIDTH)
    return o @ w_out


def moe_ffn(h, router_w, router_b, w1, b1, w2, b2):
    B, S, D = h.shape
    n_tok = B * S
    hf = h.reshape(n_tok, D)
    logits = (hf @ router_w + router_b).astype(jnp.float32)
    top_val, top_idx = lax.top_k(logits, TOP_K)
    gates = jax.nn.softmax(top_val, axis=-1)
    m = n_tok * TOP_K
    e_flat = top_idx.reshape(m)
    tok_flat = jnp.repeat(jnp.arange(n_tok, dtype=jnp.int32), TOP_K)
    g_flat = gates.reshape(m)
    order = jnp.argsort(e_flat)
    e_sorted = e_flat[order]
    counts = jnp.bincount(e_flat, length=N_EXPERTS)
    padded = (counts + EXPERT_BLOCK - 1) // EXPERT_BLOCK * EXPERT_BLOCK
    pad_end = jnp.cumsum(padded)
    pad_start = pad_end - padded
    start = jnp.cumsum(counts) - counts
    dest = pad_start[e_sorted] + jnp.arange(m) - start[e_sorted]
    m_pad = -(-m // EXPERT_BLOCK) * EXPERT_BLOCK + N_EXPERTS * EXPERT_BLOCK
    row_tok = jnp.zeros((m_pad,), jnp.int32).at[dest].set(tok_flat[order])
    row_gate = jnp.zeros((m_pad,), jnp.float32).at[dest].set(g_flat[order])
    n_blk = m_pad // EXPERT_BLOCK
    blk_expert = jnp.minimum(jnp.searchsorted(pad_end, jnp.arange(n_blk) * EXPERT_BLOCK, side='right'),
                             N_EXPERTS - 1)
    xs = hf[row_tok].reshape(n_blk, EXPERT_BLOCK, D)

    def expert_block(args):
        xb, e = args
        gu = xb @ w1[e] + b1[e]
        g = jnp.minimum(gu[:, :D_EXPERT], SWIGLU_LIMIT)
        u = jnp.clip(gu[:, D_EXPERT:], -SWIGLU_LIMIT, SWIGLU_LIMIT)
        y = (u + 1.0) * g * jax.nn.sigmoid(SWIGLU_ALPHA * g)
        return y @ w2[e] + b2[e]

    ys = lax.map(expert_block, (xs, blk_expert)).reshape(m_pad, D)
    out = jnp.zeros((n_tok, D), jnp.float32).at[row_tok].add(ys.astype(jnp.float32) * row_gate[:, None])
    return out.reshape(B, S, D).astype(h.dtype)


def setup_inputs(seed: int = 0) -> dict:
    key = jax.random.key(seed)
    ks = jax.random.split(key, 27)
    nrm = lambda k, shape, s: s * jax.random.normal(k, shape, jnp.float32)
    gain = lambda k, shape: 1.0 + 0.02 * jax.random.normal(k, shape, jnp.float32)
    L, D, E, F, Hm = DEPTH, D_MODEL, N_EXPERTS, D_EXPERT, HEADS_PER_MIXER
    return {
        'x': nrm(ks[0], (BATCH, SEQ, D), 1.0),
        'c': nrm(ks[1], (BATCH, D), 1.0),
        'w_mod': nrm(ks[2], (L, D, 6 * D), 0.5 * D ** -0.5),
        'b_mod': nrm(ks[3], (L, 6 * D), 0.01),
        'norm1_g': gain(ks[4], (L, D)),
        'norm2_g': gain(ks[5], (L, D)),
        'w_in': nrm(ks[6], (L, D, IN_COLS), D ** -0.5),
        'mla_cq_g': gain(ks[7], (L, MLA_Q_RANK)),
        'mla_w_uq': nrm(ks[8], (L, MLA_Q_RANK, Hm * (MLA_NOPE + MLA_ROPE)), MLA_Q_RANK ** -0.5),
        'mla_ckv_g': gain(ks[9], (L, MLA_KV_RANK)),
        'mla_w_ukv': nrm(ks[10], (L, MLA_KV_RANK, Hm * (MLA_NOPE + MLA_V)), MLA_KV_RANK ** -0.5),
        'mla_q_g': gain(ks[11], (L, MLA_NOPE + MLA_ROPE)),
        'mla_k_g': gain(ks[12], (L, MLA_NOPE + MLA_ROPE)),
        'fox_q_g': gain(ks[13], (L, HEAD_DIM)),
        'fox_k_g': gain(ks[14], (L, HEAD_DIM)),
        'fox_b_f': nrm(ks[15], (L, Hm), 0.1),
        'moba_q_g': gain(ks[16], (L, HEAD_DIM)),
        'moba_k_g': gain(ks[17], (L, HEAD_DIM)),
        'dil_q_g': gain(ks[18], (L, HEAD_DIM)),
        'dil_k_g': gain(ks[19], (L, HEAD_DIM)),
        'w_out': nrm(ks[20], (L, MIX_WIDTH, D), MIX_WIDTH ** -0.5),
        'router_w': nrm(ks[21], (L, D, E), D ** -0.5),
        'router_b': nrm(ks[22], (L, E), 0.01),
        'exp_w1': nrm(ks[23], (L, E, D, 2 * F), D ** -0.5),
        'exp_b1': nrm(ks[24], (L, E, 2 * F), 0.01),
        'exp_w2': nrm(ks[25], (L, E, F, D), F ** -0.5),
        'exp_b2': nrm(ks[26], (L, E, D), 0.01),
    }


def reference(x, c, w_mod, b_mod, norm1_g, norm2_g, w_in, mla_cq_g, mla_w_uq, mla_ckv_g, mla_w_ukv,
              mla_q_g, mla_k_g, fox_q_g, fox_k_g, fox_b_f, moba_q_g, moba_k_g, dil_q_g, dil_k_g, w_out,
              router_w, router_b, exp_w1, exp_b1, exp_w2, exp_b2):
    for l in range(DEPTH):
        mod = jax.nn.silu(c) @ w_mod[l] + b_mod[l]
        sh1, sc1, g1, sh2, sc2, g2 = jnp.split(mod[:, None, :], 6, axis=-1)
        h = rms_norm(x, norm1_g[l]) * (1 + sc1) + sh1
        x = x + g1 * token_mixers(h, w_in[l], mla_cq_g[l], mla_w_uq[l], mla_ckv_g[l], mla_w_ukv[l],
                                  mla_q_g[l], mla_k_g[l], fox_q_g[l], fox_k_g[l], fox_b_f[l],
                                  moba_q_g[l], moba_k_g[l], dil_q_g[l], dil_k_g[l], w_out[l])
        h = rms_norm(x, norm2_g[l]) * (1 + sc2) + sh2
        x = x + g2 * moe_ffn(h, router_w[l], router_b[l], exp_w1[l], exp_b1[l], exp_w2[l], exp_b2[l])
    return x
```

```python
import functools

import numpy as np
import jax
import jax.numpy as jnp
from jax import lax
from jax.experimental import pallas as pl
from jax.experimental.pallas import tpu as pltpu

F32 = jnp.float32
BF16 = jnp.bfloat16

D_MODEL = 1024
HEAD_DIM = 64
HEADS = 4
GROUP = HEADS * HEAD_DIM
LANES = 128
MLA_Q_RANK = 256
MLA_KV_RANK = 128
MLA_NOPE = 64
MLA_ROPE = 32
MLA_QK = MLA_NOPE + MLA_ROPE
ROPE_THETA = 10000.0
MOBA_BLOCK = 256
MOBA_TOPK = 3
MOBA_MAX_BLOCKS = 64
DIL_PATTERNS = ((128, 1), (512, 4), (2048, 16))
DIL_SPAN = 128
DIL_TILE = 2048
N_EXPERTS = 32
TOP_K = 4
D_EXPERT = 1024
SWIGLU_LIMIT = 7.0
SWIGLU_ALPHA = 1.702
EPS = 1e-6
NEG = -1e30

TOKEN_TILE = 256
EXPERT_ROWS = 256
VMEM_LIMIT = 56 * 1024 * 1024

COLS_MLA = 512
COLS_FOX = 896
COLS_MOBA = 768
COLS_DIL = 768
COLS_IN = COLS_MLA + COLS_FOX + COLS_MOBA + COLS_DIL


def _dot(a, b):
    return jnp.dot(a, b, preferred_element_type=F32)


def _dot_nt(a, b):
    return lax.dot_general(a, b, (((1,), (1,)), ((), ())), preferred_element_type=F32)


def _split2(x):
    hi = x.astype(BF16)
    lo = (x - hi.astype(F32)).astype(BF16)
    return hi, lo


def _split3(x):
    a = x.astype(BF16)
    r = x - a.astype(F32)
    b = r.astype(BF16)
    c = (r - b.astype(F32)).astype(BF16)
    return a, b, c


def _head_of_lane():
    return jnp.right_shift(lax.broadcasted_iota(jnp.int32, (1, GROUP), 1), 6)


def _full_spec(shape):
    nd = len(shape)
    return pl.BlockSpec(shape, lambda *_: (0,) * nd)


def _params(sem):
    return pltpu.CompilerParams(dimension_semantics=sem, vmem_limit_bytes=VMEM_LIMIT)


def _mod_kernel(c_ref, w_ref, b_ref, o_ref):
    c = c_ref[...]
    s = c * (1.0 / (1.0 + jnp.exp(-c)))
    s8 = jnp.broadcast_to(s, (8, D_MODEL))
    r = jnp.dot(s8, w_ref[0], preferred_element_type=F32, precision=lax.Precision.HIGHEST)
    o_ref[0, 0] = r[0:1, :] + b_ref[0, 0]


def _modulation(c, w_mod, b_mod):
    depth = w_mod.shape[0]
    b4 = b_mod.reshape(depth, 6, 1, D_MODEL)
    return pl.pallas_call(
        _mod_kernel,
        grid=(depth, 6),
        in_specs=[
            pl.BlockSpec((1, D_MODEL), lambda l, j: (0, 0)),
            pl.BlockSpec((1, D_MODEL, D_MODEL), lambda l, j: (l, 0, j)),
            pl.BlockSpec((1, 1, 1, D_MODEL), lambda l, j: (l, j, 0, 0)),
        ],
        out_specs=pl.BlockSpec((1, 1, 1, D_MODEL), lambda l, j: (l, j, 0, 0)),
        out_shape=jax.ShapeDtypeStruct((depth, 6, 1, D_MODEL), F32),
        compiler_params=_params(("arbitrary", "arbitrary")),
        name="modulation",
    )(c, w_mod, b4)


def _head_norm(x, g, bd):
    hi, lo = _split2(x * x)
    ss = _dot(hi, bd) + _dot(lo, bd)
    return x * lax.rsqrt(ss * (1.0 / HEAD_DIM) + EPS) * g


def _prep_kernel(x_ref, mod_ref, g1_ref, win_ref, cqg_ref, wuq_ref, ckvg_ref, wuk_ref, wuv_ref,
                 qg_ref, kg_ref, rc_ref, rs1_ref, rs2_ref,
                 fqg_ref, fkg_ref, fb_ref, mqg_ref, mkg_ref, dqg_ref, dkg_ref,
                 bd_ref, tri_ref, eq_ref, ek_ref, sel_ref,
                 qa_ref, ka_ref, va_ref, qf_ref, kf_ref, vf_ref, qm_ref, km_ref, vm_ref,
                 qd_ref, kd_ref, vd_ref,
                 fcarry, kmean):
    i = pl.program_id(0)
    tm = x_ref.shape[0]

    @pl.when(i == 0)
    def _():
        fcarry[...] = jnp.zeros_like(fcarry)
        kmean[...] = jnp.zeros_like(kmean)

    x = x_ref[...]
    y = x * lax.rsqrt(jnp.mean(x * x, axis=-1, keepdims=True) + EPS) * g1_ref[...]
    hb = (y * (1.0 + mod_ref[1]) + mod_ref[0]).astype(BF16)
    bd = bd_ref[...]
    lane = lax.broadcasted_iota(jnp.int32, (1, LANES), 1)
    lane_f = lane.astype(F32)
    head_of_lane = _head_of_lane()

    pa = _dot(hb, win_ref[:, 0:COLS_MLA])
    cq = pa[:, 0:MLA_Q_RANK]
    ckv = pa[:, MLA_Q_RANK:MLA_Q_RANK + MLA_KV_RANK]
    kr = pa[:, MLA_Q_RANK + MLA_KV_RANK:COLS_MLA]
    cqn = (cq * lax.rsqrt(jnp.mean(cq * cq, axis=-1, keepdims=True) + EPS) * cqg_ref[...]).astype(BF16)
    ckvn = (ckv * lax.rsqrt(jnp.mean(ckv * ckv, axis=-1, keepdims=True) + EPS) * ckvg_ref[...]).astype(BF16)
    q_all = _dot(cqn, wuq_ref[...])
    k_all = _dot(ckvn, wuk_ref[...])
    va_ref[...] = _dot(ckvn, wuv_ref[...]).astype(BF16)
    rc, rs1, rs2 = rc_ref[...], rs1_ref[...], rs2_ref[...]

    def rope(t):
        return t * rc + pltpu.roll(t, LANES - MLA_ROPE // 2, 1) * rs1 + pltpu.roll(t, MLA_ROPE // 2, 1) * rs2

    for h in range(HEADS):
        q = q_all[:, h * LANES:(h + 1) * LANES]
        q = q * lax.rsqrt(jnp.sum(q * q, axis=-1, keepdims=True) * (1.0 / MLA_QK) + EPS) * qg_ref[...]
        qa_ref[h] = (rope(q) * (MLA_QK ** -0.5)).astype(BF16)
        k = k_all[:, h * LANES:(h + 1) * LANES] + kr
        k = k * lax.rsqrt(jnp.sum(k * k, axis=-1, keepdims=True) * (1.0 / MLA_QK) + EPS) * kg_ref[...]
        ka_ref[h] = rope(k).astype(BF16)

    pf = _dot(hb, win_ref[:, COLS_MLA:COLS_MLA + COLS_FOX])
    fqn = (_head_norm(pf[:, 0:GROUP], fqg_ref[...], bd) * (HEAD_DIM ** -0.5)).astype(BF16)
    fkn = _head_norm(pf[:, GROUP:2 * GROUP], fkg_ref[...], bd).astype(BF16)
    vf_ref[...] = pf[:, 2 * GROUP:3 * GROUP].astype(BF16)
    z = pf[:, 3 * GROUP:3 * GROUP + LANES] + fb_ref[...]
    log_f = jnp.minimum(z, 0.0) - jnp.log(1.0 + jnp.exp(-jnp.abs(z)))
    tri = tri_ref[...]
    a1, a2, a3 = _split3(log_f)
    cum = fcarry[...] + (_dot(tri, a1) + _dot(tri, a2) + _dot(tri, a3))
    fcarry[...] = cum[tm - 1:tm, :]
    f1, f2, f3 = _split3(cum)
    xq = jnp.concatenate([fqn, f1, f2, f3], axis=1)
    xk = jnp.concatenate([fkn, f1, f2, f3], axis=1)
    ones_q = jnp.where((lane >= HEAD_DIM + 3) & (lane < HEAD_DIM + 6), 1.0, 0.0)
    ones_k = jnp.where((lane >= HEAD_DIM) & (lane < HEAD_DIM + 3), 1.0, 0.0)
    for h in range(HEADS):
        qf_ref[h] = (_dot(xq, eq_ref[h]) + ones_q).astype(BF16)
        kf_ref[h] = (_dot(xk, ek_ref[h]) + ones_k).astype(BF16)

    pm = _dot(hb, win_ref[:, COLS_MLA + COLS_FOX:COLS_MLA + COLS_FOX + COLS_MOBA])
    mqn = _head_norm(pm[:, 0:GROUP], mqg_ref[...], bd) * (HEAD_DIM ** -0.5)
    mkn = _head_norm(pm[:, GROUP:2 * GROUP], mkg_ref[...], bd)
    vm_ref[...] = pm[:, 2 * GROUP:3 * GROUP].astype(BF16)
    col_mean = jnp.mean(mkn, axis=0, keepdims=True)
    mqb = mqn.astype(BF16)
    mkb = mkn.astype(BF16)
    blk = lane - HEAD_DIM
    blk_f = blk.astype(F32)
    past = (blk >= 0) & (blk < i)
    i_f = i.astype(F32)
    for h in range(HEADS):
        kmean[h, pl.ds(HEAD_DIM + i, 1), :] = jnp.where(head_of_lane == h, col_mean, 0.0)
        qh_hi, qh_lo = _split2(jnp.where(head_of_lane == h, mqn, 0.0))
        km_hi, km_lo = _split2(kmean[h])
        gate = _dot_nt(qh_hi, km_hi) + _dot_nt(qh_hi, km_lo) + _dot_nt(qh_lo, km_hi)
        g = jnp.where(past, gate, NEG)
        chosen = jnp.zeros((tm, LANES), F32)
        for _ in range(MOBA_TOPK):
            m = jnp.max(g, axis=-1, keepdims=True)
            first = jnp.min(jnp.where(g == m, lane_f, 1e9), axis=-1, keepdims=True)
            pick = (lane_f == first) & (m > NEG)
            chosen = jnp.where(pick, 1.0, chosen)
            g = jnp.where(pick, NEG, g)
        slope = 2.0 ** (-(2 * h + 2))
        keep = (chosen > 0.0) | (blk == i)
        bias = jnp.where(keep, (slope * MOBA_BLOCK) * (blk_f - i_f), NEG)
        bias = jnp.where(blk >= 0, bias, 0.0)
        qm_ref[h] = (_dot(mqb, sel_ref[h]) + bias).astype(BF16)
        onehot = jnp.where(blk == i, 1.0, 0.0)
        km_ref[h] = (_dot(mkb, sel_ref[h]) + onehot).astype(BF16)

    pd = _dot(hb, win_ref[:, COLS_MLA + COLS_FOX + COLS_MOBA:COLS_IN])
    qd_ref[...] = _head_norm(pd[:, 0:GROUP], dqg_ref[...], bd) * (HEAD_DIM ** -0.5)
    kd_ref[...] = _head_norm(pd[:, GROUP:2 * GROUP], dkg_ref[...], bd)
    vd_ref[...] = pd[:, 2 * GROUP:3 * GROUP]


def _prep_constants(seq, tm):
    half = MLA_ROPE // 2
    inv = 1.0 / (ROPE_THETA ** (jnp.arange(half, dtype=F32) / half))
    ang = jnp.arange(seq, dtype=F32)[:, None] * inv[None, :]
    cos, sin = jnp.cos(ang), jnp.sin(ang)
    z = lambda n: jnp.zeros((seq, n), F32)
    rc = jnp.concatenate([jnp.ones((seq, MLA_NOPE), F32), cos, cos, z(LANES - MLA_QK)], axis=1)
    rs1 = jnp.concatenate([z(MLA_NOPE), -sin, z(LANES - MLA_NOPE - half)], axis=1)
    rs2 = jnp.concatenate([z(MLA_NOPE + half), sin, z(LANES - MLA_QK)], axis=1)
    bd =np.kron(np.eye(HEADS, dtype=np.float32), np.ones((HEAD_DIM, HEAD_DIM), np.float32))
    tri = np.tril(np.ones((tm, tm), np.float32))
    tri_strict = np.tril(np.ones((tm, tm), np.float32), -1)
    sel = np.zeros((HEADS, GROUP, LANES), np.float32)
    eq = np.zeros((HEADS, GROUP + 3 * LANES, LANES), np.float32)
    ek = np.zeros((HEADS, GROUP + 3 * LANES, LANES), np.float32)
    for h in range(HEADS):
        for d in range(HEAD_DIM):
            sel[h, h * HEAD_DIM + d, d] = 1.0
        eq[h, :GROUP] = sel[h]
        ek[h, :GROUP] = sel[h]
        for piece in range(3):
            eq[h, GROUP + piece * LANES + h, HEAD_DIM + piece] = 1.0
            ek[h, GROUP + piece * LANES + h, HEAD_DIM + 3 + piece] = -1.0
    as_bf = lambda a: jnp.asarray(a, BF16)
    return dict(rc=rc, rs1=rs1, rs2=rs2, bd=as_bf(bd),
                tri=as_bf(tri), tri_strict=as_bf(tri_strict), sel=as_bf(sel), eq=as_bf(eq), ek=as_bf(ek))


def _prep(x2, mod_l, consts, p):
    seq = x2.shape[0]
    tm = TOKEN_TILE
    row = lambda n: pl.BlockSpec((tm, n), lambda i: (i, 0))
    heads = pl.BlockSpec((HEADS, tm, LANES), lambda i: (0, i, 0))
    in_arrays = [
        (x2, row(D_MODEL)), (mod_l, _full_spec(mod_l.shape)), (p["g1"], None), (p["w_in"], None),
        (p["cq_g"], None), (p["w_uq"], None), (p["ckv_g"], None), (p["w_uk"], None), (p["w_uv"], None),
        (p["q_g"], None), (p["k_g"], None),
        (consts["rc"], row(LANES)), (consts["rs1"], row(LANES)), (consts["rs2"], row(LANES)),
        (p["fq_g"], None), (p["fk_g"], None), (p["f_b"], None), (p["mq_g"], None), (p["mk_g"], None),
        (p["dq_g"], None), (p["dk_g"], None),
        (consts["bd"], None), (consts["tri"], None), (consts["eq"], None), (consts["ek"], None),
        (consts["sel"], None),
    ]
    args = [a for a, _ in in_arrays]
    specs = [s if s is not None else _full_spec(a.shape) for a, s in in_arrays]
    hshape = jax.ShapeDtypeStruct((HEADS, seq, LANES), BF16)
    vshape = jax.ShapeDtypeStruct((seq, GROUP), BF16)
    dshape = jax.ShapeDtypeStruct((seq, GROUP), F32)
    return pl.pallas_call(
        _prep_kernel,
        grid=(seq // tm,),
        in_specs=specs,
        out_specs=[heads, heads, row(GROUP)] * 3 + [row(GROUP)] * 3,
        out_shape=[hshape, hshape, vshape] * 3 + [dshape] * 3,
        scratch_shapes=[pltpu.VMEM((1, LANES), F32), pltpu.VMEM((HEADS, LANES, GROUP), F32)],
        compiler_params=_params(("arbitrary",)),
        name="prep",
    )(*args)


def _flash_kernel(q_ref, k_ref, v_ref, arow_ref, o_ref, m_sc, l_sc, acc_sc, *, tile, use_arow):
    i = pl.program_id(1)
    m_sc[...] = jnp.full_like(m_sc, -jnp.inf)
    l_sc[...] = jnp.zeros_like(l_sc)
    acc_sc[...] = jnp.zeros_like(acc_sc)
    row = lax.broadcasted_iota(jnp.int32, (tile, tile), 0)
    col = lax.broadcasted_iota(jnp.int32, (tile, tile), 1)

    def step(j, causal):
        start = pl.multiple_of(j * tile, tile)
        v = v_ref[pl.ds(start, tile), :]
        for hh in range(2):
            k = k_ref[hh, pl.ds(start, tile), :]
            s = _dot_nt(q_ref[hh], k)
            if use_arow:
                s = s + arow_ref[hh]
            if causal:
                s = jnp.where(col <= row, s, NEG)
            m_prev = m_sc[hh]
            m_new = jnp.maximum(m_prev, jnp.max(s, axis=-1, keepdims=True))
            alpha = jnp.exp(m_prev - m_new)
            p = jnp.exp(s - m_new)
            l_sc[hh] = alpha * l_sc[hh] + jnp.sum(p, axis=-1, keepdims=True)
            acc_sc[hh] = alpha * acc_sc[hh] + _dot(p.astype(BF16), v)
            m_sc[hh] = m_new

    def body(j, carry):
        step(j, False)
        return carry

    lax.fori_loop(0, i, body, 0)
    step(i, True)
    lane = lax.broadcasted_iota(jnp.int32, (tile, LANES), 1)
    o0 = acc_sc[0] / l_sc[0]
    o1 = acc_sc[1] / l_sc[1]
    o_ref[...] = jnp.where(lane < HEAD_DIM, o0, o1).astype(o_ref.dtype)


def _flash(q, k, v, arow, use_arow):
    seq = v.shape[0]
    tile = MOBA_BLOCK
    kern = functools.partial(_flash_kernel, tile=tile, use_arow=use_arow)
    return pl.pallas_call(
        kern,
        grid=(HEADS // 2, seq // tile),
        in_specs=[
            pl.BlockSpec((2, tile, LANES), lambda p, i: (p, i, 0)),
            pl.BlockSpec((2, seq, LANES), lambda p, i: (p, 0, 0)),
            pl.BlockSpec((seq, LANES), lambda p, i: (0, p)),
            pl.BlockSpec((2, 1, tile), lambda p, i: (p, 0, 0)),
        ],
        out_specs=pl.BlockSpec((tile, LANES), lambda p, i: (i, p)),
        out_shape=jax.ShapeDtypeStruct((seq, GROUP), BF16),
        scratch_shapes=[pltpu.VMEM((2, tile, 1), F32), pltpu.VMEM((2, tile, 1), F32),
                        pltpu.VMEM((2, tile, LANES), F32)],
        compiler_params=_params(("arbitrary", "arbitrary")),
        name="flash_arow" if use_arow else "flash",
    )(q, k, v, arow)


def _dilated_kernel(q_ref, k_ref, v_ref, o_ref, kbuf, vbuf, acc_s, m_s, l_s):
    pair = pl.program_id(0)
    i = pl.program_id(1)
    T = q_ref.shape[0]

    @pl.when(i == 0)
    def _():
        kbuf[...] = jnp.zeros_like(kbuf)
        vbuf[...] = jnp.zeros_like(vbuf)

    kbuf[0:T, :] = kbuf[T:2 * T, :]
    vbuf[0:T, :] = vbuf[T:2 * T, :]
    kbuf[T:2 * T, :] = k_ref[...]
    vbuf[T:2 * T, :] = v_ref[...]

    ii = lax.broadcasted_iota(jnp.int32, (DIL_SPAN, 2 * DIL_SPAN), 0)
    jj = lax.broadcasted_iota(jnp.int32, (DIL_SPAN, 2 * DIL_SPAN), 1)
    dist = ii + DIL_SPAN - jj
    band = (dist >= 0) & (dist <= DIL_SPAN)
    dist_f = dist.astype(F32)
    upper = lax.broadcasted_iota(jnp.int32, (1, LANES), 1) >= HEAD_DIM

    for pi, (window, r) in enumerate(DIL_PATTERNS):
        assert window // r == DIL_SPAN
        sub = DIL_SPAN * r

        def body(idx, carry, r=r, sub=sub, pi=pi):
            n = idx // r
            rho = idx - n * r
            base = n * sub + rho
            q = q_ref[pl.ds(base, DIL_SPAN, stride=r), :]
            kc = kbuf[pl.ds(T + base - sub, 2 * DIL_SPAN, stride=r), :].astype(BF16)
            vc = vbuf[pl.ds(T + base - sub, 2 * DIL_SPAN, stride=r), :].astype(BF16)
            first_key = jnp.where((i == 0) & (n == 0), DIL_SPAN, 0)
            valid = band & (jj >= first_key)
            stats = []
            for hh in range(2):
                slope = jnp.where(pair == 0, 2.0 ** (-(2 * hh + 1)), 2.0 ** (-(2 * hh + 5)))
                qh = (jnp.where(upper, q, 0.0) if hh else jnp.where(upper, 0.0, q)).astype(BF16)
                s = _dot_nt(qh, kc) - (slope * r) * dist_f
                s = jnp.where(valid, s, NEG)
                m = jnp.max(s, axis=-1, keepdims=True)
                p = jnp.exp(s - m)
                l = jnp.sum(p, axis=-1, keepdims=True)
                stats.append((_dot(p.astype(BF16), vc), m, l))
            rows = pl.ds(pi * T + base, DIL_SPAN, stride=r)
            acc_s[rows, :] = jnp.where(upper, stats[1][0], stats[0][0])
            m_s[rows, :] = jnp.where(upper, stats[1][1], stats[0][1])
            l_s[rows, :] = jnp.where(upper, stats[1][2], stats[0][2])
            return carry

        lax.fori_loop(0, T // DIL_SPAN, body, 0)

    npat = len(DIL_PATTERNS)
    ms = [m_s[pi * T:(pi + 1) * T, :] for pi in range(npat)]
    m_top = functools.reduce(jnp.maximum, ms)
    num = jnp.zeros((T, LANES), F32)
    den = jnp.zeros((T, LANES), F32)
    for pi in range(npat):
        w = jnp.exp(ms[pi] - m_top)
        num = num + w * acc_s[pi * T:(pi + 1) * T, :]
        den = den + w * l_s[pi * T:(pi + 1) * T, :]
    o_ref[...] = (num / den).astype(o_ref.dtype)


def _dilated(q, k, v):
    seq = q.shape[0]
    T = DIL_TILE
    spec = pl.BlockSpec((T, LANES), lambda p, i: (i, p))
    npat = len(DIL_PATTERNS)
    return pl.pallas_call(
        _dilated_kernel,
        grid=(HEADS // 2, seq // T),
        in_specs=[spec, spec, spec],
        out_specs=spec,
        out_shape=jax.ShapeDtypeStruct((seq, GROUP), BF16),
        scratch_shapes=[pltpu.VMEM((2 * T, LANES), F32), pltpu.VMEM((2 * T, LANES), F32),
                        pltpu.VMEM((npat * T, LANES), F32), pltpu.VMEM((npat * T, LANES), F32),
                        pltpu.VMEM((npat * T, LANES), F32)],
        compiler_params=_params(("arbitrary", "arbitrary")),
        name="dilated",
    )(q, k, v)


def _post_kernel(x_ref, oa_ref, ob_ref, oc_ref, od_ref, wout_ref, mod_ref, g2_ref,
                 rwh_ref, rwl_ref, rb_ref, tri_ref,
                 x1_ref, h2_ref, eidx_ref, gate_ref, rank_ref, cnt_ref, carry):
    i = pl.program_id(0)
    tm = x_ref.shape[0]

    @pl.when(i == 0)
    def _():
        carry[...] = jnp.zeros_like(carry)

    o = (_dot(oa_ref[...], wout_ref[0]) + _dot(ob_ref[...], wout_ref[1])
         + _dot(oc_ref[...], wout_ref[2]) + _dot(od_ref[...], wout_ref[3]))
    x1 = x_ref[...] + mod_ref[2] * o
    x1_ref[...] = x1
    y = x1 * lax.rsqrt(jnp.mean(x1 * x1, axis=-1, keepdims=True) + EPS) * g2_ref[...]
    h2 = y * (1.0 + mod_ref[4]) + mod_ref[3]
    h2_ref[...] = h2

    h_hi, h_lo = _split2(h2)
    logits = (_dot(h_hi, rwh_ref[...]) + _dot(h_hi, rwl_ref[...]) + _dot(h_lo, rwh_ref[...])
              + rb_ref[...])
    lane = lax.broadcasted_iota(jnp.int32, (tm, LANES), 1)
    lane_f = lane.astype(F32)
    g = logits
    chosen = jnp.zeros((tm, LANES), F32)
    vals, idxs = [], []
    for _ in range(TOP_K):
        m = jnp.max(g, axis=-1, keepdims=True)
        first = jnp.min(jnp.where(g == m, lane_f, 1e9), axis=-1, keepdims=True)
        pick = lane_f == first
        chosen = jnp.where(pick, 1.0, chosen)
        g = jnp.where(pick, -jnp.inf, g)
        vals.append(m)
        idxs.append(first)
    exps = [jnp.exp(v - vals[0]) for v in vals]
    den = exps[0] + exps[1] + exps[2] + exps[3]
    before = _dot(tri_ref[...], chosen.astype(BF16)) + carry[...]
    carry[...] = carry[...] + jnp.sum(chosen, axis=0, keepdims=True)
    cnt_ref[...] = carry[...]
    e_out = jnp.zeros((tm, LANES), F32)
    g_out = jnp.zeros((tm, LANES), F32)
    r_out = jnp.zeros((tm, LANES), F32)
    for k in range(TOP_K):
        rank_k = jnp.sum(jnp.where(lane_f == idxs[k], before, 0.0), axis=-1, keepdims=True)
        e_out = jnp.where(lane == k, idxs[k], e_out)
        g_out = jnp.where(lane == k, exps[k] / den, g_out)
        r_out = jnp.where(lane == k, rank_k, r_out)
    eidx_ref[...] = e_out.astype(jnp.int32)
    gate_ref[...] = g_out
    rank_ref[...] = r_out.astype(jnp.int32)


def _post(x2, oa, ob, oc, od, mod_l, consts, p):
    seq = x2.shape[0]
    tm = TOKEN_TILE
    row = lambda n: pl.BlockSpec((tm, n), lambda i: (i, 0))
    full = [p["w_out"], mod_l, p["g2"], p["rw_hi"], p["rw_lo"], p["r_b"], consts["tri_strict"]]
    f32 = lambda n: jax.ShapeDtypeStruct((seq, n), F32)
    i32 = lambda n: jax.ShapeDtypeStruct((seq, n), jnp.int32)
    return pl.pallas_call(
        _post_kernel,
        grid=(seq // tm,),
        in_specs=[row(D_MODEL)] + [row(GROUP)] * 4 + [_full_spec(a.shape) for a in full],
        out_specs=[row(D_MODEL), row(D_MODEL), row(LANES), row(LANES), row(LANES),
                   _full_spec((1, LANES))],
        out_shape=[f32(D_MODEL), f32(D_MODEL), i32(LANES), f32(LANES), i32(LANES),
                   jax.ShapeDtypeStruct((1, LANES), F32)],
        scratch_shapes=[pltpu.VMEM((1, LANES), F32)],
        compiler_params=_params(("arbitrary",)),
        name="post",
    )(x2, oa, ob, oc, od, *full)


def _row_copy(src, dst, s, d, sem):
    return pltpu.make_async_copy(src.at[pl.ds(s, 1), :], dst.at[pl.ds(d, 1), :], sem)


def _dispatch_kernel(dest_ref, h_ref, xs_in_ref, xs_ref, sem):
    del xs_in_ref
    i = pl.program_id(0)
    n = TOKEN_TILE * TOP_K
    base = i * n

    def issue(a, carry):
        _row_copy(h_ref, xs_ref, (base + a) // TOP_K, dest_ref[base + a], sem).start()
        return carry

    lax.fori_loop(0, n, issue, 0)

    def drain(a, carry):
        _row_copy(h_ref, xs_ref, 0, 0, sem).wait()
        return carry

    lax.fori_loop(0, n, drain, 0)


def _dispatch(dest, h2, m_pad):
    seq = h2.shape[0]
    zeros = jnp.zeros((m_pad, D_MODEL), F32)
    return pl.pallas_call(
        _dispatch_kernel,
        grid_spec=pltpu.PrefetchScalarGridSpec(
            num_scalar_prefetch=1,
            grid=(seq // TOKEN_TILE,),
            in_specs=[pl.BlockSpec(memory_space=pl.ANY), pl.BlockSpec(memory_space=pl.ANY)],
            out_specs=pl.BlockSpec(memory_space=pl.ANY),
            scratch_shapes=[pltpu.SemaphoreType.DMA(())],
        ),
        out_shape=jax.ShapeDtypeStruct((m_pad, D_MODEL), F32),
        input_output_aliases={2: 0},
        compiler_params=pltpu.CompilerParams(dimension_semantics=("arbitrary",)),
        name="dispatch",
    )(dest, h2, zeros)


def _expert_kernel(be_ref, nu_ref, xs_ref, w1_ref, b1_ref, w2_ref, b2_ref, ys_ref, w1b, w2b):
    b = pl.program_id(0)
    prev = be_ref[jnp.maximum(b - 1, 0)]
    fresh = (b == 0) | (be_ref[b] != prev)

    @pl.when(fresh)
    def _():
        w1b[...] = w1_ref[0].astype(BF16)
        w2b[...] = w2_ref[0].astype(BF16)

    @pl.when(b < nu_ref[0])
    def _():
        gu = _dot(xs_ref[...].astype(BF16), w1b[...]) + b1_ref[0]
        g = jnp.minimum(gu[:, :D_EXPERT], SWIGLU_LIMIT)
        u = jnp.clip(gu[:, D_EXPERT:], -SWIGLU_LIMIT, SWIGLU_LIMIT)
        y = (u + 1.0) * g * (1.0 / (1.0 + jnp.exp(-SWIGLU_ALPHA * g)))
        ys_ref[...] = _dot(y.astype(BF16), w2b[...]) + b2_ref[0]

    @pl.when(b >= nu_ref[0])
    def _():
        ys_ref[...] = jnp.zeros_like(ys_ref)


def _experts(blk_expert, n_used, xs, w1, b1, w2, b2):
    m_pad = xs.shape[0]
    bm = EXPERT_ROWS
    rows = lambda b, be, nu: (jnp.minimum(b, nu[0] - 1), 0)
    ex = lambda b, be, nu: (be[jnp.minimum(b, nu[0] - 1)], 0, 0)
    return pl.pallas_call(
        _expert_kernel,
        grid_spec=pltpu.PrefetchScalarGridSpec(
            num_scalar_prefetch=2,
            grid=(m_pad // bm,),
            in_specs=[
                pl.BlockSpec((bm, D_MODEL), rows),
                pl.BlockSpec((1, D_MODEL, 2 * D_EXPERT), ex),
                pl.BlockSpec((1, 1, 2 * D_EXPERT), ex),
                pl.BlockSpec((1, D_EXPERT, D_MODEL), ex),
                pl.BlockSpec((1, 1, D_MODEL), ex),
            ],
            out_specs=pl.BlockSpec((bm, D_MODEL), lambda b, be, nu: (b, 0)),
            scratch_shapes=[pltpu.VMEM((D_MODEL, 2 * D_EXPERT), BF16),
                            pltpu.VMEM((D_EXPERT, D_MODEL), BF16)],
        ),
        out_shape=jax.ShapeDtypeStruct((m_pad, D_MODEL), F32),
        compiler_params=_params(("arbitrary",)),
        name="experts",
    )(blk_expert, n_used, xs, w1, b1.reshape(N_EXPERTS, 1, -1), w2, b2.reshape(N_EXPERTS, 1, -1))


def _combine_kernel(dest_ref, ys_ref, x1_ref, gate_ref, mod_ref, o_ref, buf, sem):
    i = pl.program_id(0)
    tm = x1_ref.shape[0]
    n = tm * TOP_K
    base = i * n

    def issue(a, carry):
        r = a // TOP_K
        k = a - r * TOP_K
        pltpu.make_async_copy(ys_ref.at[pl.ds(dest_ref[base + a], 1), :],
                              buf.at[k, pl.ds(r, 1), :], sem).start()
        return carry

    lax.fori_loop(0, n, issue, 0)

    def drain(a, carry):
        pltpu.make_async_copy(ys_ref.at[pl.ds(0, 1), :], buf.at[0, pl.ds(0, 1), :], sem).wait()
        return carry

    lax.fori_loop(0, n, drain, 0)
    gates = gate_ref[...]
    mix = jnp.zeros((tm, D_MODEL), F32)
    for k in range(TOP_K):
        mix = mix + gates[:, k:k + 1] * buf[k]
    o_ref[...] = x1_ref[...] + mod_ref[5] * mix


def _combine(dest, ys, x1, gates, mod_l):
    seq = x1.shape[0]
    tm = TOKEN_TILE
    return pl.pallas_call(
        _combine_kernel,
        grid_spec=pltpu.PrefetchScalarGridSpec(
            num_scalar_prefetch=1,
            grid=(seq // tm,),
            in_specs=[
                pl.BlockSpec(memory_space=pl.ANY),
                pl.BlockSpec((tm, D_MODEL), lambda i, d: (i, 0)),
                pl.BlockSpec((tm, LANES), lambda i, d: (i, 0)),
                pl.BlockSpec(mod_l.shape, lambda i, d: (0, 0, 0)),
            ],
            out_specs=pl.BlockSpec((tm, D_MODEL), lambda i, d: (i, 0)),
            scratch_shapes=[pltpu.VMEM((TOP_K, tm, D_MODEL), F32), pltpu.SemaphoreType.DMA(())],
        ),
        out_shape=jax.ShapeDtypeStruct((seq, D_MODEL), F32),
        compiler_params=_params(("arbitrary",)),
        name="combine",
    )(dest, ys, x1, gates, mod_l)


def _moe(x1, h2, eidx, gates, rank, counts, mod_l, p):
    seq = x1.shape[0]
    bm = EXPERT_ROWS
    m_pad = seq * TOP_K + N_EXPERTS * bm
    cnt = counts[0, :N_EXPERTS].astype(jnp.int32)
    padded = (cnt + bm - 1) // bm * bm
    pad_end = jnp.cumsum(padded)
    pad_start = pad_end - padded
    dest = (pad_start[eidx[:, :TOP_K]] + rank[:, :TOP_K]).reshape(seq * TOP_K).astype(jnp.int32)
    nblk = m_pad // bm
    blk_expert = jnp.minimum(
        jnp.searchsorted(pad_end, jnp.arange(nblk, dtype=jnp.int32) * bm, side="right"),
        N_EXPERTS - 1).astype(jnp.int32)
    n_used = (pad_end[-1:] // bm).astype(jnp.int32)
    xs = _dispatch(dest, h2, m_pad)
    ys = _experts(blk_expert, n_used, xs, p["w1"], p["b1"], p["w2"], p["b2"])
    return _combine(dest, ys, x1, gates, mod_l)


def _pad_cols(a, n):
    return jnp.pad(a, ((0, 0), (0, n - a.shape[1])))


def _layer_params(l, w_in, mla_cq_g, mla_w_uq, mla_ckv_g, mla_w_ukv, mla_q_g, mla_k_g,
                  fox_q_g, fox_k_g, fox_b_f, moba_q_g, moba_k_g, dil_q_g, dil_k_g, w_out,
                  norm1_g, norm2_g, router_w, router_b, exp_w1, exp_b1, exp_w2, exp_b2):
    w = w_in[l]
    sizes = [MLA_Q_RANK, MLA_KV_RANK, MLA_ROPE, GROUP, GROUP, GROUP, HEADS] + [GROUP] * 6
    offs = np.concatenate([[0], np.cumsum(sizes)])
    part = [w[:, offs[j]:offs[j + 1]] for j in range(len(sizes))]
    zeros = lambda n: jnp.zeros((D_MODEL, n), F32)
    w_in_r = jnp.concatenate(
        [part[0], part[1], zeros(MLA_NOPE), part[2], zeros(LANES - MLA_QK),
         part[3], part[4], part[5], part[6], zeros(LANES - HEADS)] + part[7:13], axis=1).astype(BF16)
    assert w_in_r.shape[1] == COLS_IN
    w_uq = jnp.pad(mla_w_uq[l].reshape(MLA_Q_RANK, HEADS, MLA_QK),
                   ((0, 0), (0, 0), (0, LANES - MLA_QK))).reshape(MLA_Q_RANK, HEADS * LANES)
    w_ukv = mla_w_ukv[l].reshape(MLA_KV_RANK, HEADS, MLA_NOPE + HEAD_DIM)
    w_uk = jnp.pad(w_ukv[:, :, :MLA_NOPE], ((0, 0), (0, 0), (0, LANES - MLA_NOPE)))
    w_uv = w_ukv[:, :, MLA_NOPE:]
    tile4 = lambda g: jnp.tile(g, HEADS)[None, :]
    rw = _pad_cols(router_w[l], LANES)
    rw_hi = rw.astype(BF16)
    rw_lo = (rw - rw_hi.astype(F32)).astype(BF16)
    r_b = jnp.concatenate([router_b[l], jnp.full((LANES - N_EXPERTS,), NEG, F32)])[None, :]
    return dict(
        g1=norm1_g[l][None, :], g2=norm2_g[l][None, :], w_in=w_in_r,
        cq_g=mla_cq_g[l][None, :], w_uq=w_uq.astype(BF16), ckv_g=mla_ckv_g[l][None, :],
        w_uk=w_uk.reshape(MLA_KV_RANK, HEADS * LANES).astype(BF16),
        w_uv=w_uv.reshape(MLA_KV_RANK, GROUP).astype(BF16),
        q_g=_pad_cols(mla_q_g[l][None, :], LANES), k_g=_pad_cols(mla_k_g[l][None, :], LANES),
        fq_g=tile4(fox_q_g[l]), fk_g=tile4(fox_k_g[l]), f_b=_pad_cols(fox_b_f[l][None, :], LANES),
        mq_g=tile4(moba_q_g[l]), mk_g=tile4(moba_k_g[l]), dq_g=tile4(dil_q_g[l]), dk_g=tile4(dil_k_g[l]),
        w_out=w_out[l].reshape(HEADS, GROUP, D_MODEL).astype(BF16),
        rw_hi=rw_hi, rw_lo=rw_lo, r_b=r_b,
        w1=exp_w1[l], b1=exp_b1[l], w2=exp_w2[l], b2=exp_b2[l],
    )


def kernel(x, c, w_mod, b_mod, norm1_g, norm2_g, w_in, mla_cq_g, mla_w_uq, mla_ckv_g, mla_w_ukv, mla_q_g, mla_k_g, fox_q_g, fox_k_g, fox_b_f, moba_q_g, moba_k_g, dil_q_g, dil_k_g, w_out, router_w, router_b, exp_w1, exp_b1, exp_w2, exp_b2):
    batch, seq, d = x.shape
    assert batch == 1 and d == D_MODEL
    assert seq % DIL_TILE == 0 and seq // MOBA_BLOCK <= MOBA_MAX_BLOCKS
    depth = w_mod.shape[0]
    consts = _prep_constants(seq, TOKEN_TILE)
    mod = _modulation(c, w_mod, b_mod)
    slopes_c = 2.0 ** (-(2.0 * np.arange(HEADS) + 2.0))
    arow_c = jnp.asarray(slopes_c[:, None, None] * np.arange(MOBA_BLOCK)[None, None, :], F32)
    arow_0 = jnp.zeros((HEADS, 1, MOBA_BLOCK), F32)
    x2 = x.reshape(seq, d)
    for l in range(depth):
        p = _layer_params(l, w_in, mla_cq_g, mla_w_uq, mla_ckv_g, mla_w_ukv, mla_q_g, mla_k_g,
                          fox_q_g, fox_k_g, fox_b_f, moba_q_g, moba_k_g, dil_q_g, dil_k_g, w_out,
                          norm1_g, norm2_g, router_w, router_b, exp_w1, exp_b1, exp_w2, exp_b2)
        mod_l = mod[l]
        qa, ka, va, qf, kf, vf, qm, km, vm, qd, kd, vd = _prep(x2, mod_l, consts, p)
        oa = _flash(qa, ka, va, arow_0, False)
        ob = _flash(qf, kf, vf, arow_0, False)
        oc = _flash(qm, km, vm, arow_c, True)
        od = _dilated(qd, kd, vd)
        x1, h2, eidx, gates, rank, counts = _post(x2, oa, ob, oc, od, mod_l, consts, p)
        x2 = _moe(x1, h2, eidx, gates, rank, counts, mod_l, p)
    return x2.reshape(batch, seq, d)
```

```python
import functools

import numpy as np
import jax
import jax.numpy as jnp
from jax import lax
from jax.experimental import pallas as pl
from jax.experimental.pallas import tpu as pltpu

F32 = jnp.float32
BF16 = jnp.bfloat16

D_MODEL = 1024
HEAD_DIM = 64
HEADS = 4
GROUP = HEADS * HEAD_DIM
LANES = 128
CHUNKS = D_MODEL // LANES
MLA_Q_RANK = 256
MLA_KV_RANK = 128
MLA_NOPE = 64
MLA_ROPE = 32
MLA_QK = MLA_NOPE + MLA_ROPE
ROPE_THETA = 10000.0
MOBA_BLOCK = 256
MOBA_TOPK = 3
MOBA_MAX_BLOCKS = 64
DIL_PATTERNS = ((128, 1), (512, 4), (2048, 16))
DIL_SPAN = 128
DIL_TILE = 2048
N_EXPERTS = 32
TOP_K = 4
D_EXPERT = 1024
SWIGLU_LIMIT = 7.0
SWIGLU_ALPHA = 1.702
EPS = 1e-6
NEG = -1e30

FLASH_TILE = 512
TOKEN_TILE = 256
EXPERT_ROWS = 256
VMEM_LIMIT = 56 * 1024 * 1024

COLS_MLA = 512
COLS_FOX = 896
COLS_MOBA = 768
COLS_DIL = 768
COLS_IN = COLS_MLA + COLS_FOX + COLS_MOBA + COLS_DIL


def _dot(a, b):
    return jnp.dot(a, b, preferred_element_type=F32)


def _dot_nt(a, b):
    return lax.dot_general(a, b, (((1,), (1,)), ((), ())), preferred_element_type=F32)


def _split2(x):
    hi = x.astype(BF16)
    lo = (x - hi.astype(F32)).astype(BF16)
    return hi, lo


def _split3(x):
    a = x.astype(BF16)
    r = x - a.astype(F32)
    b = r.astype(BF16)
    c = (r - b.astype(F32)).astype(BF16)
    return a, b, c


def _head_of_lane():
    return jnp.right_shift(lax.broadcasted_iota(jnp.int32, (1, GROUP), 1), 6)


def _store_token_tiles(ref, x, offset=0):
    n = x.shape[0]
    for c in range(CHUNKS):
        ref[pl.ds(offset + c, n, stride=CHUNKS), :] = x[:, c * LANES:(c + 1) * LANES]


def _load_token_chunk(ref, n, c, offset=0):
    return ref[pl.ds(offset + c, n, stride=CHUNKS), :]


def _full_spec(shape):
    nd = len(shape)
    return pl.BlockSpec(shape, lambda *_: (0,) * nd)


def _params(sem):
    return pltpu.CompilerParams(dimension_semantics=sem, vmem_limit_bytes=VMEM_LIMIT)


def _mod_kernel(c_ref, w_ref, b_ref, o_ref):
    c = c_ref[...]
    s = c * (1.0 / (1.0 + jnp.exp(-c)))
    s8 = jnp.broadcast_to(s, (8, D_MODEL))
    r = jnp.dot(s8, w_ref[0], preferred_element_type=F32, precision=lax.Precision.HIGHEST)
    o_ref[0, 0] = r[0:1, :] + b_ref[0, 0]


def _modulation(c, w_mod, b_mod):
    depth = w_mod.shape[0]
    b4 = b_mod.reshape(depth, 6, 1, D_MODEL)
    return pl.pallas_call(
        _mod_kernel,
        grid=(depth, 6),
        in_specs=[
            pl.BlockSpec((1, D_MODEL), lambda l, j: (0, 0)),
            pl.BlockSpec((1, D_MODEL, D_MODEL), lambda l, j: (l, 0, j)),
            pl.BlockSpec((1, 1, 1, D_MODEL), lambda l, j: (l, j, 0, 0)),
        ],
        out_specs=pl.BlockSpec((1, 1, 1, D_MODEL), lambda l, j: (l, j, 0, 0)),
        out_shape=jax.ShapeDtypeStruct((depth, 6, 1, D_MODEL), F32),
        compiler_params=_params(("arbitrary", "arbitrary")),
        name="modulation",
    )(c, w_mod, b4)


def _head_norm(x, g, bd):
    hi, lo = _split2(x * x)
    ss = _dot(hi, bd) + _dot(lo, bd)
    return x * lax.rsqrt(ss * (1.0 / HEAD_DIM) + EPS) * g


def _prep_kernel(x_ref, mod_ref, g1_ref, win_ref, cqg_ref, wuq_ref, ckvg_ref, wuk_ref, wuv_ref,
                 qg_ref, kg_ref, rc_ref, rs1_ref, rs2_ref,
                 fqg_ref, fkg_ref, fb_ref, mqg_ref, mkg_ref, dqg_ref, dkg_ref,
                 bd_ref, tri_ref, eq_ref, ek_ref, sel_ref,
                 qa_ref, ka_ref, va_ref, qf_ref, kf_ref, vf_ref, qm_ref, km_ref, vm_ref,
                 qd_ref, kd_ref, vd_ref,
                 fcarry, kmean):
    i = pl.program_id(0)
    tm = x_ref.shape[0]

    @pl.when(i == 0)
    def _():
        fcarry[...] = jnp.zeros_like(fcarry)
        kmean[...] = jnp.zeros_like(kmean)

    x = x_ref[...]
    y = x * lax.rsqrt(jnp.mean(x * x, axis=-1, keepdims=True) + EPS) * g1_ref[...]
    hb = (y * (1.0 + mod_ref[1]) + mod_ref[0]).astype(BF16)
    bd = bd_ref[...]
    lane = lax.broadcasted_iota(jnp.int32, (1, LANES), 1)
    lane_f = lane.astype(F32)
    head_of_lane = _head_of_lane()

    pa = _dot(hb, win_ref[:, 0:COLS_MLA])
    cq = pa[:, 0:MLA_Q_RANK]
    ckv = pa[:, MLA_Q_RANK:MLA_Q_RANK + MLA_KV_RANK]
    kr = pa[:, MLA_Q_RANK + MLA_KV_RANK:COLS_MLA]
    cqn = (cq * lax.rsqrt(jnp.mean(cq * cq, axis=-1, keepdims=True) + EPS) * cqg_ref[...]).astype(BF16)
    ckvn = (ckv * lax.rsqrt(jnp.mean(ckv * ckv, axis=-1, keepdims=True) + EPS) * ckvg_ref[...]).astype(BF16)
    q_all = _dot(cqn, wuq_ref[...])
    k_all = _dot(ckvn, wuk_ref[...])
    va_ref[...] = _dot(ckvn, wuv_ref[...]).astype(BF16)
    rc, rs1, rs2 = rc_ref[...], rs1_ref[...], rs2_ref[...]

    def rope(t):
        return t * rc + pltpu.roll(t, LANES - MLA_ROPE // 2, 1) * rs1 + pltpu.roll(t, MLA_ROPE // 2, 1) * rs2

    for h in range(HEADS):
        q = q_all[:, h * LANES:(h + 1) * LANES]
        q = q * lax.rsqrt(jnp.sum(q * q, axis=-1, keepdims=True) * (1.0 / MLA_QK) + EPS) * qg_ref[...]
        qa_ref[h] = (rope(q) * (MLA_QK ** -0.5)).astype(BF16)
        k = k_all[:, h * LANES:(h + 1) * LANES] + kr
        k = k * lax.rsqrt(jnp.sum(k * k, axis=-1, keepdims=True) * (1.0 / MLA_QK) + EPS) * kg_ref[...]
        ka_ref[h] = rope(k).astype(BF16)

    pf = _dot(hb, win_ref[:, COLS_MLA:COLS_MLA + COLS_FOX])
    fqn = (_head_norm(pf[:, 0:GROUP], fqg_ref[...], bd) * (HEAD_DIM ** -0.5)).astype(BF16)
    fkn = _head_norm(pf[:, GROUP:2 * GROUP], fkg_ref[...], bd).astype(BF16)
    vf_ref[...] = pf[:, 2 * GROUP:3 * GROUP].astype(BF16)
    z = pf[:, 3 * GROUP:3 * GROUP + LANES] + fb_ref[...]
    log_f = jnp.minimum(z, 0.0) - jnp.log(1.0 + jnp.exp(-jnp.abs(z)))
    tri = tri_ref[...]
    a1, a2, a3 = _split3(log_f)
    cum = fcarry[...] + (_dot(tri, a1) + _dot(tri, a2) + _dot(tri, a3))
    fcarry[...] = cum[tm - 1:tm, :]
    f1, f2, f3 = _split3(cum)
    xq = jnp.concatenate([fqn, f1, f2, f3], axis=1)
    xk = jnp.concatenate([fkn, f1, f2, f3], axis=1)
    ones_q = jnp.where((lane >= HEAD_DIM + 3) & (lane < HEAD_DIM + 6), 1.0, 0.0)
    ones_k = jnp.where((lane >= HEAD_DIM) & (lane < HEAD_DIM + 3), 1.0, 0.0)
    for h in range(HEADS):
        qf_ref[h] = (_dot(xq, eq_ref[h]) + ones_q).astype(BF16)
        kf_ref[h] = (_dot(xk, ek_ref[h]) + ones_k).astype(BF16)

    pm = _dot(hb, win_ref[:, COLS_MLA + COLS_FOX:COLS_MLA + COLS_FOX + COLS_MOBA])
    mqn = _head_norm(pm[:, 0:GROUP], mqg_ref[...], bd) * (HEAD_DIM ** -0.5)
    mkn = _head_norm(pm[:, GROUP:2 * GROUP], mkg_ref[...], bd)
    vm_ref[...] = pm[:, 2 * GROUP:3 * GROUP].astype(BF16)
    col_mean = jnp.mean(mkn, axis=0, keepdims=True)
    mqb = mqn.astype(BF16)
    mkb = mkn.astype(BF16)
    blk = lane - HEAD_DIM
    blk_f = blk.astype(F32)
    past = (blk >= 0) & (blk < i)
    i_f = i.astype(F32)
    for h in range(HEADS):
        kmean[h, pl.ds(HEAD_DIM + i, 1), :] = jnp.where(head_of_lane == h, col_mean, 0.0)
        qh_hi, qh_lo = _split2(jnp.where(head_of_lane == h, mqn, 0.0))
        km_hi, km_lo = _split2(kmean[h])
        gate = _dot_nt(qh_hi, km_hi) + _dot_nt(qh_hi, km_lo) + _dot_nt(qh_lo, km_hi)
        g = jnp.where(past, gate, NEG)
        chosen = jnp.zeros((tm, LANES), F32)
        for _ in range(MOBA_TOPK):
            m = jnp.max(g, axis=-1, keepdims=True)
            first = jnp.min(jnp.where(g == m, lane_f, 1e9), axis=-1, keepdims=True)
            pick = (lane_f == first) & (m > NEG)
            chosen = jnp.where(pick, 1.0, chosen)
            g = jnp.where(pick, NEG, g)
        slope = 2.0 ** (-(2 * h + 2))
        keep = (chosen > 0.0) | (blk == i)
        bias = jnp.where(keep, (slope * MOBA_BLOCK) * (blk_f - i_f), NEG)
        bias = jnp.where(blk >= 0, bias, 0.0)
        qm_ref[h] = (_dot(mqb, sel_ref[h]) + bias).astype(BF16)
        onehot = jnp.where(blk == i, 1.0, 0.0)
        km_ref[h] = (_dot(mkb, sel_ref[h]) + onehot).astype(BF16)

    pd = _dot(hb, win_ref[:, COLS_MLA + COLS_FOX + COLS_MOBA:COLS_IN])
    qd_ref[...] = _head_norm(pd[:, 0:GROUP], dqg_ref[...], bd) * (HEAD_DIM ** -0.5)
    kd_ref[...] = _head_norm(pd[:, GROUP:2 * GROUP], dkg_ref[...], bd)
    vd_ref[...] = pd[:, 2 * GROUP:3 * GROUP]


def _prep_constants(seq, tm):
    half = MLA_ROPE // 2
    inv = 1.0 / (ROPE_THETA ** (jnp.arange(half, dtype=F32) / half))
    ang = jnp.arange(seq, dtype=F32)[:, None] * inv[None, :]
    cos, sin = jnp.cos(ang), jnp.sin(ang)
    z = lambda n: jnp.zeros((seq, n), F32)
    rc = jnp.concatenate([jnp.ones((seq, MLA_NOPE), F32), cos, cos, z(LANES - MLA_QK)], axis=1)
    rs1 = jnp.concatenate([z(MLA_NOPE), -sin, z(LANES - MLA_NOPE - half)], axis=1)
    rs2 = jnp.concatenate([z(MLA_NOPE + half), sin, z(LANES - MLA_QK)], axis=1)
    bd =np.kron(np.eye(HEADS, dtype=np.float32), np.ones((HEAD_DIM, HEAD_DIM), np.float32))
    tri = np.tril(np.ones((tm, tm), np.float32))
    tri_strict = np.tril(np.ones((tm, tm), np.float32), -1)
    sel = np.zeros((HEADS, GROUP, LANES), np.float32)
    eq = np.zeros((HEADS, GROUP + 3 * LANES, LANES), np.float32)
    ek = np.zeros((HEADS, GROUP + 3 * LANES, LANES), np.float32)
    for h in range(HEADS):
        for d in range(HEAD_DIM):
            sel[h, h * HEAD_DIM + d, d] = 1.0
        eq[h, :GROUP] = sel[h]
        ek[h, :GROUP] = sel[h]
        for piece in range(3):
            eq[h, GROUP + piece * LANES + h, HEAD_DIM + piece] = 1.0
            ek[h, GROUP + piece * LANES + h, HEAD_DIM + 3 + piece] = -1.0
    as_bf = lambda a: jnp.asarray(a, BF16)
    return dict(rc=rc, rs1=rs1, rs2=rs2, bd=as_bf(bd),
                tri=as_bf(tri), tri_strict=as_bf(tri_strict), sel=as_bf(sel), eq=as_bf(eq), ek=as_bf(ek))


def _prep(x2, mod_l, consts, p):
    seq = x2.shape[0]
    tm = TOKEN_TILE
    row = lambda n: pl.BlockSpec((tm, n), lambda i: (i, 0))
    heads = pl.BlockSpec((HEADS, tm, LANES), lambda i: (0, i, 0))
    in_arrays = [
        (x2, row(D_MODEL)), (mod_l, _full_spec(mod_l.shape)), (p["g1"], None), (p["w_in"], None),
        (p["cq_g"], None), (p["w_uq"], None), (p["ckv_g"], None), (p["w_uk"], None), (p["w_uv"], None),
        (p["q_g"], None), (p["k_g"], None),
        (consts["rc"], row(LANES)), (consts["rs1"], row(LANES)), (consts["rs2"], row(LANES)),
        (p["fq_g"], None), (p["fk_g"], None), (p["f_b"], None), (p["mq_g"], None), (p["mk_g"], None),
        (p["dq_g"], None), (p["dk_g"], None),
        (consts["bd"], None), (consts["tri"], None), (consts["eq"], None), (consts["ek"], None),
        (consts["sel"], None),
    ]
    args = [a for a, _ in in_arrays]
    specs = [s if s is not None else _full_spec(a.shape) for a, s in in_arrays]
    hshape = jax.ShapeDtypeStruct((HEADS, seq, LANES), BF16)
    vshape = jax.ShapeDtypeStruct((seq, GROUP), BF16)
    dshape = jax.ShapeDtypeStruct((seq, GROUP), F32)
    return pl.pallas_call(
        _prep_kernel,
        grid=(seq // tm,),
        in_specs=specs,
        out_specs=[heads, heads, row(GROUP)] * 3 + [row(GROUP)] * 3,
        out_shape=[hshape, hshape, vshape] * 3 + [dshape] * 3,
        scratch_shapes=[pltpu.VMEM((1, LANES), F32), pltpu.VMEM((HEADS, LANES, GROUP), F32)],
        compiler_params=_params(("arbitrary",)),
        name="prep",
    )(*args)


def _flash_kernel(q_ref, k_ref, v_ref, arow_ref, o_ref, m_sc, l_sc, acc_sc, *, tile, use_arow):
    i = pl.program_id(1)
    m_sc[...] = jnp.full_like(m_sc, -jnp.inf)
    l_sc[...] = jnp.zeros_like(l_sc)
    acc_sc[...] = jnp.zeros_like(acc_sc)

    def step(j, causal):
        start = pl.multiple_of(j * tile, tile)
        v = v_ref[pl.ds(start, tile), :]
        for hh in range(2):
            k = k_ref[hh, pl.ds(start, tile), :]
            s = _dot_nt(q_ref[hh], k)
            if use_arow:
                s = s + arow_ref[hh]
            if causal:
                row = lax.broadcasted_iota(jnp.int32, (tile, tile), 0)
                col = lax.broadcasted_iota(jnp.int32, (tile, tile), 1)
                s = jnp.where(col <= row, s, NEG)
            m_prev = m_sc[hh]
            m_new = jnp.maximum(m_prev, jnp.max(s, axis=-1, keepdims=True))
            alpha = jnp.exp(m_prev - m_new)
            p = jnp.exp(s - jnp.concatenate([m_new] * (tile // LANES), axis=1))
            l_sc[hh] = alpha * l_sc[hh] + jnp.sum(p, axis=-1, keepdims=True)
            acc_sc[hh] = alpha * acc_sc[hh] + _dot(p.astype(BF16), v)
            m_sc[hh] = m_new

    def body(j, carry):
        step(j, False)
        return carry

    lax.fori_loop(0, i, body, 0)
    step(i, True)
    lane = lax.broadcasted_iota(jnp.int32, (tile, LANES), 1)
    o0 = acc_sc[0] / l_sc[0]
    o1 = acc_sc[1] / l_sc[1]
    o_ref[...] = jnp.where(lane < HEAD_DIM, o0, o1).astype(o_ref.dtype)


def _flash(q, k, v, arow, use_arow):
    seq = v.shape[0]
    tile = FLASH_TILE
    kern = functools.partial(_flash_kernel, tile=tile, use_arow=use_arow)
    return pl.pallas_call(
        kern,
        grid=(HEADS // 2, seq // tile),
        in_specs=[
            pl.BlockSpec((2, tile, LANES), lambda p, i: (p, i, 0)),
            pl.BlockSpec((2, seq, LANES), lambda p, i: (p, 0, 0)),
            pl.BlockSpec((seq, LANES), lambda p, i: (0, p)),
            pl.BlockSpec((2, 1, tile), lambda p, i: (p, 0, 0)),
        ],
        out_specs=pl.BlockSpec((tile, LANES), lambda p, i: (i, p)),
        out_shape=jax.ShapeDtypeStruct((seq, GROUP), BF16),
        scratch_shapes=[pltpu.VMEM((2, tile, LANES), F32)] * 3,
        compiler_params=_params(("arbitrary", "arbitrary")),
        name="flash_arow" if use_arow else "flash",
    )(q, k, v, arow)


def _dilated_kernel(q_ref, k_ref, v_ref, o_ref, kbuf, vbuf, acc_s, m_s, l_s):
    pair = pl.program_id(0)
    i = pl.program_id(1)
    T = q_ref.shape[0]

    @pl.when(i == 0)
    def _():
        kbuf[...] = jnp.zeros_like(kbuf)
        vbuf[...] = jnp.zeros_like(vbuf)

    kbuf[0:T, :] = kbuf[T:2 * T, :]
    vbuf[0:T, :] = vbuf[T:2 * T, :]
    kbuf[T:2 * T, :] = k_ref[...]
    vbuf[T:2 * T, :] = v_ref[...]

    ii = lax.broadcasted_iota(jnp.int32, (DIL_SPAN, 2 * DIL_SPAN), 0)
    jj = lax.broadcasted_iota(jnp.int32, (DIL_SPAN, 2 * DIL_SPAN), 1)
    dist = ii + DIL_SPAN - jj
    band = (dist >= 0) & (dist <= DIL_SPAN)
    dist_f = dist.astype(F32)
    upper = lax.broadcasted_iota(jnp.int32, (1, LANES), 1) >= HEAD_DIM

    for pi, (window, r) in enumerate(DIL_PATTERNS):
        assert window // r == DIL_SPAN
        sub = DIL_SPAN * r

        def body(idx, carry, r=r, sub=sub, pi=pi):
            n = idx // r
            rho = idx - n * r
            base = n * sub + rho
            q = q_ref[pl.ds(base, DIL_SPAN, stride=r), :]
            kc = kbuf[pl.ds(T + base - sub, 2 * DIL_SPAN, stride=r), :].astype(BF16)
            vc = vbuf[pl.ds(T + base - sub, 2 * DIL_SPAN, stride=r), :].astype(BF16)
            first_key = jnp.where((i == 0) & (n == 0), DIL_SPAN, 0)
            valid = band & (jj >= first_key)
            stats = []
            for hh in range(2):
                slope = jnp.where(pair == 0, 2.0 ** (-(2 * hh + 1)), 2.0 ** (-(2 * hh + 5)))
                qh = (jnp.where(upper, q, 0.0) if hh else jnp.where(upper, 0.0, q)).astype(BF16)
                s = _dot_nt(qh, kc) - (slope * r) * dist_f
                s = jnp.where(valid, s, NEG)
                m = jnp.max(s, axis=-1, keepdims=True)
                p = jnp.exp(s - m)
                l = jnp.sum(p, axis=-1, keepdims=True)
                stats.append((_dot(p.astype(BF16), vc), m, l))
            rows = pl.ds(pi * T + base, DIL_SPAN, stride=r)
            acc_s[rows, :] = jnp.where(upper, stats[1][0], stats[0][0])
            m_s[rows, :] = jnp.where(upper, stats[1][1], stats[0][1])
            l_s[rows, :] = jnp.where(upper, stats[1][2], stats[0][2])
            return carry

        lax.fori_loop(0, T // DIL_SPAN, body, 0)

    npat = len(DIL_PATTERNS)
    ms = [m_s[pi * T:(pi + 1) * T, :] for pi in range(npat)]
    m_top = functools.reduce(jnp.maximum, ms)
    num = jnp.zeros((T, LANES), F32)
    den = jnp.zeros((T, LANES), F32)
    for pi in range(npat):
        w = jnp.exp(ms[pi] - m_top)
        num = num + w * acc_s[pi * T:(pi + 1) * T, :]
        den = den + w * l_s[pi * T:(pi + 1) * T, :]
    o_ref[...] = (num / den).astype(o_ref.dtype)


def _dilated(q, k, v):
    seq = q.shape[0]
    T = DIL_TILE
    spec = pl.BlockSpec((T, LANES), lambda p, i: (i, p))
    npat = len(DIL_PATTERNS)
    return pl.pallas_call(
        _dilated_kernel,
        grid=(HEADS // 2, seq // T),
        in_specs=[spec, spec, spec],
        out_specs=spec,
        out_shape=jax.ShapeDtypeStruct((seq, GROUP), BF16),
        scratch_shapes=[pltpu.VMEM((2 * T, LANES), F32), pltpu.VMEM((2 * T, LANES), F32),
                        pltpu.VMEM((npat * T, LANES), F32), pltpu.VMEM((npat * T, LANES), F32),
                        pltpu.VMEM((npat * T, LANES), F32)],
        compiler_params=_params(("arbitrary", "arbitrary")),
        name="dilated",
    )(q, k, v)


def _post_kernel(x_ref, oa_ref, ob_ref, oc_ref, od_ref, wout_ref, mod_ref, g2_ref,
                 rwh_ref, rwl_ref, rb_ref, tri_ref,
                 x1_ref, h2_ref, eidx_ref, gate_ref, rank_ref, cnt_ref, carry):
    i = pl.program_id(0)
    tm = x_ref.shape[0]

    @pl.when(i == 0)
    def _():
        carry[...] = jnp.zeros_like(carry)

    o = (_dot(oa_ref[...], wout_ref[0]) + _dot(ob_ref[...], wout_ref[1])
         + _dot(oc_ref[...], wout_ref[2]) + _dot(od_ref[...], wout_ref[3]))
    x1 = x_ref[...] + mod_ref[2] * o
    x1_ref[...] = x1
    y = x1 * lax.rsqrt(jnp.mean(x1 * x1, axis=-1, keepdims=True) + EPS) * g2_ref[...]
    h2 = y * (1.0 + mod_ref[4]) + mod_ref[3]
    _store_token_tiles(h2_ref, h2)

    h_hi, h_lo = _split2(h2)
    logits = (_dot(h_hi, rwh_ref[...]) + _dot(h_hi, rwl_ref[...]) + _dot(h_lo, rwh_ref[...])
              + rb_ref[...])
    lane = lax.broadcasted_iota(jnp.int32, (tm, LANES), 1)
    lane_f = lane.astype(F32)
    g = logits
    chosen = jnp.zeros((tm, LANES), F32)
    vals, idxs = [], []
    for _ in range(TOP_K):
        m = jnp.max(g, axis=-1, keepdims=True)
        first = jnp.min(jnp.where(g == m, lane_f, 1e9), axis=-1, keepdims=True)
        pick = lane_f == first
        chosen = jnp.where(pick, 1.0, chosen)
        g = jnp.where(pick, -jnp.inf, g)
        vals.append(m)
        idxs.append(first)
    exps = [jnp.exp(v - vals[0]) for v in vals]
    den = exps[0] + exps[1] + exps[2] + exps[3]
    before = _dot(tri_ref[...], chosen.astype(BF16)) + carry[...]
    carry[...] = carry[...] + jnp.sum(chosen, axis=0, keepdims=True)
    cnt_ref[...] = carry[...]
    e_out = jnp.zeros((tm, LANES), F32)
    g_out = jnp.zeros((tm, LANES), F32)
    r_out = jnp.zeros((tm, LANES), F32)
    for k in range(TOP_K):
        rank_k = jnp.sum(jnp.where(lane_f == idxs[k], before, 0.0), axis=-1, keepdims=True)
        e_out = jnp.where(lane == k, idxs[k], e_out)
        g_out = jnp.where(lane == k, exps[k] / den, g_out)
        r_out = jnp.where(lane == k, rank_k, r_out)
    eidx_ref[...] = e_out.astype(jnp.int32)
    gate_ref[...] = g_out
    rank_ref[...] = r_out.astype(jnp.int32)


def _post(x2, oa, ob, oc, od, mod_l, consts, p):
    seq = x2.shape[0]
    tm = TOKEN_TILE
    row = lambda n: pl.BlockSpec((tm, n), lambda i: (i, 0))
    full = [p["w_out"], mod_l, p["g2"], p["rw_hi"], p["rw_lo"], p["r_b"], consts["tri_strict"]]
    f32 = lambda n: jax.ShapeDtypeStruct((seq, n), F32)
    i32 = lambda n: jax.ShapeDtypeStruct((seq, n), jnp.int32)
    return pl.pallas_call(
        _post_kernel,
        grid=(seq // tm,),
        in_specs=[row(D_MODEL)] + [row(GROUP)] * 4 + [_full_spec(a.shape) for a in full],
        out_specs=[row(D_MODEL), pl.BlockSpec((tm * CHUNKS, LANES), lambda i: (i, 0)),
                   row(LANES), row(LANES), row(LANES), _full_spec((1, LANES))],
        out_shape=[f32(D_MODEL), jax.ShapeDtypeStruct((seq * CHUNKS, LANES), F32),
                   i32(LANES), f32(LANES), i32(LANES), jax.ShapeDtypeStruct((1, LANES), F32)],
        scratch_shapes=[pltpu.VMEM((1, LANES), F32)],
        compiler_params=_params(("arbitrary",)),
        name="post",
    )(x2, oa, ob, oc, od, *full)


def _tile_copy(src, s, dst, d, sem):
    return pltpu.make_async_copy(src.at[pl.ds(pl.multiple_of(s * CHUNKS, CHUNKS), CHUNKS), :],
                                 dst.at[pl.ds(pl.multiple_of(d * CHUNKS, CHUNKS), CHUNKS), :], sem)


def _dispatch_kernel(dest_ref, h_ref, xs_in_ref, xs_ref, sem):
    del xs_in_ref
    i = pl.program_id(0)
    n = TOKEN_TILE * TOP_K
    base = i * n

    def issue(a, carry):
        _tile_copy(h_ref, a // TOP_K, xs_ref, dest_ref[base + a], sem).start()
        return carry

    lax.fori_loop(0, n, issue, 0)

    def drain(a, carry):
        _tile_copy(h_ref, 0, xs_ref, 0, sem).wait()
        return carry

    lax.fori_loop(0, n, drain, 0)


def _dispatch(dest, h2_tiles, m_pad):
    seq = h2_tiles.shape[0] // CHUNKS
    zeros = jnp.zeros((m_pad * CHUNKS, LANES), F32)
    return pl.pallas_call(
        _dispatch_kernel,
        grid_spec=pltpu.PrefetchScalarGridSpec(
            num_scalar_prefetch=1,
            grid=(seq // TOKEN_TILE,),
            in_specs=[pl.BlockSpec((TOKEN_TILE * CHUNKS, LANES), lambda i, d: (i, 0)),
                      pl.BlockSpec(memory_space=pl.ANY)],
            out_specs=pl.BlockSpec(memory_space=pl.ANY),
            scratch_shapes=[pltpu.SemaphoreType.DMA(())],
        ),
        out_shape=jax.ShapeDtypeStruct((m_pad * CHUNKS, LANES), F32),
        input_output_aliases={2: 0},
        compiler_params=_params(("arbitrary",)),
        name="dispatch",
    )(dest, h2_tiles, zeros)


def _expert_kernel(be_ref, nu_ref, xs_ref, w1_ref, b1_ref, w2_ref, b2_ref, ys_ref, w1b, w2b):
    b = pl.program_id(0)
    prev = be_ref[jnp.maximum(b - 1, 0)]
    fresh = (b == 0) | (be_ref[b] != prev)

    @pl.when(fresh)
    def _():
        w1b[...] = w1_ref[0].astype(BF16)
        w2b[...] = w2_ref[0].astype(BF16)

    @pl.when(b < nu_ref[0])
    def _():
        bm = EXPERT_ROWS
        xb = jnp.concatenate([_load_token_chunk(xs_ref, bm, c) for c in range(CHUNKS)],
                             axis=1).astype(BF16)
        gu = _dot(xb, w1b[...]) + b1_ref[0]
        g = jnp.minimum(gu[:, :D_EXPERT], SWIGLU_LIMIT)
        u = jnp.clip(gu[:, D_EXPERT:], -SWIGLU_LIMIT, SWIGLU_LIMIT)
        y = (u + 1.0) * g * (1.0 / (1.0 + jnp.exp(-SWIGLU_ALPHA * g)))
        _store_token_tiles(ys_ref, _dot(y.astype(BF16), w2b[...]) + b2_ref[0])

    @pl.when(b >= nu_ref[0])
    def _():
        ys_ref[...] = jnp.zeros_like(ys_ref)


def _experts(blk_expert, n_used, xs, w1, b1, w2, b2):
    m_pad = xs.shape[0] // CHUNKS
    bm = EXPERT_ROWS
    n_all = w1.shape[0] * w1.shape[1]
    rows = lambda b, be, nu: (jnp.minimum(b, nu[0] - 1), 0)
    ex = lambda b, be, nu: (be[jnp.minimum(b, nu[0] - 1)], 0, 0)
    return pl.pallas_call(
        _expert_kernel,
        grid_spec=pltpu.PrefetchScalarGridSpec(
            num_scalar_prefetch=2,
            grid=(m_pad // bm,),
            in_specs=[
                pl.BlockSpec((bm * CHUNKS, LANES), rows),
                pl.BlockSpec((1, D_MODEL, 2 * D_EXPERT), ex),
                pl.BlockSpec((1, 1, 2 * D_EXPERT), ex),
                pl.BlockSpec((1, D_EXPERT, D_MODEL), ex),
                pl.BlockSpec((1, 1, D_MODEL), ex),
            ],
            out_specs=pl.BlockSpec((bm * CHUNKS, LANES), lambda b, be, nu: (b, 0)),
            scratch_shapes=[pltpu.VMEM((D_MODEL, 2 * D_EXPERT), BF16),
                            pltpu.VMEM((D_EXPERT, D_MODEL), BF16)],
        ),
        out_shape=jax.ShapeDtypeStruct((m_pad * CHUNKS, LANES), F32),
        compiler_params=_params(("arbitrary",)),
        name="experts",
    )(blk_expert, n_used, xs, w1.reshape(n_all, D_MODEL, 2 * D_EXPERT), b1.reshape(n_all, 1, -1),
      w2.reshape(n_all, D_EXPERT, D_MODEL), b2.reshape(n_all, 1, -1))


def _combine_kernel(dest_ref, ys_ref, x1_ref, gate_ref, mod_ref, o_ref, buf, sem):
    i = pl.program_id(0)
    tm = x1_ref.shape[0]
    n = tm * TOP_K
    base = i * n

    def issue(a, carry):
        r = a // TOP_K
        k = a - r * TOP_K
        _tile_copy(ys_ref, dest_ref[base + a], buf, k * tm + r, sem).start()
        return carry

    lax.fori_loop(0, n, issue, 0)

    def drain(a, carry):
        _tile_copy(ys_ref, 0, buf, 0, sem).wait()
        return carry

    lax.fori_loop(0, n, drain, 0)
    gates = gate_ref[...]
    g2 = mod_ref[5]
    for c in range(CHUNKS):
        cols = slice(c * LANES, (c + 1) * LANES)
        mix = jnp.zeros((tm, LANES), F32)
        for k in range(TOP_K):
            mix = mix + gates[:, k:k + 1] * _load_token_chunk(buf, tm, c, offset=k * tm * CHUNKS)
        o_ref[:, cols] = x1_ref[:, cols] + g2[:, cols] * mix


def _combine(dest, ys, x1, gates, mod_l):
    seq = x1.shape[0]
    tm = TOKEN_TILE
    return pl.pallas_call(
        _combine_kernel,
        grid_spec=pltpu.PrefetchScalarGridSpec(
            num_scalar_prefetch=1,
            grid=(seq // tm,),
            in_specs=[
                pl.BlockSpec(memory_space=pl.ANY),
                pl.BlockSpec((tm, D_MODEL), lambda i, d: (i, 0)),
                pl.BlockSpec((tm, LANES), lambda i, d: (i, 0)),
                pl.BlockSpec(mod_l.shape, lambda i, d: (0, 0, 0)),
            ],
            out_specs=pl.BlockSpec((tm, D_MODEL), lambda i, d: (i, 0)),
            scratch_shapes=[pltpu.VMEM((TOP_K * tm * CHUNKS, LANES), F32),
                            pltpu.SemaphoreType.DMA(())],
        ),
        out_shape=jax.ShapeDtypeStruct((seq, D_MODEL), F32),
        compiler_params=_params(("arbitrary",)),
        name="combine",
    )(dest, ys, x1, gates, mod_l)


def _moe(l, x1, h2_tiles, eidx, gates, rank, counts, mod_l, w1, b1, w2, b2):
    seq = x1.shape[0]
    bm = EXPERT_ROWS
    m_pad = seq * TOP_K + N_EXPERTS * bm
    cnt = counts[0, :N_EXPERTS].astype(jnp.int32)
    padded = (cnt + bm - 1) // bm * bm
    pad_end = jnp.cumsum(padded)
    pad_start = pad_end - padded
    onehot = eidx[:, :TOP_K, None] == jnp.arange(N_EXPERTS, dtype=jnp.int32)
    start_of = jnp.sum(jnp.where(onehot, pad_start, 0), axis=-1)
    dest = (start_of + rank[:, :TOP_K]).reshape(seq * TOP_K).astype(jnp.int32)
    nblk = m_pad // bm
    blk_start = jnp.arange(nblk, dtype=jnp.int32) * bm
    blk_expert = jnp.minimum(jnp.sum(pad_end[None, :] <= blk_start[:, None], axis=1), N_EXPERTS - 1)
    blk_expert = (blk_expert + l * N_EXPERTS).astype(jnp.int32)
    n_used = (pad_end[-1:] // bm).astype(jnp.int32)
    xs = _dispatch(dest, h2_tiles, m_pad)
    ys = _experts(blk_expert, n_used, xs, w1, b1, w2, b2)
    return _combine(dest, ys, x1, gates, mod_l)


def _pad_cols(a, n):
    return jnp.pad(a, ((0, 0), (0, n - a.shape[1])))


def _layer_params(l, w_in, mla_cq_g, mla_w_uq, mla_ckv_g, mla_w_ukv, mla_q_g, mla_k_g,
                  fox_q_g, fox_k_g, fox_b_f, moba_q_g, moba_k_g, dil_q_g, dil_k_g, w_out,
                  norm1_g, norm2_g, router_w, router_b):
    w = w_in[l]
    sizes = [MLA_Q_RANK, MLA_KV_RANK, MLA_ROPE, GROUP, GROUP, GROUP, HEADS] + [GROUP] * 6
    offs = np.concatenate([[0], np.cumsum(sizes)])
    part = [w[:, offs[j]:offs[j + 1]] for j in range(len(sizes))]
    zeros = lambda n: jnp.zeros((D_MODEL, n), F32)
    w_in_r = jnp.concatenate(
        [part[0], part[1], zeros(MLA_NOPE), part[2], zeros(LANES - MLA_QK),
         part[3], part[4], part[5], part[6], zeros(LANES - HEADS)] + part[7:13], axis=1).astype(BF16)
    assert w_in_r.shape[1] == COLS_IN
    w_uq = jnp.pad(mla_w_uq[l].reshape(MLA_Q_RANK, HEADS, MLA_QK),
                   ((0, 0), (0, 0), (0, LANES - MLA_QK))).reshape(MLA_Q_RANK, HEADS * LANES)
    w_ukv = mla_w_ukv[l].reshape(MLA_KV_RANK, HEADS, MLA_NOPE + HEAD_DIM)
    w_uk = jnp.pad(w_ukv[:, :, :MLA_NOPE], ((0, 0), (0, 0), (0, LANES - MLA_NOPE)))
    w_uv = w_ukv[:, :, MLA_NOPE:]
    tile4 = lambda g: jnp.tile(g, HEADS)[None, :]
    rw = _pad_cols(router_w[l], LANES)
    rw_hi = rw.astype(BF16)
    rw_lo = (rw - rw_hi.astype(F32)).astype(BF16)
    r_b = jnp.concatenate([router_b[l], jnp.full((LANES - N_EXPERTS,), NEG, F32)])[None, :]
    return dict(
        g1=norm1_g[l][None, :], g2=norm2_g[l][None, :], w_in=w_in_r,
        cq_g=mla_cq_g[l][None, :], w_uq=w_uq.astype(BF16), ckv_g=mla_ckv_g[l][None, :],
        w_uk=w_uk.reshape(MLA_KV_RANK, HEADS * LANES).astype(BF16),
        w_uv=w_uv.reshape(MLA_KV_RANK, GROUP).astype(BF16),
        q_g=_pad_cols(mla_q_g[l][None, :], LANES), k_g=_pad_cols(mla_k_g[l][None, :], LANES),
        fq_g=tile4(fox_q_g[l]), fk_g=tile4(fox_k_g[l]), f_b=_pad_cols(fox_b_f[l][None, :], LANES),
        mq_g=tile4(moba_q_g[l]), mk_g=tile4(moba_k_g[l]), dq_g=tile4(dil_q_g[l]), dk_g=tile4(dil_k_g[l]),
        w_out=w_out[l].reshape(HEADS, GROUP, D_MODEL).astype(BF16),
        rw_hi=rw_hi, rw_lo=rw_lo, r_b=r_b,
    )


def kernel(x, c, w_mod, b_mod, norm1_g, norm2_g, w_in, mla_cq_g, mla_w_uq, mla_ckv_g, mla_w_ukv, mla_q_g, mla_k_g, fox_q_g, fox_k_g, fox_b_f, moba_q_g, moba_k_g, dil_q_g, dil_k_g, w_out, router_w, router_b, exp_w1, exp_b1, exp_w2, exp_b2):
    batch, seq, d = x.shape
    assert batch == 1 and d == D_MODEL
    assert seq % DIL_TILE == 0 and seq // MOBA_BLOCK <= MOBA_MAX_BLOCKS
    depth = w_mod.shape[0]
    consts = _prep_constants(seq, TOKEN_TILE)
    mod = _modulation(c, w_mod, b_mod)
    slopes_c = 2.0 ** (-(2.0 * np.arange(HEADS) + 2.0))
    in_block = np.arange(FLASH_TILE) % MOBA_BLOCK
    arow_c = jnp.asarray(slopes_c[:, None, None] * in_block[None, None, :], F32)
    arow_0 = jnp.zeros((HEADS, 1, FLASH_TILE), F32)
    x2 = x.reshape(seq, d)
    for l in range(depth):
        p = _layer_params(l, w_in, mla_cq_g, mla_w_uq, mla_ckv_g, mla_w_ukv, mla_q_g, mla_k_g,
                          fox_q_g, fox_k_g, fox_b_f, moba_q_g, moba_k_g, dil_q_g, dil_k_g, w_out,
                          norm1_g, norm2_g, router_w, router_b)
        mod_l = mod[l]
        qa, ka, va, qf, kf, vf, qm, km, vm, qd, kd, vd = _prep(x2, mod_l, consts, p)
        oa = _flash(qa, ka, va, arow_0, False)
        ob = _flash(qf, kf, vf, arow_0, False)
        oc = _flash(qm, km, vm, arow_c, True)
        od = _dilated(qd, kd, vd)
        x1, h2, eidx, gates, rank, counts = _post(x2, oa, ob, oc, od, mod_l, consts, p)
        x2 = _moe(l, x1, h2, eidx, gates, rank, counts, mod_l, exp_w1, exp_b1, exp_w2, exp_b2)
    return x2.reshape(batch, seq, d)
```

```python
import functools

import numpy as np
import jax
import jax.numpy as jnp
from jax import lax
from jax.experimental import pallas as pl
from jax.experimental.pallas import tpu as pltpu

F32 = jnp.float32
BF16 = jnp.bfloat16

D_MODEL = 1024
HEAD_DIM = 64
HEADS = 4
GROUP = HEADS * HEAD_DIM
LANES = 128
CHUNKS = D_MODEL // LANES
MLA_Q_RANK = 256
MLA_KV_RANK = 128
MLA_NOPE = 64
MLA_ROPE = 32
MLA_QK = MLA_NOPE + MLA_ROPE
ROPE_THETA = 10000.0
MOBA_BLOCK = 256
MOBA_TOPK = 3
MOBA_MAX_BLOCKS = 64
DIL_PATTERNS = ((128, 1), (512, 4), (2048, 16))
DIL_SPAN = 128
DIL_TILE = 2048
N_EXPERTS = 32
TOP_K = 4
D_EXPERT = 1024
SWIGLU_LIMIT = 7.0
SWIGLU_ALPHA = 1.702
EPS = 1e-6
NEG = -1e30

FLASH_TILE = 512
FOX_SKIP_LOG = -106.0
FOX_NORM_SLACK = 1.02
TOKEN_TILE = 256
EXPERT_ROWS = 256
VMEM_LIMIT = 56 * 1024 * 1024

COLS_MLA = 512
COLS_FOX = 640
COLS_MOBA = 512
COLS_DIL = 768
COLS_IN = COLS_MLA + COLS_FOX + COLS_MOBA + COLS_DIL


def _dot(a, b):
    return jnp.dot(a, b, preferred_element_type=F32)


def _dot_nt(a, b):
    return lax.dot_general(a, b, (((1,), (1,)), ((), ())), preferred_element_type=F32)


def _split2(x):
    hi = x.astype(BF16)
    lo = (x - hi.astype(F32)).astype(BF16)
    return hi, lo


def _split3(x):
    a = x.astype(BF16)
    r = x - a.astype(F32)
    b = r.astype(BF16)
    c = (r - b.astype(F32)).astype(BF16)
    return a, b, c


def _head_of_lane():
    return jnp.right_shift(lax.broadcasted_iota(jnp.int32, (1, GROUP), 1), 6)


def _store_token_tiles(ref, x, offset=0):
    n = x.shape[0]
    for c in range(CHUNKS):
        ref[pl.ds(offset + c, n, stride=CHUNKS), :] = x[:, c * LANES:(c + 1) * LANES]


def _load_token_chunk(ref, n, c, offset=0):
    return ref[pl.ds(offset + c, n, stride=CHUNKS), :]


def _full_spec(shape):
    nd = len(shape)
    return pl.BlockSpec(shape, lambda *_: (0,) * nd)


def _params(sem):
    return pltpu.CompilerParams(dimension_semantics=sem, vmem_limit_bytes=VMEM_LIMIT)


def _mod_kernel(c_ref, w_ref, b_ref, o_ref):
    c = c_ref[...]
    s = c * (1.0 / (1.0 + jnp.exp(-c)))
    s8 = jnp.broadcast_to(s, (8, D_MODEL))
    r = jnp.dot(s8, w_ref[0], preferred_element_type=F32, precision=lax.Precision.HIGHEST)
    o_ref[0, 0] = r[0:1, :] + b_ref[0, 0]


def _modulation(c, w_mod, b_mod):
    depth = w_mod.shape[0]
    b4 = b_mod.reshape(depth, 6, 1, D_MODEL)
    return pl.pallas_call(
        _mod_kernel,
        grid=(depth, 6),
        in_specs=[
            pl.BlockSpec((1, D_MODEL), lambda l, j: (0, 0)),
            pl.BlockSpec((1, D_MODEL, D_MODEL), lambda l, j: (l, 0, j)),
            pl.BlockSpec((1, 1, 1, D_MODEL), lambda l, j: (l, j, 0, 0)),
        ],
        out_specs=pl.BlockSpec((1, 1, 1, D_MODEL), lambda l, j: (l, j, 0, 0)),
        out_shape=jax.ShapeDtypeStruct((depth, 6, 1, D_MODEL), F32),
        compiler_params=_params(("arbitrary", "arbitrary")),
        name="modulation",
    )(c, w_mod, b4)


def _head_norm(x, g, bd):
    hi, lo = _split2(x * x)
    ss = _dot(hi, bd) + _dot(lo, bd)
    return x * lax.rsqrt(ss * (1.0 / HEAD_DIM) + EPS) * g


def _prep_kernel(x_ref, mod_ref, g1_ref, win_ref, wvt_ref, cqg_ref, wuq_ref, ckvg_ref, wuk_ref,
                 wuvt_ref, qg_ref, kg_ref, rc_ref, rs1_ref, rs2_ref,
                 fqg_ref, fkg_ref, fb_ref, mqg_ref, mkg_ref, dqg_ref, dkg_ref,
                 bd_ref, tri_ref, eq_ref, ek_ref, sel_ref,
                 qa_ref, ka_ref, va_ref, qf_ref, kf_ref, vf_ref, qm_ref, km_ref, vm_ref,
                 qd_ref, kd_ref, vd_ref, f_ref,
                 fcarry, kmean):
    i = pl.program_id(0)
    tm = x_ref.shape[0]

    @pl.when(i == 0)
    def _():
        fcarry[...] = jnp.zeros_like(fcarry)
        kmean[...] = jnp.zeros_like(kmean)

    x = x_ref[...]
    y = x * lax.rsqrt(jnp.mean(x * x, axis=-1, keepdims=True) + EPS) * g1_ref[...]
    hb = (y * (1.0 + mod_ref[1]) + mod_ref[0]).astype(BF16)
    bd = bd_ref[...]
    lane = lax.broadcasted_iota(jnp.int32, (1, LANES), 1)
    lane_f = lane.astype(F32)
    head_of_lane = _head_of_lane()

    pa = _dot(hb, win_ref[:, 0:COLS_MLA])
    cq = pa[:, 0:MLA_Q_RANK]
    ckv = pa[:, MLA_Q_RANK:MLA_Q_RANK + MLA_KV_RANK]
    kr = pa[:, MLA_Q_RANK + MLA_KV_RANK:COLS_MLA]
    cqn = (cq * lax.rsqrt(jnp.mean(cq * cq, axis=-1, keepdims=True) + EPS) * cqg_ref[...]).astype(BF16)
    ckvn = (ckv * lax.rsqrt(jnp.mean(ckv * ckv, axis=-1, keepdims=True) + EPS) * ckvg_ref[...]).astype(BF16)
    q_all = _dot(cqn, wuq_ref[...])
    k_all = _dot(ckvn, wuk_ref[...])
    va_ref[...] = _dot_nt(wuvt_ref[...], ckvn).astype(BF16)
    rc, rs1, rs2 = rc_ref[...], rs1_ref[...], rs2_ref[...]

    def rope(t):
        return t * rc + pltpu.roll(t, LANES - MLA_ROPE // 2, 1) * rs1 + pltpu.roll(t, MLA_ROPE // 2, 1) * rs2

    for h in range(HEADS):
        q = q_all[:, h * LANES:(h + 1) * LANES]
        q = q * lax.rsqrt(jnp.sum(q * q, axis=-1, keepdims=True) * (1.0 / MLA_QK) + EPS) * qg_ref[...]
        qa_ref[h] = (rope(q) * (MLA_QK ** -0.5)).astype(BF16)
        k = k_all[:, h * LANES:(h + 1) * LANES] + kr
        k = k * lax.rsqrt(jnp.sum(k * k, axis=-1, keepdims=True) * (1.0 / MLA_QK) + EPS) * kg_ref[...]
        ka_ref[h] = rope(k).astype(BF16)

    pf = _dot(hb, win_ref[:, COLS_MLA:COLS_MLA + COLS_FOX])
    fqn = (_head_norm(pf[:, 0:GROUP], fqg_ref[...], bd) * (HEAD_DIM ** -0.5)).astype(BF16)
    fkn = _head_norm(pf[:, GROUP:2 * GROUP], fkg_ref[...], bd).astype(BF16)
    vf_ref[...] = _dot_nt(wvt_ref[0], hb).astype(BF16)
    z = pf[:, 2 * GROUP:2 * GROUP + LANES] + fb_ref[...]
    log_f = jnp.minimum(z, 0.0) - jnp.log(1.0 + jnp.exp(-jnp.abs(z)))
    tri = tri_ref[...]
    a1, a2, a3 = _split3(log_f)
    cum = fcarry[...] + (_dot(tri, a1) + _dot(tri, a2) + _dot(tri, a3))
    fcarry[...] = cum[tm - 1:tm, :]
    f_ref[...] = cum
    f1, f2, f3 = _split3(cum)
    xq = jnp.concatenate([fqn, f1, f2, f3], axis=1)
    xk = jnp.concatenate([fkn, f1, f2, f3], axis=1)
    ones_q = jnp.where((lane >= HEAD_DIM + 3) & (lane < HEAD_DIM + 6), 1.0, 0.0)
    ones_k = jnp.where((lane >= HEAD_DIM) & (lane < HEAD_DIM + 3), 1.0, 0.0)
    for h in range(HEADS):
        qf_ref[h] = (_dot(xq, eq_ref[h]) + ones_q).astype(BF16)
        kf_ref[h] = (_dot(xk, ek_ref[h]) + ones_k).astype(BF16)

    pm = _dot(hb, win_ref[:, COLS_MLA + COLS_FOX:COLS_MLA + COLS_FOX + COLS_MOBA])
    mqn = _head_norm(pm[:, 0:GROUP], mqg_ref[...], bd) * (HEAD_DIM ** -0.5)
    mkn = _head_norm(pm[:, GROUP:2 * GROUP], mkg_ref[...], bd)
    vm_ref[...] = _dot_nt(wvt_ref[1], hb).astype(BF16)
    col_mean = jnp.mean(mkn, axis=0, keepdims=True)
    mqb = mqn.astype(BF16)
    mkb = mkn.astype(BF16)
    blk = lane - HEAD_DIM
    blk_f = blk.astype(F32)
    past = (blk >= 0) & (blk < i)
    i_f = i.astype(F32)
    for h in range(HEADS):
        kmean[h, pl.ds(HEAD_DIM + i, 1), :] = jnp.where(head_of_lane == h, col_mean, 0.0)
        qh_hi, qh_lo = _split2(jnp.where(head_of_lane == h, mqn, 0.0))
        km_hi, km_lo = _split2(kmean[h])
        gate = _dot_nt(qh_hi, km_hi) + _dot_nt(qh_hi, km_lo) + _dot_nt(qh_lo, km_hi)
        g = jnp.where(past, gate, NEG)
        chosen = jnp.zeros((tm, LANES), F32)
        for _ in range(MOBA_TOPK):
            m = jnp.max(g, axis=-1, keepdims=True)
            first = jnp.min(jnp.where(g == m, lane_f, 1e9), axis=-1, keepdims=True)
            pick = (lane_f == first) & (m > NEG)
            chosen = jnp.where(pick, 1.0, chosen)
            g = jnp.where(pick, NEG, g)
        slope = 2.0 ** (-(2 * h + 2))
        keep = (chosen > 0.0) | (blk == i)
        bias = jnp.where(keep, (slope * MOBA_BLOCK) * (blk_f - i_f), NEG)
        bias = jnp.where(blk >= 0, bias, 0.0)
        qm_ref[h] = (_dot(mqb, sel_ref[h]) + bias).astype(BF16)
        onehot = jnp.where(blk == i, 1.0, 0.0)
        km_ref[h] = (_dot(mkb, sel_ref[h]) + onehot).astype(BF16)

    pd = _dot(hb, win_ref[:, COLS_MLA + COLS_FOX + COLS_MOBA:COLS_IN])
    qd_ref[...] = _head_norm(pd[:, 0:GROUP], dqg_ref[...], bd) * (HEAD_DIM ** -0.5)
    kd_ref[...] = _head_norm(pd[:, GROUP:2 * GROUP], dkg_ref[...], bd)
    vd_ref[...] = pd[:, 2 * GROUP:3 * GROUP]


def _prep_constants(seq, tm):
    half = MLA_ROPE // 2
    inv = 1.0 / (ROPE_THETA ** (jnp.arange(half, dtype=F32) / half))
    ang = jnp.arange(seq, dtype=F32)[:, None] * inv[None, :]
    cos, sin = jnp.cos(ang), jnp.sin(ang)
    z = lambda n: jnp.zeros((seq, n), F32)
    rc = jnp.concatenate([jnp.ones((seq, MLA_NOPE), F32), cos, cos, z(LANES - MLA_QK)], axis=1)
    rs1 = jnp.concatenate([z(MLA_NOPE), -sin, z(LANES - MLA_NOPE - half)], axis=1)
    rs2 = jnp.concatenate([z(MLA_NOPE + half), sin, z(LANES - MLA_QK)], axis=1)
    bd =np.kron(np.eye(HEADS, dtype=np.float32), np.ones((HEAD_DIM, HEAD_DIM), np.float32))
    tri = np.tril(np.ones((tm, tm), np.float32))
    tri_strict = np.tril(np.ones((tm, tm), np.float32), -1)
    sel = np.zeros((HEADS, GROUP, LANES), np.float32)
    eq = np.zeros((HEADS, GROUP + 3 * LANES, LANES), np.float32)
    ek = np.zeros((HEADS, GROUP + 3 * LANES, LANES), np.float32)
    for h in range(HEADS):
        for d in range(HEAD_DIM):
            sel[h, h * HEAD_DIM + d, d] = 1.0
        eq[h, :GROUP] = sel[h]
        ek[h, :GROUP] = sel[h]
        for piece in range(3):
            eq[h, GROUP + piece * LANES + h, HEAD_DIM + piece] = 1.0
            ek[h, GROUP + piece * LANES + h, HEAD_DIM + 3 + piece] = -1.0
    as_bf = lambda a: jnp.asarray(a, BF16)
    return dict(rc=rc, rs1=rs1, rs2=rs2, bd=as_bf(bd),
                tri=as_bf(tri), tri_strict=as_bf(tri_strict), sel=as_bf(sel), eq=as_bf(eq), ek=as_bf(ek))


def _prep(x2, mod_l, consts, p):
    seq = x2.shape[0]
    tm = TOKEN_TILE
    row = lambda n: pl.BlockSpec((tm, n), lambda i: (i, 0))
    heads = pl.BlockSpec((HEADS, tm, LANES), lambda i: (0, i, 0))
    in_arrays = [
        (x2, row(D_MODEL)), (mod_l, _full_spec(mod_l.shape)), (p["g1"], None), (p["w_in"], None),
        (p["w_vt"], None),
        (p["cq_g"], None), (p["w_uq"], None), (p["ckv_g"], None), (p["w_uk"], None), (p["w_uvt"], None),
        (p["q_g"], None), (p["k_g"], None),
        (consts["rc"], row(LANES)), (consts["rs1"], row(LANES)), (consts["rs2"], row(LANES)),
        (p["fq_g"], None), (p["fk_g"], None), (p["f_b"], None), (p["mq_g"], None), (p["mk_g"], None),
        (p["dq_g"], None), (p["dk_g"], None),
        (consts["bd"], None), (consts["tri"], None), (consts["eq"], None), (consts["ek"], None),
        (consts["sel"], None),
    ]
    args = [a for a, _ in in_arrays]
    specs = [s if s is not None else _full_spec(a.shape) for a, s in in_arrays]
    hshape = jax.ShapeDtypeStruct((HEADS, seq, LANES), BF16)
    vshape = jax.ShapeDtypeStruct((GROUP, seq), BF16)
    dshape = jax.ShapeDtypeStruct((seq, GROUP), F32)
    vt = pl.BlockSpec((GROUP, tm), lambda i: (0, i))
    return pl.pallas_call(
        _prep_kernel,
        grid=(seq // tm,),
        in_specs=specs,
        out_specs=[heads, heads, vt] * 3 + [row(GROUP)] * 3 + [row(LANES)],
        out_shape=[hshape, hshape, vshape] * 3 + [dshape] * 3
                  + [jax.ShapeDtypeStruct((seq, LANES), F32)],
        scratch_shapes=[pltpu.VMEM((1, LANES), F32), pltpu.VMEM((HEADS, LANES, GROUP), F32)],
        compiler_params=_params(("arbitrary",)),
        name="prep",
    )(*args)


def _flash_kernel(jlo_ref, q_ref, k_ref, vt_ref, kbias_ref, o_ref, m_sc, l_sc, acc_sc,
                  *, tile, use_kbias):
    pair = pl.program_id(0)
    i = pl.program_id(1)
    m_sc[...] = jnp.full_like(m_sc, -jnp.inf)
    l_sc[...] = jnp.zeros_like(l_sc)
    acc_sc[...] = jnp.zeros_like(acc_sc)

    def step(j, causal):
        start = pl.multiple_of(j * tile, tile)
        scores = [_dot_nt(k_ref[hh, pl.ds(start, tile), :], q_ref[hh]) for hh in range(2)]
        for hh in range(2):
            s = scores[hh]
            if use_kbias:
                s = s + jnp.concatenate([kbias_ref[hh]] * (tile // LANES), axis=1)
            if causal:
                key = lax.broadcasted_iota(jnp.int32, (tile, tile), 0)
                qry = lax.broadcasted_iota(jnp.int32, (tile, tile), 1)
                s = jnp.where(key <= qry, s, NEG)
            m_prev = m_sc[hh]
            m_new = jnp.maximum(m_prev, jnp.max(s, axis=0, keepdims=True))
            alpha = jnp.exp(m_prev - m_new)
            p = jnp.exp(s - m_new)
            l_sc[hh] = alpha * l_sc[hh] + jnp.sum(p, axis=0, keepdims=True)
            vt = vt_ref[hh * HEAD_DIM:(hh + 1) * HEAD_DIM, pl.ds(start, tile)]
            acc_sc[hh] = alpha * acc_sc[hh] + _dot(vt, p.astype(BF16))
            m_sc[hh] = m_new

    def body(j, carry):
        step(j, False)
        return carry

    lax.fori_loop(jlo_ref[pair * pl.num_programs(1) + i], i, body, 0)
    step(i, True)
    o_t = jnp.concatenate([acc_sc[0] / l_sc[0], acc_sc[1] / l_sc[1]], axis=0)
    o_ref[...] = o_t.T.astype(o_ref.dtype)


def _flash(q, k, v_t, kbias, first_tile, use_kbias):
    seq = v_t.shape[1]
    tile = FLASH_TILE
    kern = functools.partial(_flash_kernel, tile=tile, use_kbias=use_kbias)
    return pl.pallas_call(
        kern,
        grid_spec=pltpu.PrefetchScalarGridSpec(
            num_scalar_prefetch=1,
            grid=(HEADS // 2, seq // tile),
            in_specs=[
                pl.BlockSpec((2, tile, LANES), lambda p, i, f: (p, i, 0)),
                pl.BlockSpec((2, seq, LANES), lambda p, i, f: (p, 0, 0)),
                pl.BlockSpec((2 * HEAD_DIM, seq), lambda p, i, f: (p, 0)),
                pl.BlockSpec((2, tile, LANES), lambda p, i, f: (p, 0, 0)),
            ],
            out_specs=pl.BlockSpec((tile, LANES), lambda p, i, f: (i, p)),
            scratch_shapes=[pltpu.VMEM((2, 1, tile), F32), pltpu.VMEM((2, 1, tile), F32),
                            pltpu.VMEM((2, HEAD_DIM, tile), F32)],
        ),
        out_shape=jax.ShapeDtypeStruct((seq, GROUP), BF16),
        compiler_params=_params(("arbitrary", "arbitrary")),
        name="flash_kbias" if use_kbias else "flash",
    )(first_tile, q, k, v_t, kbias)


def _fox_first_tile(decay, qk_bound):
    seq = decay.shape[0]
    nq = seq // FLASH_TILE
    f = decay[:, :HEADS]
    f_first = f[0::FLASH_TILE]
    f_last = f[FLASH_TILE - 1::FLASH_TILE]
    gap = f_first[:, None, :] - f_last[None, :, :] + 2.0 * qk_bound
    jj = jnp.arange(nq, dtype=jnp.int32)
    needed = (gap >= FOX_SKIP_LOG) | (jj[None, :, None] >= jj[:, None, None])
    first = jnp.min(jnp.where(needed, jj[None, :, None], nq), axis=1)
    first_pair = jnp.minimum(first[:, 0::2], first[:, 1::2])
    return first_pair.T.reshape(-1).astype(jnp.int32)


def _dilated_kernel(q_ref, k_ref, v_ref, o_ref, kbuf, vbuf, acc_s, m_s, l_s):
    pair = pl.program_id(0)
    i = pl.program_id(1)
    T = q_ref.shape[0]

    @pl.when(i == 0)
    def _():
        kbuf[...] = jnp.zeros_like(kbuf)
        vbuf[...] = jnp.zeros_like(vbuf)

    kbuf[0:T, :] = kbuf[T:2 * T, :]
    vbuf[0:T, :] = vbuf[T:2 * T, :]
    kbuf[T:2 * T, :] = k_ref[...]
    vbuf[T:2 * T, :] = v_ref[...]

    ii = lax.broadcasted_iota(jnp.int32, (DIL_SPAN, 2 * DIL_SPAN), 0)
    jj = lax.broadcasted_iota(jnp.int32, (DIL_SPAN, 2 * DIL_SPAN), 1)
    dist = ii + DIL_SPAN - jj
    band = (dist >= 0) & (dist <= DIL_SPAN)
    dist_f = dist.astype(F32)
    upper = lax.broadcasted_iota(jnp.int32, (1, LANES), 1) >= HEAD_DIM

    for pi, (window, r) in enumerate(DIL_PATTERNS):
        assert window // r == DIL_SPAN
        sub = DIL_SPAN * r

        def body(idx, carry, r=r, sub=sub, pi=pi):
            n = idx // r
            rho = idx - n * r
            base = n * sub + rho
            q = q_ref[pl.ds(base, DIL_SPAN, stride=r), :]
            kc = kbuf[pl.ds(T + base - sub, 2 * DIL_SPAN, stride=r), :].astype(BF16)
            vc = vbuf[pl.ds(T + base - sub, 2 * DIL_SPAN, stride=r), :].astype(BF16)
            first_key = jnp.where((i == 0) & (n == 0), DIL_SPAN, 0)
            valid = band & (jj >= first_key)
            stats = []
            for hh in range(2):
                slope = jnp.where(pair == 0, 2.0 ** (-(2 * hh + 1)), 2.0 ** (-(2 * hh + 5)))
                qh = (jnp.where(upper, q, 0.0) if hh else jnp.where(upper, 0.0, q)).astype(BF16)
                s = _dot_nt(qh, kc) - (slope * r) * dist_f
                s = jnp.where(valid, s, NEG)
                m = jnp.max(s, axis=-1, keepdims=True)
                p = jnp.exp(s - m)
                l = jnp.sum(p, axis=-1, keepdims=True)
                stats.append((_dot(p.astype(BF16), vc), m, l))
            rows = pl.ds(pi * T + base, DIL_SPAN, stride=r)
            acc_s[rows, :] = jnp.where(upper, stats[1][0], stats[0][0])
            m_s[rows, :] = jnp.where(upper, stats[1][1], stats[0][1])
            l_s[rows, :] = jnp.where(upper, stats[1][2], stats[0][2])
            return carry

        lax.fori_loop(0, T // DIL_SPAN, body, 0)

    npat = len(DIL_PATTERNS)
    ms = [m_s[pi * T:(pi + 1) * T, :] for pi in range(npat)]
    m_top = functools.reduce(jnp.maximum, ms)
    num = jnp.zeros((T, LANES), F32)
    den = jnp.zeros((T, LANES), F32)
    for pi in range(npat):
        w = jnp.exp(ms[pi] - m_top)
        num = num + w * acc_s[pi * T:(pi + 1) * T, :]
        den = den + w * l_s[pi * T:(pi + 1) * T, :]
    o_ref[...] = (num / den).astype(o_ref.dtype)


def _dilated(q, k, v):
    seq = q.shape[0]
    T = DIL_TILE
    spec = pl.BlockSpec((T, LANES), lambda p, i: (i, p))
    npat = len(DIL_PATTERNS)
    return pl.pallas_call(
        _dilated_kernel,
        grid=(HEADS // 2, seq // T),
        in_specs=[spec, spec, spec],
        out_specs=spec,
        out_shape=jax.ShapeDtypeStruct((seq, GROUP), BF16),
        scratch_shapes=[pltpu.VMEM((2 * T, LANES), F32), pltpu.VMEM((2 * T, LANES), F32),
                        pltpu.VMEM((npat * T, LANES), F32), pltpu.VMEM((npat * T, LANES), F32),
                        pltpu.VMEM((npat * T, LANES), F32)],
        compiler_params=_params(("arbitrary", "arbitrary")),
        name="dilated",
    )(q, k, v)


def _post_kernel(x_ref, oa_ref, ob_ref, oc_ref, od_ref, wout_ref, mod_ref, g2_ref,
                 rwh_ref, rwl_ref, rb_ref, tri_ref,
                 x1_ref, h2_ref, eidx_ref, gate_ref, rank_ref, cnt_ref, carry):
    i = pl.program_id(0)
    tm = x_ref.shape[0]

    @pl.when(i == 0)
    def _():
        carry[...] = jnp.zeros_like(carry)

    o = (_dot(oa_ref[...], wout_ref[0]) + _dot(ob_ref[...], wout_ref[1])
         + _dot(oc_ref[...], wout_ref[2]) + _dot(od_ref[...], wout_ref[3]))
    x1 = x_ref[...] + mod_ref[2] * o
    x1_ref[...] = x1
    y = x1 * lax.rsqrt(jnp.mean(x1 * x1, axis=-1, keepdims=True) + EPS) * g2_ref[...]
    h2 = y * (1.0 + mod_ref[4]) + mod_ref[3]
    _store_token_tiles(h2_ref, h2)

    h_hi, h_lo = _split2(h2)
    logits = (_dot(h_hi, rwh_ref[...]) + _dot(h_hi, rwl_ref[...]) + _dot(h_lo, rwh_ref[...])
              + rb_ref[...])
    lane = lax.broadcasted_iota(jnp.int32, (tm, LANES), 1)
    lane_f = lane.astype(F32)
    g = logits
    chosen = jnp.zeros((tm, LANES), F32)
    vals, idxs = [], []
    for _ in range(TOP_K):
        m = jnp.max(g, axis=-1, keepdims=True)
        first = jnp.min(jnp.where(g == m, lane_f, 1e9), axis=-1, keepdims=True)
        pick = lane_f == first
        chosen = jnp.where(pick, 1.0, chosen)
        g = jnp.where(pick, -jnp.inf, g)
        vals.append(m)
        idxs.append(first)
    exps = [jnp.exp(v - vals[0]) for v in vals]
    den = exps[0] + exps[1] + exps[2] + exps[3]
    before = _dot(tri_ref[...], chosen.astype(BF16)) + carry[...]
    carry[...] = carry[...] + jnp.sum(chosen, axis=0, keepdims=True)
    cnt_ref[...] = carry[...]
    e_out = jnp.zeros((tm, LANES), F32)
    g_out = jnp.zeros((tm, LANES), F32)
    r_out = jnp.zeros((tm, LANES), F32)
    for k in range(TOP_K):
        rank_k = jnp.sum(jnp.where(lane_f == idxs[k], before, 0.0), axis=-1, keepdims=True)
        e_out = jnp.where(lane == k, idxs[k], e_out)
        g_out = jnp.where(lane == k, exps[k] / den, g_out)
        r_out = jnp.where(lane == k, rank_k, r_out)
    eidx_ref[...] = e_out.astype(jnp.int32)
    gate_ref[...] = g_out
    rank_ref[...] = r_out.astype(jnp.int32)


def _post(x2, oa, ob, oc, od, mod_l, consts, p):
    seq = x2.shape[0]
    tm = TOKEN_TILE
    row = lambda n: pl.BlockSpec((tm, n), lambda i: (i, 0))
    full = [p["w_out"], mod_l, p["g2"], p["rw_hi"], p["rw_lo"], p["r_b"], consts["tri_strict"]]
    f32 = lambda n: jax.ShapeDtypeStruct((seq, n), F32)
    i32 = lambda n: jax.ShapeDtypeStruct((seq, n), jnp.int32)
    return pl.pallas_call(
        _post_kernel,
        grid=(seq // tm,),
        in_specs=[row(D_MODEL)] + [row(GROUP)] * 4 + [_full_spec(a.shape) for a in full],
        out_specs=[row(D_MODEL), pl.BlockSpec((tm * CHUNKS, LANES), lambda i: (i, 0)),
                   row(LANES), row(LANES), row(LANES), _full_spec((1, LANES))],
        out_shape=[f32(D_MODEL), jax.ShapeDtypeStruct((seq * CHUNKS, LANES), F32),
                   i32(LANES), f32(LANES), i32(LANES), jax.ShapeDtypeStruct((1, LANES), F32)],
        scratch_shapes=[pltpu.VMEM((1, LANES), F32)],
        compiler_params=_params(("arbitrary",)),
        name="post",
    )(x2, oa, ob, oc, od, *full)


def _tile_copy(src, s, dst, d, sem):
    return pltpu.make_async_copy(src.at[pl.ds(pl.multiple_of(s * CHUNKS, CHUNKS), CHUNKS), :],
                                 dst.at[pl.ds(pl.multiple_of(d * CHUNKS, CHUNKS), CHUNKS), :], sem)


def _dispatch_kernel(dest_ref, h_ref, xs_in_ref, xs_ref, sem):
    del xs_in_ref
    i = pl.program_id(0)
    n = TOKEN_TILE * TOP_K
    base = i * n

    def issue(r, carry):
        for k in range(TOP_K):
            _tile_copy(h_ref, r, xs_ref, dest_ref[base + r * TOP_K + k], sem).start()
        return carry

    lax.fori_loop(0, TOKEN_TILE, issue, 0, unroll=2)
    rows = pl.ds(0, n * CHUNKS)
    pltpu.make_async_copy(xs_ref.at[rows, :], xs_ref.at[rows, :], sem).wait()


def _dispatch(dest, h2_tiles, m_pad):
    seq = h2_tiles.shape[0] // CHUNKS
    zeros = jnp.zeros((m_pad * CHUNKS, LANES), F32)
    return pl.pallas_call(
        _dispatch_kernel,
        grid_spec=pltpu.PrefetchScalarGridSpec(
            num_scalar_prefetch=1,
            grid=(seq // TOKEN_TILE,),
            in_specs=[pl.BlockSpec((TOKEN_TILE * CHUNKS, LANES), lambda i, d: (i, 0)),
                      pl.BlockSpec(memory_space=pl.ANY)],
            out_specs=pl.BlockSpec(memory_space=pl.ANY),
            scratch_shapes=[pltpu.SemaphoreType.DMA(())],
        ),
        out_shape=jax.ShapeDtypeStruct((m_pad * CHUNKS, LANES), F32),
        input_output_aliases={2: 0},
        compiler_params=_params(("arbitrary",)),
        name="dispatch",
    )(dest, h2_tiles, zeros)


def _expert_kernel(be_ref, nu_ref, xs_ref, w1_ref, b1_ref, w2_ref, b2_ref, ys_ref, w1b, w2b):
    b = pl.program_id(0)
    prev = be_ref[jnp.maximum(b - 1, 0)]
    fresh = (b == 0) | (be_ref[b] != prev)

    @pl.when(fresh)
    def _():
        w1b[...] = w1_ref[0].astype(BF16)
        w2b[...] = w2_ref[0].astype(BF16)

    @pl.when(b < nu_ref[0])
    def _():
        bm = EXPERT_ROWS
        xb = jnp.concatenate([_load_token_chunk(xs_ref, bm, c) for c in range(CHUNKS)],
                             axis=1).astype(BF16)
        gu = _dot(xb, w1b[...]) + b1_ref[0]
        g = jnp.minimum(gu[:, :D_EXPERT], SWIGLU_LIMIT)
        u = jnp.clip(gu[:, D_EXPERT:], -SWIGLU_LIMIT, SWIGLU_LIMIT)
        y = (u + 1.0) * g * (1.0 / (1.0 + jnp.exp(-SWIGLU_ALPHA * g)))
        _store_token_tiles(ys_ref, _dot(y.astype(BF16), w2b[...]) + b2_ref[0])

    @pl.when(b >= nu_ref[0])
    def _():
        ys_ref[...] = jnp.zeros_like(ys_ref)


def _experts(blk_expert, n_used, xs, w1, b1, w2, b2):
    m_pad = xs.shape[0] // CHUNKS
    bm = EXPERT_ROWS
    n_all = w1.shape[0] * w1.shape[1]
    rows = lambda b, be, nu: (jnp.minimum(b, nu[0] - 1), 0)
    ex = lambda b, be, nu: (be[jnp.minimum(b, nu[0] - 1)], 0, 0)
    return pl.pallas_call(
        _expert_kernel,
        grid_spec=pltpu.PrefetchScalarGridSpec(
            num_scalar_prefetch=2,
            grid=(m_pad // bm,),
            in_specs=[
                pl.BlockSpec((bm * CHUNKS, LANES), rows),
                pl.BlockSpec((1, D_MODEL, 2 * D_EXPERT), ex),
                pl.BlockSpec((1, 1, 2 * D_EXPERT), ex),
                pl.BlockSpec((1, D_EXPERT, D_MODEL), ex),
                pl.BlockSpec((1, 1, D_MODEL), ex),
            ],
            out_specs=pl.BlockSpec((bm * CHUNKS, LANES), lambda b, be, nu: (b, 0)),
            scratch_shapes=[pltpu.VMEM((D_MODEL, 2 * D_EXPERT), BF16),
                            pltpu.VMEM((D_EXPERT, D_MODEL), BF16)],
        ),
        out_shape=jax.ShapeDtypeStruct((m_pad * CHUNKS, LANES), F32),
        compiler_params=_params(("arbitrary",)),
        name="experts",
    )(blk_expert, n_used, xs, w1.reshape(n_all, D_MODEL, 2 * D_EXPERT), b1.reshape(n_all, 1, -1),
      w2.reshape(n_all, D_EXPERT, D_MODEL), b2.reshape(n_all, 1, -1))


def _combine_kernel(dest_ref, ys_ref, x1_ref, gate_ref, mod_ref, o_ref, buf, sem):
    i = pl.program_id(0)
    tm = x1_ref.shape[0]
    n = tm * TOP_K
    base = i * n

    def issue(r, carry):
        for k in range(TOP_K):
            _tile_copy(ys_ref, dest_ref[base + r * TOP_K + k], buf, k * tm + r, sem).start()
        return carry

    lax.fori_loop(0, tm, issue, 0, unroll=2)
    pltpu.make_async_copy(ys_ref.at[pl.ds(0, n * CHUNKS), :], buf, sem).wait()
    gates = gate_ref[...]
    g2 = mod_ref[5]
    for c in range(CHUNKS):
        cols = slice(c * LANES, (c + 1) * LANES)
        mix = jnp.zeros((tm, LANES), F32)
        for k in range(TOP_K):
            mix = mix + gates[:, k:k + 1] * _load_token_chunk(buf, tm, c, offset=k * tm * CHUNKS)
        o_ref[:, cols] = x1_ref[:, cols] + g2[:, cols] * mix


def _combine(dest, ys, x1, gates, mod_l):
    seq = x1.shape[0]
    tm = TOKEN_TILE
    return pl.pallas_call(
        _combine_kernel,
        grid_spec=pltpu.PrefetchScalarGridSpec(
            num_scalar_prefetch=1,
            grid=(seq // tm,),
            in_specs=[
                pl.BlockSpec(memory_space=pl.ANY),
                pl.BlockSpec((tm, D_MODEL), lambda i, d: (i, 0)),
                pl.BlockSpec((tm, LANES), lambda i, d: (i, 0)),
                pl.BlockSpec(mod_l.shape, lambda i, d: (0, 0, 0)),
            ],
            out_specs=pl.BlockSpec((tm, D_MODEL), lambda i, d: (i, 0)),
            scratch_shapes=[pltpu.VMEM((TOP_K * tm * CHUNKS, LANES), F32),
                            pltpu.SemaphoreType.DMA(())],
        ),
        out_shape=jax.ShapeDtypeStruct((seq, D_MODEL), F32),
        compiler_params=_params(("arbitrary",)),
        name="combine",
    )(dest, ys, x1, gates, mod_l)


def _moe(l, x1, h2_tiles, eidx, gates, rank, counts, mod_l, w1, b1, w2, b2):
    seq = x1.shape[0]
    bm = EXPERT_ROWS
    m_pad = seq * TOP_K + N_EXPERTS * bm
    cnt = counts[0, :N_EXPERTS].astype(jnp.int32)
    padded = (cnt + bm - 1) // bm * bm
    pad_end = jnp.cumsum(padded)
    pad_start = pad_end - padded
    onehot = eidx[:, :TOP_K, None] == jnp.arange(N_EXPERTS, dtype=jnp.int32)
    start_of = jnp.sum(jnp.where(onehot, pad_start, 0), axis=-1)
    dest = (start_of + rank[:, :TOP_K]).reshape(seq * TOP_K).astype(jnp.int32)
    nblk = m_pad // bm
    blk_start = jnp.arange(nblk, dtype=jnp.int32) * bm
    blk_expert = jnp.minimum(jnp.sum(pad_end[None, :] <= blk_start[:, None], axis=1), N_EXPERTS - 1)
    blk_expert = (blk_expert + l * N_EXPERTS).astype(jnp.int32)
    n_used = (pad_end[-1:] // bm).astype(jnp.int32)
    xs = _dispatch(dest, h2_tiles, m_pad)
    ys = _experts(blk_expert, n_used, xs, w1, b1, w2, b2)
    return _combine(dest, ys, x1, gates, mod_l)


def _pad_cols(a, n):
    return jnp.pad(a, ((0, 0), (0, n - a.shape[1])))


def _layer_params(l, w_in, mla_cq_g, mla_w_uq, mla_ckv_g, mla_w_ukv, mla_q_g, mla_k_g,
                  fox_q_g, fox_k_g, fox_b_f, moba_q_g, moba_k_g, dil_q_g, dil_k_g, w_out,
                  norm1_g, norm2_g, router_w, router_b):
    w = w_in[l]
    sizes = [MLA_Q_RANK, MLA_KV_RANK, MLA_ROPE, GROUP, GROUP, GROUP, HEADS] + [GROUP] * 6
    offs = np.concatenate([[0], np.cumsum(sizes)])
    part = [w[:, offs[j]:offs[j + 1]] for j in range(len(sizes))]
    zeros = lambda n: jnp.zeros((D_MODEL, n), F32)
    w_in_r = jnp.concatenate(
        [part[0], part[1], zeros(MLA_NOPE), part[2], zeros(LANES - MLA_QK),
         part[3], part[4], part[6], zeros(LANES - HEADS),
         part[7], part[8], part[10], part[11], part[12]], axis=1).astype(BF16)
    assert w_in_r.shape[1] == COLS_IN
    w_vt = jnp.stack([part[5].T, part[9].T]).astype(BF16)
    w_uq = jnp.pad(mla_w_uq[l].reshape(MLA_Q_RANK, HEADS, MLA_QK),
                   ((0, 0), (0, 0), (0, LANES - MLA_QK))).reshape(MLA_Q_RANK, HEADS * LANES)
    w_ukv = mla_w_ukv[l].reshape(MLA_KV_RANK, HEADS, MLA_NOPE + HEAD_DIM)
    w_uk = jnp.pad(w_ukv[:, :, :MLA_NOPE], ((0, 0), (0, 0), (0, LANES - MLA_NOPE)))
    w_uv = w_ukv[:, :, MLA_NOPE:]
    tile4 = lambda g: jnp.tile(g, HEADS)[None, :]
    rw = _pad_cols(router_w[l], LANES)
    rw_hi = rw.astype(BF16)
    rw_lo = (rw - rw_hi.astype(F32)).astype(BF16)
    r_b = jnp.concatenate([router_b[l], jnp.full((LANES - N_EXPERTS,), NEG, F32)])[None, :]
    return dict(
        g1=norm1_g[l][None, :], g2=norm2_g[l][None, :], w_in=w_in_r, w_vt=w_vt,
        cq_g=mla_cq_g[l][None, :], w_uq=w_uq.astype(BF16), ckv_g=mla_ckv_g[l][None, :],
        w_uk=w_uk.reshape(MLA_KV_RANK, HEADS * LANES).astype(BF16),
        w_uvt=w_uv.reshape(MLA_KV_RANK, GROUP).T.astype(BF16),
        fox_bound=FOX_NORM_SLACK * HEAD_DIM ** 0.5 * jnp.max(jnp.abs(fox_q_g[l]))
        * jnp.max(jnp.abs(fox_k_g[l])),
        q_g=_pad_cols(mla_q_g[l][None, :], LANES), k_g=_pad_cols(mla_k_g[l][None, :], LANES),
        fq_g=tile4(fox_q_g[l]), fk_g=tile4(fox_k_g[l]), f_b=_pad_cols(fox_b_f[l][None, :], LANES),
        mq_g=tile4(moba_q_g[l]), mk_g=tile4(moba_k_g[l]), dq_g=tile4(dil_q_g[l]), dk_g=tile4(dil_k_g[l]),
        w_out=w_out[l].reshape(HEADS, GROUP, D_MODEL).astype(BF16),
        rw_hi=rw_hi, rw_lo=rw_lo, r_b=r_b,
    )


def kernel(x, c, w_mod, b_mod, norm1_g, norm2_g, w_in, mla_cq_g, mla_w_uq, mla_ckv_g, mla_w_ukv, mla_q_g, mla_k_g, fox_q_g, fox_k_g, fox_b_f, moba_q_g, moba_k_g, dil_q_g, dil_k_g, w_out, router_w, router_b, exp_w1, exp_b1, exp_w2, exp_b2):
    batch, seq, d = x.shape
    assert batch == 1 and d == D_MODEL
    assert seq % DIL_TILE == 0 and seq // MOBA_BLOCK <= MOBA_MAX_BLOCKS
    depth = w_mod.shape[0]
    consts = _prep_constants(seq, TOKEN_TILE)
    mod = _modulation(c, w_mod, b_mod)
    slopes_c = 2.0 ** (-(2.0 * np.arange(HEADS) + 2.0))
    in_block = np.arange(FLASH_TILE) % MOBA_BLOCK
    kbias_c = jnp.asarray(np.broadcast_to(slopes_c[:, None, None] * in_block[None, :, None],
                                          (HEADS, FLASH_TILE, LANES)), F32)
    kbias_0 = jnp.zeros((HEADS, FLASH_TILE, LANES), F32)
    all_tiles = jnp.zeros(((HEADS // 2) * (seq // FLASH_TILE),), jnp.int32)
    x2 = x.reshape(seq, d)
    for l in range(depth):
        p = _layer_params(l, w_in, mla_cq_g, mla_w_uq, mla_ckv_g, mla_w_ukv, mla_q_g, mla_k_g,
                          fox_q_g, fox_k_g, fox_b_f, moba_q_g, moba_k_g, dil_q_g, dil_k_g, w_out,
                          norm1_g, norm2_g, router_w, router_b)
        mod_l = mod[l]
        qa, ka, va, qf, kf, vf, qm, km, vm, qd, kd, vd, decay = _prep(x2, mod_l, consts, p)
        oa = _flash(qa, ka, va, kbias_0, all_tiles, False)
        ob = _flash(qf, kf, vf, kbias_0, _fox_first_tile(decay, p["fox_bound"]), False)
        oc = _flash(qm, km, vm, kbias_c, all_tiles, True)
        od = _dilated(qd, kd, vd)
        x1, h2, eidx, gates, rank, counts = _post(x2, oa, ob, oc, od, mod_l, consts, p)
        x2 = _moe(l, x1, h2, eidx, gates, rank, counts, mod_l, exp_w1, exp_b1, exp_w2, exp_b2)
    return x2.reshape(batch, seq, d)
```

```python
import functools

import numpy as np
import jax
import jax.numpy as jnp
from jax import lax
from jax.experimental import pallas as pl
from jax.experimental.pallas import tpu as pltpu

F32 = jnp.float32
BF16 = jnp.bfloat16

D_MODEL = 1024
HEAD_DIM = 64
HEADS = 4
GROUP = HEADS * HEAD_DIM
LANES = 128
CHUNKS = D_MODEL // LANES
MLA_Q_RANK = 256
MLA_KV_RANK = 128
MLA_NOPE = 64
MLA_ROPE = 32
MLA_QK = MLA_NOPE + MLA_ROPE
ROPE_THETA = 10000.0
MOBA_BLOCK = 256
MOBA_TOPK = 3
MOBA_MAX_BLOCKS = 64
DIL_PATTERNS = ((128, 1), (512, 4), (2048, 16))
DIL_SPAN = 128
DIL_TILE = 2048
N_EXPERTS = 32
TOP_K = 4
D_EXPERT = 1024
SWIGLU_LIMIT = 7.0
SWIGLU_ALPHA = 1.702
EPS = 1e-6
NEG = -1e30

FLASH_TILE = 512
ONES_ROWS = 16
FOX_SKIP_LOG = -106.0
FOX_NORM_SLACK = 1.02
TOKEN_TILE = 256
EXPERT_ROWS = 256
VMEM_LIMIT = 56 * 1024 * 1024

COLS_MLA = 512
COLS_FOX = 640
COLS_MOBA = 512
COLS_DIL = 768
COLS_IN = COLS_MLA + COLS_FOX + COLS_MOBA + COLS_DIL


def _dot(a, b):
    return jnp.dot(a, b, preferred_element_type=F32)


def _dot_nt(a, b):
    return lax.dot_general(a, b, (((1,), (1,)), ((), ())), preferred_element_type=F32)


def _split2(x):
    hi = x.astype(BF16)
    lo = (x - hi.astype(F32)).astype(BF16)
    return hi, lo


def _split3(x):
    a = x.astype(BF16)
    r = x - a.astype(F32)
    b = r.astype(BF16)
    c = (r - b.astype(F32)).astype(BF16)
    return a, b, c


def _head_of_lane():
    return jnp.right_shift(lax.broadcasted_iota(jnp.int32, (1, GROUP), 1), 6)


def _store_token_tiles(ref, x, offset=0):
    n = x.shape[0]
    for c in range(CHUNKS):
        ref[pl.ds(offset + c, n, stride=CHUNKS), :] = x[:, c * LANES:(c + 1) * LANES]


def _load_token_chunk(ref, n, c, offset=0):
    return ref[pl.ds(offset + c, n, stride=CHUNKS), :]


def _full_spec(shape):
    nd = len(shape)
    return pl.BlockSpec(shape, lambda *_: (0,) * nd)


def _params(sem):
    return pltpu.CompilerParams(dimension_semantics=sem, vmem_limit_bytes=VMEM_LIMIT)


def _mod_kernel(c_ref, w_ref, b_ref, o_ref):
    c = c_ref[...]
    s = c * (1.0 / (1.0 + jnp.exp(-c)))
    s8 = jnp.broadcast_to(s, (8, D_MODEL))
    r = jnp.dot(s8, w_ref[0], preferred_element_type=F32, precision=lax.Precision.HIGHEST)
    o_ref[0, 0] = r[0:1, :] + b_ref[0, 0]


def _modulation(c, w_mod, b_mod):
    depth = w_mod.shape[0]
    b4 = b_mod.reshape(depth, 6, 1, D_MODEL)
    return pl.pallas_call(
        _mod_kernel,
        grid=(depth, 6),
        in_specs=[
            pl.BlockSpec((1, D_MODEL), lambda l, j: (0, 0)),
            pl.BlockSpec((1, D_MODEL, D_MODEL), lambda l, j: (l, 0, j)),
            pl.BlockSpec((1, 1, 1, D_MODEL), lambda l, j: (l, j, 0, 0)),
        ],
        out_specs=pl.BlockSpec((1, 1, 1, D_MODEL), lambda l, j: (l, j, 0, 0)),
        out_shape=jax.ShapeDtypeStruct((depth, 6, 1, D_MODEL), F32),
        compiler_params=_params(("arbitrary", "arbitrary")),
        name="modulation",
    )(c, w_mod, b4)


def _head_norm(x, g, bd):
    hi, lo = _split2(x * x)
    ss = _dot(hi, bd) + _dot(lo, bd)
    return x * lax.rsqrt(ss * (1.0 / HEAD_DIM) + EPS) * g


def _prep_kernel(x_ref, mod_ref, g1_ref, win_ref, wvt_ref, cqg_ref, wuq_ref, ckvg_ref, wuk_ref,
                 wuvt_ref, qg_ref, kg_ref, rc_ref, rs1_ref, rs2_ref,
                 fqg_ref, fkg_ref, fb_ref, mqg_ref, mkg_ref, dqg_ref, dkg_ref,
                 bd_ref, tri_ref, eq_ref, ek_ref, sel_ref,
                 qa_ref, ka_ref, va_ref, qf_ref, kf_ref, vf_ref, qm_ref, km_ref, vm_ref,
                 qd_ref, kd_ref, vd_ref, f_ref,
                 fcarry, kmean):
    i = pl.program_id(0)
    tm = x_ref.shape[0]

    @pl.when(i == 0)
    def _():
        fcarry[...] = jnp.zeros_like(fcarry)
        kmean[...] = jnp.zeros_like(kmean)

    x = x_ref[...]
    y = x * lax.rsqrt(jnp.mean(x * x, axis=-1, keepdims=True) + EPS) * g1_ref[...]
    hb = (y * (1.0 + mod_ref[1]) + mod_ref[0]).astype(BF16)
    bd = bd_ref[...]
    lane = lax.broadcasted_iota(jnp.int32, (1, LANES), 1)
    lane_f = lane.astype(F32)
    head_of_lane = _head_of_lane()

    pa = _dot(hb, win_ref[:, 0:COLS_MLA])
    cq = pa[:, 0:MLA_Q_RANK]
    ckv = pa[:, MLA_Q_RANK:MLA_Q_RANK + MLA_KV_RANK]
    kr = pa[:, MLA_Q_RANK + MLA_KV_RANK:COLS_MLA]
    cqn = (cq * lax.rsqrt(jnp.mean(cq * cq, axis=-1, keepdims=True) + EPS) * cqg_ref[...]).astype(BF16)
    ckvn = (ckv * lax.rsqrt(jnp.mean(ckv * ckv, axis=-1, keepdims=True) + EPS) * ckvg_ref[...]).astype(BF16)
    q_all = _dot(cqn, wuq_ref[...])
    k_all = _dot(ckvn, wuk_ref[...])
    va_ref[...] = _dot_nt(wuvt_ref[...], ckvn).astype(BF16)
    rc, rs1, rs2 = rc_ref[...], rs1_ref[...], rs2_ref[...]

    def rope(t):
        return t * rc + pltpu.roll(t, LANES - MLA_ROPE // 2, 1) * rs1 + pltpu.roll(t, MLA_ROPE // 2, 1) * rs2

    for h in range(HEADS):
        q = q_all[:, h * LANES:(h + 1) * LANES]
        q = q * lax.rsqrt(jnp.sum(q * q, axis=-1, keepdims=True) * (1.0 / MLA_QK) + EPS) * qg_ref[...]
        qa_ref[h] = (rope(q) * (MLA_QK ** -0.5)).astype(BF16)
        k = k_all[:, h * LANES:(h + 1) * LANES] + kr
        k = k * lax.rsqrt(jnp.sum(k * k, axis=-1, keepdims=True) * (1.0 / MLA_QK) + EPS) * kg_ref[...]
        ka_ref[h] = rope(k).astype(BF16)

    pf = _dot(hb, win_ref[:, COLS_MLA:COLS_MLA + COLS_FOX])
    fqn = (_head_norm(pf[:, 0:GROUP], fqg_ref[...], bd) * (HEAD_DIM ** -0.5)).astype(BF16)
    fkn = _head_norm(pf[:, GROUP:2 * GROUP], fkg_ref[...], bd).astype(BF16)
    vf_ref[...] = _dot_nt(wvt_ref[0], hb).astype(BF16)
    z = pf[:, 2 * GROUP:2 * GROUP + LANES] + fb_ref[...]
    log_f = jnp.minimum(z, 0.0) - jnp.log(1.0 + jnp.exp(-jnp.abs(z)))
    tri = tri_ref[...]
    a1, a2, a3 = _split3(log_f)
    cum = fcarry[...] + (_dot(tri, a1) + _dot(tri, a2) + _dot(tri, a3))
    fcarry[...] = cum[tm - 1:tm, :]
    f_ref[...] = cum
    f1, f2, f3 = _split3(cum)
    xq = jnp.concatenate([fqn, f1, f2, f3], axis=1)
    xk = jnp.concatenate([fkn, f1, f2, f3], axis=1)
    ones_q = jnp.where((lane >= HEAD_DIM + 3) & (lane < HEAD_DIM + 6), 1.0, 0.0)
    ones_k = jnp.where((lane >= HEAD_DIM) & (lane < HEAD_DIM + 3), 1.0, 0.0)
    for h in range(HEADS):
        qf_ref[h] = (_dot(xq, eq_ref[h]) + ones_q).astype(BF16)
        kf_ref[h] = (_dot(xk, ek_ref[h]) + ones_k).astype(BF16)

    pm = _dot(hb, win_ref[:, COLS_MLA + COLS_FOX:COLS_MLA + COLS_FOX + COLS_MOBA])
    mqn = _head_norm(pm[:, 0:GROUP], mqg_ref[...], bd) * (HEAD_DIM ** -0.5)
    mkn = _head_norm(pm[:, GROUP:2 * GROUP], mkg_ref[...], bd)
    vm_ref[...] = _dot_nt(wvt_ref[1], hb).astype(BF16)
    col_mean = jnp.mean(mkn, axis=0, keepdims=True)
    mqb = mqn.astype(BF16)
    mkb = mkn.astype(BF16)
    blk = lane - HEAD_DIM
    blk_f = blk.astype(F32)
    past = (blk >= 0) & (blk < i)
    i_f = i.astype(F32)
    for h in range(HEADS):
        kmean[h, pl.ds(HEAD_DIM + i, 1), :] = jnp.where(head_of_lane == h, col_mean, 0.0)
        qh_hi, qh_lo = _split2(jnp.where(head_of_lane == h, mqn, 0.0))
        km_hi, km_lo = _split2(kmean[h])
        gate = _dot_nt(qh_hi, km_hi) + _dot_nt(qh_hi, km_lo) + _dot_nt(qh_lo, km_hi)
        g = jnp.where(past, gate, NEG)
        chosen = jnp.zeros((tm, LANES), F32)
        for _ in range(MOBA_TOPK):
            m = jnp.max(g, axis=-1, keepdims=True)
            first = jnp.min(jnp.where(g == m, lane_f, 1e9), axis=-1, keepdims=True)
            pick = (lane_f == first) & (m > NEG)
            chosen = jnp.where(pick, 1.0, chosen)
            g = jnp.where(pick, NEG, g)
        slope = 2.0 ** (-(2 * h + 2))
        keep = (chosen > 0.0) | (blk == i)
        bias = jnp.where(keep, (slope * MOBA_BLOCK) * (blk_f - i_f), NEG)
        bias = jnp.where(blk >= 0, bias, 0.0)
        qm_ref[h] = (_dot(mqb, sel_ref[h]) + bias).astype(BF16)
        onehot = jnp.where(blk == i, 1.0, 0.0)
        km_ref[h] = (_dot(mkb, sel_ref[h]) + onehot).astype(BF16)

    pd = _dot(hb, win_ref[:, COLS_MLA + COLS_FOX + COLS_MOBA:COLS_IN])
    qd_ref[...] = _head_norm(pd[:, 0:GROUP], dqg_ref[...], bd) * (HEAD_DIM ** -0.5)
    kd_ref[...] = _head_norm(pd[:, GROUP:2 * GROUP], dkg_ref[...], bd)
    vd_ref[...] = pd[:, 2 * GROUP:3 * GROUP]


def _prep_constants(seq, tm):
    half = MLA_ROPE // 2
    inv = 1.0 / (ROPE_THETA ** (jnp.arange(half, dtype=F32) / half))
    ang = jnp.arange(seq, dtype=F32)[:, None] * inv[None, :]
    cos, sin = jnp.cos(ang), jnp.sin(ang)
    z = lambda n: jnp.zeros((seq, n), F32)
    rc = jnp.concatenate([jnp.ones((seq, MLA_NOPE), F32), cos, cos, z(LANES - MLA_QK)], axis=1)
    rs1 = jnp.concatenate([z(MLA_NOPE), -sin, z(LANES - MLA_NOPE - half)], axis=1)
    rs2 = jnp.concatenate([z(MLA_NOPE + half), sin, z(LANES - MLA_QK)], axis=1)
    bd =np.kron(np.eye(HEADS, dtype=np.float32), np.ones((HEAD_DIM, HEAD_DIM), np.float32))
    tri = np.tril(np.ones((tm, tm), np.float32))
    tri_strict = np.tril(np.ones((tm, tm), np.float32), -1)
    sel = np.zeros((HEADS, GROUP, LANES), np.float32)
    eq = np.zeros((HEADS, GROUP + 3 * LANES, LANES), np.float32)
    ek = np.zeros((HEADS, GROUP + 3 * LANES, LANES), np.float32)
    for h in range(HEADS):
        for d in range(HEAD_DIM):
            sel[h, h * HEAD_DIM + d, d] = 1.0
        eq[h, :GROUP] = sel[h]
        ek[h, :GROUP] = sel[h]
        for piece in range(3):
            eq[h, GROUP + piece * LANES + h, HEAD_DIM + piece] = 1.0
            ek[h, GROUP + piece * LANES + h, HEAD_DIM + 3 + piece] = -1.0
    as_bf = lambda a: jnp.asarray(a, BF16)
    return dict(rc=rc, rs1=rs1, rs2=rs2, bd=as_bf(bd),
                tri=as_bf(tri), tri_strict=as_bf(tri_strict), sel=as_bf(sel), eq=as_bf(eq), ek=as_bf(ek))


def _prep(x2, mod_l, consts, p):
    seq = x2.shape[0]
    tm = TOKEN_TILE
    row = lambda n: pl.BlockSpec((tm, n), lambda i: (i, 0))
    heads = pl.BlockSpec((HEADS, tm, LANES), lambda i: (0, i, 0))
    in_arrays = [
        (x2, row(D_MODEL)), (mod_l, _full_spec(mod_l.shape)), (p["g1"], None), (p["w_in"], None),
        (p["w_vt"], None),
        (p["cq_g"], None), (p["w_uq"], None), (p["ckv_g"], None), (p["w_uk"], None), (p["w_uvt"], None),
        (p["q_g"], None), (p["k_g"], None),
        (consts["rc"], row(LANES)), (consts["rs1"], row(LANES)), (consts["rs2"], row(LANES)),
        (p["fq_g"], None), (p["fk_g"], None), (p["f_b"], None), (p["mq_g"], None), (p["mk_g"], None),
        (p["dq_g"], None), (p["dk_g"], None),
        (consts["bd"], None), (consts["tri"], None), (consts["eq"], None), (consts["ek"], None),
        (consts["sel"], None),
    ]
    args = [a for a, _ in in_arrays]
    specs = [s if s is not None else _full_spec(a.shape) for a, s in in_arrays]
    hshape = jax.ShapeDtypeStruct((HEADS, seq, LANES), BF16)
    vshape = jax.ShapeDtypeStruct((GROUP, seq), BF16)
    dshape = jax.ShapeDtypeStruct((seq, GROUP), F32)
    vt = pl.BlockSpec((GROUP, tm), lambda i: (0, i))
    return pl.pallas_call(
        _prep_kernel,
        grid=(seq // tm,),
        in_specs=specs,
        out_specs=[heads, heads, vt] * 3 + [row(GROUP)] * 3 + [row(LANES)],
        out_shape=[hshape, hshape, vshape] * 3 + [dshape] * 3
                  + [jax.ShapeDtypeStruct((seq, LANES), F32)],
        scratch_shapes=[pltpu.VMEM((1, LANES), F32), pltpu.VMEM((HEADS, LANES, GROUP), F32)],
        compiler_params=_params(("arbitrary",)),
        name="prep",
    )(*args)


def _flash_kernel(jlo_ref, q_ref, k_ref, vt_ref, kbias_ref, o_ref, m_sc, acc_sc, sa_sc, sb_sc,
                  *, tile, use_kbias):
    pair = pl.program_id(0)
    i = pl.program_id(1)
    m_sc[...] = jnp.full_like(m_sc, -jnp.inf)
    acc_sc[...] = jnp.zeros_like(acc_sc)

    def score(hh, j, buf):
        start = pl.multiple_of(j * tile, tile)
        buf[hh] = _dot_nt(k_ref[hh, pl.ds(start, tile), :], q_ref[hh])

    def absorb(hh, j, buf, causal):
        start = pl.multiple_of(j * tile, tile)
        s = buf[hh]
        if use_kbias:
            s = s + jnp.concatenate([kbias_ref[hh]] * (tile // LANES), axis=1)
        if causal:
            key = lax.broadcasted_iota(jnp.int32, (tile, tile), 0)
            qry = lax.broadcasted_iota(jnp.int32, (tile, tile), 1)
            s = jnp.where(key <= qry, s, NEG)
        m_prev = m_sc[hh]
        m_cur = jnp.max(jnp.max(s.reshape(8, tile // 8, tile), axis=0), axis=0, keepdims=True)
        m_new = jnp.maximum(m_prev, m_cur)
        alpha = jnp.exp(m_prev - m_new)
        p = jnp.exp((s - m_new).astype(BF16))
        vt = vt_ref[hh * HEAD_DIM:(hh + 1) * HEAD_DIM, pl.ds(start, tile)]
        vt = jnp.concatenate([vt, jnp.ones((ONES_ROWS, tile), BF16)], axis=0)
        acc_sc[hh] = alpha * acc_sc[hh] + _dot(vt, p)
        m_sc[hh] = m_new

    first = jlo_ref[pair * pl.num_programs(1) + i]
    n_off = i - first
    for hh in range(2):
        score(hh, first, sa_sc)

    def body(t, carry):
        j = first + 2 * t
        for hh in range(2):
            score(hh, j + 1, sb_sc)
            absorb(hh, j, sa_sc, False)
        for hh in range(2):
            score(hh, j + 2, sa_sc)
            absorb(hh, j + 1, sb_sc, False)
        return carry

    lax.fori_loop(0, n_off // 2, body, 0)

    @pl.when(n_off % 2 == 1)
    def _():
        for hh in range(2):
            score(hh, i, sb_sc)
            absorb(hh, i - 1, sa_sc, False)
        for hh in range(2):
            absorb(hh, i, sb_sc, True)

    @pl.when(n_off % 2 == 0)
    def _():
        for hh in range(2):
            absorb(hh, i, sa_sc, True)

    o_t = jnp.concatenate([acc_sc[hh, 0:HEAD_DIM, :] / acc_sc[hh, HEAD_DIM:HEAD_DIM + 1, :]
                           for hh in range(2)], axis=0)
    o_ref[...] = o_t.T.astype(o_ref.dtype)


def _flash(q, k, v_t, kbias, first_tile, use_kbias):
    seq = v_t.shape[1]
    tile = FLASH_TILE
    kern = functools.partial(_flash_kernel, tile=tile, use_kbias=use_kbias)
    return pl.pallas_call(
        kern,
        grid_spec=pltpu.PrefetchScalarGridSpec(
            num_scalar_prefetch=1,
            grid=(HEADS // 2, seq // tile),
            in_specs=[
                pl.BlockSpec((2, tile, LANES), lambda p, i, f: (p, i, 0)),
                pl.BlockSpec((2, seq, LANES), lambda p, i, f: (p, 0, 0)),
                pl.BlockSpec((2 * HEAD_DIM, seq), lambda p, i, f: (p, 0)),
                pl.BlockSpec((2, tile, LANES), lambda p, i, f: (p, 0, 0)),
            ],
            out_specs=pl.BlockSpec((tile, LANES), lambda p, i, f: (i, p)),
            scratch_shapes=[pltpu.VMEM((2, 1, tile), F32),
                            pltpu.VMEM((2, HEAD_DIM + ONES_ROWS, tile), F32),
                            pltpu.VMEM((2, tile, tile), F32), pltpu.VMEM((2, tile, tile), F32)],
        ),
        out_shape=jax.ShapeDtypeStruct((seq, GROUP), BF16),
        compiler_params=_params(("arbitrary", "arbitrary")),
        name="flash_kbias" if use_kbias else "flash",
    )(first_tile, q, k, v_t, kbias)


def _fox_first_tile(decay, qk_bound):
    seq = decay.shape[0]
    nq = seq // FLASH_TILE
    f = decay[:, :HEADS]
    f_first = f[0::FLASH_TILE]
    f_last = f[FLASH_TILE - 1::FLASH_TILE]
    gap = f_first[:, None, :] - f_last[None, :, :] + 2.0 * qk_bound
    jj = jnp.arange(nq, dtype=jnp.int32)
    needed = (gap >= FOX_SKIP_LOG) | (jj[None, :, None] >= jj[:, None, None])
    first = jnp.min(jnp.where(needed, jj[None, :, None], nq), axis=1)
    first_pair = jnp.minimum(first[:, 0::2], first[:, 1::2])
    return first_pair.T.reshape(-1).astype(jnp.int32)


def _dilated_kernel(q_ref, k_ref, v_ref, o_ref, kbuf, vbuf, acc_s, m_s, l_s):
    pair = pl.program_id(0)
    i = pl.program_id(1)
    T = q_ref.shape[0]

    @pl.when(i == 0)
    def _():
        kbuf[...] = jnp.zeros_like(kbuf)
        vbuf[...] = jnp.zeros_like(vbuf)

    kbuf[0:T, :] = kbuf[T:2 * T, :]
    vbuf[0:T, :] = vbuf[T:2 * T, :]
    kbuf[T:2 * T, :] = k_ref[...]
    vbuf[T:2 * T, :] = v_ref[...]

    ii = lax.broadcasted_iota(jnp.int32, (DIL_SPAN, 2 * DIL_SPAN), 0)
    jj = lax.broadcasted_iota(jnp.int32, (DIL_SPAN, 2 * DIL_SPAN), 1)
    dist = ii + DIL_SPAN - jj
    band = (dist >= 0) & (dist <= DIL_SPAN)
    dist_f = dist.astype(F32)
    upper = lax.broadcasted_iota(jnp.int32, (1, LANES), 1) >= HEAD_DIM

    for pi, (window, r) in enumerate(DIL_PATTERNS):
        assert window // r == DIL_SPAN
        sub = DIL_SPAN * r

        def body(idx, carry, r=r, sub=sub, pi=pi):
            n = idx // r
            rho = idx - n * r
            base = n * sub + rho
            q = q_ref[pl.ds(base, DIL_SPAN, stride=r), :]
            kc = kbuf[pl.ds(T + base - sub, 2 * DIL_SPAN, stride=r), :].astype(BF16)
            vc = vbuf[pl.ds(T + base - sub, 2 * DIL_SPAN, stride=r), :].astype(BF16)
            first_key = jnp.where((i == 0) & (n == 0), DIL_SPAN, 0)
            valid = band & (jj >= first_key)
            stats = []
            for hh in range(2):
                slope = jnp.where(pair == 0, 2.0 ** (-(2 * hh + 1)), 2.0 ** (-(2 * hh + 5)))
                qh = (jnp.where(upper, q, 0.0) if hh else jnp.where(upper, 0.0, q)).astype(BF16)
                s = _dot_nt(qh, kc) - (slope * r) * dist_f
                s = jnp.where(valid, s, NEG)
                m = jnp.max(s, axis=-1, keepdims=True)
                p = jnp.exp(s - m)
                l = jnp.sum(p, axis=-1, keepdims=True)
                stats.append((_dot(p.astype(BF16), vc), m, l))
            rows = pl.ds(pi * T + base, DIL_SPAN, stride=r)
            acc_s[rows, :] = jnp.where(upper, stats[1][0], stats[0][0])
            m_s[rows, :] = jnp.where(upper, stats[1][1], stats[0][1])
            l_s[rows, :] = jnp.where(upper, stats[1][2], stats[0][2])
            return carry

        lax.fori_loop(0, T // DIL_SPAN, body, 0, unroll=4)

    npat = len(DIL_PATTERNS)
    ms = [m_s[pi * T:(pi + 1) * T, :] for pi in range(npat)]
    m_top = functools.reduce(jnp.maximum, ms)
    num = jnp.zeros((T, LANES), F32)
    den = jnp.zeros((T, LANES), F32)
    for pi in range(npat):
        w = jnp.exp(ms[pi] - m_top)
        num = num + w * acc_s[pi * T:(pi + 1) * T, :]
        den = den + w * l_s[pi * T:(pi + 1) * T, :]
    o_ref[...] = (num / den).astype(o_ref.dtype)


def _dilated(q, k, v):
    seq = q.shape[0]
    T = DIL_TILE
    spec = pl.BlockSpec((T, LANES), lambda p, i: (i, p))
    npat = len(DIL_PATTERNS)
    return pl.pallas_call(
        _dilated_kernel,
        grid=(HEADS // 2, seq // T),
        in_specs=[spec, spec, spec],
        out_specs=spec,
        out_shape=jax.ShapeDtypeStruct((seq, GROUP), BF16),
        scratch_shapes=[pltpu.VMEM((2 * T, LANES), F32), pltpu.VMEM((2 * T, LANES), F32),
                        pltpu.VMEM((npat * T, LANES), F32), pltpu.VMEM((npat * T, LANES), F32),
                        pltpu.VMEM((npat * T, LANES), F32)],
        compiler_params=_params(("arbitrary", "arbitrary")),
        name="dilated",
    )(q, k, v)


def _post_kernel(x_ref, oa_ref, ob_ref, oc_ref, od_ref, wout_ref, mod_ref, g2_ref,
                 rwh_ref, rwl_ref, rb_ref, tri_ref,
                 x1_ref, h2_ref, eidx_ref, gate_ref, rank_ref, cnt_ref, carry):
    i = pl.program_id(0)
    tm = x_ref.shape[0]

    @pl.when(i == 0)
    def _():
        carry[...] = jnp.zeros_like(carry)

    o = (_dot(oa_ref[...], wout_ref[0]) + _dot(ob_ref[...], wout_ref[1])
         + _dot(oc_ref[...], wout_ref[2]) + _dot(od_ref[...], wout_ref[3]))
    x1 = x_ref[...] + mod_ref[2] * o
    x1_ref[...] = x1
    y = x1 * lax.rsqrt(jnp.mean(x1 * x1, axis=-1, keepdims=True) + EPS) * g2_ref[...]
    h2 = y * (1.0 + mod_ref[4]) + mod_ref[3]
    _store_token_tiles(h2_ref, h2)

    h_hi, h_lo = _split2(h2)
    logits = (_dot(h_hi, rwh_ref[...]) + _dot(h_hi, rwl_ref[...]) + _dot(h_lo, rwh_ref[...])
              + rb_ref[...])
    lane = lax.broadcasted_iota(jnp.int32, (tm, LANES), 1)
    lane_f = lane.astype(F32)
    g = logits
    chosen = jnp.zeros((tm, LANES), F32)
    vals, idxs = [], []
    for _ in range(TOP_K):
        m = jnp.max(g, axis=-1, keepdims=True)
        first = jnp.min(jnp.where(g == m, lane_f, 1e9), axis=-1, keepdims=True)
        pick = lane_f == first
        chosen = jnp.where(pick, 1.0, chosen)
        g = jnp.where(pick, -jnp.inf, g)
        vals.append(m)
        idxs.append(first)
    exps = [jnp.exp(v - vals[0]) for v in vals]
    den = exps[0] + exps[1] + exps[2] + exps[3]
    before = _dot(tri_ref[...], chosen.astype(BF16)) + carry[...]
    carry[...] = carry[...] + jnp.sum(chosen, axis=0, keepdims=True)
    cnt_ref[...] = carry[...]
    e_out = jnp.zeros((tm, LANES), F32)
    g_out = jnp.zeros((tm, LANES), F32)
    r_out = jnp.zeros((tm, LANES), F32)
    for k in range(TOP_K):
        rank_k = jnp.sum(jnp.where(lane_f == idxs[k], before, 0.0), axis=-1, keepdims=True)
        e_out = jnp.where(lane == k, idxs[k], e_out)
        g_out = jnp.where(lane == k, exps[k] / den, g_out)
        r_out = jnp.where(lane == k, rank_k, r_out)
    eidx_ref[...] = e_out.astype(jnp.int32)
    gate_ref[...] = g_out
    rank_ref[...] = r_out.astype(jnp.int32)


def _post(x2, oa, ob, oc, od, mod_l, consts, p):
    seq = x2.shape[0]
    tm = TOKEN_TILE
    row = lambda n: pl.BlockSpec((tm, n), lambda i: (i, 0))
    full = [p["w_out"], mod_l, p["g2"], p["rw_hi"], p["rw_lo"], p["r_b"], consts["tri_strict"]]
    f32 = lambda n: jax.ShapeDtypeStruct((seq, n), F32)
    i32 = lambda n: jax.ShapeDtypeStruct((seq, n), jnp.int32)
    return pl.pallas_call(
        _post_kernel,
        grid=(seq // tm,),
        in_specs=[row(D_MODEL)] + [row(GROUP)] * 4 + [_full_spec(a.shape) for a in full],
        out_specs=[row(D_MODEL), pl.BlockSpec((tm * CHUNKS, LANES), lambda i: (i, 0)),
                   row(LANES), row(LANES), row(LANES), _full_spec((1, LANES))],
        out_shape=[f32(D_MODEL), jax.ShapeDtypeStruct((seq * CHUNKS, LANES), F32),
                   i32(LANES), f32(LANES), i32(LANES), jax.ShapeDtypeStruct((1, LANES), F32)],
        scratch_shapes=[pltpu.VMEM((1, LANES), F32)],
        compiler_params=_params(("arbitrary",)),
        name="post",
    )(x2, oa, ob, oc, od, *full)


def _tile_copy(src, s, dst, d, sem):
    return pltpu.make_async_copy(src.at[pl.ds(pl.multiple_of(s * CHUNKS, CHUNKS), CHUNKS), :],
                                 dst.at[pl.ds(pl.multiple_of(d * CHUNKS, CHUNKS), CHUNKS), :], sem)


def _dispatch_kernel(dest_ref, h_ref, xs_in_ref, xs_ref, sem):
    del xs_in_ref
    i = pl.program_id(0)
    n = TOKEN_TILE * TOP_K
    base = i * n

    def issue(r, carry):
        for k in range(TOP_K):
            _tile_copy(h_ref, r, xs_ref, dest_ref[base + r * TOP_K + k], sem).start()
        return carry

    lax.fori_loop(0, TOKEN_TILE, issue, 0, unroll=2)
    rows = pl.ds(0, n * CHUNKS)
    pltpu.make_async_copy(xs_ref.at[rows, :], xs_ref.at[rows, :], sem).wait()


def _dispatch(dest, h2_tiles, m_pad):
    seq = h2_tiles.shape[0] // CHUNKS
    zeros = jnp.zeros((m_pad * CHUNKS, LANES), F32)
    return pl.pallas_call(
        _dispatch_kernel,
        grid_spec=pltpu.PrefetchScalarGridSpec(
            num_scalar_prefetch=1,
            grid=(seq // TOKEN_TILE,),
            in_specs=[pl.BlockSpec((TOKEN_TILE * CHUNKS, LANES), lambda i, d: (i, 0)),
                      pl.BlockSpec(memory_space=pl.ANY)],
            out_specs=pl.BlockSpec(memory_space=pl.ANY),
            scratch_shapes=[pltpu.SemaphoreType.DMA(())],
        ),
        out_shape=jax.ShapeDtypeStruct((m_pad * CHUNKS, LANES), F32),
        input_output_aliases={2: 0},
        compiler_params=_params(("arbitrary",)),
        name="dispatch",
    )(dest, h2_tiles, zeros)


def _expert_kernel(be_ref, nu_ref, xs_ref, w1_ref, b1_ref, w2_ref, b2_ref, ys_ref, w1b, w2b):
    b = pl.program_id(0)
    prev = be_ref[jnp.maximum(b - 1, 0)]
    fresh = (b == 0) | (be_ref[b] != prev)

    @pl.when(fresh)
    def _():
        w1b[...] = w1_ref[0].astype(BF16)
        w2b[...] = w2_ref[0].astype(BF16)

    @pl.when(b < nu_ref[0])
    def _():
        xb = jnp.concatenate([_load_token_chunk(xs_ref, EXPERT_ROWS, c) for c in range(CHUNKS)],
                             axis=1).astype(BF16)
        gu = _dot(xb, w1b[...]) + b1_ref[0]
        g = jnp.minimum(gu[:, :D_EXPERT], SWIGLU_LIMIT)
        u = jnp.clip(gu[:, D_EXPERT:], -SWIGLU_LIMIT, SWIGLU_LIMIT)
        y = (u + 1.0) * g * (1.0 / (1.0 + jnp.exp(-SWIGLU_ALPHA * g)))
        _store_token_tiles(ys_ref, _dot(y.astype(BF16), w2b[...]) + b2_ref[0])

    @pl.when(b >= nu_ref[0])
    def _():
        ys_ref[...] = jnp.zeros_like(ys_ref)


def _experts(blk_expert, n_used, xs, w1, b1, w2, b2):
    m_pad = xs.shape[0] // CHUNKS
    bm = EXPERT_ROWS
    n_all = w1.shape[0] * w1.shape[1]
    rows = lambda b, be, nu: (jnp.minimum(b, nu[0] - 1), 0)
    ex = lambda b, be, nu: (be[jnp.minimum(b, nu[0] - 1)], 0, 0)
    return pl.pallas_call(
        _expert_kernel,
        grid_spec=pltpu.PrefetchScalarGridSpec(
            num_scalar_prefetch=2,
            grid=(m_pad // bm,),
            in_specs=[
                pl.BlockSpec((bm * CHUNKS, LANES), rows),
                pl.BlockSpec((1, D_MODEL, 2 * D_EXPERT), ex),
                pl.BlockSpec((1, 1, 2 * D_EXPERT), ex),
                pl.BlockSpec((1, D_EXPERT, D_MODEL), ex),
                pl.BlockSpec((1, 1, D_MODEL), ex),
            ],
            out_specs=pl.BlockSpec((bm * CHUNKS, LANES), lambda b, be, nu: (b, 0)),
            scratch_shapes=[pltpu.VMEM((D_MODEL, 2 * D_EXPERT), BF16),
                            pltpu.VMEM((D_EXPERT, D_MODEL), BF16)],
        ),
        out_shape=jax.ShapeDtypeStruct((m_pad * CHUNKS, LANES), F32),
        compiler_params=_params(("arbitrary",)),
        name="experts",
    )(blk_expert, n_used, xs, w1.reshape(n_all, D_MODEL, 2 * D_EXPERT), b1.reshape(n_all, 1, -1),
      w2.reshape(n_all, D_EXPERT, D_MODEL), b2.reshape(n_all, 1, -1))


def _combine_kernel(dest_ref, ys_ref, x1_ref, gate_ref, mod_ref, o_ref, buf, sem):
    i = pl.program_id(0)
    tm = x1_ref.shape[0]
    n = tm * TOP_K
    base = i * n

    def issue(r, carry):
        for k in range(TOP_K):
            _tile_copy(ys_ref, dest_ref[base + r * TOP_K + k], buf, k * tm + r, sem).start()
        return carry

    lax.fori_loop(0, tm, issue, 0, unroll=2)
    pltpu.make_async_copy(ys_ref.at[pl.ds(0, n * CHUNKS), :], buf, sem).wait()
    gates = gate_ref[...]
    g2 = mod_ref[5]
    for c in range(CHUNKS):
        cols = slice(c * LANES, (c + 1) * LANES)
        mix = jnp.zeros((tm, LANES), F32)
        for k in range(TOP_K):
            mix = mix + gates[:, k:k + 1] * _load_token_chunk(buf, tm, c, offset=k * tm * CHUNKS)
        o_ref[:, cols] = x1_ref[:, cols] + g2[:, cols] * mix


def _combine(dest, ys, x1, gates, mod_l):
    seq = x1.shape[0]
    tm = TOKEN_TILE
    return pl.pallas_call(
        _combine_kernel,
        grid_spec=pltpu.PrefetchScalarGridSpec(
            num_scalar_prefetch=1,
            grid=(seq // tm,),
            in_specs=[
                pl.BlockSpec(memory_space=pl.ANY),
                pl.BlockSpec((tm, D_MODEL), lambda i, d: (i, 0)),
                pl.BlockSpec((tm, LANES), lambda i, d: (i, 0)),
                pl.BlockSpec(mod_l.shape, lambda i, d: (0, 0, 0)),
            ],
            out_specs=pl.BlockSpec((tm, D_MODEL), lambda i, d: (i, 0)),
            scratch_shapes=[pltpu.VMEM((TOP_K * tm * CHUNKS, LANES), F32),
                            pltpu.SemaphoreType.DMA(())],
        ),
        out_shape=jax.ShapeDtypeStruct((seq, D_MODEL), F32),
        compiler_params=_params(("arbitrary",)),
        name="combine",
    )(dest, ys, x1, gates, mod_l)


def _moe(l, x1, h2_tiles, eidx, gates, rank, counts, mod_l, w1, b1, w2, b2):
    seq = x1.shape[0]
    bm = EXPERT_ROWS
    m_pad = seq * TOP_K + N_EXPERTS * bm
    cnt = counts[0, :N_EXPERTS].astype(jnp.int32)
    padded = (cnt + bm - 1) // bm * bm
    pad_end = jnp.cumsum(padded)
    pad_start = pad_end - padded
    onehot = eidx[:, :TOP_K, None] == jnp.arange(N_EXPERTS, dtype=jnp.int32)
    start_of = jnp.sum(jnp.where(onehot, pad_start, 0), axis=-1)
    dest = (start_of + rank[:, :TOP_K]).reshape(seq * TOP_K).astype(jnp.int32)
    nblk = m_pad // bm
    blk_start = jnp.arange(nblk, dtype=jnp.int32) * bm
    blk_expert = jnp.minimum(jnp.sum(pad_end[None, :] <= blk_start[:, None], axis=1), N_EXPERTS - 1)
    blk_expert = (blk_expert + l * N_EXPERTS).astype(jnp.int32)
    n_used = (pad_end[-1:] // bm).astype(jnp.int32)
    xs = _dispatch(dest, h2_tiles, m_pad)
    ys = _experts(blk_expert, n_used, xs, w1, b1, w2, b2)
    return _combine(dest, ys, x1, gates, mod_l)


def _pad_cols(a, n):
    return jnp.pad(a, ((0, 0), (0, n - a.shape[1])))


def _layer_params(l, w_in, mla_cq_g, mla_w_uq, mla_ckv_g, mla_w_ukv, mla_q_g, mla_k_g,
                  fox_q_g, fox_k_g, fox_b_f, moba_q_g, moba_k_g, dil_q_g, dil_k_g, w_out,
                  norm1_g, norm2_g, router_w, router_b):
    w = w_in[l]
    sizes = [MLA_Q_RANK, MLA_KV_RANK, MLA_ROPE, GROUP, GROUP, GROUP, HEADS] + [GROUP] * 6
    offs = np.concatenate([[0], np.cumsum(sizes)])
    part = [w[:, offs[j]:offs[j + 1]] for j in range(len(sizes))]
    zeros = lambda n: jnp.zeros((D_MODEL, n), F32)
    w_in_r = jnp.concatenate(
        [part[0], part[1], zeros(MLA_NOPE), part[2], zeros(LANES - MLA_QK),
         part[3], part[4], part[6], zeros(LANES - HEADS),
         part[7], part[8], part[10], part[11], part[12]], axis=1).astype(BF16)
    assert w_in_r.shape[1] == COLS_IN
    w_vt = jnp.stack([part[5].T, part[9].T]).astype(BF16)
    w_uq = jnp.pad(mla_w_uq[l].reshape(MLA_Q_RANK, HEADS, MLA_QK),
                   ((0, 0), (0, 0), (0, LANES - MLA_QK))).reshape(MLA_Q_RANK, HEADS * LANES)
    w_ukv = mla_w_ukv[l].reshape(MLA_KV_RANK, HEADS, MLA_NOPE + HEAD_DIM)
    w_uk = jnp.pad(w_ukv[:, :, :MLA_NOPE], ((0, 0), (0, 0), (0, LANES - MLA_NOPE)))
    w_uv = w_ukv[:, :, MLA_NOPE:]
    tile4 = lambda g: jnp.tile(g, HEADS)[None, :]
    rw = _pad_cols(router_w[l], LANES)
    rw_hi = rw.astype(BF16)
    rw_lo = (rw - rw_hi.astype(F32)).astype(BF16)
    r_b = jnp.concatenate([router_b[l], jnp.full((LANES - N_EXPERTS,), NEG, F32)])[None, :]
    return dict(
        g1=norm1_g[l][None, :], g2=norm2_g[l][None, :], w_in=w_in_r, w_vt=w_vt,
        cq_g=mla_cq_g[l][None, :], w_uq=w_uq.astype(BF16), ckv_g=mla_ckv_g[l][None, :],
        w_uk=w_uk.reshape(MLA_KV_RANK, HEADS * LANES).astype(BF16),
        w_uvt=w_uv.reshape(MLA_KV_RANK, GROUP).T.astype(BF16),
        fox_bound=FOX_NORM_SLACK * HEAD_DIM ** 0.5 * jnp.max(jnp.abs(fox_q_g[l]))
        * jnp.max(jnp.abs(fox_k_g[l])),
        q_g=_pad_cols(mla_q_g[l][None, :], LANES), k_g=_pad_cols(mla_k_g[l][None, :], LANES),
        fq_g=tile4(fox_q_g[l]), fk_g=tile4(fox_k_g[l]), f_b=_pad_cols(fox_b_f[l][None, :], LANES),
        mq_g=tile4(moba_q_g[l]), mk_g=tile4(moba_k_g[l]), dq_g=tile4(dil_q_g[l]), dk_g=tile4(dil_k_g[l]),
        w_out=w_out[l].reshape(HEADS, GROUP, D_MODEL).astype(BF16),
        rw_hi=rw_hi, rw_lo=rw_lo, r_b=r_b,
    )


def kernel(x, c, w_mod, b_mod, norm1_g, norm2_g, w_in, mla_cq_g, mla_w_uq, mla_ckv_g, mla_w_ukv, mla_q_g, mla_k_g, fox_q_g, fox_k_g, fox_b_f, moba_q_g, moba_k_g, dil_q_g, dil_k_g, w_out, router_w, router_b, exp_w1, exp_b1, exp_w2, exp_b2):
    batch, seq, d = x.shape
    assert batch == 1 and d == D_MODEL
    assert seq % DIL_TILE == 0 and seq // MOBA_BLOCK <= MOBA_MAX_BLOCKS
    depth = w_mod.shape[0]
    consts = _prep_constants(seq, TOKEN_TILE)
    mod = _modulation(c, w_mod, b_mod)
    slopes_c = 2.0 ** (-(2.0 * np.arange(HEADS) + 2.0))
    in_block = np.arange(FLASH_TILE) % MOBA_BLOCK
    kbias_c = jnp.asarray(np.broadcast_to(slopes_c[:, None, None] * in_block[None, :, None],
                                          (HEADS, FLASH_TILE, LANES)), F32)
    kbias_0 = jnp.zeros((HEADS, FLASH_TILE, LANES), F32)
    all_tiles = jnp.zeros(((HEADS // 2) * (seq // FLASH_TILE),), jnp.int32)
    x2 = x.reshape(seq, d)
    for l in range(depth):
        p = _layer_params(l, w_in, mla_cq_g, mla_w_uq, mla_ckv_g, mla_w_ukv, mla_q_g, mla_k_g,
                          fox_q_g, fox_k_g, fox_b_f, moba_q_g, moba_k_g, dil_q_g, dil_k_g, w_out,
                          norm1_g, norm2_g, router_w, router_b)
        mod_l = mod[l]
        qa, ka, va, qf, kf, vf, qm, km, vm, qd, kd, vd, decay = _prep(x2, mod_l, consts, p)
        oa = _flash(qa, ka, va, kbias_0, all_tiles, False)
        ob = _flash(qf, kf, vf, kbias_0, _fox_first_tile(decay, p["fox_bound"]), False)
        oc = _flash(qm, km, vm, kbias_c, all_tiles, True)
        od = _dilated(qd, kd, vd)
        x1, h2, eidx, gates, rank, counts = _post(x2, oa, ob, oc, od, mod_l, consts, p)
        x2 = _moe(l, x1, h2, eidx, gates, rank, counts, mod_l, exp_w1, exp_b1, exp_w2, exp_b2)
    return x2.reshape(batch, seq, d)
```

```python
import functools

import numpy as np
import jax
import jax.numpy as jnp
from jax import lax
from jax.experimental import pallas as pl
from jax.experimental.pallas import tpu as pltpu

F32 = jnp.float32
BF16 = jnp.bfloat16

D_MODEL = 1024
HEAD_DIM = 64
HEADS = 4
GROUP = HEADS * HEAD_DIM
LANES = 128
CHUNKS = D_MODEL // LANES
MLA_Q_RANK = 256
MLA_KV_RANK = 128
MLA_NOPE = 64
MLA_ROPE = 32
MLA_QK = MLA_NOPE + MLA_ROPE
ROPE_THETA = 10000.0
MOBA_BLOCK = 256
MOBA_TOPK = 3
MOBA_MAX_BLOCKS = 64
DIL_PATTERNS = ((128, 1), (512, 4), (2048, 16))
DIL_SPAN = 128
DIL_TILE = 2048
N_EXPERTS = 32
TOP_K = 4
D_EXPERT = 1024
SWIGLU_LIMIT = 7.0
SWIGLU_ALPHA = 1.702
EPS = 1e-6
NEG = -1e30

FLASH_TILE = 512
ONES_ROWS = 16
FOX_SKIP_LOG = -106.0
FOX_NORM_SLACK = 1.02
TOKEN_TILE = 256
EXPERT_ROWS = 512
VMEM_LIMIT = 56 * 1024 * 1024

COLS_MLA = 512
COLS_FOX = 640
COLS_MOBA = 512
COLS_DIL = 768
COLS_IN = COLS_MLA + COLS_FOX + COLS_MOBA + COLS_DIL


def _dot(a, b):
    return jnp.dot(a, b, preferred_element_type=F32)


def _dot_nt(a, b):
    return lax.dot_general(a, b, (((1,), (1,)), ((), ())), preferred_element_type=F32)


def _split2(x):
    hi = x.astype(BF16)
    lo = (x - hi.astype(F32)).astype(BF16)
    return hi, lo


def _split3(x):
    a = x.astype(BF16)
    r = x - a.astype(F32)
    b = r.astype(BF16)
    c = (r - b.astype(F32)).astype(BF16)
    return a, b, c


def _head_of_lane():
    return jnp.right_shift(lax.broadcasted_iota(jnp.int32, (1, GROUP), 1), 6)


def _store_token_tiles(ref, x, offset=0):
    n = x.shape[0]
    for c in range(CHUNKS):
        ref[pl.ds(offset + c, n, stride=CHUNKS), :] = x[:, c * LANES:(c + 1) * LANES]


def _load_token_chunk(ref, n, c, offset=0):
    return ref[pl.ds(offset + c, n, stride=CHUNKS), :]


def _full_spec(shape):
    nd = len(shape)
    return pl.BlockSpec(shape, lambda *_: (0,) * nd)


def _params(sem):
    return pltpu.CompilerParams(dimension_semantics=sem, vmem_limit_bytes=VMEM_LIMIT)


def _mod_kernel(c_ref, w_ref, b_ref, o_ref):
    c = c_ref[...]
    s = c * (1.0 / (1.0 + jnp.exp(-c)))
    s8 = jnp.broadcast_to(s, (8, D_MODEL))
    r = jnp.dot(s8, w_ref[0], preferred_element_type=F32, precision=lax.Precision.HIGHEST)
    o_ref[0, 0] = r[0:1, :] + b_ref[0, 0]


def _modulation(c, w_mod, b_mod):
    depth = w_mod.shape[0]
    b4 = b_mod.reshape(depth, 6, 1, D_MODEL)
    return pl.pallas_call(
        _mod_kernel,
        grid=(depth, 6),
        in_specs=[
            pl.BlockSpec((1, D_MODEL), lambda l, j: (0, 0)),
            pl.BlockSpec((1, D_MODEL, D_MODEL), lambda l, j: (l, 0, j)),
            pl.BlockSpec((1, 1, 1, D_MODEL), lambda l, j: (l, j, 0, 0)),
        ],
        out_specs=pl.BlockSpec((1, 1, 1, D_MODEL), lambda l, j: (l, j, 0, 0)),
        out_shape=jax.ShapeDtypeStruct((depth, 6, 1, D_MODEL), F32),
        compiler_params=_params(("arbitrary", "arbitrary")),
        name="modulation",
    )(c, w_mod, b4)


def _head_norm(x, g, bd):
    hi, lo = _split2(x * x)
    ss = _dot(hi, bd) + _dot(lo, bd)
    return x * lax.rsqrt(ss * (1.0 / HEAD_DIM) + EPS) * g


def _prep_kernel(x_ref, mod_ref, g1_ref, win_ref, wvt_ref, cqg_ref, wuq_ref, ckvg_ref, wuk_ref,
                 wuvt_ref, qg_ref, kg_ref, rc_ref, rs1_ref, rs2_ref,
                 fqg_ref, fkg_ref, fb_ref, mqg_ref, mkg_ref, dqg_ref, dkg_ref,
                 bd_ref, tri_ref, eq_ref, ek_ref, sel_ref,
                 qa_ref, ka_ref, va_ref, qf_ref, kf_ref, vf_ref, qm_ref, km_ref, vm_ref,
                 qd_ref, kd_ref, vd_ref, f_ref,
                 fcarry, kmean):
    i = pl.program_id(0)
    tm = x_ref.shape[0]

    @pl.when(i == 0)
    def _():
        fcarry[...] = jnp.zeros_like(fcarry)
        kmean[...] = jnp.zeros_like(kmean)

    x = x_ref[...]
    y = x * lax.rsqrt(jnp.mean(x * x, axis=-1, keepdims=True) + EPS) * g1_ref[...]
    hb = (y * (1.0 + mod_ref[1]) + mod_ref[0]).astype(BF16)
    bd = bd_ref[...]
    lane = lax.broadcasted_iota(jnp.int32, (1, LANES), 1)
    lane_f = lane.astype(F32)
    head_of_lane = _head_of_lane()

    pa = _dot(hb, win_ref[:, 0:COLS_MLA])
    cq = pa[:, 0:MLA_Q_RANK]
    ckv = pa[:, MLA_Q_RANK:MLA_Q_RANK + MLA_KV_RANK]
    kr = pa[:, MLA_Q_RANK + MLA_KV_RANK:COLS_MLA]
    cqn = (cq * lax.rsqrt(jnp.mean(cq * cq, axis=-1, keepdims=True) + EPS) * cqg_ref[...]).astype(BF16)
    ckvn = (ckv * lax.rsqrt(jnp.mean(ckv * ckv, axis=-1, keepdims=True) + EPS) * ckvg_ref[...]).astype(BF16)
    q_all = _dot(cqn, wuq_ref[...])
    k_all = _dot(ckvn, wuk_ref[...])
    va_ref[...] = _dot_nt(wuvt_ref[...], ckvn).astype(BF16)
    rc, rs1, rs2 = rc_ref[...], rs1_ref[...], rs2_ref[...]

    def rope(t):
        return t * rc + pltpu.roll(t, LANES - MLA_ROPE // 2, 1) * rs1 + pltpu.roll(t, MLA_ROPE // 2, 1) * rs2

    for h in range(HEADS):
        q = q_all[:, h * LANES:(h + 1) * LANES]
        q = q * lax.rsqrt(jnp.sum(q * q, axis=-1, keepdims=True) * (1.0 / MLA_QK) + EPS) * qg_ref[...]
        qa_ref[h] = (rope(q) * (MLA_QK ** -0.5)).astype(BF16)
        k = k_all[:, h * LANES:(h + 1) * LANES] + kr
        k = k * lax.rsqrt(jnp.sum(k * k, axis=-1, keepdims=True) * (1.0 / MLA_QK) + EPS) * kg_ref[...]
        ka_ref[h] = rope(k).astype(BF16)

    pf = _dot(hb, win_ref[:, COLS_MLA:COLS_MLA + COLS_FOX])
    fqn = (_head_norm(pf[:, 0:GROUP], fqg_ref[...], bd) * (HEAD_DIM ** -0.5)).astype(BF16)
    fkn = _head_norm(pf[:, GROUP:2 * GROUP], fkg_ref[...], bd).astype(BF16)
    vf_ref[...] = _dot_nt(wvt_ref[0], hb).astype(BF16)
    z = pf[:, 2 * GROUP:2 * GROUP + LANES] + fb_ref[...]
    log_f = jnp.minimum(z, 0.0) - jnp.log(1.0 + jnp.exp(-jnp.abs(z)))
    tri = tri_ref[...]
    a1, a2, a3 = _split3(log_f)
    cum = fcarry[...] + (_dot(tri, a1) + _dot(tri, a2) + _dot(tri, a3))
    fcarry[...] = cum[tm - 1:tm, :]
    f_ref[...] = cum
    f1, f2, f3 = _split3(cum)
    xq = jnp.concatenate([fqn, f1, f2, f3], axis=1)
    xk = jnp.concatenate([fkn, f1, f2, f3], axis=1)
    ones_q = jnp.where((lane >= HEAD_DIM + 3) & (lane < HEAD_DIM + 6), 1.0, 0.0)
    ones_k = jnp.where((lane >= HEAD_DIM) & (lane < HEAD_DIM + 3), 1.0, 0.0)
    for h in range(HEADS):
        qf_ref[h] = (_dot(xq, eq_ref[h]) + ones_q).astype(BF16)
        kf_ref[h] = (_dot(xk, ek_ref[h]) + ones_k).astype(BF16)

    pm = _dot(hb, win_ref[:, COLS_MLA + COLS_FOX:COLS_MLA + COLS_FOX + COLS_MOBA])
    mqn = _head_norm(pm[:, 0:GROUP], mqg_ref[...], bd) * (HEAD_DIM ** -0.5)
    mkn = _head_norm(pm[:, GROUP:2 * GROUP], mkg_ref[...], bd)
    vm_ref[...] = _dot_nt(wvt_ref[1], hb).astype(BF16)
    col_mean = jnp.mean(mkn, axis=0, keepdims=True)
    mqb = mqn.astype(BF16)
    mkb = mkn.astype(BF16)
    blk = lane - HEAD_DIM
    blk_f = blk.astype(F32)
    past = (blk >= 0) & (blk < i)
    i_f = i.astype(F32)
    for h in range(HEADS):
        kmean[h, pl.ds(HEAD_DIM + i, 1), :] = jnp.where(head_of_lane == h, col_mean, 0.0)
        qh_hi, qh_lo = _split2(jnp.where(head_of_lane == h, mqn, 0.0))
        km_hi, km_lo = _split2(kmean[h])
        gate = _dot_nt(qh_hi, km_hi) + _dot_nt(qh_hi, km_lo) + _dot_nt(qh_lo, km_hi)
        g = jnp.where(past, gate, NEG)
        chosen = jnp.zeros((tm, LANES), F32)
        for _ in range(MOBA_TOPK):
            m = jnp.max(g, axis=-1, keepdims=True)
            first = jnp.min(jnp.where(g == m, lane_f, 1e9), axis=-1, keepdims=True)
            pick = (lane_f == first) & (m > NEG)
            chosen = jnp.where(pick, 1.0, chosen)
            g = jnp.where(pick, NEG, g)
        slope = 2.0 ** (-(2 * h + 2))
        keep = (chosen > 0.0) | (blk == i)
        bias = jnp.where(keep, (slope * MOBA_BLOCK) * (blk_f - i_f), NEG)
        bias = jnp.where(blk >= 0, bias, 0.0)
        qm_ref[h] = (_dot(mqb, sel_ref[h]) + bias).astype(BF16)
        onehot = jnp.where(blk == i, 1.0, 0.0)
        km_ref[h] = (_dot(mkb, sel_ref[h]) + onehot).astype(BF16)

    pd = _dot(hb, win_ref[:, COLS_MLA + COLS_FOX + COLS_MOBA:COLS_IN])
    qd_ref[...] = _head_norm(pd[:, 0:GROUP], dqg_ref[...], bd) * (HEAD_DIM ** -0.5)
    kd_ref[...] = _head_norm(pd[:, GROUP:2 * GROUP], dkg_ref[...], bd)
    vd_ref[...] = pd[:, 2 * GROUP:3 * GROUP]


def _prep_constants(seq, tm):
    half = MLA_ROPE // 2
    inv = 1.0 / (ROPE_THETA ** (jnp.arange(half, dtype=F32) / half))
    ang = jnp.arange(seq, dtype=F32)[:, None] * inv[None, :]
    cos, sin = jnp.cos(ang), jnp.sin(ang)
    z = lambda n: jnp.zeros((seq, n), F32)
    rc = jnp.concatenate([jnp.ones((seq, MLA_NOPE), F32), cos, cos, z(LANES - MLA_QK)], axis=1)
    rs1 = jnp.concatenate([z(MLA_NOPE), -sin, z(LANES - MLA_NOPE - half)], axis=1)
    rs2 = jnp.concatenate([z(MLA_NOPE + half), sin, z(LANES - MLA_QK)], axis=1)
    bd =np.kron(np.eye(HEADS, dtype=np.float32), np.ones((HEAD_DIM, HEAD_DIM), np.float32))
    tri = np.tril(np.ones((tm, tm), np.float32))
    tri_strict = np.tril(np.ones((tm, tm), np.float32), -1)
    sel = np.zeros((HEADS, GROUP, LANES), np.float32)
    eq = np.zeros((HEADS, GROUP + 3 * LANES, LANES), np.float32)
    ek = np.zeros((HEADS, GROUP + 3 * LANES, LANES), np.float32)
    for h in range(HEADS):
        for d in range(HEAD_DIM):
            sel[h, h * HEAD_DIM + d, d] = 1.0
        eq[h, :GROUP] = sel[h]
        ek[h, :GROUP] = sel[h]
        for piece in range(3):
            eq[h, GROUP + piece * LANES + h, HEAD_DIM + piece] = 1.0
            ek[h, GROUP + piece * LANES + h, HEAD_DIM + 3 + piece] = -1.0
    as_bf = lambda a: jnp.asarray(a, BF16)
    return dict(rc=rc, rs1=rs1, rs2=rs2, bd=as_bf(bd),
                tri=as_bf(tri), tri_strict=as_bf(tri_strict), sel=as_bf(sel), eq=as_bf(eq), ek=as_bf(ek))


def _prep(x2, mod_l, consts, p):
    seq = x2.shape[0]
    tm = TOKEN_TILE
    row = lambda n: pl.BlockSpec((tm, n), lambda i: (i, 0))
    heads = pl.BlockSpec((HEADS, tm, LANES), lambda i: (0, i, 0))
    in_arrays = [
        (x2, row(D_MODEL)), (mod_l, _full_spec(mod_l.shape)), (p["g1"], None), (p["w_in"], None),
        (p["w_vt"], None),
        (p["cq_g"], None), (p["w_uq"], None), (p["ckv_g"], None), (p["w_uk"], None), (p["w_uvt"], None),
        (p["q_g"], None), (p["k_g"], None),
        (consts["rc"], row(LANES)), (consts["rs1"], row(LANES)), (consts["rs2"], row(LANES)),
        (p["fq_g"], None), (p["fk_g"], None), (p["f_b"], None), (p["mq_g"], None), (p["mk_g"], None),
        (p["dq_g"], None), (p["dk_g"], None),
        (consts["bd"], None), (consts["tri"], None), (consts["eq"], None), (consts["ek"], None),
        (consts["sel"], None),
    ]
    args = [a for a, _ in in_arrays]
    specs = [s if s is not None else _full_spec(a.shape) for a, s in in_arrays]
    hshape = jax.ShapeDtypeStruct((HEADS, seq, LANES), BF16)
    vshape = jax.ShapeDtypeStruct((GROUP, seq), BF16)
    dshape = jax.ShapeDtypeStruct((seq, GROUP), F32)
    vt = pl.BlockSpec((GROUP, tm), lambda i: (0, i))
    return pl.pallas_call(
        _prep_kernel,
        grid=(seq // tm,),
        in_specs=specs,
        out_specs=[heads, heads, vt] * 3 + [row(GROUP)] * 3 + [row(LANES)],
        out_shape=[hshape, hshape, vshape] * 3 + [dshape] * 3
                  + [jax.ShapeDtypeStruct((seq, LANES), F32)],
        scratch_shapes=[pltpu.VMEM((1, LANES), F32), pltpu.VMEM((HEADS, LANES, GROUP), F32)],
        compiler_params=_params(("arbitrary",)),
        name="prep",
    )(*args)


def _flash_kernel(jlo_ref, q_ref, k_ref, vt_ref, kbias_ref, o_ref, m_sc, acc_sc, sa_sc, sb_sc,
                  *, tile, use_kbias):
    i = pl.program_id(0)
    m_sc[...] = jnp.full_like(m_sc, -jnp.inf)
    acc_sc[...] = jnp.zeros_like(acc_sc)

    def score(hh, j, buf):
        start = pl.multiple_of(j * tile, tile)
        buf[hh] = _dot_nt(k_ref[hh, pl.ds(start, tile), :], q_ref[hh])

    def absorb(hh, j, buf, causal):
        start = pl.multiple_of(j * tile, tile)
        s = buf[hh]
        if use_kbias:
            s = s + jnp.concatenate([kbias_ref[hh]] * (tile // LANES), axis=1)
        if causal:
            key = lax.broadcasted_iota(jnp.int32, (tile, tile), 0)
            qry = lax.broadcasted_iota(jnp.int32, (tile, tile), 1)
            s = jnp.where(key <= qry, s, NEG)
        m_prev = m_sc[hh]
        m_cur = jnp.max(jnp.max(s.reshape(8, tile // 8, tile), axis=0), axis=0, keepdims=True)
        m_new = jnp.maximum(m_prev, m_cur)
        alpha = jnp.exp(m_prev - m_new)
        p = jnp.exp((s - m_new).astype(BF16))
        vt = vt_ref[hh * HEAD_DIM:(hh + 1) * HEAD_DIM, pl.ds(start, tile)]
        vt = jnp.concatenate([vt, jnp.ones((ONES_ROWS, tile), BF16)], axis=0)
        acc_sc[hh] = alpha * acc_sc[hh] + _dot(vt, p)
        m_sc[hh] = m_new

    first = jlo_ref[i]
    n_off = i - first
    for hh in range(HEADS):
        score(hh, first, sa_sc)

    def body(t, carry):
        j = first + 2 * t
        for hh in range(HEADS):
            score(hh, j + 1, sb_sc)
            absorb(hh, j, sa_sc, False)
        for hh in range(HEADS):
            score(hh, j + 2, sa_sc)
            absorb(hh, j + 1, sb_sc, False)
        return carry

    lax.fori_loop(0, n_off // 2, body, 0)

    @pl.when(n_off % 2 == 1)
    def _():
        for hh in range(HEADS):
            score(hh, i, sb_sc)
            absorb(hh, i - 1, sa_sc, False)
        for hh in range(HEADS):
            absorb(hh, i, sb_sc, True)

    @pl.when(n_off % 2 == 0)
    def _():
        for hh in range(HEADS):
            absorb(hh, i, sa_sc, True)

    o_t = jnp.concatenate([acc_sc[hh, 0:HEAD_DIM, :] / acc_sc[hh, HEAD_DIM:HEAD_DIM + 1, :]
                           for hh in range(HEADS)], axis=0)
    o_ref[...] = o_t.T.astype(o_ref.dtype)


def _flash(q, k, v_t, kbias, first_tile, use_kbias):
    seq = v_t.shape[1]
    tile = FLASH_TILE
    kern = functools.partial(_flash_kernel, tile=tile, use_kbias=use_kbias)
    resident = pl.Buffered(1)
    return pl.pallas_call(
        kern,
        grid_spec=pltpu.PrefetchScalarGridSpec(
            num_scalar_prefetch=1,
            grid=(seq // tile,),
            in_specs=[
                pl.BlockSpec((HEADS, tile, LANES), lambda i, f: (0, i, 0)),
                pl.BlockSpec((HEADS, seq, LANES), lambda i, f: (0, 0, 0), pipeline_mode=resident),
                pl.BlockSpec((GROUP, seq), lambda i, f: (0, 0), pipeline_mode=resident),
                pl.BlockSpec((HEADS, tile, LANES), lambda i, f: (0, 0, 0)),
            ],
            out_specs=pl.BlockSpec((tile, GROUP), lambda i, f: (i, 0)),
            scratch_shapes=[pltpu.VMEM((HEADS, 1, tile), F32),
                            pltpu.VMEM((HEADS, HEAD_DIM + ONES_ROWS, tile), F32),
                            pltpu.VMEM((HEADS, tile, tile), F32), pltpu.VMEM((HEADS, tile, tile), F32)],
        ),
        out_shape=jax.ShapeDtypeStruct((seq, GROUP), BF16),
        compiler_params=_params(("arbitrary",)),
        name="flash_kbias" if use_kbias else "flash",
    )(first_tile, q, k, v_t, kbias)


def _fox_first_tile(decay, qk_bound):
    seq = decay.shape[0]
    nq = seq // FLASH_TILE
    f = decay[:, :HEADS]
    f_first = f[0::FLASH_TILE]
    f_last = f[FLASH_TILE - 1::FLASH_TILE]
    gap = f_first[:, None, :] - f_last[None, :, :] + 2.0 * qk_bound
    jj = jnp.arange(nq, dtype=jnp.int32)
    needed = (gap >= FOX_SKIP_LOG) | (jj[None, :, None] >= jj[:, None, None])
    first = jnp.min(jnp.where(needed, jj[None, :, None], nq), axis=1)
    return jnp.min(first, axis=1).astype(jnp.int32)


def _dilated_kernel(q_ref, k_ref, v_ref, o_ref, kbuf, vbuf, acc_s, m_s, l_s):
    pair = pl.program_id(0)
    i = pl.program_id(1)
    T = q_ref.shape[0]

    @pl.when(i == 0)
    def _():
        kbuf[...] = jnp.zeros_like(kbuf)
        vbuf[...] = jnp.zeros_like(vbuf)

    kbuf[0:T, :] = kbuf[T:2 * T, :]
    vbuf[0:T, :] = vbuf[T:2 * T, :]
    kbuf[T:2 * T, :] = k_ref[...]
    vbuf[T:2 * T, :] = v_ref[...]

    ii = lax.broadcasted_iota(jnp.int32, (DIL_SPAN, 2 * DIL_SPAN), 0)
    jj = lax.broadcasted_iota(jnp.int32, (DIL_SPAN, 2 * DIL_SPAN), 1)
    dist = ii + DIL_SPAN - jj
    band = (dist >= 0) & (dist <= DIL_SPAN)
    dist_f = dist.astype(F32)
    upper = lax.broadcasted_iota(jnp.int32, (1, LANES), 1) >= HEAD_DIM

    for pi, (window, r) in enumerate(DIL_PATTERNS):
        assert window // r == DIL_SPAN
        sub = DIL_SPAN * r

        def body(idx, carry, r=r, sub=sub, pi=pi):
            n = idx // r
            rho = idx - n * r
            base = n * sub + rho
            q = q_ref[pl.ds(base, DIL_SPAN, stride=r), :]
            kc = kbuf[pl.ds(T + base - sub, 2 * DIL_SPAN, stride=r), :].astype(BF16)
            vc = vbuf[pl.ds(T + base - sub, 2 * DIL_SPAN, stride=r), :].astype(BF16)
            first_key = jnp.where((i == 0) & (n == 0), DIL_SPAN, 0)
            valid = band & (jj >= first_key)
            stats = []
            for hh in range(2):
                slope = jnp.where(pair == 0, 2.0 ** (-(2 * hh + 1)), 2.0 ** (-(2 * hh + 5)))
                qh = (jnp.where(upper, q, 0.0) if hh else jnp.where(upper, 0.0, q)).astype(BF16)
                s = _dot_nt(qh, kc) - (slope * r) * dist_f
                s = jnp.where(valid, s, NEG)
                m = jnp.max(s, axis=-1, keepdims=True)
                p = jnp.exp(s - m)
                l = jnp.sum(p, axis=-1, keepdims=True)
                stats.append((_dot(p.astype(BF16), vc), m, l))
            rows = pl.ds(pi * T + base, DIL_SPAN, stride=r)
            acc_s[rows, :] = jnp.where(upper, stats[1][0], stats[0][0])
            m_s[rows, :] = jnp.where(upper, stats[1][1], stats[0][1])
            l_s[rows, :] = jnp.where(upper, stats[1][2], stats[0][2])
            return carry

        lax.fori_loop(0, T // DIL_SPAN, body, 0, unroll=4)

    npat = len(DIL_PATTERNS)
    ms = [m_s[pi * T:(pi + 1) * T, :] for pi in range(npat)]
    m_top = functools.reduce(jnp.maximum, ms)
    num = jnp.zeros((T, LANES), F32)
    den = jnp.zeros((T, LANES), F32)
    for pi in range(npat):
        w = jnp.exp(ms[pi] - m_top)
        num = num + w * acc_s[pi * T:(pi + 1) * T, :]
        den = den + w * l_s[pi * T:(pi + 1) * T, :]
    o_ref[...] = (num / den).astype(o_ref.dtype)


def _dilated(q, k, v):
    seq = q.shape[0]
    T = DIL_TILE
    spec = pl.BlockSpec((T, LANES), lambda p, i: (i, p))
    npat = len(DIL_PATTERNS)
    return pl.pallas_call(
        _dilated_kernel,
        grid=(HEADS // 2, seq // T),
        in_specs=[spec, spec, spec],
        out_specs=spec,
        out_shape=jax.ShapeDtypeStruct((seq, GROUP), BF16),
        scratch_shapes=[pltpu.VMEM((2 * T, LANES), F32), pltpu.VMEM((2 * T, LANES), F32),
                        pltpu.VMEM((npat * T, LANES), F32), pltpu.VMEM((npat * T, LANES), F32),
                        pltpu.VMEM((npat * T, LANES), F32)],
        compiler_params=_params(("arbitrary", "arbitrary")),
        name="dilated",
    )(q, k, v)


def _post_kernel(x_ref, oa_ref, ob_ref, oc_ref, od_ref, wout_ref, mod_ref, g2_ref,
                 rwh_ref, rwl_ref, rb_ref, tri_ref,
                 x1_ref, h2_ref, eidx_ref, gate_ref, rank_ref, cnt_ref, carry):
    i = pl.program_id(0)
    tm = x_ref.shape[0]

    @pl.when(i == 0)
    def _():
        carry[...] = jnp.zeros_like(carry)

    o = (_dot(oa_ref[...], wout_ref[0]) + _dot(ob_ref[...], wout_ref[1])
         + _dot(oc_ref[...], wout_ref[2]) + _dot(od_ref[...], wout_ref[3]))
    x1 = x_ref[...] + mod_ref[2] * o
    x1_ref[...] = x1
    y = x1 * lax.rsqrt(jnp.mean(x1 * x1, axis=-1, keepdims=True) + EPS) * g2_ref[...]
    h2 = y * (1.0 + mod_ref[4]) + mod_ref[3]
    _store_token_tiles(h2_ref, h2)

    h_hi, h_lo = _split2(h2)
    logits = (_dot(h_hi, rwh_ref[...]) + _dot(h_hi, rwl_ref[...]) + _dot(h_lo, rwh_ref[...])
              + rb_ref[...])
    lane = lax.broadcasted_iota(jnp.int32, (tm, LANES), 1)
    lane_f = lane.astype(F32)
    g = logits
    chosen = jnp.zeros((tm, LANES), F32)
    vals, idxs = [], []
    for _ in range(TOP_K):
        m = jnp.max(g, axis=-1, keepdims=True)
        first = jnp.min(jnp.where(g == m, lane_f, 1e9), axis=-1, keepdims=True)
        pick = lane_f == first
        chosen = jnp.where(pick, 1.0, chosen)
        g = jnp.where(pick, -jnp.inf, g)
        vals.append(m)
        idxs.append(first)
    exps = [jnp.exp(v - vals[0]) for v in vals]
    den = exps[0] + exps[1] + exps[2] + exps[3]
    before = _dot(tri_ref[...], chosen.astype(BF16)) + carry[...]
    carry[...] = carry[...] + jnp.sum(chosen, axis=0, keepdims=True)
    cnt_ref[...] = carry[...]
    e_out = jnp.zeros((tm, LANES), F32)
    g_out = jnp.zeros((tm, LANES), F32)
    r_out = jnp.zeros((tm, LANES), F32)
    for k in range(TOP_K):
        rank_k = jnp.sum(jnp.where(lane_f == idxs[k], before, 0.0), axis=-1, keepdims=True)
        e_out = jnp.where(lane == k, idxs[k], e_out)
        g_out = jnp.where(lane == k, exps[k] / den, g_out)
        r_out = jnp.where(lane == k, rank_k, r_out)
    eidx_ref[...] = e_out.astype(jnp.int32)
    gate_ref[...] = g_out
    rank_ref[...] = r_out.astype(jnp.int32)


def _post(x2, oa, ob, oc, od, mod_l, consts, p):
    seq = x2.shape[0]
    tm = TOKEN_TILE
    row = lambda n: pl.BlockSpec((tm, n), lambda i: (i, 0))
    full = [p["w_out"], mod_l, p["g2"], p["rw_hi"], p["rw_lo"], p["r_b"], consts["tri_strict"]]
    f32 = lambda n: jax.ShapeDtypeStruct((seq, n), F32)
    i32 = lambda n: jax.ShapeDtypeStruct((seq, n), jnp.int32)
    return pl.pallas_call(
        _post_kernel,
        grid=(seq // tm,),
        in_specs=[row(D_MODEL)] + [row(GROUP)] * 4 + [_full_spec(a.shape) for a in full],
        out_specs=[row(D_MODEL), pl.BlockSpec((tm * CHUNKS, LANES), lambda i: (i, 0)),
                   row(LANES), row(LANES), row(LANES), _full_spec((1, LANES))],
        out_shape=[f32(D_MODEL), jax.ShapeDtypeStruct((seq * CHUNKS, LANES), F32),
                   i32(LANES), f32(LANES), i32(LANES), jax.ShapeDtypeStruct((1, LANES), F32)],
        scratch_shapes=[pltpu.VMEM((1, LANES), F32)],
        compiler_params=_params(("arbitrary",)),
        name="post",
    )(x2, oa, ob, oc, od, *full)


def _tile_copy(src, s, dst, d, sem):
    return pltpu.make_async_copy(src.at[pl.ds(pl.multiple_of(s * CHUNKS, CHUNKS), CHUNKS), :],
                                 dst.at[pl.ds(pl.multiple_of(d * CHUNKS, CHUNKS), CHUNKS), :], sem)


def _dispatch_kernel(dest_ref, h_ref, xs_in_ref, xs_ref, sem):
    del xs_in_ref
    i = pl.program_id(0)
    n = TOKEN_TILE * TOP_K
    base = i * n

    def issue(r, carry):
        for k in range(TOP_K):
            _tile_copy(h_ref, r, xs_ref, dest_ref[base + r * TOP_K + k], sem).start(priority=k % 2)
        return carry

    lax.fori_loop(0, TOKEN_TILE, issue, 0, unroll=2)
    rows = pl.ds(0, n * CHUNKS)
    pltpu.make_async_copy(xs_ref.at[rows, :], xs_ref.at[rows, :], sem).wait()


def _dispatch(dest, h2_tiles, m_pad):
    seq = h2_tiles.shape[0] // CHUNKS
    zeros = jnp.zeros((m_pad * CHUNKS, LANES), F32)
    return pl.pallas_call(
        _dispatch_kernel,
        grid_spec=pltpu.PrefetchScalarGridSpec(
            num_scalar_prefetch=1,
            grid=(seq // TOKEN_TILE,),
            in_specs=[pl.BlockSpec((TOKEN_TILE * CHUNKS, LANES), lambda i, d: (i, 0)),
                      pl.BlockSpec(memory_space=pl.ANY)],
            out_specs=pl.BlockSpec(memory_space=pl.ANY),
            scratch_shapes=[pltpu.SemaphoreType.DMA(())],
        ),
        out_shape=jax.ShapeDtypeStruct((m_pad * CHUNKS, LANES), F32),
        input_output_aliases={2: 0},
        compiler_params=_params(("arbitrary",)),
        name="dispatch",
    )(dest, h2_tiles, zeros)


def _expert_kernel(be_ref, nu_ref, xs_ref, w1_ref, b1_ref, w2_ref, b2_ref, ys_ref, w1b, w2b):
    b = pl.program_id(0)
    prev = be_ref[jnp.maximum(b - 1, 0)]
    fresh = (b == 0) | (be_ref[b] != prev)

    @pl.when(fresh)
    def _():
        w1b[...] = w1_ref[0].astype(BF16)
        w2b[...] = w2_ref[0].astype(BF16)

    @pl.when(b < nu_ref[0])
    def _():
        half = EXPERT_ROWS // 2
        gus = []
        for r in range(2):
            xb = jnp.concatenate(
                [_load_token_chunk(xs_ref, half, c, offset=r * half * CHUNKS) for c in range(CHUNKS)],
                axis=1).astype(BF16)
            gus.append(_dot(xb, w1b[...]) + b1_ref[0])
        for r in range(2):
            g = jnp.minimum(gus[r][:, :D_EXPERT], SWIGLU_LIMIT)
            u = jnp.clip(gus[r][:, D_EXPERT:], -SWIGLU_LIMIT, SWIGLU_LIMIT)
            y = (u + 1.0) * g * (1.0 / (1.0 + jnp.exp(-SWIGLU_ALPHA * g)))
            _store_token_tiles(ys_ref, _dot(y.astype(BF16), w2b[...]) + b2_ref[0],
                               offset=r * half * CHUNKS)

    @pl.when(b >= nu_ref[0])
    def _():
        ys_ref[...] = jnp.zeros_like(ys_ref)


def _experts(blk_expert, n_used, xs, w1, b1, w2, b2):
    m_pad = xs.shape[0] // CHUNKS
    bm = EXPERT_ROWS
    n_all = w1.shape[0] * w1.shape[1]
    rows = lambda b, be, nu: (jnp.minimum(b, nu[0] - 1), 0)
    ex = lambda b, be, nu: (be[jnp.minimum(b, nu[0] - 1)], 0, 0)
    return pl.pallas_call(
        _expert_kernel,
        grid_spec=pltpu.PrefetchScalarGridSpec(
            num_scalar_prefetch=2,
            grid=(m_pad // bm,),
            in_specs=[
                pl.BlockSpec((bm * CHUNKS, LANES), rows),
                pl.BlockSpec((1, D_MODEL, 2 * D_EXPERT), ex),
                pl.BlockSpec((1, 1, 2 * D_EXPERT), ex),
                pl.BlockSpec((1, D_EXPERT, D_MODEL), ex),
                pl.BlockSpec((1, 1, D_MODEL), ex),
            ],
            out_specs=pl.BlockSpec((bm * CHUNKS, LANES), lambda b, be, nu: (b, 0)),
            scratch_shapes=[pltpu.VMEM((D_MODEL, 2 * D_EXPERT), BF16),
                            pltpu.VMEM((D_EXPERT, D_MODEL), BF16)],
        ),
        out_shape=jax.ShapeDtypeStruct((m_pad * CHUNKS, LANES), F32),
        compiler_params=_params(("arbitrary",)),
        name="experts",
    )(blk_expert, n_used, xs, w1.reshape(n_all, D_MODEL, 2 * D_EXPERT), b1.reshape(n_all, 1, -1),
      w2.reshape(n_all, D_EXPERT, D_MODEL), b2.reshape(n_all, 1, -1))


def _combine_kernel(dest_ref, ys_ref, x1_ref, gate_ref, mod_ref, o_ref, buf, sem):
    i = pl.program_id(0)
    tm = x1_ref.shape[0]
    n = tm * TOP_K
    base = i * n

    def issue(r, carry):
        for k in range(TOP_K):
            _tile_copy(ys_ref, dest_ref[base + r * TOP_K + k], buf, k * tm + r, sem).start(
                priority=k % 2)
        return carry

    lax.fori_loop(0, tm, issue, 0, unroll=2)
    pltpu.make_async_copy(ys_ref.at[pl.ds(0, n * CHUNKS), :], buf, sem).wait()
    gates = gate_ref[...]
    g2 = mod_ref[5]
    for c in range(CHUNKS):
        cols = slice(c * LANES, (c + 1) * LANES)
        mix = jnp.zeros((tm, LANES), F32)
        for k in range(TOP_K):
            mix = mix + gates[:, k:k + 1] * _load_token_chunk(buf, tm, c, offset=k * tm * CHUNKS)
        o_ref[:, cols] = x1_ref[:, cols] + g2[:, cols] * mix


def _combine(dest, ys, x1, gates, mod_l):
    seq = x1.shape[0]
    tm = TOKEN_TILE
    return pl.pallas_call(
        _combine_kernel,
        grid_spec=pltpu.PrefetchScalarGridSpec(
            num_scalar_prefetch=1,
            grid=(seq // tm,),
            in_specs=[
                pl.BlockSpec(memory_space=pl.ANY),
                pl.BlockSpec((tm, D_MODEL), lambda i, d: (i, 0)),
                pl.BlockSpec((tm, LANES), lambda i, d: (i, 0)),
                pl.BlockSpec(mod_l.shape, lambda i, d: (0, 0, 0)),
            ],
            out_specs=pl.BlockSpec((tm, D_MODEL), lambda i, d: (i, 0)),
            scratch_shapes=[pltpu.VMEM((TOP_K * tm * CHUNKS, LANES), F32),
                            pltpu.SemaphoreType.DMA(())],
        ),
        out_shape=jax.ShapeDtypeStruct((seq, D_MODEL), F32),
        compiler_params=_params(("arbitrary",)),
        name="combine",
    )(dest, ys, x1, gates, mod_l)


def _moe(l, x1, h2_tiles, eidx, gates, rank, counts, mod_l, w1, b1, w2, b2):
    seq = x1.shape[0]
    bm = EXPERT_ROWS
    m_pad = seq * TOP_K + N_EXPERTS * bm
    cnt = counts[0, :N_EXPERTS].astype(jnp.int32)
    padded = (cnt + bm - 1) // bm * bm
    pad_end = jnp.cumsum(padded)
    pad_start = pad_end - padded
    onehot = eidx[:, :TOP_K, None] == jnp.arange(N_EXPERTS, dtype=jnp.int32)
    start_of = jnp.sum(jnp.where(onehot, pad_start, 0), axis=-1)
    dest = (start_of + rank[:, :TOP_K]).reshape(seq * TOP_K).astype(jnp.int32)
    nblk = m_pad // bm
    blk_start = jnp.arange(nblk, dtype=jnp.int32) * bm
    blk_expert = jnp.minimum(jnp.sum(pad_end[None, :] <= blk_start[:, None], axis=1), N_EXPERTS - 1)
    blk_expert = (blk_expert + l * N_EXPERTS).astype(jnp.int32)
    n_used = (pad_end[-1:] // bm).astype(jnp.int32)
    xs = _dispatch(dest, h2_tiles, m_pad)
    ys = _experts(blk_expert, n_used, xs, w1, b1, w2, b2)
    return _combine(dest, ys, x1, gates, mod_l)


def _pad_cols(a, n):
    return jnp.pad(a, ((0, 0), (0, n - a.shape[1])))


def _layer_params(l, w_in, mla_cq_g, mla_w_uq, mla_ckv_g, mla_w_ukv, mla_q_g, mla_k_g,
                  fox_q_g, fox_k_g, fox_b_f, moba_q_g, moba_k_g, dil_q_g, dil_k_g, w_out,
                  norm1_g, norm2_g, router_w, router_b):
    w = w_in[l]
    sizes = [MLA_Q_RANK, MLA_KV_RANK, MLA_ROPE, GROUP, GROUP, GROUP, HEADS] + [GROUP] * 6
    offs = np.concatenate([[0], np.cumsum(sizes)])
    part = [w[:, offs[j]:offs[j + 1]] for j in range(len(sizes))]
    zeros = lambda n: jnp.zeros((D_MODEL, n), F32)
    w_in_r = jnp.concatenate(
        [part[0], part[1], zeros(MLA_NOPE), part[2], zeros(LANES - MLA_QK),
         part[3], part[4], part[6], zeros(LANES - HEADS),
         part[7], part[8], part[10], part[11], part[12]], axis=1).astype(BF16)
    assert w_in_r.shape[1] == COLS_IN
    w_vt = jnp.stack([part[5].T, part[9].T]).astype(BF16)
    w_uq = jnp.pad(mla_w_uq[l].reshape(MLA_Q_RANK, HEADS, MLA_QK),
                   ((0, 0), (0, 0), (0, LANES - MLA_QK))).reshape(MLA_Q_RANK, HEADS * LANES)
    w_ukv = mla_w_ukv[l].reshape(MLA_KV_RANK, HEADS, MLA_NOPE + HEAD_DIM)
    w_uk = jnp.pad(w_ukv[:, :, :MLA_NOPE], ((0, 0), (0, 0), (0, LANES - MLA_NOPE)))
    w_uv = w_ukv[:, :, MLA_NOPE:]
    tile4 = lambda g: jnp.tile(g, HEADS)[None, :]
    rw = _pad_cols(router_w[l], LANES)
    rw_hi = rw.astype(BF16)
    rw_lo = (rw - rw_hi.astype(F32)).astype(BF16)
    r_b = jnp.concatenate([router_b[l], jnp.full((LANES - N_EXPERTS,), NEG, F32)])[None, :]
    return dict(
        g1=norm1_g[l][None, :], g2=norm2_g[l][None, :], w_in=w_in_r, w_vt=w_vt,
        cq_g=mla_cq_g[l][None, :], w_uq=w_uq.astype(BF16), ckv_g=mla_ckv_g[l][None, :],
        w_uk=w_uk.reshape(MLA_KV_RANK, HEADS * LANES).astype(BF16),
        w_uvt=w_uv.reshape(MLA_KV_RANK, GROUP).T.astype(BF16),
        fox_bound=FOX_NORM_SLACK * HEAD_DIM ** 0.5 * jnp.max(jnp.abs(fox_q_g[l]))
        * jnp.max(jnp.abs(fox_k_g[l])),
        q_g=_pad_cols(mla_q_g[l][None, :], LANES), k_g=_pad_cols(mla_k_g[l][None, :], LANES),
        fq_g=tile4(fox_q_g[l]), fk_g=tile4(fox_k_g[l]), f_b=_pad_cols(fox_b_f[l][None, :], LANES),
        mq_g=tile4(moba_q_g[l]), mk_g=tile4(moba_k_g[l]), dq_g=tile4(dil_q_g[l]), dk_g=tile4(dil_k_g[l]),
        w_out=w_out[l].reshape(HEADS, GROUP, D_MODEL).astype(BF16),
        rw_hi=rw_hi, rw_lo=rw_lo, r_b=r_b,
    )


def kernel(x, c, w_mod, b_mod, norm1_g, norm2_g, w_in, mla_cq_g, mla_w_uq, mla_ckv_g, mla_w_ukv, mla_q_g, mla_k_g, fox_q_g, fox_k_g, fox_b_f, moba_q_g, moba_k_g, dil_q_g, dil_k_g, w_out, router_w, router_b, exp_w1, exp_b1, exp_w2, exp_b2):
    batch, seq, d = x.shape
    assert batch == 1 and d == D_MODEL
    assert seq % DIL_TILE == 0 and seq // MOBA_BLOCK <= MOBA_MAX_BLOCKS
    depth = w_mod.shape[0]
    consts = _prep_constants(seq, TOKEN_TILE)
    mod = _modulation(c, w_mod, b_mod)
    slopes_c = 2.0 ** (-(2.0 * np.arange(HEADS) + 2.0))
    in_block = np.arange(FLASH_TILE) % MOBA_BLOCK
    kbias_c = jnp.asarray(np.broadcast_to(slopes_c[:, None, None] * in_block[None, :, None],
                                          (HEADS, FLASH_TILE, LANES)), F32)
    kbias_0 = jnp.zeros((HEADS, FLASH_TILE, LANES), F32)
    all_tiles = jnp.zeros((seq // FLASH_TILE,), jnp.int32)
    x2 = x.reshape(seq, d)
    for l in range(depth):
        p = _layer_params(l, w_in, mla_cq_g, mla_w_uq, mla_ckv_g, mla_w_ukv, mla_q_g, mla_k_g,
                          fox_q_g, fox_k_g, fox_b_f, moba_q_g, moba_k_g, dil_q_g, dil_k_g, w_out,
                          norm1_g, norm2_g, router_w, router_b)
        mod_l = mod[l]
        qa, ka, va, qf, kf, vf, qm, km, vm, qd, kd, vd, decay = _prep(x2, mod_l, consts, p)
        oa = _flash(qa, ka, va, kbias_0, all_tiles, False)
        ob = _flash(qf, kf, vf, kbias_0, _fox_first_tile(decay, p["fox_bound"]), False)
        oc = _flash(qm, km, vm, kbias_c, all_tiles, True)
        od = _dilated(qd, kd, vd)
        x1, h2, eidx, gates, rank, counts = _post(x2, oa, ob, oc, od, mod_l, consts, p)
        x2 = _moe(l, x1, h2, eidx, gates, rank, counts, mod_l, exp_w1, exp_b1, exp_w2, exp_b2)
    return x2.reshape(batch, seq, d)
```

```python
import functools

import numpy as np
import jax
import jax.numpy as jnp
from jax import lax
from jax.experimental import pallas as pl
from jax.experimental.pallas import tpu as pltpu

F32 = jnp.float32
BF16 = jnp.bfloat16

D_MODEL = 1024
HEAD_DIM = 64
HEADS = 4
GROUP = HEADS * HEAD_DIM
LANES = 128
CHUNKS = D_MODEL // LANES
MLA_Q_RANK = 256
MLA_KV_RANK = 128
MLA_NOPE = 64
MLA_ROPE = 32
MLA_QK = MLA_NOPE + MLA_ROPE
ROPE_THETA = 10000.0
MOBA_BLOCK = 256
MOBA_TOPK = 3
MOBA_MAX_BLOCKS = 64
DIL_PATTERNS = ((128, 1), (512, 4), (2048, 16))
DIL_SPAN = 128
DIL_TILE = 2048
N_EXPERTS = 32
TOP_K = 4
D_EXPERT = 1024
SWIGLU_LIMIT = 7.0
SWIGLU_ALPHA = 1.702
EPS = 1e-6
NEG = -1e30

FLASH_TILE = 512
ONES_ROWS = 16
FOX_SKIP_LOG = -106.0
FOX_NORM_SLACK = 1.02
TOKEN_TILE = 256
EXPERT_ROWS = 512
VMEM_LIMIT = 56 * 1024 * 1024

COLS_MLA = 512
COLS_FOX = 640
COLS_MOBA = 512
COLS_DIL = 768
COLS_IN = COLS_MLA + COLS_FOX + COLS_MOBA + COLS_DIL


def _dot(a, b):
    return jnp.dot(a, b, preferred_element_type=F32)


def _dot_nt(a, b):
    return lax.dot_general(a, b, (((1,), (1,)), ((), ())), preferred_element_type=F32)


def _split2(x):
    hi = x.astype(BF16)
    lo = (x - hi.astype(F32)).astype(BF16)
    return hi, lo


def _split3(x):
    a = x.astype(BF16)
    r = x - a.astype(F32)
    b = r.astype(BF16)
    c = (r - b.astype(F32)).astype(BF16)
    return a, b, c


def _head_of_lane():
    return jnp.right_shift(lax.broadcasted_iota(jnp.int32, (1, GROUP), 1), 6)


def _store_token_tiles(ref, x, offset=0):
    n = x.shape[0]
    for c in range(CHUNKS):
        ref[pl.ds(offset + c, n, stride=CHUNKS), :] = x[:, c * LANES:(c + 1) * LANES]


def _load_token_chunk(ref, n, c, offset=0):
    return ref[pl.ds(offset + c, n, stride=CHUNKS), :]


def _full_spec(shape):
    nd = len(shape)
    return pl.BlockSpec(shape, lambda *_: (0,) * nd)


def _params(sem):
    return pltpu.CompilerParams(dimension_semantics=sem, vmem_limit_bytes=VMEM_LIMIT)


def _mod_kernel(c_ref, w_ref, b_ref, o_ref):
    c = c_ref[...]
    s = c * (1.0 / (1.0 + jnp.exp(-c)))
    s8 = jnp.broadcast_to(s, (8, D_MODEL))
    r = jnp.dot(s8, w_ref[0], preferred_element_type=F32, precision=lax.Precision.HIGHEST)
    o_ref[0, 0] = r[0:1, :] + b_ref[0, 0]


def _modulation(c, w_mod, b_mod):
    depth = w_mod.shape[0]
    b4 = b_mod.reshape(depth, 6, 1, D_MODEL)
    return pl.pallas_call(
        _mod_kernel,
        grid=(depth, 6),
        in_specs=[
            pl.BlockSpec((1, D_MODEL), lambda l, j: (0, 0)),
            pl.BlockSpec((1, D_MODEL, D_MODEL), lambda l, j: (l, 0, j)),
            pl.BlockSpec((1, 1, 1, D_MODEL), lambda l, j: (l, j, 0, 0)),
        ],
        out_specs=pl.BlockSpec((1, 1, 1, D_MODEL), lambda l, j: (l, j, 0, 0)),
        out_shape=jax.ShapeDtypeStruct((depth, 6, 1, D_MODEL), F32),
        compiler_params=_params(("arbitrary", "arbitrary")),
        name="modulation",
    )(c, w_mod, b4)


def _head_norm(x, g, bd):
    hi, lo = _split2(x * x)
    ss = _dot(hi, bd) + _dot(lo, bd)
    return x * lax.rsqrt(ss * (1.0 / HEAD_DIM) + EPS) * g


def _prep_kernel(x_ref, mod_ref, g1_ref, win_ref, wvt_ref, cqg_ref, wuq_ref, ckvg_ref, wuk_ref,
                 wuvt_ref, qg_ref, kg_ref, rc_ref, rs1_ref, rs2_ref,
                 fqg_ref, fkg_ref, fb_ref, mqg_ref, mkg_ref, dqg_ref, dkg_ref,
                 bd_ref, tri_ref, eq_ref, ek_ref, sel_ref,
                 qa_ref, ka_ref, va_ref, qf_ref, kf_ref, vf_ref, qm_ref, km_ref, vm_ref,
                 qd_ref, kd_ref, vd_ref, f_ref,
                 fcarry, kmean):
    i = pl.program_id(0)
    tm = x_ref.shape[0]

    @pl.when(i == 0)
    def _():
        fcarry[...] = jnp.zeros_like(fcarry)
        kmean[...] = jnp.zeros_like(kmean)

    x = x_ref[...]
    y = x * lax.rsqrt(jnp.mean(x * x, axis=-1, keepdims=True) + EPS) * g1_ref[...]
    hb = (y * (1.0 + mod_ref[1]) + mod_ref[0]).astype(BF16)
    bd = bd_ref[...]
    lane = lax.broadcasted_iota(jnp.int32, (1, LANES), 1)
    lane_f = lane.astype(F32)
    head_of_lane = _head_of_lane()

    pa = _dot(hb, win_ref[:, 0:COLS_MLA])
    cq = pa[:, 0:MLA_Q_RANK]
    ckv = pa[:, MLA_Q_RANK:MLA_Q_RANK + MLA_KV_RANK]
    kr = pa[:, MLA_Q_RANK + MLA_KV_RANK:COLS_MLA]
    cqn = (cq * lax.rsqrt(jnp.mean(cq * cq, axis=-1, keepdims=True) + EPS) * cqg_ref[...]).astype(BF16)
    ckvn = (ckv * lax.rsqrt(jnp.mean(ckv * ckv, axis=-1, keepdims=True) + EPS) * ckvg_ref[...]).astype(BF16)
    q_all = _dot(cqn, wuq_ref[...])
    k_all = _dot(ckvn, wuk_ref[...])
    va_ref[...] = _dot_nt(wuvt_ref[...], ckvn).astype(BF16)
    rc, rs1, rs2 = rc_ref[...], rs1_ref[...], rs2_ref[...]

    def rope(t):
        return t * rc + pltpu.roll(t, LANES - MLA_ROPE // 2, 1) * rs1 + pltpu.roll(t, MLA_ROPE // 2, 1) * rs2

    for h in range(HEADS):
        q = q_all[:, h * LANES:(h + 1) * LANES]
        q = q * lax.rsqrt(jnp.sum(q * q, axis=-1, keepdims=True) * (1.0 / MLA_QK) + EPS) * qg_ref[...]
        qa_ref[h] = (rope(q) * (MLA_QK ** -0.5)).astype(BF16)
        k = k_all[:, h * LANES:(h + 1) * LANES] + kr
        k = k * lax.rsqrt(jnp.sum(k * k, axis=-1, keepdims=True) * (1.0 / MLA_QK) + EPS) * kg_ref[...]
        ka_ref[h] = rope(k).astype(BF16)

    pf = _dot(hb, win_ref[:, COLS_MLA:COLS_MLA + COLS_FOX])
    fqn = (_head_norm(pf[:, 0:GROUP], fqg_ref[...], bd) * (HEAD_DIM ** -0.5)).astype(BF16)
    fkn = _head_norm(pf[:, GROUP:2 * GROUP], fkg_ref[...], bd).astype(BF16)
    vf_ref[...] = _dot_nt(wvt_ref[0], hb).astype(BF16)
    z = pf[:, 2 * GROUP:2 * GROUP + LANES] + fb_ref[...]
    log_f = jnp.minimum(z, 0.0) - jnp.log(1.0 + jnp.exp(-jnp.abs(z)))
    tri = tri_ref[...]
    a1, a2, a3 = _split3(log_f)
    cum = fcarry[...] + (_dot(tri, a1) + _dot(tri, a2) + _dot(tri, a3))
    fcarry[...] = cum[tm - 1:tm, :]
    f_ref[...] = cum
    f1, f2, f3 = _split3(cum)
    xq = jnp.concatenate([fqn, f1, f2, f3], axis=1)
    xk = jnp.concatenate([fkn, f1, f2, f3], axis=1)
    ones_q = jnp.where((lane >= HEAD_DIM + 3) & (lane < HEAD_DIM + 6), 1.0, 0.0)
    ones_k = jnp.where((lane >= HEAD_DIM) & (lane < HEAD_DIM + 3), 1.0, 0.0)
    qf_all = _dot(xq, eq_ref[...])
    kf_all = _dot(xk, ek_ref[...])
    for h in range(HEADS):
        qf_ref[h] = (qf_all[:, h * LANES:(h + 1) * LANES] + ones_q).astype(BF16)
        kf_ref[h] = (kf_all[:, h * LANES:(h + 1) * LANES] + ones_k).astype(BF16)

    pm = _dot(hb, win_ref[:, COLS_MLA + COLS_FOX:COLS_MLA + COLS_FOX + COLS_MOBA])
    mqn = _head_norm(pm[:, 0:GROUP], mqg_ref[...], bd) * (HEAD_DIM ** -0.5)
    mkn = _head_norm(pm[:, GROUP:2 * GROUP], mkg_ref[...], bd)
    vm_ref[...] = _dot_nt(wvt_ref[1], hb).astype(BF16)
    col_mean = jnp.mean(mkn, axis=0, keepdims=True)
    mqb = mqn.astype(BF16)
    mkb = mkn.astype(BF16)
    blk = lane - HEAD_DIM
    blk_f = blk.astype(F32)
    past = (blk >= 0) & (blk < i)
    i_f = i.astype(F32)
    for h in range(HEADS):
        kmean[pl.ds(h * LANES + HEAD_DIM + i, 1), :] = jnp.where(head_of_lane == h, col_mean, 0.0)
    q_hi, q_lo = _split2(mqn)
    km_hi, km_lo = _split2(kmean[...])
    gate_all = _dot_nt(q_hi, km_hi) + _dot_nt(q_hi, km_lo) + _dot_nt(q_lo, km_hi)
    qm_all = _dot(mqb, sel_ref[...])
    km_all = _dot(mkb, sel_ref[...])
    for h in range(HEADS):
        g = jnp.where(past, gate_all[:, h * LANES:(h + 1) * LANES], NEG)
        chosen = jnp.zeros((tm, LANES), F32)
        for _ in range(MOBA_TOPK):
            m = jnp.max(g, axis=-1, keepdims=True)
            first = jnp.min(jnp.where(g == m, lane_f, 1e9), axis=-1, keepdims=True)
            pick = (lane_f == first) & (m > NEG)
            chosen = jnp.where(pick, 1.0, chosen)
            g = jnp.where(pick, NEG, g)
        slope = 2.0 ** (-(2 * h + 2))
        keep = (chosen > 0.0) | (blk == i)
        bias = jnp.where(keep, (slope * MOBA_BLOCK) * (blk_f - i_f), NEG)
        bias = jnp.where(blk >= 0, bias, 0.0)
        qm_ref[h] = (qm_all[:, h * LANES:(h + 1) * LANES] + bias).astype(BF16)
        onehot = jnp.where(blk == i, 1.0, 0.0)
        km_ref[h] = (km_all[:, h * LANES:(h + 1) * LANES] + onehot).astype(BF16)

    pd = _dot(hb, win_ref[:, COLS_MLA + COLS_FOX + COLS_MOBA:COLS_IN])
    qd_ref[...] = _head_norm(pd[:, 0:GROUP], dqg_ref[...], bd) * (HEAD_DIM ** -0.5)
    kd_ref[...] = _head_norm(pd[:, GROUP:2 * GROUP], dkg_ref[...], bd)
    vd_ref[...] = pd[:, 2 * GROUP:3 * GROUP]


def _prep_constants(seq, tm):
    half = MLA_ROPE // 2
    inv = 1.0 / (ROPE_THETA ** (jnp.arange(half, dtype=F32) / half))
    ang = jnp.arange(seq, dtype=F32)[:, None] * inv[None, :]
    cos, sin = jnp.cos(ang), jnp.sin(ang)
    z = lambda n: jnp.zeros((seq, n), F32)
    rc = jnp.concatenate([jnp.ones((seq, MLA_NOPE), F32), cos, cos, z(LANES - MLA_QK)], axis=1)
    rs1 = jnp.concatenate([z(MLA_NOPE), -sin, z(LANES - MLA_NOPE - half)], axis=1)
    rs2 = jnp.concatenate([z(MLA_NOPE + half), sin, z(LANES - MLA_QK)], axis=1)
    bd =np.kron(np.eye(HEADS, dtype=np.float32), np.ones((HEAD_DIM, HEAD_DIM), np.float32))
    tri = np.tril(np.ones((tm, tm), np.float32))
    tri_strict = np.tril(np.ones((tm, tm), np.float32), -1)
    sel = np.zeros((GROUP, HEADS * LANES), np.float32)
    eq = np.zeros((GROUP + 3 * LANES, HEADS * LANES), np.float32)
    ek = np.zeros((GROUP + 3 * LANES, HEADS * LANES), np.float32)
    for h in range(HEADS):
        for d in range(HEAD_DIM):
            sel[h * HEAD_DIM + d, h * LANES + d] = 1.0
        for piece in range(3):
            eq[GROUP + piece * LANES + h, h * LANES + HEAD_DIM + piece] = 1.0
            ek[GROUP + piece * LANES + h, h * LANES + HEAD_DIM + 3 + piece] = -1.0
    eq[:GROUP] = sel
    ek[:GROUP] = sel
    as_bf = lambda a: jnp.asarray(a, BF16)
    return dict(rc=rc, rs1=rs1, rs2=rs2, bd=as_bf(bd),
                tri=as_bf(tri), tri_strict=as_bf(tri_strict), sel=as_bf(sel), eq=as_bf(eq), ek=as_bf(ek))


def _prep(x2, mod_l, consts, p):
    seq = x2.shape[0]
    tm = TOKEN_TILE
    row = lambda n: pl.BlockSpec((tm, n), lambda i: (i, 0))
    heads = pl.BlockSpec((HEADS, tm, LANES), lambda i: (0, i, 0))
    in_arrays = [
        (x2, row(D_MODEL)), (mod_l, _full_spec(mod_l.shape)), (p["g1"], None), (p["w_in"], None),
        (p["w_vt"], None),
        (p["cq_g"], None), (p["w_uq"], None), (p["ckv_g"], None), (p["w_uk"], None), (p["w_uvt"], None),
        (p["q_g"], None), (p["k_g"], None),
        (consts["rc"], row(LANES)), (consts["rs1"], row(LANES)), (consts["rs2"], row(LANES)),
        (p["fq_g"], None), (p["fk_g"], None), (p["f_b"], None), (p["mq_g"], None), (p["mk_g"], None),
        (p["dq_g"], None), (p["dk_g"], None),
        (consts["bd"], None), (consts["tri"], None), (consts["eq"], None), (consts["ek"], None),
        (consts["sel"], None),
    ]
    args = [a for a, _ in in_arrays]
    specs = [s if s is not None else _full_spec(a.shape) for a, s in in_arrays]
    hshape = jax.ShapeDtypeStruct((HEADS, seq, LANES), BF16)
    vshape = jax.ShapeDtypeStruct((GROUP, seq), BF16)
    dshape = jax.ShapeDtypeStruct((seq, GROUP), F32)
    vt = pl.BlockSpec((GROUP, tm), lambda i: (0, i))
    return pl.pallas_call(
        _prep_kernel,
        grid=(seq // tm,),
        in_specs=specs,
        out_specs=[heads, heads, vt] * 3 + [row(GROUP)] * 3 + [row(LANES)],
        out_shape=[hshape, hshape, vshape] * 3 + [dshape] * 3
                  + [jax.ShapeDtypeStruct((seq, LANES), F32)],
        scratch_shapes=[pltpu.VMEM((1, LANES), F32), pltpu.VMEM((HEADS * LANES, GROUP), F32)],
        compiler_params=_params(("arbitrary",)),
        name="prep",
    )(*args)


def _flash_kernel(jlo_ref, q_ref, k_ref, vt_ref, kbias_ref, o_ref, m_sc, acc_sc, sa_sc, sb_sc,
                  *, tile, use_kbias):
    i = pl.program_id(0)
    m_sc[...] = jnp.full_like(m_sc, -jnp.inf)
    acc_sc[...] = jnp.zeros_like(acc_sc)

    def score(hh, j, buf):
        start = pl.multiple_of(j * tile, tile)
        buf[hh] = _dot_nt(k_ref[hh, pl.ds(start, tile), :], q_ref[hh])

    def absorb(hh, j, buf, causal):
        start = pl.multiple_of(j * tile, tile)
        s = buf[hh]
        if use_kbias:
            s = s + jnp.concatenate([kbias_ref[hh]] * (tile // LANES), axis=1)
        if causal:
            key = lax.broadcasted_iota(jnp.int32, (tile, tile), 0)
            qry = lax.broadcasted_iota(jnp.int32, (tile, tile), 1)
            s = jnp.where(key <= qry, s, NEG)
        m_prev = m_sc[hh]
        m_cur = jnp.max(jnp.max(s.reshape(8, tile // 8, tile), axis=0), axis=0, keepdims=True)
        m_new = jnp.maximum(m_prev, m_cur)
        alpha = jnp.exp(m_prev - m_new)
        p = jnp.exp((s - m_new).astype(BF16))
        vt = vt_ref[hh * HEAD_DIM:(hh + 1) * HEAD_DIM, pl.ds(start, tile)]
        vt = jnp.concatenate([vt, jnp.ones((ONES_ROWS, tile), BF16)], axis=0)
        acc_sc[hh] = alpha * acc_sc[hh] + _dot(vt, p)
        m_sc[hh] = m_new

    first = jlo_ref[i]
    n_off = i - first
    for hh in range(HEADS):
        score(hh, first, sa_sc)

    def body(t, carry):
        j = first + 2 * t
        for hh in range(HEADS):
            score(hh, j + 1, sb_sc)
            absorb(hh, j, sa_sc, False)
        for hh in range(HEADS):
            score(hh, j + 2, sa_sc)
            absorb(hh, j + 1, sb_sc, False)
        return carry

    lax.fori_loop(0, n_off // 2, body, 0)

    @pl.when(n_off % 2 == 1)
    def _():
        for hh in range(HEADS):
            score(hh, i, sb_sc)
            absorb(hh, i - 1, sa_sc, False)
        for hh in range(HEADS):
            absorb(hh, i, sb_sc, True)

    @pl.when(n_off % 2 == 0)
    def _():
        for hh in range(HEADS):
            absorb(hh, i, sa_sc, True)

    o_t = jnp.concatenate([acc_sc[hh, 0:HEAD_DIM, :] / acc_sc[hh, HEAD_DIM:HEAD_DIM + 1, :]
                           for hh in range(HEADS)], axis=0)
    o_ref[...] = o_t.T.astype(o_ref.dtype)


def _flash(q, k, v_t, kbias, first_tile, use_kbias):
    seq = v_t.shape[1]
    tile = FLASH_TILE
    kern = functools.partial(_flash_kernel, tile=tile, use_kbias=use_kbias)
    resident = pl.Buffered(1)
    return pl.pallas_call(
        kern,
        grid_spec=pltpu.PrefetchScalarGridSpec(
            num_scalar_prefetch=1,
            grid=(seq // tile,),
            in_specs=[
                pl.BlockSpec((HEADS, tile, LANES), lambda i, f: (0, i, 0)),
                pl.BlockSpec((HEADS, seq, LANES), lambda i, f: (0, 0, 0), pipeline_mode=resident),
                pl.BlockSpec((GROUP, seq), lambda i, f: (0, 0), pipeline_mode=resident),
                pl.BlockSpec((HEADS, tile, LANES), lambda i, f: (0, 0, 0)),
            ],
            out_specs=pl.BlockSpec((tile, GROUP), lambda i, f: (i, 0)),
            scratch_shapes=[pltpu.VMEM((HEADS, 1, tile), F32),
                            pltpu.VMEM((HEADS, HEAD_DIM + ONES_ROWS, tile), F32),
                            pltpu.VMEM((HEADS, tile, tile), F32), pltpu.VMEM((HEADS, tile, tile), F32)],
        ),
        out_shape=jax.ShapeDtypeStruct((seq, GROUP), BF16),
        compiler_params=_params(("arbitrary",)),
        name="flash_kbias" if use_kbias else "flash",
    )(first_tile, q, k, v_t, kbias)


def _fox_first_tile(decay, qk_bound):
    seq = decay.shape[0]
    nq = seq // FLASH_TILE
    f = decay[:, :HEADS]
    f_first = f[0::FLASH_TILE]
    f_last = f[FLASH_TILE - 1::FLASH_TILE]
    gap = f_first[:, None, :] - f_last[None, :, :] + 2.0 * qk_bound
    jj = jnp.arange(nq, dtype=jnp.int32)
    needed = (gap >= FOX_SKIP_LOG) | (jj[None, :, None] >= jj[:, None, None])
    first = jnp.min(jnp.where(needed, jj[None, :, None], nq), axis=1)
    return jnp.min(first, axis=1).astype(jnp.int32)


def _dilated_kernel(q_ref, k_ref, v_ref, o_ref, kbuf, vbuf, acc_s, m_s, l_s):
    pair = pl.program_id(0)
    i = pl.program_id(1)
    T = q_ref.shape[0]

    @pl.when(i == 0)
    def _():
        kbuf[...] = jnp.zeros_like(kbuf)
        vbuf[...] = jnp.zeros_like(vbuf)

    kbuf[0:T, :] = kbuf[T:2 * T, :]
    vbuf[0:T, :] = vbuf[T:2 * T, :]
    kbuf[T:2 * T, :] = k_ref[...]
    vbuf[T:2 * T, :] = v_ref[...]

    ii = lax.broadcasted_iota(jnp.int32, (DIL_SPAN, 2 * DIL_SPAN), 0)
    jj = lax.broadcasted_iota(jnp.int32, (DIL_SPAN, 2 * DIL_SPAN), 1)
    dist = ii + DIL_SPAN - jj
    band = (dist >= 0) & (dist <= DIL_SPAN)
    dist_f = dist.astype(F32)
    upper = lax.broadcasted_iota(jnp.int32, (1, LANES), 1) >= HEAD_DIM

    for pi, (window, r) in enumerate(DIL_PATTERNS):
        assert window // r == DIL_SPAN
        sub = DIL_SPAN * r

        def body(idx, carry, r=r, sub=sub, pi=pi):
            n = idx // r
            rho = idx - n * r
            base = n * sub + rho
            q = q_ref[pl.ds(base, DIL_SPAN, stride=r), :]
            kc = kbuf[pl.ds(T + base - sub, 2 * DIL_SPAN, stride=r), :].astype(BF16)
            vc = vbuf[pl.ds(T + base - sub, 2 * DIL_SPAN, stride=r), :].astype(BF16)
            first_key = jnp.where((i == 0) & (n == 0), DIL_SPAN, 0)
            valid = band & (jj >= first_key)
            stats = []
            for hh in range(2):
                slope = jnp.where(pair == 0, 2.0 ** (-(2 * hh + 1)), 2.0 ** (-(2 * hh + 5)))
                qh = (jnp.where(upper, q, 0.0) if hh else jnp.where(upper, 0.0, q)).astype(BF16)
                s = _dot_nt(qh, kc) - (slope * r) * dist_f
                s = jnp.where(valid, s, NEG)
                m = jnp.max(s, axis=-1, keepdims=True)
                p = jnp.exp(s - m)
                l = jnp.sum(p, axis=-1, keepdims=True)
                stats.append((_dot(p.astype(BF16), vc), m, l))
            rows = pl.ds(pi * T + base, DIL_SPAN, stride=r)
            acc_s[rows, :] = jnp.where(upper, stats[1][0], stats[0][0])
            m_s[rows, :] = jnp.where(upper, stats[1][1], stats[0][1])
            l_s[rows, :] = jnp.where(upper, stats[1][2], stats[0][2])
            return carry

        lax.fori_loop(0, T // DIL_SPAN, body, 0, unroll=8)

    npat = len(DIL_PATTERNS)
    ms = [m_s[pi * T:(pi + 1) * T, :] for pi in range(npat)]
    m_top = functools.reduce(jnp.maximum, ms)
    num = jnp.zeros((T, LANES), F32)
    den = jnp.zeros((T, LANES), F32)
    for pi in range(npat):
        w = jnp.exp(ms[pi] - m_top)
        num = num + w * acc_s[pi * T:(pi + 1) * T, :]
        den = den + w * l_s[pi * T:(pi + 1) * T, :]
    o_ref[...] = (num / den).astype(o_ref.dtype)


def _dilated(q, k, v):
    seq = q.shape[0]
    T = DIL_TILE
    spec = pl.BlockSpec((T, LANES), lambda p, i: (i, p))
    npat = len(DIL_PATTERNS)
    return pl.pallas_call(
        _dilated_kernel,
        grid=(HEADS // 2, seq // T),
        in_specs=[spec, spec, spec],
        out_specs=spec,
        out_shape=jax.ShapeDtypeStruct((seq, GROUP), BF16),
        scratch_shapes=[pltpu.VMEM((2 * T, LANES), F32), pltpu.VMEM((2 * T, LANES), F32),
                        pltpu.VMEM((npat * T, LANES), F32), pltpu.VMEM((npat * T, LANES), F32),
                        pltpu.VMEM((npat * T, LANES), F32)],
        compiler_params=_params(("arbitrary", "arbitrary")),
        name="dilated",
    )(q, k, v)


def _post_kernel(x_ref, oa_ref, ob_ref, oc_ref, od_ref, wout_ref, mod_ref, g2_ref,
                 rw_ref, rb_ref, tri_ref,
                 x1_ref, h2_ref, eidx_ref, gate_ref, rank_ref, cnt_ref, carry):
    i = pl.program_id(0)
    tm = x_ref.shape[0]

    @pl.when(i == 0)
    def _():
        carry[...] = jnp.zeros_like(carry)

    o = (_dot(oa_ref[...], wout_ref[0]) + _dot(ob_ref[...], wout_ref[1])
         + _dot(oc_ref[...], wout_ref[2]) + _dot(od_ref[...], wout_ref[3]))
    x1 = x_ref[...] + mod_ref[2] * o
    x1_ref[...] = x1
    y = x1 * lax.rsqrt(jnp.mean(x1 * x1, axis=-1, keepdims=True) + EPS) * g2_ref[...]
    h2 = y * (1.0 + mod_ref[4]) + mod_ref[3]
    _store_token_tiles(h2_ref, h2)

    h_hi, h_lo = _split2(h2)
    hh = _dot(h_hi, rw_ref[...])
    logits = (hh[:, :LANES] + hh[:, LANES:] + _dot(h_lo, rw_ref[:, :LANES])
              + rb_ref[...])
    lane = lax.broadcasted_iota(jnp.int32, (tm, LANES), 1)
    lane_f = lane.astype(F32)
    g = logits
    chosen = jnp.zeros((tm, LANES), F32)
    vals, idxs = [], []
    for _ in range(TOP_K):
        m = jnp.max(g, axis=-1, keepdims=True)
        first = jnp.min(jnp.where(g == m, lane_f, 1e9), axis=-1, keepdims=True)
        pick = lane_f == first
        chosen = jnp.where(pick, 1.0, chosen)
        g = jnp.where(pick, -jnp.inf, g)
        vals.append(m)
        idxs.append(first)
    exps = [jnp.exp(v - vals[0]) for v in vals]
    den = exps[0] + exps[1] + exps[2] + exps[3]
    before = _dot(tri_ref[...], chosen.astype(BF16)) + carry[...]
    carry[...] = carry[...] + jnp.sum(chosen, axis=0, keepdims=True)
    cnt_ref[...] = carry[...]
    e_out = jnp.zeros((tm, LANES), F32)
    g_out = jnp.zeros((tm, LANES), F32)
    r_out = jnp.zeros((tm, LANES), F32)
    for k in range(TOP_K):
        rank_k = jnp.sum(jnp.where(lane_f == idxs[k], before, 0.0), axis=-1, keepdims=True)
        e_out = jnp.where(lane == k, idxs[k], e_out)
        g_out = jnp.where(lane == k, exps[k] / den, g_out)
        r_out = jnp.where(lane == k, rank_k, r_out)
    eidx_ref[...] = e_out.astype(jnp.int32)
    gate_ref[...] = g_out
    rank_ref[...] = r_out.astype(jnp.int32)


def _post(x2, oa, ob, oc, od, mod_l, consts, p):
    seq = x2.shape[0]
    tm = TOKEN_TILE
    row = lambda n: pl.BlockSpec((tm, n), lambda i: (i, 0))
    full = [p["w_out"], mod_l, p["g2"], p["rw"], p["r_b"], consts["tri_strict"]]
    f32 = lambda n: jax.ShapeDtypeStruct((seq, n), F32)
    i32 = lambda n: jax.ShapeDtypeStruct((seq, n), jnp.int32)
    return pl.pallas_call(
        _post_kernel,
        grid=(seq // tm,),
        in_specs=[row(D_MODEL)] + [row(GROUP)] * 4 + [_full_spec(a.shape) for a in full],
        out_specs=[row(D_MODEL), pl.BlockSpec((tm * CHUNKS, LANES), lambda i: (i, 0)),
                   row(LANES), row(LANES), row(LANES), _full_spec((1, LANES))],
        out_shape=[f32(D_MODEL), jax.ShapeDtypeStruct((seq * CHUNKS, LANES), F32),
                   i32(LANES), f32(LANES), i32(LANES), jax.ShapeDtypeStruct((1, LANES), F32)],
        scratch_shapes=[pltpu.VMEM((1, LANES), F32)],
        compiler_params=_params(("arbitrary",)),
        name="post",
    )(x2, oa, ob, oc, od, *full)


def _tile_copy(src, s, dst, d, sem):
    return pltpu.make_async_copy(src.at[pl.ds(pl.multiple_of(s * CHUNKS, CHUNKS), CHUNKS), :],
                                 dst.at[pl.ds(pl.multiple_of(d * CHUNKS, CHUNKS), CHUNKS), :], sem)


def _dispatch_kernel(dest_ref, padlo_ref, padn_ref, nu_ref, h_ref, xs_ref, zbuf, sem, zsem):
    i = pl.program_id(0)
    n = TOKEN_TILE * TOP_K
    base = i * n
    block_rows = EXPERT_ROWS * CHUNKS

    @pl.when(i == 0)
    def _():
        zbuf[...] = jnp.zeros_like(zbuf)

        def fill(wait):
            def go(copy):
                copy.wait() if wait else copy.start()

            def per_expert(e, carry):
                lo, cnt = padlo_ref[e], padn_ref[e]
                off = lo
                p = EXPERT_ROWS // 2
                while p >= 1:
                    rows = p * CHUNKS

                    @pl.when((cnt & p) != 0)
                    def _(off=off, rows=rows):
                        go(pltpu.make_async_copy(
                            zbuf.at[pl.ds(0, rows), :],
                            xs_ref.at[pl.ds(pl.multiple_of(off * CHUNKS, CHUNKS), rows), :], zsem))

                    off = off + (cnt & p)
                    p //= 2
                return carry

            lax.fori_loop(0, N_EXPERTS, per_expert, 0)

            def per_block(b, carry):
                go(pltpu.make_async_copy(
                    zbuf, xs_ref.at[pl.ds(pl.multiple_of(b * block_rows, block_rows), block_rows), :],
                    zsem))
                return carry

            lax.fori_loop(nu_ref[0], xs_ref.shape[0] // block_rows, per_block, 0)

        fill(False)
        fill(True)

    def issue(r, carry):
        for k in range(TOP_K):
            _tile_copy(h_ref, r, xs_ref, dest_ref[base + r * TOP_K + k], sem).start(priority=k % 2)
        return carry

    lax.fori_loop(0, TOKEN_TILE, issue, 0, unroll=2)
    rows = pl.ds(0, n * CHUNKS)
    pltpu.make_async_copy(xs_ref.at[rows, :], xs_ref.at[rows, :], sem).wait()


def _dispatch(dest, pad_lo, pad_n, n_used, h2_tiles, m_pad):
    seq = h2_tiles.shape[0] // CHUNKS
    return pl.pallas_call(
        _dispatch_kernel,
        grid_spec=pltpu.PrefetchScalarGridSpec(
            num_scalar_prefetch=4,
            grid=(seq // TOKEN_TILE,),
            in_specs=[pl.BlockSpec((TOKEN_TILE * CHUNKS, LANES), lambda i, *_: (i, 0))],
            out_specs=pl.BlockSpec(memory_space=pl.ANY),
            scratch_shapes=[pltpu.VMEM((EXPERT_ROWS * CHUNKS, LANES), F32),
                            pltpu.SemaphoreType.DMA(()), pltpu.SemaphoreType.DMA(())],
        ),
        out_shape=jax.ShapeDtypeStruct((m_pad * CHUNKS, LANES), F32),
        compiler_params=_params(("arbitrary",)),
        name="dispatch",
    )(dest, pad_lo, pad_n, n_used, h2_tiles)


def _expert_kernel(be_ref, nu_ref, xs_ref, w1_ref, b1_ref, w2_ref, b2_ref, ys_ref, w1b, w2b):
    b = pl.program_id(0)
    prev = be_ref[jnp.maximum(b - 1, 0)]
    fresh = (b == 0) | (be_ref[b] != prev)

    @pl.when(fresh)
    def _():
        w1b[...] = w1_ref[0].astype(BF16)
        w2b[...] = w2_ref[0].astype(BF16)

    @pl.when(b < nu_ref[0])
    def _():
        half = EXPERT_ROWS // 2
        gus = []
        for r in range(2):
            xb = jnp.concatenate(
                [_load_token_chunk(xs_ref, half, c, offset=r * half * CHUNKS) for c in range(CHUNKS)],
                axis=1).astype(BF16)
            gus.append(_dot(xb, w1b[...]) + b1_ref[0])
        for r in range(2):
            g = jnp.minimum(gus[r][:, :D_EXPERT], SWIGLU_LIMIT)
            u = jnp.clip(gus[r][:, D_EXPERT:], -SWIGLU_LIMIT, SWIGLU_LIMIT)
            y = (u + 1.0) * g * (1.0 / (1.0 + jnp.exp(-SWIGLU_ALPHA * g)))
            _store_token_tiles(ys_ref, _dot(y.astype(BF16), w2b[...]) + b2_ref[0],
                               offset=r * half * CHUNKS)

    @pl.when(b >= nu_ref[0])
    def _():
        ys_ref[...] = jnp.zeros_like(ys_ref)


def _experts(blk_expert, n_used, xs, w1, b1, w2, b2):
    m_pad = xs.shape[0] // CHUNKS
    bm = EXPERT_ROWS
    n_all = w1.shape[0] * w1.shape[1]
    rows = lambda b, be, nu: (jnp.minimum(b, nu[0] - 1), 0)
    ex = lambda b, be, nu: (be[jnp.minimum(b, nu[0] - 1)], 0, 0)
    return pl.pallas_call(
        _expert_kernel,
        grid_spec=pltpu.PrefetchScalarGridSpec(
            num_scalar_prefetch=2,
            grid=(m_pad // bm,),
            in_specs=[
                pl.BlockSpec((bm * CHUNKS, LANES), rows),
                pl.BlockSpec((1, D_MODEL, 2 * D_EXPERT), ex),
                pl.BlockSpec((1, 1, 2 * D_EXPERT), ex),
                pl.BlockSpec((1, D_EXPERT, D_MODEL), ex),
                pl.BlockSpec((1, 1, D_MODEL), ex),
            ],
            out_specs=pl.BlockSpec((bm * CHUNKS, LANES), lambda b, be, nu: (b, 0)),
            scratch_shapes=[pltpu.VMEM((D_MODEL, 2 * D_EXPERT), BF16),
                            pltpu.VMEM((D_EXPERT, D_MODEL), BF16)],
        ),
        out_shape=jax.ShapeDtypeStruct((m_pad * CHUNKS, LANES), F32),
        compiler_params=_params(("arbitrary",)),
        name="experts",
    )(blk_expert, n_used, xs, w1.reshape(n_all, D_MODEL, 2 * D_EXPERT), b1.reshape(n_all, 1, -1),
      w2.reshape(n_all, D_EXPERT, D_MODEL), b2.reshape(n_all, 1, -1))


def _combine_kernel(dest_ref, ys_ref, x1_ref, gate_ref, mod_ref, o_ref, buf, sem):
    i = pl.program_id(0)
    tm = x1_ref.shape[0]
    n = tm * TOP_K
    base = i * n

    def issue(r, carry):
        for k in range(TOP_K):
            _tile_copy(ys_ref, dest_ref[base + r * TOP_K + k], buf, k * tm + r, sem).start(
                priority=k % 2)
        return carry

    lax.fori_loop(0, tm, issue, 0, unroll=2)
    pltpu.make_async_copy(ys_ref.at[pl.ds(0, n * CHUNKS), :], buf, sem).wait()
    gates = gate_ref[...]
    g2 = mod_ref[5]
    for c in range(CHUNKS):
        cols = slice(c * LANES, (c + 1) * LANES)
        mix = jnp.zeros((tm, LANES), F32)
        for k in range(TOP_K):
            mix = mix + gates[:, k:k + 1] * _load_token_chunk(buf, tm, c, offset=k * tm * CHUNKS)
        o_ref[:, cols] = x1_ref[:, cols] + g2[:, cols] * mix


def _combine(dest, ys, x1, gates, mod_l):
    seq = x1.shape[0]
    tm = TOKEN_TILE
    return pl.pallas_call(
        _combine_kernel,
        grid_spec=pltpu.PrefetchScalarGridSpec(
            num_scalar_prefetch=1,
            grid=(seq // tm,),
            in_specs=[
                pl.BlockSpec(memory_space=pl.ANY),
                pl.BlockSpec((tm, D_MODEL), lambda i, d: (i, 0)),
                pl.BlockSpec((tm, LANES), lambda i, d: (i, 0)),
                pl.BlockSpec(mod_l.shape, lambda i, d: (0, 0, 0)),
            ],
            out_specs=pl.BlockSpec((tm, D_MODEL), lambda i, d: (i, 0)),
            scratch_shapes=[pltpu.VMEM((TOP_K * tm * CHUNKS, LANES), F32),
                            pltpu.SemaphoreType.DMA(())],
        ),
        out_shape=jax.ShapeDtypeStruct((seq, D_MODEL), F32),
        compiler_params=_params(("arbitrary",)),
        name="combine",
    )(dest, ys, x1, gates, mod_l)


def _moe(l, x1, h2_tiles, eidx, gates, rank, counts, mod_l, w1, b1, w2, b2):
    seq = x1.shape[0]
    bm = EXPERT_ROWS
    m_pad = seq * TOP_K + N_EXPERTS * bm
    cnt = counts[0, :N_EXPERTS].astype(jnp.int32)
    padded = (cnt + bm - 1) // bm * bm
    pad_end = jnp.cumsum(padded)
    pad_start = pad_end - padded
    onehot = eidx[:, :TOP_K, None] == jnp.arange(N_EXPERTS, dtype=jnp.int32)
    start_of = jnp.sum(jnp.where(onehot, pad_start, 0), axis=-1)
    dest = (start_of + rank[:, :TOP_K]).reshape(seq * TOP_K).astype(jnp.int32)
    nblk = m_pad // bm
    blk_start = jnp.arange(nblk, dtype=jnp.int32) * bm
    blk_expert = jnp.minimum(jnp.sum(pad_end[None, :] <= blk_start[:, None], axis=1), N_EXPERTS - 1)
    blk_expert = (blk_expert + l * N_EXPERTS).astype(jnp.int32)
    n_used = (pad_end[-1:] // bm).astype(jnp.int32)
    xs = _dispatch(dest, (pad_start + cnt).astype(jnp.int32), (padded - cnt).astype(jnp.int32),
                   n_used, h2_tiles, m_pad)
    ys = _experts(blk_expert, n_used, xs, w1, b1, w2, b2)
    return _combine(dest, ys, x1, gates, mod_l)


def _pad_cols(a, n):
    return jnp.pad(a, ((0, 0), (0, n - a.shape[1])))


def _layer_params(l, w_in, mla_cq_g, mla_w_uq, mla_ckv_g, mla_w_ukv, mla_q_g, mla_k_g,
                  fox_q_g, fox_k_g, fox_b_f, moba_q_g, moba_k_g, dil_q_g, dil_k_g, w_out,
                  norm1_g, norm2_g, router_w, router_b):
    w = w_in[l]
    sizes = [MLA_Q_RANK, MLA_KV_RANK, MLA_ROPE, GROUP, GROUP, GROUP, HEADS] + [GROUP] * 6
    offs = np.concatenate([[0], np.cumsum(sizes)])
    part = [w[:, offs[j]:offs[j + 1]] for j in range(len(sizes))]
    zeros = lambda n: jnp.zeros((D_MODEL, n), F32)
    w_in_r = jnp.concatenate(
        [part[0], part[1], zeros(MLA_NOPE), part[2], zeros(LANES - MLA_QK),
         part[3], part[4], part[6], zeros(LANES - HEADS),
         part[7], part[8], part[10], part[11], part[12]], axis=1).astype(BF16)
    assert w_in_r.shape[1] == COLS_IN
    w_vt = jnp.stack([part[5].T, part[9].T]).astype(BF16)
    w_uq = jnp.pad(mla_w_uq[l].reshape(MLA_Q_RANK, HEADS, MLA_QK),
                   ((0, 0), (0, 0), (0, LANES - MLA_QK))).reshape(MLA_Q_RANK, HEADS * LANES)
    w_ukv = mla_w_ukv[l].reshape(MLA_KV_RANK, HEADS, MLA_NOPE + HEAD_DIM)
    w_uk = jnp.pad(w_ukv[:, :, :MLA_NOPE], ((0, 0), (0, 0), (0, LANES - MLA_NOPE)))
    w_uv = w_ukv[:, :, MLA_NOPE:]
    tile4 = lambda g: jnp.tile(g, HEADS)[None, :]
    rw = _pad_cols(router_w[l], LANES)
    rw_hi = rw.astype(BF16)
    rw_lo = (rw - rw_hi.astype(F32)).astype(BF16)
    r_b = jnp.concatenate([router_b[l], jnp.full((LANES - N_EXPERTS,), NEG, F32)])[None, :]
    return dict(
        g1=norm1_g[l][None, :], g2=norm2_g[l][None, :], w_in=w_in_r, w_vt=w_vt,
        cq_g=mla_cq_g[l][None, :], w_uq=w_uq.astype(BF16), ckv_g=mla_ckv_g[l][None, :],
        w_uk=w_uk.reshape(MLA_KV_RANK, HEADS * LANES).astype(BF16),
        w_uvt=w_uv.reshape(MLA_KV_RANK, GROUP).T.astype(BF16),
        fox_bound=FOX_NORM_SLACK * HEAD_DIM ** 0.5 * jnp.max(jnp.abs(fox_q_g[l]))
        * jnp.max(jnp.abs(fox_k_g[l])),
        q_g=_pad_cols(mla_q_g[l][None, :], LANES), k_g=_pad_cols(mla_k_g[l][None, :], LANES),
        fq_g=tile4(fox_q_g[l]), fk_g=tile4(fox_k_g[l]), f_b=_pad_cols(fox_b_f[l][None, :], LANES),
        mq_g=tile4(moba_q_g[l]), mk_g=tile4(moba_k_g[l]), dq_g=tile4(dil_q_g[l]), dk_g=tile4(dil_k_g[l]),
        w_out=w_out[l].reshape(HEADS, GROUP, D_MODEL).astype(BF16),
        rw=jnp.concatenate([rw_hi, rw_lo], axis=1), r_b=r_b,
    )


def kernel(x, c, w_mod, b_mod, norm1_g, norm2_g, w_in, mla_cq_g, mla_w_uq, mla_ckv_g, mla_w_ukv, mla_q_g, mla_k_g, fox_q_g, fox_k_g, fox_b_f, moba_q_g, moba_k_g, dil_q_g, dil_k_g, w_out, router_w, router_b, exp_w1, exp_b1, exp_w2, exp_b2):
    batch, seq, d = x.shape
    assert batch == 1 and d == D_MODEL
    assert seq % DIL_TILE == 0 and seq // MOBA_BLOCK <= MOBA_MAX_BLOCKS
    depth = w_mod.shape[0]
    consts = _prep_constants(seq, TOKEN_TILE)
    mod = _modulation(c, w_mod, b_mod)
    slopes_c = 2.0 ** (-(2.0 * np.arange(HEADS) + 2.0))
    in_block = np.arange(FLASH_TILE) % MOBA_BLOCK
    kbias_c = jnp.asarray(np.broadcast_to(slopes_c[:, None, None] * in_block[None, :, None],
                                          (HEADS, FLASH_TILE, LANES)), F32)
    kbias_0 = jnp.zeros((HEADS, FLASH_TILE, LANES), F32)
    all_tiles = jnp.zeros((seq // FLASH_TILE,), jnp.int32)
    x2 = x.reshape(seq, d)
    for l in range(depth):
        p = _layer_params(l, w_in, mla_cq_g, mla_w_uq, mla_ckv_g, mla_w_ukv, mla_q_g, mla_k_g,
                          fox_q_g, fox_k_g, fox_b_f, moba_q_g, moba_k_g, dil_q_g, dil_k_g, w_out,
                          norm1_g, norm2_g, router_w, router_b)
        mod_l = mod[l]
        qa, ka, va, qf, kf, vf, qm, km, vm, qd, kd, vd, decay = _prep(x2, mod_l, consts, p)
        oa = _flash(qa, ka, va, kbias_0, all_tiles, False)
        ob = _flash(qf, kf, vf, kbias_0, _fox_first_tile(decay, p["fox_bound"]), False)
        oc = _flash(qm, km, vm, kbias_c, all_tiles, True)
        od = _dilated(qd, kd, vd)
        x1, h2, eidx, gates, rank, counts = _post(x2, oa, ob, oc, od, mod_l, consts, p)
        x2 = _moe(l, x1, h2, eidx, gates, rank, counts, mod_l, exp_w1, exp_b1, exp_w2, exp_b2)
    return x2.reshape(batch, seq, d)
```

```python
import functools

import numpy as np
import jax
import jax.numpy as jnp
from jax import lax
from jax.experimental import pallas as pl
from jax.experimental.pallas import tpu as pltpu

F32 = jnp.float32
BF16 = jnp.bfloat16

D_MODEL = 1024
HEAD_DIM = 64
HEADS = 4
GROUP = HEADS * HEAD_DIM
LANES = 128
CHUNKS = D_MODEL // LANES
MLA_Q_RANK = 256
MLA_KV_RANK = 128
MLA_NOPE = 64
MLA_ROPE = 32
MLA_QK = MLA_NOPE + MLA_ROPE
ROPE_THETA = 10000.0
MOBA_BLOCK = 256
MOBA_TOPK = 3
MOBA_MAX_BLOCKS = 64
DIL_PATTERNS = ((128, 1), (512, 4), (2048, 16))
DIL_SPAN = 128
DIL_TILE = 2048
N_EXPERTS = 32
TOP_K = 4
D_EXPERT = 1024
SWIGLU_LIMIT = 7.0
SWIGLU_ALPHA = 1.702
EPS = 1e-6
NEG = -1e30

FLASH_TILE = 512
ONES_ROWS = 16
FOX_SKIP_LOG = -106.0
FOX_NORM_SLACK = 1.02
TOKEN_TILE = 256
EXPERT_ROWS = 512
VMEM_LIMIT = 56 * 1024 * 1024

COLS_MLA = 512
COLS_FOX = 640
COLS_MOBA = 512
COLS_DIL = 768
COLS_IN = COLS_MLA + COLS_FOX + COLS_MOBA + COLS_DIL


def _dot(a, b):
    return jnp.dot(a, b, preferred_element_type=F32)


def _dot_nt(a, b):
    return lax.dot_general(a, b, (((1,), (1,)), ((), ())), preferred_element_type=F32)


def _split2(x):
    hi = x.astype(BF16)
    lo = (x - hi.astype(F32)).astype(BF16)
    return hi, lo


def _split3(x):
    a = x.astype(BF16)
    r = x - a.astype(F32)
    b = r.astype(BF16)
    c = (r - b.astype(F32)).astype(BF16)
    return a, b, c


def _head_of_lane():
    return jnp.right_shift(lax.broadcasted_iota(jnp.int32, (1, GROUP), 1), 6)


def _store_token_tiles(ref, x, offset=0):
    n = x.shape[0]
    for c in range(CHUNKS):
        ref[pl.ds(offset + c, n, stride=CHUNKS), :] = x[:, c * LANES:(c + 1) * LANES]


def _load_token_chunk(ref, n, c, offset=0):
    return ref[pl.ds(offset + c, n, stride=CHUNKS), :]


def _full_spec(shape):
    nd = len(shape)
    return pl.BlockSpec(shape, lambda *_: (0,) * nd)


def _params(sem):
    return pltpu.CompilerParams(dimension_semantics=sem, vmem_limit_bytes=VMEM_LIMIT)


def _mod_kernel(c_ref, w_ref, b_ref, o_ref):
    c = c_ref[...]
    s = c * (1.0 / (1.0 + jnp.exp(-c)))
    s8 = jnp.broadcast_to(s, (8, D_MODEL))
    r = jnp.dot(s8, w_ref[0], preferred_element_type=F32, precision=lax.Precision.HIGHEST)
    o_ref[0, 0] = r[0:1, :] + b_ref[0, 0]


def _modulation(c, w_mod, b_mod):
    depth = w_mod.shape[0]
    b4 = b_mod.reshape(depth, 6, 1, D_MODEL)
    return pl.pallas_call(
        _mod_kernel,
        grid=(depth, 6),
        in_specs=[
            pl.BlockSpec((1, D_MODEL), lambda l, j: (0, 0)),
            pl.BlockSpec((1, D_MODEL, D_MODEL), lambda l, j: (l, 0, j)),
            pl.BlockSpec((1, 1, 1, D_MODEL), lambda l, j: (l, j, 0, 0)),
        ],
        out_specs=pl.BlockSpec((1, 1, 1, D_MODEL), lambda l, j: (l, j, 0, 0)),
        out_shape=jax.ShapeDtypeStruct((depth, 6, 1, D_MODEL), F32),
        compiler_params=_params(("arbitrary", "arbitrary")),
        name="modulation",
    )(c, w_mod, b4)


def _head_norm(x, g, bd):
    hi, lo = _split2(x * x)
    ss = _dot(hi, bd) + _dot(lo, bd)
    return x * lax.rsqrt(ss * (1.0 / HEAD_DIM) + EPS) * g


def _prep_kernel(x_ref, mod_ref, g1_ref, win_ref, wvt_ref, cqg_ref, wuq_ref, ckvg_ref, wuk_ref,
                 wuvt_ref, qg_ref, kg_ref, rc_ref, rs1_ref, rs2_ref,
                 fqg_ref, fkg_ref, fb_ref, mqg_ref, mkg_ref, dqg_ref, dkg_ref,
                 bd_ref, tri_ref, eq_ref, ek_ref, sel_ref,
                 qa_ref, ka_ref, va_ref, qf_ref, kf_ref, vf_ref, qm_ref, km_ref, vm_ref,
                 qd_ref, kd_ref, vd_ref, f_ref,
                 fcarry, kmean):
    i = pl.program_id(0)
    tm = x_ref.shape[0]

    @pl.when(i == 0)
    def _():
        fcarry[...] = jnp.zeros_like(fcarry)
        kmean[...] = jnp.zeros_like(kmean)

    x = x_ref[...]
    y = x * lax.rsqrt(jnp.mean(x * x, axis=-1, keepdims=True) + EPS) * g1_ref[...]
    hb = (y * (1.0 + mod_ref[1]) + mod_ref[0]).astype(BF16)
    bd = bd_ref[...]
    lane = lax.broadcasted_iota(jnp.int32, (1, LANES), 1)
    lane_f = lane.astype(F32)
    head_of_lane = _head_of_lane()

    pa = _dot(hb, win_ref[:, 0:COLS_MLA])
    cq = pa[:, 0:MLA_Q_RANK]
    ckv = pa[:, MLA_Q_RANK:MLA_Q_RANK + MLA_KV_RANK]
    kr = pa[:, MLA_Q_RANK + MLA_KV_RANK:COLS_MLA]
    cqn = (cq * lax.rsqrt(jnp.mean(cq * cq, axis=-1, keepdims=True) + EPS) * cqg_ref[...]).astype(BF16)
    ckvn = (ckv * lax.rsqrt(jnp.mean(ckv * ckv, axis=-1, keepdims=True) + EPS) * ckvg_ref[...]).astype(BF16)
    q_all = _dot(cqn, wuq_ref[...])
    k_all = _dot(ckvn, wuk_ref[...])
    va_ref[...] = _dot_nt(wuvt_ref[...], ckvn).astype(BF16)
    rc, rs1, rs2 = rc_ref[...], rs1_ref[...], rs2_ref[...]

    def rope(t):
        return t * rc + pltpu.roll(t, LANES - MLA_ROPE // 2, 1) * rs1 + pltpu.roll(t, MLA_ROPE // 2, 1) * rs2

    for h in range(HEADS):
        q = q_all[:, h * LANES:(h + 1) * LANES]
        q = q * lax.rsqrt(jnp.sum(q * q, axis=-1, keepdims=True) * (1.0 / MLA_QK) + EPS) * qg_ref[...]
        qa_ref[h] = (rope(q) * (MLA_QK ** -0.5)).astype(BF16)
        k = k_all[:, h * LANES:(h + 1) * LANES] + kr
        k = k * lax.rsqrt(jnp.sum(k * k, axis=-1, keepdims=True) * (1.0 / MLA_QK) + EPS) * kg_ref[...]
        ka_ref[h] = rope(k).astype(BF16)

    pf = _dot(hb, win_ref[:, COLS_MLA:COLS_MLA + COLS_FOX])
    fqn = (_head_norm(pf[:, 0:GROUP], fqg_ref[...], bd) * (HEAD_DIM ** -0.5)).astype(BF16)
    fkn = _head_norm(pf[:, GROUP:2 * GROUP], fkg_ref[...], bd).astype(BF16)
    vf_ref[...] = _dot_nt(wvt_ref[0], hb).astype(BF16)
    z = pf[:, 2 * GROUP:2 * GROUP + LANES] + fb_ref[...]
    log_f = jnp.minimum(z, 0.0) - jnp.log(1.0 + jnp.exp(-jnp.abs(z)))
    tri = tri_ref[...]
    a1, a2, a3 = _split3(log_f)
    cum = fcarry[...] + (_dot(tri, a1) + _dot(tri, a2) + _dot(tri, a3))
    fcarry[...] = cum[tm - 1:tm, :]
    f_ref[...] = cum
    f1, f2, f3 = _split3(cum)
    xq = jnp.concatenate([fqn, f1, f2, f3], axis=1)
    xk = jnp.concatenate([fkn, f1, f2, f3], axis=1)
    ones_q = jnp.where((lane >= HEAD_DIM + 3) & (lane < HEAD_DIM + 6), 1.0, 0.0)
    ones_k = jnp.where((lane >= HEAD_DIM) & (lane < HEAD_DIM + 3), 1.0, 0.0)
    qf_all = _dot(xq, eq_ref[...])
    kf_all = _dot(xk, ek_ref[...])
    for h in range(HEADS):
        qf_ref[h] = (qf_all[:, h * LANES:(h + 1) * LANES] + ones_q).astype(BF16)
        kf_ref[h] = (kf_all[:, h * LANES:(h + 1) * LANES] + ones_k).astype(BF16)

    pm = _dot(hb, win_ref[:, COLS_MLA + COLS_FOX:COLS_MLA + COLS_FOX + COLS_MOBA])
    mqn = _head_norm(pm[:, 0:GROUP], mqg_ref[...], bd) * (HEAD_DIM ** -0.5)
    mkn = _head_norm(pm[:, GROUP:2 * GROUP], mkg_ref[...], bd)
    vm_ref[...] = _dot_nt(wvt_ref[1], hb).astype(BF16)
    col_mean = jnp.mean(mkn, axis=0, keepdims=True)
    mqb = mqn.astype(BF16)
    mkb = mkn.astype(BF16)
    blk = lane - HEAD_DIM
    blk_f = blk.astype(F32)
    past = (blk >= 0) & (blk < i)
    i_f = i.astype(F32)
    for h in range(HEADS):
        kmean[pl.ds(h * LANES + HEAD_DIM + i, 1), :] = jnp.where(head_of_lane == h, col_mean, 0.0)
    q_hi, q_lo = _split2(mqn)
    km_hi, km_lo = _split2(kmean[...])
    gate_all = _dot_nt(q_hi, km_hi) + _dot_nt(q_hi, km_lo) + _dot_nt(q_lo, km_hi)
    qm_all = _dot(mqb, sel_ref[...])
    km_all = _dot(mkb, sel_ref[...])
    for h in range(HEADS):
        g = jnp.where(past, gate_all[:, h * LANES:(h + 1) * LANES], NEG)
        chosen = jnp.zeros((tm, LANES), F32)
        for _ in range(MOBA_TOPK):
            m = jnp.max(g, axis=-1, keepdims=True)
            first = jnp.min(jnp.where(g == m, lane_f, 1e9), axis=-1, keepdims=True)
            pick = (lane_f == first) & (m > NEG)
            chosen = jnp.where(pick, 1.0, chosen)
            g = jnp.where(pick, NEG, g)
        slope = 2.0 ** (-(2 * h + 2))
        keep = (chosen > 0.0) | (blk == i)
        bias = jnp.where(keep, (slope * MOBA_BLOCK) * (blk_f - i_f), NEG)
        bias = jnp.where(blk >= 0, bias, 0.0)
        qm_ref[h] = (qm_all[:, h * LANES:(h + 1) * LANES] + bias).astype(BF16)
        onehot = jnp.where(blk == i, 1.0, 0.0)
        km_ref[h] = (km_all[:, h * LANES:(h + 1) * LANES] + onehot).astype(BF16)

    pd = _dot(hb, win_ref[:, COLS_MLA + COLS_FOX + COLS_MOBA:COLS_IN])
    qd_ref[...] = _head_norm(pd[:, 0:GROUP], dqg_ref[...], bd) * (HEAD_DIM ** -0.5)
    kd_ref[...] = _head_norm(pd[:, GROUP:2 * GROUP], dkg_ref[...], bd)
    vd_ref[...] = pd[:, 2 * GROUP:3 * GROUP]


def _prep_constants(seq, tm):
    half = MLA_ROPE // 2
    inv = 1.0 / (ROPE_THETA ** (jnp.arange(half, dtype=F32) / half))
    ang = jnp.arange(seq, dtype=F32)[:, None] * inv[None, :]
    cos, sin = jnp.cos(ang), jnp.sin(ang)
    z = lambda n: jnp.zeros((seq, n), F32)
    rc = jnp.concatenate([jnp.ones((seq, MLA_NOPE), F32), cos, cos, z(LANES - MLA_QK)], axis=1)
    rs1 = jnp.concatenate([z(MLA_NOPE), -sin, z(LANES - MLA_NOPE - half)], axis=1)
    rs2 = jnp.concatenate([z(MLA_NOPE + half), sin, z(LANES - MLA_QK)], axis=1)
    bd =np.kron(np.eye(HEADS, dtype=np.float32), np.ones((HEAD_DIM, HEAD_DIM), np.float32))
    tri = np.tril(np.ones((tm, tm), np.float32))
    tri_strict = np.tril(np.ones((tm, tm), np.float32), -1)
    sel = np.zeros((GROUP, HEADS * LANES), np.float32)
    eq = np.zeros((GROUP + 3 * LANES, HEADS * LANES), np.float32)
    ek = np.zeros((GROUP + 3 * LANES, HEADS * LANES), np.float32)
    for h in range(HEADS):
        for d in range(HEAD_DIM):
            sel[h * HEAD_DIM + d, h * LANES + d] = 1.0
        for piece in range(3):
            eq[GROUP + piece * LANES + h, h * LANES + HEAD_DIM + piece] = 1.0
            ek[GROUP + piece * LANES + h, h * LANES + HEAD_DIM + 3 + piece] = -1.0
    eq[:GROUP] = sel
    ek[:GROUP] = sel
    as_bf = lambda a: jnp.asarray(a, BF16)
    return dict(rc=rc, rs1=rs1, rs2=rs2, bd=as_bf(bd),
                tri=as_bf(tri), tri_strict=as_bf(tri_strict), sel=as_bf(sel), eq=as_bf(eq), ek=as_bf(ek))


def _prep(x2, mod_l, consts, p):
    seq = x2.shape[0]
    tm = TOKEN_TILE
    row = lambda n: pl.BlockSpec((tm, n), lambda i: (i, 0))
    heads = pl.BlockSpec((HEADS, tm, LANES), lambda i: (0, i, 0))
    in_arrays = [
        (x2, row(D_MODEL)), (mod_l, _full_spec(mod_l.shape)), (p["g1"], None), (p["w_in"], None),
        (p["w_vt"], None),
        (p["cq_g"], None), (p["w_uq"], None), (p["ckv_g"], None), (p["w_uk"], None), (p["w_uvt"], None),
        (p["q_g"], None), (p["k_g"], None),
        (consts["rc"], row(LANES)), (consts["rs1"], row(LANES)), (consts["rs2"], row(LANES)),
        (p["fq_g"], None), (p["fk_g"], None), (p["f_b"], None), (p["mq_g"], None), (p["mk_g"], None),
        (p["dq_g"], None), (p["dk_g"], None),
        (consts["bd"], None), (consts["tri"], None), (consts["eq"], None), (consts["ek"], None),
        (consts["sel"], None),
    ]
    args = [a for a, _ in in_arrays]
    specs = [s if s is not None else _full_spec(a.shape) for a, s in in_arrays]
    hshape = jax.ShapeDtypeStruct((HEADS, seq, LANES), BF16)
    vshape = jax.ShapeDtypeStruct((GROUP, seq), BF16)
    dshape = jax.ShapeDtypeStruct((seq, GROUP), F32)
    vt = pl.BlockSpec((GROUP, tm), lambda i: (0, i))
    return pl.pallas_call(
        _prep_kernel,
        grid=(seq // tm,),
        in_specs=specs,
        out_specs=[heads, heads, vt] * 3 + [row(GROUP)] * 3 + [row(LANES)],
        out_shape=[hshape, hshape, vshape] * 3 + [dshape] * 3
                  + [jax.ShapeDtypeStruct((seq, LANES), F32)],
        scratch_shapes=[pltpu.VMEM((1, LANES), F32), pltpu.VMEM((HEADS * LANES, GROUP), F32)],
        compiler_params=_params(("arbitrary",)),
        name="prep",
    )(*args)


def _flash_kernel(jlo_ref, q_ref, k_ref, vt_ref, kbias_ref, o_ref, m_sc, acc_sc, sa_sc, sb_sc,
                  *, tile, use_kbias):
    i = pl.program_id(0)
    m_sc[...] = jnp.full_like(m_sc, -jnp.inf)
    acc_sc[...] = jnp.zeros_like(acc_sc)

    def score(hh, j, buf):
        start = pl.multiple_of(j * tile, tile)
        buf[hh] = _dot_nt(k_ref[hh, pl.ds(start, tile), :], q_ref[hh])

    def absorb(hh, j, buf, causal):
        start = pl.multiple_of(j * tile, tile)
        s = buf[hh]
        if use_kbias:
            s = s + jnp.concatenate([kbias_ref[hh]] * (tile // LANES), axis=1)
        if causal:
            key = lax.broadcasted_iota(jnp.int32, (tile, tile), 0)
            qry = lax.broadcasted_iota(jnp.int32, (tile, tile), 1)
            s = jnp.where(key <= qry, s, NEG)
        m_prev = m_sc[hh]
        m_cur = jnp.max(jnp.max(s.reshape(8, tile // 8, tile), axis=0), axis=0, keepdims=True)
        m_new = jnp.maximum(m_prev, m_cur)
        alpha = jnp.exp(m_prev - m_new)
        p = jnp.exp((s - m_new).astype(BF16))
        vt = vt_ref[hh * HEAD_DIM:(hh + 1) * HEAD_DIM, pl.ds(start, tile)]
        vt = jnp.concatenate([vt, jnp.ones((ONES_ROWS, tile), BF16)], axis=0)
        acc_sc[hh] = alpha * acc_sc[hh] + _dot(vt, p)
        m_sc[hh] = m_new

    first = jlo_ref[i]
    n_off = i - first
    for hh in range(HEADS):
        score(hh, first, sa_sc)

    def body(t, carry):
        j = first + 2 * t
        for hh in range(HEADS):
            score(hh, j + 1, sb_sc)
            absorb(hh, j, sa_sc, False)
        for hh in range(HEADS):
            score(hh, j + 2, sa_sc)
            absorb(hh, j + 1, sb_sc, False)
        return carry

    lax.fori_loop(0, n_off // 2, body, 0)

    @pl.when(n_off % 2 == 1)
    def _():
        for hh in range(HEADS):
            score(hh, i, sb_sc)
            absorb(hh, i - 1, sa_sc, False)
        for hh in range(HEADS):
            absorb(hh, i, sb_sc, True)

    @pl.when(n_off % 2 == 0)
    def _():
        for hh in range(HEADS):
            absorb(hh, i, sa_sc, True)

    o_t = jnp.concatenate([acc_sc[hh, 0:HEAD_DIM, :] / acc_sc[hh, HEAD_DIM:HEAD_DIM + 1, :]
                           for hh in range(HEADS)], axis=0)
    o_ref[...] = o_t.T.astype(o_ref.dtype)


def _flash(q, k, v_t, kbias, first_tile, use_kbias):
    seq = v_t.shape[1]
    tile = FLASH_TILE
    kern = functools.partial(_flash_kernel, tile=tile, use_kbias=use_kbias)
    resident = pl.Buffered(1)
    return pl.pallas_call(
        kern,
        grid_spec=pltpu.PrefetchScalarGridSpec(
            num_scalar_prefetch=1,
            grid=(seq // tile,),
            in_specs=[
                pl.BlockSpec((HEADS, tile, LANES), lambda i, f: (0, i, 0)),
                pl.BlockSpec((HEADS, seq, LANES), lambda i, f: (0, 0, 0), pipeline_mode=resident),
                pl.BlockSpec((GROUP, seq), lambda i, f: (0, 0), pipeline_mode=resident),
                pl.BlockSpec((HEADS, tile, LANES), lambda i, f: (0, 0, 0)),
            ],
            out_specs=pl.BlockSpec((tile, GROUP), lambda i, f: (i, 0)),
            scratch_shapes=[pltpu.VMEM((HEADS, 1, tile), F32),
                            pltpu.VMEM((HEADS, HEAD_DIM + ONES_ROWS, tile), F32),
                            pltpu.VMEM((HEADS, tile, tile), F32), pltpu.VMEM((HEADS, tile, tile), F32)],
        ),
        out_shape=jax.ShapeDtypeStruct((seq, GROUP), BF16),
        compiler_params=_params(("arbitrary",)),
        name="flash_kbias" if use_kbias else "flash",
    )(first_tile, q, k, v_t, kbias)


def _fox_first_tile(decay, qk_bound):
    seq = decay.shape[0]
    nq = seq // FLASH_TILE
    f = decay[:, :HEADS]
    f_first = f[0::FLASH_TILE]
    f_last = f[FLASH_TILE - 1::FLASH_TILE]
    gap = f_first[:, None, :] - f_last[None, :, :] + 2.0 * qk_bound
    jj = jnp.arange(nq, dtype=jnp.int32)
    needed = (gap >= FOX_SKIP_LOG) | (jj[None, :, None] >= jj[:, None, None])
    first = jnp.min(jnp.where(needed, jj[None, :, None], nq), axis=1)
    return jnp.min(first, axis=1).astype(jnp.int32)


def _dilated_kernel(q_ref, k_ref, v_ref, o_ref, kbuf, vbuf, acc_s, m_s, l_s):
    pair = pl.program_id(0)
    i = pl.program_id(1)
    T = q_ref.shape[0]

    @pl.when(i == 0)
    def _():
        kbuf[...] = jnp.zeros_like(kbuf)
        vbuf[...] = jnp.zeros_like(vbuf)

    kbuf[0:T, :] = kbuf[T:2 * T, :]
    vbuf[0:T, :] = vbuf[T:2 * T, :]
    kbuf[T:2 * T, :] = k_ref[...]
    vbuf[T:2 * T, :] = v_ref[...]

    ii = lax.broadcasted_iota(jnp.int32, (DIL_SPAN, 2 * DIL_SPAN), 0)
    jj = lax.broadcasted_iota(jnp.int32, (DIL_SPAN, 2 * DIL_SPAN), 1)
    dist = ii + DIL_SPAN - jj
    band = (dist >= 0) & (dist <= DIL_SPAN)
    dist_f = dist.astype(F32)
    upper = lax.broadcasted_iota(jnp.int32, (1, LANES), 1) >= HEAD_DIM

    for pi, (window, r) in enumerate(DIL_PATTERNS):
        assert window // r == DIL_SPAN
        sub = DIL_SPAN * r

        def body(idx, carry, r=r, sub=sub, pi=pi):
            n = idx // r
            rho = idx - n * r
            base = n * sub + rho
            q = q_ref[pl.ds(base, DIL_SPAN, stride=r), :]
            kc = kbuf[pl.ds(T + base - sub, 2 * DIL_SPAN, stride=r), :].astype(BF16)
            vc = vbuf[pl.ds(T + base - sub, 2 * DIL_SPAN, stride=r), :].astype(BF16)
            first_key = jnp.where((i == 0) & (n == 0), DIL_SPAN, 0)
            valid = band & (jj >= first_key)
            stats = []
            for hh in range(2):
                slope = jnp.where(pair == 0, 2.0 ** (-(2 * hh + 1)), 2.0 ** (-(2 * hh + 5)))
                qh = (jnp.where(upper, q, 0.0) if hh else jnp.where(upper, 0.0, q)).astype(BF16)
                s = _dot_nt(qh, kc) - (slope * r) * dist_f
                s = jnp.where(valid, s, NEG)
                m = jnp.max(s, axis=-1, keepdims=True)
                p = jnp.exp(s - m)
                l = jnp.sum(p, axis=-1, keepdims=True)
                stats.append((_dot(p.astype(BF16), vc), m, l))
            rows = pl.ds(pi * T + base, DIL_SPAN, stride=r)
            acc_s[rows, :] = jnp.where(upper, stats[1][0], stats[0][0])
            m_s[rows, :] = jnp.where(upper, stats[1][1], stats[0][1])
            l_s[rows, :] = jnp.where(upper, stats[1][2], stats[0][2])
            return carry

        lax.fori_loop(0, T // DIL_SPAN, body, 0, unroll=8)

    npat = len(DIL_PATTERNS)
    ms = [m_s[pi * T:(pi + 1) * T, :] for pi in range(npat)]
    m_top = functools.reduce(jnp.maximum, ms)
    num = jnp.zeros((T, LANES), F32)
    den = jnp.zeros((T, LANES), F32)
    for pi in range(npat):
        w = jnp.exp(ms[pi] - m_top)
        num = num + w * acc_s[pi * T:(pi + 1) * T, :]
        den = den + w * l_s[pi * T:(pi + 1) * T, :]
    o_ref[...] = (num / den).astype(o_ref.dtype)


def _dilated(q, k, v):
    seq = q.shape[0]
    T = DIL_TILE
    spec = pl.BlockSpec((T, LANES), lambda p, i: (i, p))
    npat = len(DIL_PATTERNS)
    return pl.pallas_call(
        _dilated_kernel,
        grid=(HEADS // 2, seq // T),
        in_specs=[spec, spec, spec],
        out_specs=spec,
        out_shape=jax.ShapeDtypeStruct((seq, GROUP), BF16),
        scratch_shapes=[pltpu.VMEM((2 * T, LANES), F32), pltpu.VMEM((2 * T, LANES), F32),
                        pltpu.VMEM((npat * T, LANES), F32), pltpu.VMEM((npat * T, LANES), F32),
                        pltpu.VMEM((npat * T, LANES), F32)],
        compiler_params=_params(("arbitrary", "arbitrary")),
        name="dilated",
    )(q, k, v)


def _post_kernel(x_ref, oa_ref, ob_ref, oc_ref, od_ref, wout_ref, mod_ref, g2_ref,
                 rw_ref, rb_ref, tri_ref,
                 x1_ref, h2_ref, eidx_ref, gate_ref, rank_ref, cnt_ref, carry):
    i = pl.program_id(0)
    tm = x_ref.shape[0]

    @pl.when(i == 0)
    def _():
        carry[...] = jnp.zeros_like(carry)

    o = (_dot(oa_ref[...], wout_ref[0]) + _dot(ob_ref[...], wout_ref[1])
         + _dot(oc_ref[...], wout_ref[2]) + _dot(od_ref[...], wout_ref[3]))
    x1 = x_ref[...] + mod_ref[2] * o
    x1_ref[...] = x1
    y = x1 * lax.rsqrt(jnp.mean(x1 * x1, axis=-1, keepdims=True) + EPS) * g2_ref[...]
    h2 = y * (1.0 + mod_ref[4]) + mod_ref[3]
    _store_token_tiles(h2_ref, h2)

    h_hi, h_lo = _split2(h2)
    hh = _dot(h_hi, rw_ref[...])
    logits = (hh[:, :LANES] + hh[:, LANES:] + _dot(h_lo, rw_ref[:, :LANES])
              + rb_ref[...])
    lane = lax.broadcasted_iota(jnp.int32, (tm, LANES), 1)
    lane_f = lane.astype(F32)
    g = logits
    chosen = jnp.zeros((tm, LANES), F32)
    vals, idxs = [], []
    for _ in range(TOP_K):
        m = jnp.max(g, axis=-1, keepdims=True)
        first = jnp.min(jnp.where(g == m, lane_f, 1e9), axis=-1, keepdims=True)
        pick = lane_f == first
        chosen = jnp.where(pick, 1.0, chosen)
        g = jnp.where(pick, -jnp.inf, g)
        vals.append(m)
        idxs.append(first)
    exps = [jnp.exp(v - vals[0]) for v in vals]
    den = exps[0] + exps[1] + exps[2] + exps[3]
    before = _dot(tri_ref[...], chosen.astype(BF16)) + carry[...]
    carry[...] = carry[...] + jnp.sum(chosen, axis=0, keepdims=True)
    cnt_ref[...] = carry[...]
    e_out = jnp.zeros((tm, LANES), F32)
    g_out = jnp.zeros((tm, LANES), F32)
    r_out = jnp.zeros((tm, LANES), F32)
    for k in range(TOP_K):
        rank_k = jnp.sum(jnp.where(lane_f == idxs[k], before, 0.0), axis=-1, keepdims=True)
        e_out = jnp.where(lane == k, idxs[k], e_out)
        g_out = jnp.where(lane == k, exps[k] / den, g_out)
        r_out = jnp.where(lane == k, rank_k, r_out)
    eidx_ref[...] = e_out.astype(jnp.int32)
    gate_ref[...] = g_out
    rank_ref[...] = r_out.astype(jnp.int32)


def _post(x2, oa, ob, oc, od, mod_l, consts, p):
    seq = x2.shape[0]
    tm = TOKEN_TILE
    row = lambda n: pl.BlockSpec((tm, n), lambda i: (i, 0))
    full = [p["w_out"], mod_l, p["g2"], p["rw"], p["r_b"], consts["tri_strict"]]
    f32 = lambda n: jax.ShapeDtypeStruct((seq, n), F32)
    i32 = lambda n: jax.ShapeDtypeStruct((seq, n), jnp.int32)
    return pl.pallas_call(
        _post_kernel,
        grid=(seq // tm,),
        in_specs=[row(D_MODEL)] + [row(GROUP)] * 4 + [_full_spec(a.shape) for a in full],
        out_specs=[row(D_MODEL), pl.BlockSpec((tm * CHUNKS, LANES), lambda i: (i, 0)),
                   row(LANES), row(LANES), row(LANES), _full_spec((1, LANES))],
        out_shape=[f32(D_MODEL), jax.ShapeDtypeStruct((seq * CHUNKS, LANES), F32),
                   i32(LANES), f32(LANES), i32(LANES), jax.ShapeDtypeStruct((1, LANES), F32)],
        scratch_shapes=[pltpu.VMEM((1, LANES), F32)],
        compiler_params=_params(("arbitrary",)),
        name="post",
    )(x2, oa, ob, oc, od, *full)


def _tile_copy(src, s, dst, d, sem):
    return pltpu.make_async_copy(src.at[pl.ds(pl.multiple_of(s * CHUNKS, CHUNKS), CHUNKS), :],
                                 dst.at[pl.ds(pl.multiple_of(d * CHUNKS, CHUNKS), CHUNKS), :], sem)


def _dispatch_kernel(dest_ref, padlo_ref, padn_ref, nu_ref, h_ref, xs_ref, zbuf, sem, zsem):
    i = pl.program_id(0)
    n = TOKEN_TILE * TOP_K
    base = i * n
    block_rows = EXPERT_ROWS * CHUNKS

    @pl.when(i == 0)
    def _():
        zbuf[...] = jnp.zeros_like(zbuf)

        def fill(wait):
            def go(copy):
                copy.wait() if wait else copy.start()

            def per_expert(e, carry):
                lo, cnt = padlo_ref[e], padn_ref[e]
                off = lo
                p = EXPERT_ROWS // 2
                while p >= 1:
                    rows = p * CHUNKS

                    @pl.when((cnt & p) != 0)
                    def _(off=off, rows=rows):
                        go(pltpu.make_async_copy(
                            zbuf.at[pl.ds(0, rows), :],
                            xs_ref.at[pl.ds(pl.multiple_of(off * CHUNKS, CHUNKS), rows), :], zsem))

                    off = off + (cnt & p)
                    p //= 2
                return carry

            lax.fori_loop(0, N_EXPERTS, per_expert, 0)

            def per_block(b, carry):
                go(pltpu.make_async_copy(
                    zbuf, xs_ref.at[pl.ds(pl.multiple_of(b * block_rows, block_rows), block_rows), :],
                    zsem))
                return carry

            lax.fori_loop(nu_ref[0], xs_ref.shape[0] // block_rows, per_block, 0)

        fill(False)
        fill(True)

    def issue(r, carry):
        for k in range(TOP_K):
            _tile_copy(h_ref, r, xs_ref, dest_ref[base + r * TOP_K + k], sem).start(priority=k % 2)
        return carry

    lax.fori_loop(0, TOKEN_TILE, issue, 0, unroll=2)
    rows = pl.ds(0, n * CHUNKS)
    pltpu.make_async_copy(xs_ref.at[rows, :], xs_ref.at[rows, :], sem).wait()


def _dispatch(dest, pad_lo, pad_n, n_used, h2_tiles, m_pad):
    seq = h2_tiles.shape[0] // CHUNKS
    return pl.pallas_call(
        _dispatch_kernel,
        grid_spec=pltpu.PrefetchScalarGridSpec(
            num_scalar_prefetch=4,
            grid=(seq // TOKEN_TILE,),
            in_specs=[pl.BlockSpec((TOKEN_TILE * CHUNKS, LANES), lambda i, *_: (i, 0))],
            out_specs=pl.BlockSpec(memory_space=pl.ANY),
            scratch_shapes=[pltpu.VMEM((EXPERT_ROWS * CHUNKS, LANES), F32),
                            pltpu.SemaphoreType.DMA(()), pltpu.SemaphoreType.DMA(())],
        ),
        out_shape=jax.ShapeDtypeStruct((m_pad * CHUNKS, LANES), F32),
        compiler_params=_params(("arbitrary",)),
        name="dispatch",
    )(dest, pad_lo, pad_n, n_used, h2_tiles)


def _expert_kernel(be_ref, nu_ref, nxt_ref, ord_ref, xs_ref, w1_hbm, b1_ref, w2_hbm, b2_ref, ys_ref,
                   w1f, w2f, w1b, w2b, sems):
    b = pl.program_id(0)
    e = be_ref[b]
    prev = be_ref[jnp.maximum(b - 1, 0)]
    fresh = ((b == 0) | (e != prev)) & (b < nu_ref[0])
    slot = ord_ref[b] % 2

    def fetch(expert, to_slot):
        return (pltpu.make_async_copy(w1_hbm.at[expert], w1f.at[to_slot], sems.at[0, to_slot]),
                pltpu.make_async_copy(w2_hbm.at[expert], w2f.at[to_slot], sems.at[1, to_slot]))

    @pl.when(b == 0)
    def _():
        for copy in fetch(e, slot):
            copy.start()

    @pl.when(fresh)
    def _():
        for copy in fetch(e, slot):
            copy.wait()
        w1b[...] = w1f[slot].astype(BF16)
        w2b[...] = w2f[slot].astype(BF16)

        @pl.when(nxt_ref[b] >= 0)
        def _():
            for copy in fetch(nxt_ref[b], 1 - slot):
                copy.start()

    @pl.when(b < nu_ref[0])
    def _():
        half = EXPERT_ROWS // 2
        gus = []
        for r in range(2):
            xb = jnp.concatenate(
                [_load_token_chunk(xs_ref, half, c, offset=r * half * CHUNKS) for c in range(CHUNKS)],
                axis=1).astype(BF16)
            gus.append(_dot(xb, w1b[...]) + b1_ref[0])
        for r in range(2):
            g = jnp.minimum(gus[r][:, :D_EXPERT], SWIGLU_LIMIT)
            u = jnp.clip(gus[r][:, D_EXPERT:], -SWIGLU_LIMIT, SWIGLU_LIMIT)
            y = (u + 1.0) * g * (1.0 / (1.0 + jnp.exp(-SWIGLU_ALPHA * g)))
            _store_token_tiles(ys_ref, _dot(y.astype(BF16), w2b[...]) + b2_ref[0],
                               offset=r * half * CHUNKS)

    @pl.when(b >= nu_ref[0])
    def _():
        ys_ref[...] = jnp.zeros_like(ys_ref)


def _experts(blk_expert, n_used, blk_next, blk_ord, xs, w1, b1, w2, b2):
    m_pad = xs.shape[0] // CHUNKS
    bm = EXPERT_ROWS
    n_all = w1.shape[0] * w1.shape[1]
    rows = lambda b, be, nu, *_: (jnp.minimum(b, nu[0] - 1), 0)
    ex = lambda b, be, nu, *_: (be[jnp.minimum(b, nu[0] - 1)], 0, 0)
    return pl.pallas_call(
        _expert_kernel,
        grid_spec=pltpu.PrefetchScalarGridSpec(
            num_scalar_prefetch=4,
            grid=(m_pad // bm,),
            in_specs=[
                pl.BlockSpec((bm * CHUNKS, LANES), rows),
                pl.BlockSpec(memory_space=pl.ANY),
                pl.BlockSpec((1, 1, 2 * D_EXPERT), ex),
                pl.BlockSpec(memory_space=pl.ANY),
                pl.BlockSpec((1, 1, D_MODEL), ex),
            ],
            out_specs=pl.BlockSpec((bm * CHUNKS, LANES), lambda b, *_: (b, 0)),
            scratch_shapes=[pltpu.VMEM((2, D_MODEL, 2 * D_EXPERT), F32),
                            pltpu.VMEM((2, D_EXPERT, D_MODEL), F32),
                            pltpu.VMEM((D_MODEL, 2 * D_EXPERT), BF16),
                            pltpu.VMEM((D_EXPERT, D_MODEL), BF16),
                            pltpu.SemaphoreType.DMA((2, 2))],
        ),
        out_shape=jax.ShapeDtypeStruct((m_pad * CHUNKS, LANES), F32),
        compiler_params=_params(("arbitrary",)),
        name="experts",
    )(blk_expert, n_used, blk_next, blk_ord, xs,
      w1.reshape(n_all, D_MODEL, 2 * D_EXPERT), b1.reshape(n_all, 1, -1),
      w2.reshape(n_all, D_EXPERT, D_MODEL), b2.reshape(n_all, 1, -1))


def _combine_kernel(dest_ref, ys_ref, x1_ref, gate_ref, mod_ref, o_ref, buf, sems):
    i = pl.program_id(0)
    tm = x1_ref.shape[0]
    n = tm * TOP_K
    slot = i % 2

    def gather(step, to_slot):
        base = step * n

        def issue(r, carry):
            for k in range(TOP_K):
                _tile_copy(ys_ref, dest_ref[base + r * TOP_K + k], buf, to_slot * n + k * tm + r,
                           sems.at[to_slot]).start(priority=k % 2)
            return carry

        lax.fori_loop(0, tm, issue, 0, unroll=2)

    @pl.when(i == 0)
    def _():
        gather(0, 0)

    @pl.when(i + 1 < pl.num_programs(0))
    def _():
        gather(i + 1, 1 - slot)

    mine = pl.ds(pl.multiple_of(slot * n * CHUNKS, n * CHUNKS), n * CHUNKS)
    pltpu.make_async_copy(ys_ref.at[pl.ds(0, n * CHUNKS), :], buf.at[mine, :], sems.at[slot]).wait()
    gates = gate_ref[...]
    g2 = mod_ref[5]
    for c in range(CHUNKS):
        cols = slice(c * LANES, (c + 1) * LANES)
        mix = jnp.zeros((tm, LANES), F32)
        for k in range(TOP_K):
            mix = mix + gates[:, k:k + 1] * _load_token_chunk(
                buf, tm, c, offset=(slot * n + k * tm) * CHUNKS)
        o_ref[:, cols] = x1_ref[:, cols] + g2[:, cols] * mix


def _combine(dest, ys, x1, gates, mod_l):
    seq = x1.shape[0]
    tm = TOKEN_TILE
    return pl.pallas_call(
        _combine_kernel,
        grid_spec=pltpu.PrefetchScalarGridSpec(
            num_scalar_prefetch=1,
            grid=(seq // tm,),
            in_specs=[
                pl.BlockSpec(memory_space=pl.ANY),
                pl.BlockSpec((tm, D_MODEL), lambda i, d: (i, 0)),
                pl.BlockSpec((tm, LANES), lambda i, d: (i, 0)),
                pl.BlockSpec(mod_l.shape, lambda i, d: (0, 0, 0)),
            ],
            out_specs=pl.BlockSpec((tm, D_MODEL), lambda i, d: (i, 0)),
            scratch_shapes=[pltpu.VMEM((2 * TOP_K * tm * CHUNKS, LANES), F32),
                            pltpu.SemaphoreType.DMA((2,))],
        ),
        out_shape=jax.ShapeDtypeStruct((seq, D_MODEL), F32),
        compiler_params=_params(("arbitrary",)),
        name="combine",
    )(dest, ys, x1, gates, mod_l)


def _moe(l, x1, h2_tiles, eidx, gates, rank, counts, mod_l, w1, b1, w2, b2):
    seq = x1.shape[0]
    bm = EXPERT_ROWS
    m_pad = seq * TOP_K + N_EXPERTS * bm
    cnt = counts[0, :N_EXPERTS].astype(jnp.int32)
    padded = (cnt + bm - 1) // bm * bm
    pad_end = jnp.cumsum(padded)
    pad_start = pad_end - padded
    onehot = eidx[:, :TOP_K, None] == jnp.arange(N_EXPERTS, dtype=jnp.int32)
    start_of = jnp.sum(jnp.where(onehot, pad_start, 0), axis=-1)
    dest = (start_of + rank[:, :TOP_K]).reshape(seq * TOP_K).astype(jnp.int32)
    nblk = m_pad // bm
    blk_start = jnp.arange(nblk, dtype=jnp.int32) * bm
    local = jnp.minimum(jnp.sum(pad_end[None, :] <= blk_start[:, None], axis=1), N_EXPERTS - 1)
    blk_expert = (local + l * N_EXPERTS).astype(jnp.int32)
    n_used = (pad_end[-1:] // bm).astype(jnp.int32)
    ids = jnp.arange(N_EXPERTS, dtype=jnp.int32)
    live = padded > 0
    ordinal = jnp.cumsum(live.astype(jnp.int32)) - 1
    later = live[None, :] & (ids[None, :] > ids[:, None])
    nxt = jnp.min(jnp.where(later, ids[None, :], N_EXPERTS), axis=1)
    nxt = jnp.where(nxt < N_EXPERTS, nxt + l * N_EXPERTS, -1)
    xs = _dispatch(dest, (pad_start + cnt).astype(jnp.int32), (padded - cnt).astype(jnp.int32),
                   n_used, h2_tiles, m_pad)
    ys = _experts(blk_expert, n_used, nxt[local].astype(jnp.int32), ordinal[local].astype(jnp.int32),
                  xs, w1, b1, w2, b2)
    return _combine(dest, ys, x1, gates, mod_l)


def _pad_cols(a, n):
    return jnp.pad(a, ((0, 0), (0, n - a.shape[1])))


def _layer_params(l, w_in, mla_cq_g, mla_w_uq, mla_ckv_g, mla_w_ukv, mla_q_g, mla_k_g,
                  fox_q_g, fox_k_g, fox_b_f, moba_q_g, moba_k_g, dil_q_g, dil_k_g, w_out,
                  norm1_g, norm2_g, router_w, router_b):
    w = w_in[l]
    sizes = [MLA_Q_RANK, MLA_KV_RANK, MLA_ROPE, GROUP, GROUP, GROUP, HEADS] + [GROUP] * 6
    offs = np.concatenate([[0], np.cumsum(sizes)])
    part = [w[:, offs[j]:offs[j + 1]] for j in range(len(sizes))]
    zeros = lambda n: jnp.zeros((D_MODEL, n), F32)
    w_in_r = jnp.concatenate(
        [part[0], part[1], zeros(MLA_NOPE), part[2], zeros(LANES - MLA_QK),
         part[3], part[4], part[6], zeros(LANES - HEADS),
         part[7], part[8], part[10], part[11], part[12]], axis=1).astype(BF16)
    assert w_in_r.shape[1] == COLS_IN
    w_vt = jnp.stack([part[5].T, part[9].T]).astype(BF16)
    w_uq = jnp.pad(mla_w_uq[l].reshape(MLA_Q_RANK, HEADS, MLA_QK),
                   ((0, 0), (0, 0), (0, LANES - MLA_QK))).reshape(MLA_Q_RANK, HEADS * LANES)
    w_ukv = mla_w_ukv[l].reshape(MLA_KV_RANK, HEADS, MLA_NOPE + HEAD_DIM)
    w_uk = jnp.pad(w_ukv[:, :, :MLA_NOPE], ((0, 0), (0, 0), (0, LANES - MLA_NOPE)))
    w_uv = w_ukv[:, :, MLA_NOPE:]
    tile4 = lambda g: jnp.tile(g, HEADS)[None, :]
    rw = _pad_cols(router_w[l], LANES)
    rw_hi = rw.astype(BF16)
    rw_lo = (rw - rw_hi.astype(F32)).astype(BF16)
    r_b = jnp.concatenate([router_b[l], jnp.full((LANES - N_EXPERTS,), NEG, F32)])[None, :]
    return dict(
        g1=norm1_g[l][None, :], g2=norm2_g[l][None, :], w_in=w_in_r, w_vt=w_vt,
        cq_g=mla_cq_g[l][None, :], w_uq=w_uq.astype(BF16), ckv_g=mla_ckv_g[l][None, :],
        w_uk=w_uk.reshape(MLA_KV_RANK, HEADS * LANES).astype(BF16),
        w_uvt=w_uv.reshape(MLA_KV_RANK, GROUP).T.astype(BF16),
        fox_bound=FOX_NORM_SLACK * HEAD_DIM ** 0.5 * jnp.max(jnp.abs(fox_q_g[l]))
        * jnp.max(jnp.abs(fox_k_g[l])),
        q_g=_pad_cols(mla_q_g[l][None, :], LANES), k_g=_pad_cols(mla_k_g[l][None, :], LANES),
        fq_g=tile4(fox_q_g[l]), fk_g=tile4(fox_k_g[l]), f_b=_pad_cols(fox_b_f[l][None, :], LANES),
        mq_g=tile4(moba_q_g[l]), mk_g=tile4(moba_k_g[l]), dq_g=tile4(dil_q_g[l]), dk_g=tile4(dil_k_g[l]),
        w_out=w_out[l].reshape(HEADS, GROUP, D_MODEL).astype(BF16),
        rw=jnp.concatenate([rw_hi, rw_lo], axis=1), r_b=r_b,
    )


def kernel(x, c, w_mod, b_mod, norm1_g, norm2_g, w_in, mla_cq_g, mla_w_uq, mla_ckv_g, mla_w_ukv, mla_q_g, mla_k_g, fox_q_g, fox_k_g, fox_b_f, moba_q_g, moba_k_g, dil_q_g, dil_k_g, w_out, router_w, router_b, exp_w1, exp_b1, exp_w2, exp_b2):
    batch, seq, d = x.shape
    assert batch == 1 and d == D_MODEL
    assert seq % DIL_TILE == 0 and seq // MOBA_BLOCK <= MOBA_MAX_BLOCKS
    depth = w_mod.shape[0]
    consts = _prep_constants(seq, TOKEN_TILE)
    mod = _modulation(c, w_mod, b_mod)
    slopes_c = 2.0 ** (-(2.0 * np.arange(HEADS) + 2.0))
    in_block = np.arange(FLASH_TILE) % MOBA_BLOCK
    kbias_c = jnp.asarray(np.broadcast_to(slopes_c[:, None, None] * in_block[None, :, None],
                                          (HEADS, FLASH_TILE, LANES)), F32)
    kbias_0 = jnp.zeros((HEADS, FLASH_TILE, LANES), F32)
    all_tiles = jnp.zeros((seq // FLASH_TILE,), jnp.int32)
    x2 = x.reshape(seq, d)
    for l in range(depth):
        p = _layer_params(l, w_in, mla_cq_g, mla_w_uq, mla_ckv_g, mla_w_ukv, mla_q_g, mla_k_g,
                          fox_q_g, fox_k_g, fox_b_f, moba_q_g, moba_k_g, dil_q_g, dil_k_g, w_out,
                          norm1_g, norm2_g, router_w, router_b)
        mod_l = mod[l]
        qa, ka, va, qf, kf, vf, qm, km, vm, qd, kd, vd, decay = _prep(x2, mod_l, consts, p)
        oa = _flash(qa, ka, va, kbias_0, all_tiles, False)
        ob = _flash(qf, kf, vf, kbias_0, _fox_first_tile(decay, p["fox_bound"]), False)
        oc = _flash(qm, km, vm, kbias_c, all_tiles, True)
        od = _dilated(qd, kd, vd)
        x1, h2, eidx, gates, rank, counts = _post(x2, oa, ob, oc, od, mod_l, consts, p)
        x2 = _moe(l, x1, h2, eidx, gates, rank, counts, mod_l, exp_w1, exp_b1, exp_w2, exp_b2)
    return x2.reshape(batch, seq, d)
```

```python
import functools

import numpy as np
import jax
import jax.numpy as jnp
from jax import lax
from jax.experimental import pallas as pl
from jax.experimental.pallas import tpu as pltpu

F32 = jnp.float32
BF16 = jnp.bfloat16

D_MODEL = 1024
HEAD_DIM = 64
HEADS = 4
GROUP = HEADS * HEAD_DIM
LANES = 128
CHUNKS = D_MODEL // LANES
MLA_Q_RANK = 256
MLA_KV_RANK = 128
MLA_NOPE = 64
MLA_ROPE = 32
MLA_QK = MLA_NOPE + MLA_ROPE
ROPE_THETA = 10000.0
MOBA_BLOCK = 256
MOBA_TOPK = 3
MOBA_MAX_BLOCKS = 64
DIL_PATTERNS = ((128, 1), (512, 4), (2048, 16))
DIL_SPAN = 128
DIL_TILE = 2048
N_EXPERTS = 32
TOP_K = 4
D_EXPERT = 1024
SWIGLU_LIMIT = 7.0
SWIGLU_ALPHA = 1.702
EPS = 1e-6
NEG = -1e30

FLASH_TILE = 512
ONES_ROWS = 16
FOX_SKIP_LOG = -106.0
FOX_NORM_SLACK = 1.02
TOKEN_TILE = 256
ROUTE_TILE = 512
EXPERT_ROWS = 512
VMEM_LIMIT = 56 * 1024 * 1024

COLS_MLA = 512
COLS_FOX = 640
COLS_MOBA = 512
COLS_DIL = 768
COLS_IN = COLS_MLA + COLS_FOX + COLS_MOBA + COLS_DIL


def _dot(a, b):
    return jnp.dot(a, b, preferred_element_type=F32)


def _dot_nt(a, b):
    return lax.dot_general(a, b, (((1,), (1,)), ((), ())), preferred_element_type=F32)


def _split2(x):
    hi = x.astype(BF16)
    lo = (x - hi.astype(F32)).astype(BF16)
    return hi, lo


def _split3(x):
    a = x.astype(BF16)
    r = x - a.astype(F32)
    b = r.astype(BF16)
    c = (r - b.astype(F32)).astype(BF16)
    return a, b, c


def _head_of_lane():
    return jnp.right_shift(lax.broadcasted_iota(jnp.int32, (1, GROUP), 1), 6)


def _store_token_tiles(ref, x, offset=0):
    n = x.shape[0]
    for c in range(CHUNKS):
        ref[pl.ds(offset + c, n, stride=CHUNKS), :] = x[:, c * LANES:(c + 1) * LANES]


def _load_token_chunk(ref, n, c, offset=0):
    return ref[pl.ds(offset + c, n, stride=CHUNKS), :]


def _full_spec(shape):
    nd = len(shape)
    return pl.BlockSpec(shape, lambda *_: (0,) * nd)


def _params(sem):
    return pltpu.CompilerParams(dimension_semantics=sem, vmem_limit_bytes=VMEM_LIMIT)


def _mod_kernel(c_ref, w_ref, b_ref, o_ref):
    c = c_ref[...]
    s = c * (1.0 / (1.0 + jnp.exp(-c)))
    s8 = jnp.broadcast_to(s, (8, D_MODEL))
    r = jnp.dot(s8, w_ref[0], preferred_element_type=F32, precision=lax.Precision.HIGHEST)
    o_ref[0, 0] = r[0:1, :] + b_ref[0, 0]


def _modulation(c, w_mod, b_mod):
    depth = w_mod.shape[0]
    b4 = b_mod.reshape(depth, 6, 1, D_MODEL)
    return pl.pallas_call(
        _mod_kernel,
        grid=(depth, 6),
        in_specs=[
            pl.BlockSpec((1, D_MODEL), lambda l, j: (0, 0)),
            pl.BlockSpec((1, D_MODEL, D_MODEL), lambda l, j: (l, 0, j)),
            pl.BlockSpec((1, 1, 1, D_MODEL), lambda l, j: (l, j, 0, 0)),
        ],
        out_specs=pl.BlockSpec((1, 1, 1, D_MODEL), lambda l, j: (l, j, 0, 0)),
        out_shape=jax.ShapeDtypeStruct((depth, 6, 1, D_MODEL), F32),
        compiler_params=_params(("arbitrary", "arbitrary")),
        name="modulation",
    )(c, w_mod, b4)


def _head_norm(x, g, bd):
    hi, lo = _split2(x * x)
    ss = _dot(hi, bd) + _dot(lo, bd)
    return x * lax.rsqrt(ss * (1.0 / HEAD_DIM) + EPS) * g


def _prep_kernel(x_ref, mod_ref, g1_ref, win_ref, wvt_ref, cqg_ref, wuq_ref, ckvg_ref, wuk_ref,
                 wuvt_ref, qg_ref, kg_ref, rc_ref, rs1_ref, rs2_ref,
                 fqg_ref, fkg_ref, fb_ref, mqg_ref, mkg_ref, dqg_ref, dkg_ref,
                 bd_ref, tri_ref, eq_ref, ek_ref, sel_ref,
                 qa_ref, ka_ref, va_ref, qf_ref, kf_ref, vf_ref, qm_ref, km_ref, vm_ref,
                 qd_ref, kd_ref, vd_ref, f_ref,
                 fcarry, kmean):
    i = pl.program_id(0)
    tm = x_ref.shape[0]

    @pl.when(i == 0)
    def _():
        fcarry[...] = jnp.zeros_like(fcarry)
        kmean[...] = jnp.zeros_like(kmean)

    x = x_ref[...]
    y = x * lax.rsqrt(jnp.mean(x * x, axis=-1, keepdims=True) + EPS) * g1_ref[...]
    hb = (y * (1.0 + mod_ref[1]) + mod_ref[0]).astype(BF16)
    bd = bd_ref[...]
    lane = lax.broadcasted_iota(jnp.int32, (1, LANES), 1)
    lane_f = lane.astype(F32)
    head_of_lane = _head_of_lane()

    pa = _dot(hb, win_ref[:, 0:COLS_MLA])
    cq = pa[:, 0:MLA_Q_RANK]
    ckv = pa[:, MLA_Q_RANK:MLA_Q_RANK + MLA_KV_RANK]
    kr = pa[:, MLA_Q_RANK + MLA_KV_RANK:COLS_MLA]
    cqn = (cq * lax.rsqrt(jnp.mean(cq * cq, axis=-1, keepdims=True) + EPS) * cqg_ref[...]).astype(BF16)
    ckvn = (ckv * lax.rsqrt(jnp.mean(ckv * ckv, axis=-1, keepdims=True) + EPS) * ckvg_ref[...]).astype(BF16)
    q_all = _dot(cqn, wuq_ref[...])
    k_all = _dot(ckvn, wuk_ref[...])
    va_ref[...] = _dot_nt(wuvt_ref[...], ckvn).astype(BF16)
    rc, rs1, rs2 = rc_ref[...], rs1_ref[...], rs2_ref[...]

    def rope(t):
        return t * rc + pltpu.roll(t, LANES - MLA_ROPE // 2, 1) * rs1 + pltpu.roll(t, MLA_ROPE // 2, 1) * rs2

    for h in range(HEADS):
        q = q_all[:, h * LANES:(h + 1) * LANES]
        q = q * lax.rsqrt(jnp.sum(q * q, axis=-1, keepdims=True) * (1.0 / MLA_QK) + EPS) * qg_ref[...]
        qa_ref[h] = (rope(q) * (MLA_QK ** -0.5)).astype(BF16)
        k = k_all[:, h * LANES:(h + 1) * LANES] + kr
        k = k * lax.rsqrt(jnp.sum(k * k, axis=-1, keepdims=True) * (1.0 / MLA_QK) + EPS) * kg_ref[...]
        ka_ref[h] = rope(k).astype(BF16)

    pf = _dot(hb, win_ref[:, COLS_MLA:COLS_MLA + COLS_FOX])
    fqn = (_head_norm(pf[:, 0:GROUP], fqg_ref[...], bd) * (HEAD_DIM ** -0.5)).astype(BF16)
    fkn = _head_norm(pf[:, GROUP:2 * GROUP], fkg_ref[...], bd).astype(BF16)
    vf_ref[...] = _dot_nt(wvt_ref[0], hb).astype(BF16)
    z = pf[:, 2 * GROUP:2 * GROUP + LANES] + fb_ref[...]
    log_f = jnp.minimum(z, 0.0) - jnp.log(1.0 + jnp.exp(-jnp.abs(z)))
    tri = tri_ref[...]
    a1, a2, a3 = _split3(log_f)
    cum = fcarry[...] + (_dot(tri, a1) + _dot(tri, a2) + _dot(tri, a3))
    fcarry[...] = cum[tm - 1:tm, :]
    f_ref[...] = cum
    f1, f2, f3 = _split3(cum)
    xq = jnp.concatenate([fqn, f1, f2, f3], axis=1)
    xk = jnp.concatenate([fkn, f1, f2, f3], axis=1)
    ones_q = jnp.where((lane >= HEAD_DIM + 3) & (lane < HEAD_DIM + 6), 1.0, 0.0)
    ones_k = jnp.where((lane >= HEAD_DIM) & (lane < HEAD_DIM + 3), 1.0, 0.0)
    qf_all = _dot(xq, eq_ref[...])
    kf_all = _dot(xk, ek_ref[...])
    for h in range(HEADS):
        qf_ref[h] = (qf_all[:, h * LANES:(h + 1) * LANES] + ones_q).astype(BF16)
        kf_ref[h] = (kf_all[:, h * LANES:(h + 1) * LANES] + ones_k).astype(BF16)

    pm = _dot(hb, win_ref[:, COLS_MLA + COLS_FOX:COLS_MLA + COLS_FOX + COLS_MOBA])
    mqn = _head_norm(pm[:, 0:GROUP], mqg_ref[...], bd) * (HEAD_DIM ** -0.5)
    mkn = _head_norm(pm[:, GROUP:2 * GROUP], mkg_ref[...], bd)
    vm_ref[...] = _dot_nt(wvt_ref[1], hb).astype(BF16)
    col_mean = jnp.mean(mkn, axis=0, keepdims=True)
    mqb = mqn.astype(BF16)
    mkb = mkn.astype(BF16)
    blk = lane - HEAD_DIM
    blk_f = blk.astype(F32)
    past = (blk >= 0) & (blk < i)
    i_f = i.astype(F32)
    for h in range(HEADS):
        kmean[pl.ds(h * LANES + HEAD_DIM + i, 1), :] = jnp.where(head_of_lane == h, col_mean, 0.0)
    q_hi, q_lo = _split2(mqn)
    km_hi, km_lo = _split2(kmean[...])
    gate_all = _dot_nt(q_hi, km_hi) + _dot_nt(q_hi, km_lo) + _dot_nt(q_lo, km_hi)
    qm_all = _dot(mqb, sel_ref[...])
    km_all = _dot(mkb, sel_ref[...])
    for h in range(HEADS):
        g = jnp.where(past, gate_all[:, h * LANES:(h + 1) * LANES], NEG)
        chosen = jnp.zeros((tm, LANES), F32)
        for _ in range(MOBA_TOPK):
            m = jnp.max(g, axis=-1, keepdims=True)
            first = jnp.min(jnp.where(g == m, lane_f, 1e9), axis=-1, keepdims=True)
            pick = (lane_f == first) & (m > NEG)
            chosen = jnp.where(pick, 1.0, chosen)
            g = jnp.where(pick, NEG, g)
        slope = 2.0 ** (-(2 * h + 2))
        keep = (chosen > 0.0) | (blk == i)
        bias = jnp.where(keep, (slope * MOBA_BLOCK) * (blk_f - i_f), NEG)
        bias = jnp.where(blk >= 0, bias, 0.0)
        qm_ref[h] = (qm_all[:, h * LANES:(h + 1) * LANES] + bias).astype(BF16)
        onehot = jnp.where(blk == i, 1.0, 0.0)
        km_ref[h] = (km_all[:, h * LANES:(h + 1) * LANES] + onehot).astype(BF16)

    pd = _dot(hb, win_ref[:, COLS_MLA + COLS_FOX + COLS_MOBA:COLS_IN])
    qd_ref[...] = _head_norm(pd[:, 0:GROUP], dqg_ref[...], bd) * (HEAD_DIM ** -0.5)
    kd_ref[...] = _head_norm(pd[:, GROUP:2 * GROUP], dkg_ref[...], bd)
    vd_ref[...] = pd[:, 2 * GROUP:3 * GROUP]


def _prep_constants(seq, tm):
    half = MLA_ROPE // 2
    inv = 1.0 / (ROPE_THETA ** (jnp.arange(half, dtype=F32) / half))
    ang = jnp.arange(seq, dtype=F32)[:, None] * inv[None, :]
    cos, sin = jnp.cos(ang), jnp.sin(ang)
    z = lambda n: jnp.zeros((seq, n), F32)
    rc = jnp.concatenate([jnp.ones((seq, MLA_NOPE), F32), cos, cos, z(LANES - MLA_QK)], axis=1)
    rs1 = jnp.concatenate([z(MLA_NOPE), -sin, z(LANES - MLA_NOPE - half)], axis=1)
    rs2 = jnp.concatenate([z(MLA_NOPE + half), sin, z(LANES - MLA_QK)], axis=1)
    bd =np.kron(np.eye(HEADS, dtype=np.float32), np.ones((HEAD_DIM, HEAD_DIM), np.float32))
    tri = np.tril(np.ones((tm, tm), np.float32))
    tri_strict = np.tril(np.ones((tm, tm), np.float32), -1)
    sel = np.zeros((GROUP, HEADS * LANES), np.float32)
    eq = np.zeros((GROUP + 3 * LANES, HEADS * LANES), np.float32)
    ek = np.zeros((GROUP + 3 * LANES, HEADS * LANES), np.float32)
    for h in range(HEADS):
        for d in range(HEAD_DIM):
            sel[h * HEAD_DIM + d, h * LANES + d] = 1.0
        for piece in range(3):
            eq[GROUP + piece * LANES + h, h * LANES + HEAD_DIM + piece] = 1.0
            ek[GROUP + piece * LANES + h, h * LANES + HEAD_DIM + 3 + piece] = -1.0
    eq[:GROUP] = sel
    ek[:GROUP] = sel
    as_bf = lambda a: jnp.asarray(a, BF16)
    return dict(rc=rc, rs1=rs1, rs2=rs2, bd=as_bf(bd),
                tri=as_bf(tri), tri_strict=as_bf(tri_strict), sel=as_bf(sel), eq=as_bf(eq), ek=as_bf(ek))


def _prep(x2, mod_l, consts, p):
    seq = x2.shape[0]
    tm = TOKEN_TILE
    row = lambda n: pl.BlockSpec((tm, n), lambda i: (i, 0))
    heads = pl.BlockSpec((HEADS, tm, LANES), lambda i: (0, i, 0))
    in_arrays = [
        (x2, row(D_MODEL)), (mod_l, _full_spec(mod_l.shape)), (p["g1"], None), (p["w_in"], None),
        (p["w_vt"], None),
        (p["cq_g"], None), (p["w_uq"], None), (p["ckv_g"], None), (p["w_uk"], None), (p["w_uvt"], None),
        (p["q_g"], None), (p["k_g"], None),
        (consts["rc"], row(LANES)), (consts["rs1"], row(LANES)), (consts["rs2"], row(LANES)),
        (p["fq_g"], None), (p["fk_g"], None), (p["f_b"], None), (p["mq_g"], None), (p["mk_g"], None),
        (p["dq_g"], None), (p["dk_g"], None),
        (consts["bd"], None), (consts["tri"], None), (consts["eq"], None), (consts["ek"], None),
        (consts["sel"], None),
    ]
    args = [a for a, _ in in_arrays]
    specs = [s if s is not None else _full_spec(a.shape) for a, s in in_arrays]
    hshape = jax.ShapeDtypeStruct((HEADS, seq, LANES), BF16)
    vshape = jax.ShapeDtypeStruct((GROUP, seq), BF16)
    dshape = jax.ShapeDtypeStruct((seq, GROUP), F32)
    vt = pl.BlockSpec((GROUP, tm), lambda i: (0, i))
    return pl.pallas_call(
        _prep_kernel,
        grid=(seq // tm,),
        in_specs=specs,
        out_specs=[heads, heads, vt] * 3 + [row(GROUP)] * 3 + [row(LANES)],
        out_shape=[hshape, hshape, vshape] * 3 + [dshape] * 3
                  + [jax.ShapeDtypeStruct((seq, LANES), F32)],
        scratch_shapes=[pltpu.VMEM((1, LANES), F32), pltpu.VMEM((HEADS * LANES, GROUP), F32)],
        compiler_params=_params(("arbitrary",)),
        name="prep",
    )(*args)


def _flash_kernel(jlo_ref, q_ref, k_ref, vt_ref, kbias_ref, o_ref, m_sc, acc_sc, sa_sc, sb_sc,
                  *, tile, use_kbias):
    i = pl.program_id(0)
    m_sc[...] = jnp.full_like(m_sc, -jnp.inf)
    acc_sc[...] = jnp.zeros_like(acc_sc)

    def score(hh, j, buf):
        start = pl.multiple_of(j * tile, tile)
        buf[hh] = _dot_nt(k_ref[hh, pl.ds(start, tile), :], q_ref[hh])

    def absorb(hh, j, buf, causal):
        start = pl.multiple_of(j * tile, tile)
        s = buf[hh]
        if use_kbias:
            s = s + jnp.concatenate([kbias_ref[hh]] * (tile // LANES), axis=1)
        if causal:
            key = lax.broadcasted_iota(jnp.int32, (tile, tile), 0)
            qry = lax.broadcasted_iota(jnp.int32, (tile, tile), 1)
            s = jnp.where(key <= qry, s, NEG)
        m_prev = m_sc[hh]
        m_cur = jnp.max(jnp.max(s.reshape(8, tile // 8, tile), axis=0), axis=0, keepdims=True)
        m_new = jnp.maximum(m_prev, m_cur)
        alpha = jnp.exp(m_prev - m_new)
        p = jnp.exp((s - m_new).astype(BF16))
        vt = vt_ref[hh * HEAD_DIM:(hh + 1) * HEAD_DIM, pl.ds(start, tile)]
        vt = jnp.concatenate([vt, jnp.ones((ONES_ROWS, tile), BF16)], axis=0)
        acc_sc[hh] = alpha * acc_sc[hh] + _dot(vt, p)
        m_sc[hh] = m_new

    first = jlo_ref[i]
    n_off = i - first
    for hh in range(HEADS):
        score(hh, first, sa_sc)

    def body(t, carry):
        j = first + 2 * t
        for hh in range(HEADS):
            score(hh, j + 1, sb_sc)
            absorb(hh, j, sa_sc, False)
        for hh in range(HEADS):
            score(hh, j + 2, sa_sc)
            absorb(hh, j + 1, sb_sc, False)
        return carry

    lax.fori_loop(0, n_off // 2, body, 0)

    @pl.when(n_off % 2 == 1)
    def _():
        for hh in range(HEADS):
            score(hh, i, sb_sc)
            absorb(hh, i - 1, sa_sc, False)
        for hh in range(HEADS):
            absorb(hh, i, sb_sc, True)

    @pl.when(n_off % 2 == 0)
    def _():
        for hh in range(HEADS):
            absorb(hh, i, sa_sc, True)

    o_t = jnp.concatenate([acc_sc[hh, 0:HEAD_DIM, :] / acc_sc[hh, HEAD_DIM:HEAD_DIM + 1, :]
                           for hh in range(HEADS)], axis=0)
    o_ref[...] = o_t.T.astype(o_ref.dtype)


def _flash(q, k, v_t, kbias, first_tile, use_kbias):
    seq = v_t.shape[1]
    tile = FLASH_TILE
    kern = functools.partial(_flash_kernel, tile=tile, use_kbias=use_kbias)
    resident = pl.Buffered(1)
    return pl.pallas_call(
        kern,
        grid_spec=pltpu.PrefetchScalarGridSpec(
            num_scalar_prefetch=1,
            grid=(seq // tile,),
            in_specs=[
                pl.BlockSpec((HEADS, tile, LANES), lambda i, f: (0, i, 0)),
                pl.BlockSpec((HEADS, seq, LANES), lambda i, f: (0, 0, 0), pipeline_mode=resident),
                pl.BlockSpec((GROUP, seq), lambda i, f: (0, 0), pipeline_mode=resident),
                pl.BlockSpec((HEADS, tile, LANES), lambda i, f: (0, 0, 0)),
            ],
            out_specs=pl.BlockSpec((tile, GROUP), lambda i, f: (i, 0)),
            scratch_shapes=[pltpu.VMEM((HEADS, 1, tile), F32),
                            pltpu.VMEM((HEADS, HEAD_DIM + ONES_ROWS, tile), F32),
                            pltpu.VMEM((HEADS, tile, tile), F32), pltpu.VMEM((HEADS, tile, tile), F32)],
        ),
        out_shape=jax.ShapeDtypeStruct((seq, GROUP), BF16),
        compiler_params=_params(("arbitrary",)),
        name="flash_kbias" if use_kbias else "flash",
    )(first_tile, q, k, v_t, kbias)


def _fox_first_tile(decay, qk_bound):
    seq = decay.shape[0]
    nq = seq // FLASH_TILE
    f = decay[:, :HEADS]
    f_first = f[0::FLASH_TILE]
    f_last = f[FLASH_TILE - 1::FLASH_TILE]
    gap = f_first[:, None, :] - f_last[None, :, :] + 2.0 * qk_bound
    jj = jnp.arange(nq, dtype=jnp.int32)
    needed = (gap >= FOX_SKIP_LOG) | (jj[None, :, None] >= jj[:, None, None])
    first = jnp.min(jnp.where(needed, jj[None, :, None], nq), axis=1)
    return jnp.min(first, axis=1).astype(jnp.int32)


def _dilated_kernel(q_ref, k_ref, v_ref, o_ref, kbuf, vbuf, acc_s, m_s, l_s):
    pair = pl.program_id(0)
    i = pl.program_id(1)
    T = q_ref.shape[0]

    @pl.when(i == 0)
    def _():
        kbuf[...] = jnp.zeros_like(kbuf)
        vbuf[...] = jnp.zeros_like(vbuf)

    kbuf[0:T, :] = kbuf[T:2 * T, :]
    vbuf[0:T, :] = vbuf[T:2 * T, :]
    kbuf[T:2 * T, :] = k_ref[...]
    vbuf[T:2 * T, :] = v_ref[...]

    ii = lax.broadcasted_iota(jnp.int32, (DIL_SPAN, 2 * DIL_SPAN), 0)
    jj = lax.broadcasted_iota(jnp.int32, (DIL_SPAN, 2 * DIL_SPAN), 1)
    dist = ii + DIL_SPAN - jj
    band = (dist >= 0) & (dist <= DIL_SPAN)
    dist_f = dist.astype(F32)
    upper = lax.broadcasted_iota(jnp.int32, (1, LANES), 1) >= HEAD_DIM

    for pi, (window, r) in enumerate(DIL_PATTERNS):
        assert window // r == DIL_SPAN
        sub = DIL_SPAN * r

        def body(idx, carry, r=r, sub=sub, pi=pi):
            n = idx // r
            rho = idx - n * r
            base = n * sub + rho
            q = q_ref[pl.ds(base, DIL_SPAN, stride=r), :]
            kc = kbuf[pl.ds(T + base - sub, 2 * DIL_SPAN, stride=r), :].astype(BF16)
            vc = vbuf[pl.ds(T + base - sub, 2 * DIL_SPAN, stride=r), :].astype(BF16)
            first_key = jnp.where((i == 0) & (n == 0), DIL_SPAN, 0)
            valid = band & (jj >= first_key)
            stats = []
            for hh in range(2):
                slope = jnp.where(pair == 0, 2.0 ** (-(2 * hh + 1)), 2.0 ** (-(2 * hh + 5)))
                qh = (jnp.where(upper, q, 0.0) if hh else jnp.where(upper, 0.0, q)).astype(BF16)
                s = _dot_nt(qh, kc) - (slope * r) * dist_f
                s = jnp.where(valid, s, NEG)
                m = jnp.max(s, axis=-1, keepdims=True)
                p = jnp.exp(s - m)
                l = jnp.sum(p, axis=-1, keepdims=True)
                stats.append((_dot(p.astype(BF16), vc), m, l))
            rows = pl.ds(pi * T + base, DIL_SPAN, stride=r)
            acc_s[rows, :] = jnp.where(upper, stats[1][0], stats[0][0])
            m_s[rows, :] = jnp.where(upper, stats[1][1], stats[0][1])
            l_s[rows, :] = jnp.where(upper, stats[1][2], stats[0][2])
            return carry

        lax.fori_loop(0, T // DIL_SPAN, body, 0, unroll=8)

    npat = len(DIL_PATTERNS)
    ms = [m_s[pi * T:(pi + 1) * T, :] for pi in range(npat)]
    m_top = functools.reduce(jnp.maximum, ms)
    num = jnp.zeros((T, LANES), F32)
    den = jnp.zeros((T, LANES), F32)
    for pi in range(npat):
        w = jnp.exp(ms[pi] - m_top)
        num = num + w * acc_s[pi * T:(pi + 1) * T, :]
        den = den + w * l_s[pi * T:(pi + 1) * T, :]
    o_ref[...] = (num / den).astype(o_ref.dtype)


def _dilated(q, k, v):
    seq = q.shape[0]
    T = DIL_TILE
    spec = pl.BlockSpec((T, LANES), lambda p, i: (i, p))
    npat = len(DIL_PATTERNS)
    return pl.pallas_call(
        _dilated_kernel,
        grid=(HEADS // 2, seq // T),
        in_specs=[spec, spec, spec],
        out_specs=spec,
        out_shape=jax.ShapeDtypeStruct((seq, GROUP), BF16),
        scratch_shapes=[pltpu.VMEM((2 * T, LANES), F32), pltpu.VMEM((2 * T, LANES), F32),
                        pltpu.VMEM((npat * T, LANES), F32), pltpu.VMEM((npat * T, LANES), F32),
                        pltpu.VMEM((npat * T, LANES), F32)],
        compiler_params=_params(("arbitrary", "arbitrary")),
        name="dilated",
    )(q, k, v)


def _post_kernel(x_ref, oa_ref, ob_ref, oc_ref, od_ref, wout_ref, mod_ref, g2_ref,
                 rw_ref, rb_ref, tri_ref,
                 x1_ref, h2_ref, eidx_ref, gate_ref, rank_ref, cnt_ref, carry):
    i = pl.program_id(0)
    tm = x_ref.shape[0]

    @pl.when(i == 0)
    def _():
        carry[...] = jnp.zeros_like(carry)

    o = (_dot(oa_ref[...], wout_ref[0]) + _dot(ob_ref[...], wout_ref[1])
         + _dot(oc_ref[...], wout_ref[2]) + _dot(od_ref[...], wout_ref[3]))
    x1 = x_ref[...] + mod_ref[2] * o
    x1_ref[...] = x1
    y = x1 * lax.rsqrt(jnp.mean(x1 * x1, axis=-1, keepdims=True) + EPS) * g2_ref[...]
    h2 = y * (1.0 + mod_ref[4]) + mod_ref[3]
    _store_token_tiles(h2_ref, h2)

    h_hi, h_lo = _split2(h2)
    hh = _dot(h_hi, rw_ref[...])
    logits = (hh[:, :LANES] + hh[:, LANES:] + _dot(h_lo, rw_ref[:, :LANES])
              + rb_ref[...])
    lane = lax.broadcasted_iota(jnp.int32, (tm, LANES), 1)
    lane_f = lane.astype(F32)
    g = logits
    chosen = jnp.zeros((tm, LANES), F32)
    vals, idxs = [], []
    for _ in range(TOP_K):
        m = jnp.max(g, axis=-1, keepdims=True)
        first = jnp.min(jnp.where(g == m, lane_f, 1e9), axis=-1, keepdims=True)
        pick = lane_f == first
        chosen = jnp.where(pick, 1.0, chosen)
        g = jnp.where(pick, -jnp.inf, g)
        vals.append(m)
        idxs.append(first)
    exps = [jnp.exp(v - vals[0]) for v in vals]
    den = exps[0] + exps[1] + exps[2] + exps[3]
    before = _dot(tri_ref[...], chosen.astype(BF16)) + carry[...]
    carry[...] = carry[...] + jnp.sum(chosen, axis=0, keepdims=True)
    cnt_ref[...] = carry[...]
    e_out = jnp.zeros((tm, LANES), F32)
    g_out = jnp.zeros((tm, LANES), F32)
    r_out = jnp.zeros((tm, LANES), F32)
    for k in range(TOP_K):
        rank_k = jnp.sum(jnp.where(lane_f == idxs[k], before, 0.0), axis=-1, keepdims=True)
        e_out = jnp.where(lane == k, idxs[k], e_out)
        g_out = jnp.where(lane == k, exps[k] / den, g_out)
        r_out = jnp.where(lane == k, rank_k, r_out)
    eidx_ref[...] = e_out.astype(jnp.int32)
    gate_ref[...] = g_out
    rank_ref[...] = r_out.astype(jnp.int32)


def _post(x2, oa, ob, oc, od, mod_l, consts, p):
    seq = x2.shape[0]
    tm = TOKEN_TILE
    row = lambda n: pl.BlockSpec((tm, n), lambda i: (i, 0))
    full = [p["w_out"], mod_l, p["g2"], p["rw"], p["r_b"], consts["tri_strict"]]
    f32 = lambda n: jax.ShapeDtypeStruct((seq, n), F32)
    i32 = lambda n: jax.ShapeDtypeStruct((seq, n), jnp.int32)
    return pl.pallas_call(
        _post_kernel,
        grid=(seq // tm,),
        in_specs=[row(D_MODEL)] + [row(GROUP)] * 4 + [_full_spec(a.shape) for a in full],
        out_specs=[row(D_MODEL), pl.BlockSpec((tm * CHUNKS, LANES), lambda i: (i, 0)),
                   row(LANES), row(LANES), row(LANES), _full_spec((1, LANES))],
        out_shape=[f32(D_MODEL), jax.ShapeDtypeStruct((seq * CHUNKS, LANES), F32),
                   i32(LANES), f32(LANES), i32(LANES), jax.ShapeDtypeStruct((1, LANES), F32)],
        scratch_shapes=[pltpu.VMEM((1, LANES), F32)],
        compiler_params=_params(("arbitrary",)),
        name="post",
    )(x2, oa, ob, oc, od, *full)


def _tile_copy(src, s, dst, d, sem):
    return pltpu.make_async_copy(src.at[pl.ds(pl.multiple_of(s * CHUNKS, CHUNKS), CHUNKS), :],
                                 dst.at[pl.ds(pl.multiple_of(d * CHUNKS, CHUNKS), CHUNKS), :], sem)


def _dispatch_kernel(dest_ref, padlo_ref, padn_ref, nu_ref, h_ref, xs_ref, zbuf, sem, zsem):
    i = pl.program_id(0)
    n = ROUTE_TILE * TOP_K
    base = i * n
    block_rows = EXPERT_ROWS * CHUNKS

    @pl.when(i == 0)
    def _():
        zbuf[...] = jnp.zeros_like(zbuf)

        def fill(wait):
            def go(copy):
                copy.wait() if wait else copy.start()

            def per_expert(e, carry):
                lo, cnt = padlo_ref[e], padn_ref[e]
                off = lo
                p = EXPERT_ROWS // 2
                while p >= 1:
                    rows = p * CHUNKS

                    @pl.when((cnt & p) != 0)
                    def _(off=off, rows=rows):
                        go(pltpu.make_async_copy(
                            zbuf.at[pl.ds(0, rows), :],
                            xs_ref.at[pl.ds(pl.multiple_of(off * CHUNKS, CHUNKS), rows), :], zsem))

                    off = off + (cnt & p)
                    p //= 2
                return carry

            lax.fori_loop(0, N_EXPERTS, per_expert, 0)

            def per_block(b, carry):
                go(pltpu.make_async_copy(
                    zbuf, xs_ref.at[pl.ds(pl.multiple_of(b * block_rows, block_rows), block_rows), :],
                    zsem))
                return carry

            lax.fori_loop(nu_ref[0], xs_ref.shape[0] // block_rows, per_block, 0)

        fill(False)
        fill(True)

    def issue(r, carry):
        for k in range(TOP_K):
            _tile_copy(h_ref, r, xs_ref, dest_ref[base + r * TOP_K + k], sem).start(priority=k % 2)
        return carry

    lax.fori_loop(0, ROUTE_TILE, issue, 0, unroll=2)
    rows = pl.ds(0, n * CHUNKS)
    pltpu.make_async_copy(xs_ref.at[rows, :], xs_ref.at[rows, :], sem).wait()


def _dispatch(dest, pad_lo, pad_n, n_used, h2_tiles, m_pad):
    seq = h2_tiles.shape[0] // CHUNKS
    return pl.pallas_call(
        _dispatch_kernel,
        grid_spec=pltpu.PrefetchScalarGridSpec(
            num_scalar_prefetch=4,
            grid=(seq // ROUTE_TILE,),
            in_specs=[pl.BlockSpec((ROUTE_TILE * CHUNKS, LANES), lambda i, *_: (i, 0))],
            out_specs=pl.BlockSpec(memory_space=pl.ANY),
            scratch_shapes=[pltpu.VMEM((EXPERT_ROWS * CHUNKS, LANES), F32),
                            pltpu.SemaphoreType.DMA(()), pltpu.SemaphoreType.DMA(())],
        ),
        out_shape=jax.ShapeDtypeStruct((m_pad * CHUNKS, LANES), F32),
        compiler_params=_params(("arbitrary",)),
        name="dispatch",
    )(dest, pad_lo, pad_n, n_used, h2_tiles)


def _expert_kernel(be_ref, nu_ref, nxt_ref, ord_ref, xs_ref, w1_hbm, b1_ref, w2_hbm, b2_ref, ys_ref,
                   w1f, w2f, w1b, w2b, sems):
    b = pl.program_id(0)
    e = be_ref[b]
    prev = be_ref[jnp.maximum(b - 1, 0)]
    fresh = ((b == 0) | (e != prev)) & (b < nu_ref[0])
    slot = ord_ref[b] % 2

    def fetch(expert, to_slot):
        return (pltpu.make_async_copy(w1_hbm.at[expert], w1f.at[to_slot], sems.at[0, to_slot]),
                pltpu.make_async_copy(w2_hbm.at[expert], w2f.at[to_slot], sems.at[1, to_slot]))

    @pl.when(b == 0)
    def _():
        for copy in fetch(e, slot):
            copy.start()

    @pl.when(fresh)
    def _():
        for copy in fetch(e, slot):
            copy.wait()
        w1b[...] = w1f[slot].astype(BF16)
        w2b[...] = w2f[slot].astype(BF16)

        @pl.when(nxt_ref[b] >= 0)
        def _():
            for copy in fetch(nxt_ref[b], 1 - slot):
                copy.start()

    @pl.when(b < nu_ref[0])
    def _():
        half = EXPERT_ROWS // 2
        gus = []
        for r in range(2):
            xb = jnp.concatenate(
                [_load_token_chunk(xs_ref, half, c, offset=r * half * CHUNKS) for c in range(CHUNKS)],
                axis=1).astype(BF16)
            gus.append(_dot(xb, w1b[...]) + b1_ref[0])
        for r in range(2):
            g = jnp.minimum(gus[r][:, :D_EXPERT], SWIGLU_LIMIT)
            u = jnp.clip(gus[r][:, D_EXPERT:], -SWIGLU_LIMIT, SWIGLU_LIMIT)
            y = (u + 1.0) * g * (1.0 / (1.0 + jnp.exp(-SWIGLU_ALPHA * g)))
            _store_token_tiles(ys_ref, _dot(y.astype(BF16), w2b[...]) + b2_ref[0],
                               offset=r * half * CHUNKS)

    @pl.when(b >= nu_ref[0])
    def _():
        ys_ref[...] = jnp.zeros_like(ys_ref)


def _experts(blk_expert, n_used, blk_next, blk_ord, xs, w1, b1, w2, b2):
    m_pad = xs.shape[0] // CHUNKS
    bm = EXPERT_ROWS
    n_all = w1.shape[0] * w1.shape[1]
    rows = lambda b, be, nu, *_: (jnp.minimum(b, nu[0] - 1), 0)
    ex = lambda b, be, nu, *_: (be[jnp.minimum(b, nu[0] - 1)], 0, 0)
    return pl.pallas_call(
        _expert_kernel,
        grid_spec=pltpu.PrefetchScalarGridSpec(
            num_scalar_prefetch=4,
            grid=(m_pad // bm,),
            in_specs=[
                pl.BlockSpec((bm * CHUNKS, LANES), rows),
                pl.BlockSpec(memory_space=pl.ANY),
                pl.BlockSpec((1, 1, 2 * D_EXPERT), ex),
                pl.BlockSpec(memory_space=pl.ANY),
                pl.BlockSpec((1, 1, D_MODEL), ex),
            ],
            out_specs=pl.BlockSpec((bm * CHUNKS, LANES), lambda b, *_: (b, 0)),
            scratch_shapes=[pltpu.VMEM((2, D_MODEL, 2 * D_EXPERT), F32),
                            pltpu.VMEM((2, D_EXPERT, D_MODEL), F32),
                            pltpu.VMEM((D_MODEL, 2 * D_EXPERT), BF16),
                            pltpu.VMEM((D_EXPERT, D_MODEL), BF16),
                            pltpu.SemaphoreType.DMA((2, 2))],
        ),
        out_shape=jax.ShapeDtypeStruct((m_pad * CHUNKS, LANES), F32),
        compiler_params=_params(("arbitrary",)),
        name="experts",
    )(blk_expert, n_used, blk_next, blk_ord, xs,
      w1.reshape(n_all, D_MODEL, 2 * D_EXPERT), b1.reshape(n_all, 1, -1),
      w2.reshape(n_all, D_EXPERT, D_MODEL), b2.reshape(n_all, 1, -1))


def _combine_kernel(dest_ref, ys_ref, x1_ref, gate_ref, mod_ref, o_ref, buf, sems):
    i = pl.program_id(0)
    tm = x1_ref.shape[0]
    n = tm * TOP_K
    slot = i % 2

    def gather(step, to_slot):
        base = step * n

        def issue(r, carry):
            for k in range(TOP_K):
                _tile_copy(ys_ref, dest_ref[base + r * TOP_K + k], buf, to_slot * n + k * tm + r,
                           sems.at[to_slot]).start(priority=k % 2)
            return carry

        lax.fori_loop(0, tm, issue, 0, unroll=2)

    @pl.when(i == 0)
    def _():
        gather(0, 0)

    @pl.when(i + 1 < pl.num_programs(0))
    def _():
        gather(i + 1, 1 - slot)

    mine = pl.ds(pl.multiple_of(slot * n * CHUNKS, n * CHUNKS), n * CHUNKS)
    pltpu.make_async_copy(ys_ref.at[pl.ds(0, n * CHUNKS), :], buf.at[mine, :], sems.at[slot]).wait()
    gates = gate_ref[...]
    g2 = mod_ref[5]
    for c in range(CHUNKS):
        cols = slice(c * LANES, (c + 1) * LANES)
        mix = jnp.zeros((tm, LANES), F32)
        for k in range(TOP_K):
            mix = mix + gates[:, k:k + 1] * _load_token_chunk(
                buf, tm, c, offset=(slot * n + k * tm) * CHUNKS)
        o_ref[:, cols] = x1_ref[:, cols] + g2[:, cols] * mix


def _combine(dest, ys, x1, gates, mod_l):
    seq = x1.shape[0]
    tm = ROUTE_TILE
    return pl.pallas_call(
        _combine_kernel,
        grid_spec=pltpu.PrefetchScalarGridSpec(
            num_scalar_prefetch=1,
            grid=(seq // tm,),
            in_specs=[
                pl.BlockSpec(memory_space=pl.ANY),
                pl.BlockSpec((tm, D_MODEL), lambda i, d: (i, 0)),
                pl.BlockSpec((tm, LANES), lambda i, d: (i, 0)),
                pl.BlockSpec(mod_l.shape, lambda i, d: (0, 0, 0)),
            ],
            out_specs=pl.BlockSpec((tm, D_MODEL), lambda i, d: (i, 0)),
            scratch_shapes=[pltpu.VMEM((2 * TOP_K * tm * CHUNKS, LANES), F32),
                            pltpu.SemaphoreType.DMA((2,))],
        ),
        out_shape=jax.ShapeDtypeStruct((seq, D_MODEL), F32),
        compiler_params=_params(("arbitrary",)),
        name="combine",
    )(dest, ys, x1, gates, mod_l)


def _moe(l, x1, h2_tiles, eidx, gates, rank, counts, mod_l, w1, b1, w2, b2):
    seq = x1.shape[0]
    bm = EXPERT_ROWS
    m_pad = seq * TOP_K + N_EXPERTS * bm
    cnt = counts[0, :N_EXPERTS].astype(jnp.int32)
    padded = (cnt + bm - 1) // bm * bm
    pad_end = jnp.cumsum(padded)
    pad_start = pad_end - padded
    onehot = eidx[:, :TOP_K, None] == jnp.arange(N_EXPERTS, dtype=jnp.int32)
    start_of = jnp.sum(jnp.where(onehot, pad_start, 0), axis=-1)
    dest = (start_of + rank[:, :TOP_K]).reshape(seq * TOP_K).astype(jnp.int32)
    nblk = m_pad // bm
    blk_start = jnp.arange(nblk, dtype=jnp.int32) * bm
    local = jnp.minimum(jnp.sum(pad_end[None, :] <= blk_start[:, None], axis=1), N_EXPERTS - 1)
    blk_expert = (local + l * N_EXPERTS).astype(jnp.int32)
    n_used = (pad_end[-1:] // bm).astype(jnp.int32)
    ids = jnp.arange(N_EXPERTS, dtype=jnp.int32)
    live = padded > 0
    ordinal = jnp.cumsum(live.astype(jnp.int32)) - 1
    later = live[None, :] & (ids[None, :] > ids[:, None])
    nxt = jnp.min(jnp.where(later, ids[None, :], N_EXPERTS), axis=1)
    nxt = jnp.where(nxt < N_EXPERTS, nxt + l * N_EXPERTS, -1)
    xs = _dispatch(dest, (pad_start + cnt).astype(jnp.int32), (padded - cnt).astype(jnp.int32),
                   n_used, h2_tiles, m_pad)
    ys = _experts(blk_expert, n_used, nxt[local].astype(jnp.int32), ordinal[local].astype(jnp.int32),
                  xs, w1, b1, w2, b2)
    return _combine(dest, ys, x1, gates, mod_l)


def _pad_cols(a, n):
    return jnp.pad(a, ((0, 0), (0, n - a.shape[1])))


def _layer_params(l, w_in, mla_cq_g, mla_w_uq, mla_ckv_g, mla_w_ukv, mla_q_g, mla_k_g,
                  fox_q_g, fox_k_g, fox_b_f, moba_q_g, moba_k_g, dil_q_g, dil_k_g, w_out,
                  norm1_g, norm2_g, router_w, router_b):
    w = w_in[l]
    sizes = [MLA_Q_RANK, MLA_KV_RANK, MLA_ROPE, GROUP, GROUP, GROUP, HEADS] + [GROUP] * 6
    offs = np.concatenate([[0], np.cumsum(sizes)])
    part = [w[:, offs[j]:offs[j + 1]] for j in range(len(sizes))]
    zeros = lambda n: jnp.zeros((D_MODEL, n), F32)
    w_in_r = jnp.concatenate(
        [part[0], part[1], zeros(MLA_NOPE), part[2], zeros(LANES - MLA_QK),
         part[3], part[4], part[6], zeros(LANES - HEADS),
         part[7], part[8], part[10], part[11], part[12]], axis=1).astype(BF16)
    assert w_in_r.shape[1] == COLS_IN
    w_vt = jnp.stack([part[5].T, part[9].T]).astype(BF16)
    w_uq = jnp.pad(mla_w_uq[l].reshape(MLA_Q_RANK, HEADS, MLA_QK),
                   ((0, 0), (0, 0), (0, LANES - MLA_QK))).reshape(MLA_Q_RANK, HEADS * LANES)
    w_ukv = mla_w_ukv[l].reshape(MLA_KV_RANK, HEADS, MLA_NOPE + HEAD_DIM)
    w_uk = jnp.pad(w_ukv[:, :, :MLA_NOPE], ((0, 0), (0, 0), (0, LANES - MLA_NOPE)))
    w_uv = w_ukv[:, :, MLA_NOPE:]
    tile4 = lambda g: jnp.tile(g, HEADS)[None, :]
    rw = _pad_cols(router_w[l], LANES)
    rw_hi = rw.astype(BF16)
    rw_lo = (rw - rw_hi.astype(F32)).astype(BF16)
    r_b = jnp.concatenate([router_b[l], jnp.full((LANES - N_EXPERTS,), NEG, F32)])[None, :]
    return dict(
        g1=norm1_g[l][None, :], g2=norm2_g[l][None, :], w_in=w_in_r, w_vt=w_vt,
        cq_g=mla_cq_g[l][None, :], w_uq=w_uq.astype(BF16), ckv_g=mla_ckv_g[l][None, :],
        w_uk=w_uk.reshape(MLA_KV_RANK, HEADS * LANES).astype(BF16),
        w_uvt=w_uv.reshape(MLA_KV_RANK, GROUP).T.astype(BF16),
        fox_bound=FOX_NORM_SLACK * HEAD_DIM ** 0.5 * jnp.max(jnp.abs(fox_q_g[l]))
        * jnp.max(jnp.abs(fox_k_g[l])),
        q_g=_pad_cols(mla_q_g[l][None, :], LANES), k_g=_pad_cols(mla_k_g[l][None, :], LANES),
        fq_g=tile4(fox_q_g[l]), fk_g=tile4(fox_k_g[l]), f_b=_pad_cols(fox_b_f[l][None, :], LANES),
        mq_g=tile4(moba_q_g[l]), mk_g=tile4(moba_k_g[l]), dq_g=tile4(dil_q_g[l]), dk_g=tile4(dil_k_g[l]),
        w_out=w_out[l].reshape(HEADS, GROUP, D_MODEL).astype(BF16),
        rw=jnp.concatenate([rw_hi, rw_lo], axis=1), r_b=r_b,
    )


def kernel(x, c, w_mod, b_mod, norm1_g, norm2_g, w_in, mla_cq_g, mla_w_uq, mla_ckv_g, mla_w_ukv, mla_q_g, mla_k_g, fox_q_g, fox_k_g, fox_b_f, moba_q_g, moba_k_g, dil_q_g, dil_k_g, w_out, router_w, router_b, exp_w1, exp_b1, exp_w2, exp_b2):
    batch, seq, d = x.shape
    assert batch == 1 and d == D_MODEL
    assert seq % DIL_TILE == 0 and seq // MOBA_BLOCK <= MOBA_MAX_BLOCKS
    depth = w_mod.shape[0]
    consts = _prep_constants(seq, TOKEN_TILE)
    mod = _modulation(c, w_mod, b_mod)
    slopes_c = 2.0 ** (-(2.0 * np.arange(HEADS) + 2.0))
    in_block = np.arange(FLASH_TILE) % MOBA_BLOCK
    kbias_c = jnp.asarray(np.broadcast_to(slopes_c[:, None, None] * in_block[None, :, None],
                                          (HEADS, FLASH_TILE, LANES)), F32)
    kbias_0 = jnp.zeros((HEADS, FLASH_TILE, LANES), F32)
    all_tiles = jnp.zeros((seq // FLASH_TILE,), jnp.int32)
    x2 = x.reshape(seq, d)
    for l in range(depth):
        p = _layer_params(l, w_in, mla_cq_g, mla_w_uq, mla_ckv_g, mla_w_ukv, mla_q_g, mla_k_g,
                          fox_q_g, fox_k_g, fox_b_f, moba_q_g, moba_k_g, dil_q_g, dil_k_g, w_out,
                          norm1_g, norm2_g, router_w, router_b)
        mod_l = mod[l]
        qa, ka, va, qf, kf, vf, qm, km, vm, qd, kd, vd, decay = _prep(x2, mod_l, consts, p)
        oa = _flash(qa, ka, va, kbias_0, all_tiles, False)
        ob = _flash(qf, kf, vf, kbias_0, _fox_first_tile(decay, p["fox_bound"]), False)
        oc = _flash(qm, km, vm, kbias_c, all_tiles, True)
        od = _dilated(qd, kd, vd)
        x1, h2, eidx, gates, rank, counts = _post(x2, oa, ob, oc, od, mod_l, consts, p)
        x2 = _moe(l, x1, h2, eidx, gates, rank, counts, mod_l, exp_w1, exp_b1, exp_w2, exp_b2)
    return x2.reshape(batch, seq, d)
```

```python
import functools

import numpy as np
import jax
import jax.numpy as jnp
from jax import lax
from jax.experimental import pallas as pl
from jax.experimental.pallas import tpu as pltpu

F32 = jnp.float32
BF16 = jnp.bfloat16

D_MODEL = 1024
HEAD_DIM = 64
HEADS = 4
GROUP = HEADS * HEAD_DIM
LANES = 128
CHUNKS = D_MODEL // LANES
MLA_Q_RANK = 256
MLA_KV_RANK = 128
MLA_NOPE = 64
MLA_ROPE = 32
MLA_QK = MLA_NOPE + MLA_ROPE
ROPE_THETA = 10000.0
MOBA_BLOCK = 256
MOBA_TOPK = 3
MOBA_MAX_BLOCKS = 64
DIL_PATTERNS = ((128, 1), (512, 4), (2048, 16))
DIL_SPAN = 128
DIL_TILE = 2048
N_EXPERTS = 32
TOP_K = 4
D_EXPERT = 1024
SWIGLU_LIMIT = 7.0
SWIGLU_ALPHA = 1.702
EPS = 1e-6
NEG = -1e30

FLASH_TILE = 512
ONES_ROWS = 16
FOX_SKIP_LOG = -106.0
FOX_NORM_SLACK = 1.02
TOKEN_TILE = 256
ROUTE_TILE = 512
EXPERT_ROWS = 512
VMEM_LIMIT = 56 * 1024 * 1024

COLS_MLA = 512
COLS_FOX = 640
COLS_MOBA = 512
COLS_DIL = 768
COLS_IN = COLS_MLA + COLS_FOX + COLS_MOBA + COLS_DIL


def _dot(a, b):
    return jnp.dot(a, b, preferred_element_type=F32)


def _dot_nt(a, b):
    return lax.dot_general(a, b, (((1,), (1,)), ((), ())), preferred_element_type=F32)


def _split2(x):
    hi = x.astype(BF16)
    lo = (x - hi.astype(F32)).astype(BF16)
    return hi, lo


def _split3(x):
    a = x.astype(BF16)
    r = x - a.astype(F32)
    b = r.astype(BF16)
    c = (r - b.astype(F32)).astype(BF16)
    return a, b, c


def _head_of_lane():
    return jnp.right_shift(lax.broadcasted_iota(jnp.int32, (1, GROUP), 1), 6)


def _store_token_tiles(ref, x, offset=0):
    n = x.shape[0]
    for c in range(CHUNKS):
        ref[pl.ds(offset + c, n, stride=CHUNKS), :] = x[:, c * LANES:(c + 1) * LANES]


def _load_token_chunk(ref, n, c, offset=0):
    return ref[pl.ds(offset + c, n, stride=CHUNKS), :]


def _full_spec(shape):
    nd = len(shape)
    return pl.BlockSpec(shape, lambda *_: (0,) * nd)


def _params(sem):
    return pltpu.CompilerParams(dimension_semantics=sem, vmem_limit_bytes=VMEM_LIMIT)


def _mod_kernel(c_ref, w_ref, b_ref, o_ref):
    c = c_ref[...]
    s = c * (1.0 / (1.0 + jnp.exp(-c)))
    s8 = jnp.broadcast_to(s, (8, D_MODEL))
    r = jnp.dot(s8, w_ref[0], preferred_element_type=F32, precision=lax.Precision.HIGHEST)
    o_ref[0, 0] = r[0:1, :] + b_ref[0, 0]


def _modulation(c, w_mod, b_mod):
    depth = w_mod.shape[0]
    b4 = b_mod.reshape(depth, 6, 1, D_MODEL)
    return pl.pallas_call(
        _mod_kernel,
        grid=(depth, 6),
        in_specs=[
            pl.BlockSpec((1, D_MODEL), lambda l, j: (0, 0)),
            pl.BlockSpec((1, D_MODEL, D_MODEL), lambda l, j: (l, 0, j)),
            pl.BlockSpec((1, 1, 1, D_MODEL), lambda l, j: (l, j, 0, 0)),
        ],
        out_specs=pl.BlockSpec((1, 1, 1, D_MODEL), lambda l, j: (l, j, 0, 0)),
        out_shape=jax.ShapeDtypeStruct((depth, 6, 1, D_MODEL), F32),
        compiler_params=_params(("arbitrary", "arbitrary")),
        name="modulation",
    )(c, w_mod, b4)


def _head_norm(x, g, bd):
    hi, lo = _split2(x * x)
    ss = _dot(hi, bd) + _dot(lo, bd)
    return x * lax.rsqrt(ss * (1.0 / HEAD_DIM) + EPS) * g


def _prep_kernel(x_ref, mod_ref, g1_ref, win_ref, wvt_ref, cqg_ref, wuq_ref, ckvg_ref, wuk_ref,
                 wuvt_ref, qg_ref, kg_ref, rc_ref, rs1_ref, rs2_ref,
                 fqg_ref, fkg_ref, fb_ref, mqg_ref, mkg_ref, dqg_ref, dkg_ref,
                 bd_ref, tri_ref, eq_ref, ek_ref, sel_ref,
                 qa_ref, ka_ref, va_ref, qf_ref, kf_ref, vf_ref, qm_ref, km_ref, vm_ref,
                 qd_ref, kd_ref, vd_ref, f_ref,
                 fcarry, kmean):
    i = pl.program_id(0)
    tm = x_ref.shape[0]

    @pl.when(i == 0)
    def _():
        fcarry[...] = jnp.zeros_like(fcarry)
        kmean[...] = jnp.zeros_like(kmean)

    x = x_ref[...]
    y = x * lax.rsqrt(jnp.mean(x * x, axis=-1, keepdims=True) + EPS) * g1_ref[...]
    hb = (y * (1.0 + mod_ref[1]) + mod_ref[0]).astype(BF16)
    bd = bd_ref[...]
    lane = lax.broadcasted_iota(jnp.int32, (1, LANES), 1)
    lane_f = lane.astype(F32)
    head_of_lane = _head_of_lane()

    pa = _dot(hb, win_ref[:, 0:COLS_MLA])
    cq = pa[:, 0:MLA_Q_RANK]
    ckv = pa[:, MLA_Q_RANK:MLA_Q_RANK + MLA_KV_RANK]
    kr = pa[:, MLA_Q_RANK + MLA_KV_RANK:COLS_MLA]
    cqn = (cq * lax.rsqrt(jnp.mean(cq * cq, axis=-1, keepdims=True) + EPS) * cqg_ref[...]).astype(BF16)
    ckvn = (ckv * lax.rsqrt(jnp.mean(ckv * ckv, axis=-1, keepdims=True) + EPS) * ckvg_ref[...]).astype(BF16)
    q_all = _dot(cqn, wuq_ref[...])
    k_all = _dot(ckvn, wuk_ref[...])
    va_ref[...] = _dot_nt(wuvt_ref[...], ckvn).astype(BF16)
    rc, rs1, rs2 = rc_ref[...], rs1_ref[...], rs2_ref[...]

    def rope(t):
        return t * rc + pltpu.roll(t, LANES - MLA_ROPE // 2, 1) * rs1 + pltpu.roll(t, MLA_ROPE // 2, 1) * rs2

    for h in range(HEADS):
        q = q_all[:, h * LANES:(h + 1) * LANES]
        q = q * lax.rsqrt(jnp.sum(q * q, axis=-1, keepdims=True) * (1.0 / MLA_QK) + EPS) * qg_ref[...]
        qa_ref[h] = (rope(q) * (MLA_QK ** -0.5)).astype(BF16)
        k = k_all[:, h * LANES:(h + 1) * LANES] + kr
        k = k * lax.rsqrt(jnp.sum(k * k, axis=-1, keepdims=True) * (1.0 / MLA_QK) + EPS) * kg_ref[...]
        ka_ref[h] = rope(k).astype(BF16)

    pf = _dot(hb, win_ref[:, COLS_MLA:COLS_MLA + COLS_FOX])
    fqn = (_head_norm(pf[:, 0:GROUP], fqg_ref[...], bd) * (HEAD_DIM ** -0.5)).astype(BF16)
    fkn = _head_norm(pf[:, GROUP:2 * GROUP], fkg_ref[...], bd).astype(BF16)
    vf_ref[...] = _dot_nt(wvt_ref[0], hb).astype(BF16)
    z = pf[:, 2 * GROUP:2 * GROUP + LANES] + fb_ref[...]
    log_f = jnp.minimum(z, 0.0) - jnp.log(1.0 + jnp.exp(-jnp.abs(z)))
    tri = tri_ref[...]
    a1, a2, a3 = _split3(log_f)
    cum = fcarry[...] + (_dot(tri, a1) + _dot(tri, a2) + _dot(tri, a3))
    fcarry[...] = cum[tm - 1:tm, :]
    f_ref[...] = cum
    f1, f2, f3 = _split3(cum)
    xq = jnp.concatenate([fqn, f1, f2, f3], axis=1)
    xk = jnp.concatenate([fkn, f1, f2, f3], axis=1)
    ones_q = jnp.where((lane >= HEAD_DIM + 3) & (lane < HEAD_DIM + 6), 1.0, 0.0)
    ones_k = jnp.where((lane >= HEAD_DIM) & (lane < HEAD_DIM + 3), 1.0, 0.0)
    qf_all = _dot(xq, eq_ref[...])
    kf_all = _dot(xk, ek_ref[...])
    for h in range(HEADS):
        qf_ref[h] = (qf_all[:, h * LANES:(h + 1) * LANES] + ones_q).astype(BF16)
        kf_ref[h] = (kf_all[:, h * LANES:(h + 1) * LANES] + ones_k).astype(BF16)

    pm = _dot(hb, win_ref[:, COLS_MLA + COLS_FOX:COLS_MLA + COLS_FOX + COLS_MOBA])
    mqn = _head_norm(pm[:, 0:GROUP], mqg_ref[...], bd) * (HEAD_DIM ** -0.5)
    mkn = _head_norm(pm[:, GROUP:2 * GROUP], mkg_ref[...], bd)
    vm_ref[...] = _dot_nt(wvt_ref[1], hb).astype(BF16)
    col_mean = jnp.mean(mkn, axis=0, keepdims=True)
    mqb = mqn.astype(BF16)
    mkb = mkn.astype(BF16)
    blk = lane - HEAD_DIM
    blk_f = blk.astype(F32)
    past = (blk >= 0) & (blk < i)
    i_f = i.astype(F32)
    for h in range(HEADS):
        kmean[pl.ds(h * LANES + HEAD_DIM + i, 1), :] = jnp.where(head_of_lane == h, col_mean, 0.0)
    q_hi, q_lo = _split2(mqn)
    km_hi, km_lo = _split2(kmean[...])
    gate_all = _dot_nt(q_hi, km_hi) + _dot_nt(q_hi, km_lo) + _dot_nt(q_lo, km_hi)
    qm_all = _dot(mqb, sel_ref[...])
    km_all = _dot(mkb, sel_ref[...])
    for h in range(HEADS):
        g = jnp.where(past, gate_all[:, h * LANES:(h + 1) * LANES], NEG)
        chosen = jnp.zeros((tm, LANES), F32)
        for _ in range(MOBA_TOPK):
            m = jnp.max(g, axis=-1, keepdims=True)
            first = jnp.min(jnp.where(g == m, lane_f, 1e9), axis=-1, keepdims=True)
            pick = (lane_f == first) & (m > NEG)
            chosen = jnp.where(pick, 1.0, chosen)
            g = jnp.where(pick, NEG, g)
        slope = 2.0 ** (-(2 * h + 2))
        keep = (chosen > 0.0) | (blk == i)
        bias = jnp.where(keep, (slope * MOBA_BLOCK) * (blk_f - i_f), NEG)
        bias = jnp.where(blk >= 0, bias, 0.0)
        qm_ref[h] = (qm_all[:, h * LANES:(h + 1) * LANES] + bias).astype(BF16)
        onehot = jnp.where(blk == i, 1.0, 0.0)
        km_ref[h] = (km_all[:, h * LANES:(h + 1) * LANES] + onehot).astype(BF16)

    pd = _dot(hb, win_ref[:, COLS_MLA + COLS_FOX + COLS_MOBA:COLS_IN])
    qd_ref[...] = _head_norm(pd[:, 0:GROUP], dqg_ref[...], bd) * (HEAD_DIM ** -0.5)
    kd_ref[...] = _head_norm(pd[:, GROUP:2 * GROUP], dkg_ref[...], bd)
    vd_ref[...] = pd[:, 2 * GROUP:3 * GROUP]


def _prep_constants(seq, tm):
    half = MLA_ROPE // 2
    inv = 1.0 / (ROPE_THETA ** (jnp.arange(half, dtype=F32) / half))
    ang = jnp.arange(seq, dtype=F32)[:, None] * inv[None, :]
    cos, sin = jnp.cos(ang), jnp.sin(ang)
    z = lambda n: jnp.zeros((seq, n), F32)
    rc = jnp.concatenate([jnp.ones((seq, MLA_NOPE), F32), cos, cos, z(LANES - MLA_QK)], axis=1)
    rs1 = jnp.concatenate([z(MLA_NOPE), -sin, z(LANES - MLA_NOPE - half)], axis=1)
    rs2 = jnp.concatenate([z(MLA_NOPE + half), sin, z(LANES - MLA_QK)], axis=1)
    bd =np.kron(np.eye(HEADS, dtype=np.float32), np.ones((HEAD_DIM, HEAD_DIM), np.float32))
    tri = np.tril(np.ones((tm, tm), np.float32))
    tri_strict = np.tril(np.ones((tm, tm), np.float32), -1)
    sel = np.zeros((GROUP, HEADS * LANES), np.float32)
    eq = np.zeros((GROUP + 3 * LANES, HEADS * LANES), np.float32)
    ek = np.zeros((GROUP + 3 * LANES, HEADS * LANES), np.float32)
    for h in range(HEADS):
        for d in range(HEAD_DIM):
            sel[h * HEAD_DIM + d, h * LANES + d] = 1.0
        for piece in range(3):
            eq[GROUP + piece * LANES + h, h * LANES + HEAD_DIM + piece] = 1.0
            ek[GROUP + piece * LANES + h, h * LANES + HEAD_DIM + 3 + piece] = -1.0
    eq[:GROUP] = sel
    ek[:GROUP] = sel
    as_bf = lambda a: jnp.asarray(a, BF16)
    return dict(rc=rc, rs1=rs1, rs2=rs2, bd=as_bf(bd),
                tri=as_bf(tri), tri_strict=as_bf(tri_strict), sel=as_bf(sel), eq=as_bf(eq), ek=as_bf(ek))


def _prep(x2, mod_l, consts, p):
    seq = x2.shape[0]
    tm = TOKEN_TILE
    row = lambda n: pl.BlockSpec((tm, n), lambda i: (i, 0))
    heads = pl.BlockSpec((HEADS, tm, LANES), lambda i: (0, i, 0))
    in_arrays = [
        (x2, row(D_MODEL)), (mod_l, _full_spec(mod_l.shape)), (p["g1"], None), (p["w_in"], None),
        (p["w_vt"], None),
        (p["cq_g"], None), (p["w_uq"], None), (p["ckv_g"], None), (p["w_uk"], None), (p["w_uvt"], None),
        (p["q_g"], None), (p["k_g"], None),
        (consts["rc"], row(LANES)), (consts["rs1"], row(LANES)), (consts["rs2"], row(LANES)),
        (p["fq_g"], None), (p["fk_g"], None), (p["f_b"], None), (p["mq_g"], None), (p["mk_g"], None),
        (p["dq_g"], None), (p["dk_g"], None),
        (consts["bd"], None), (consts["tri"], None), (consts["eq"], None), (consts["ek"], None),
        (consts["sel"], None),
    ]
    args = [a for a, _ in in_arrays]
    specs = [s if s is not None else _full_spec(a.shape) for a, s in in_arrays]
    hshape = jax.ShapeDtypeStruct((HEADS, seq, LANES), BF16)
    vshape = jax.ShapeDtypeStruct((GROUP, seq), BF16)
    dshape = jax.ShapeDtypeStruct((seq, GROUP), F32)
    vt = pl.BlockSpec((GROUP, tm), lambda i: (0, i))
    return pl.pallas_call(
        _prep_kernel,
        grid=(seq // tm,),
        in_specs=specs,
        out_specs=[heads, heads, vt] * 3 + [row(GROUP)] * 3 + [row(LANES)],
        out_shape=[hshape, hshape, vshape] * 3 + [dshape] * 3
                  + [jax.ShapeDtypeStruct((seq, LANES), F32)],
        scratch_shapes=[pltpu.VMEM((1, LANES), F32), pltpu.VMEM((HEADS * LANES, GROUP), F32)],
        compiler_params=_params(("arbitrary",)),
        name="prep",
    )(*args)


def _flash_kernel(jlo_ref, q_ref, k_ref, vt_ref, kbias_ref, o_ref, m_sc, acc_sc, sa_sc, sb_sc,
                  *, tile, use_kbias):
    i = pl.program_id(0)
    m_sc[...] = jnp.full_like(m_sc, -jnp.inf)
    acc_sc[...] = jnp.zeros_like(acc_sc)

    def score(hh, j, buf):
        start = pl.multiple_of(j * tile, tile)
        buf[hh] = _dot_nt(k_ref[hh, pl.ds(start, tile), :], q_ref[hh])

    def absorb(hh, j, buf, causal):
        start = pl.multiple_of(j * tile, tile)
        s = buf[hh]
        if use_kbias:
            s = s + jnp.concatenate([kbias_ref[hh]] * (tile // LANES), axis=1)
        if causal:
            key = lax.broadcasted_iota(jnp.int32, (tile, tile), 0)
            qry = lax.broadcasted_iota(jnp.int32, (tile, tile), 1)
            s = jnp.where(key <= qry, s, NEG)
        m_prev = m_sc[hh]
        m_cur = jnp.max(jnp.max(s.reshape(8, tile // 8, tile), axis=0), axis=0, keepdims=True)
        m_new = jnp.maximum(m_prev, m_cur)
        alpha = jnp.exp(m_prev - m_new)
        p = jnp.exp((s - m_new).astype(BF16))
        vt = vt_ref[hh * HEAD_DIM:(hh + 1) * HEAD_DIM, pl.ds(start, tile)]
        vt = jnp.concatenate([vt, jnp.ones((ONES_ROWS, tile), BF16)], axis=0)
        acc_sc[hh] = alpha * acc_sc[hh] + _dot(vt, p)
        m_sc[hh] = m_new

    first = jlo_ref[i]
    n_off = i - first
    for hh in range(HEADS):
        score(hh, first, sa_sc)

    def body(t, carry):
        j = first + 2 * t
        for hh in range(HEADS):
            score(hh, j + 1, sb_sc)
            absorb(hh, j, sa_sc, False)
        for hh in range(HEADS):
            score(hh, j + 2, sa_sc)
            absorb(hh, j + 1, sb_sc, False)
        return carry

    lax.fori_loop(0, n_off // 2, body, 0)

    @pl.when(n_off % 2 == 1)
    def _():
        for hh in range(HEADS):
            score(hh, i, sb_sc)
            absorb(hh, i - 1, sa_sc, False)
        for hh in range(HEADS):
            absorb(hh, i, sb_sc, True)

    @pl.when(n_off % 2 == 0)
    def _():
        for hh in range(HEADS):
            absorb(hh, i, sa_sc, True)

    o_t = jnp.concatenate([acc_sc[hh, 0:HEAD_DIM, :] / acc_sc[hh, HEAD_DIM:HEAD_DIM + 1, :]
                           for hh in range(HEADS)], axis=0)
    o_ref[...] = o_t.T.astype(o_ref.dtype)


def _flash(q, k, v_t, kbias, first_tile, use_kbias):
    seq = v_t.shape[1]
    tile = FLASH_TILE
    kern = functools.partial(_flash_kernel, tile=tile, use_kbias=use_kbias)
    resident = pl.Buffered(1)
    return pl.pallas_call(
        kern,
        grid_spec=pltpu.PrefetchScalarGridSpec(
            num_scalar_prefetch=1,
            grid=(seq // tile,),
            in_specs=[
                pl.BlockSpec((HEADS, tile, LANES), lambda i, f: (0, i, 0)),
                pl.BlockSpec((HEADS, seq, LANES), lambda i, f: (0, 0, 0), pipeline_mode=resident),
                pl.BlockSpec((GROUP, seq), lambda i, f: (0, 0), pipeline_mode=resident),
                pl.BlockSpec((HEADS, tile, LANES), lambda i, f: (0, 0, 0)),
            ],
            out_specs=pl.BlockSpec((tile, GROUP), lambda i, f: (i, 0)),
            scratch_shapes=[pltpu.VMEM((HEADS, 1, tile), F32),
                            pltpu.VMEM((HEADS, HEAD_DIM + ONES_ROWS, tile), F32),
                            pltpu.VMEM((HEADS, tile, tile), F32), pltpu.VMEM((HEADS, tile, tile), F32)],
        ),
        out_shape=jax.ShapeDtypeStruct((seq, GROUP), BF16),
        compiler_params=_params(("arbitrary",)),
        name="flash_kbias" if use_kbias else "flash",
    )(first_tile, q, k, v_t, kbias)


def _fox_first_tile(decay, qk_bound):
    seq = decay.shape[0]
    nq = seq // FLASH_TILE
    f = decay[:, :HEADS]
    f_first = f[0::FLASH_TILE]
    f_last = f[FLASH_TILE - 1::FLASH_TILE]
    gap = f_first[:, None, :] - f_last[None, :, :] + 2.0 * qk_bound
    jj = jnp.arange(nq, dtype=jnp.int32)
    needed = (gap >= FOX_SKIP_LOG) | (jj[None, :, None] >= jj[:, None, None])
    first = jnp.min(jnp.where(needed, jj[None, :, None], nq), axis=1)
    return jnp.min(first, axis=1).astype(jnp.int32)


def _dilated_kernel(q_ref, k_ref, v_ref, o_ref, kbuf, vbuf, acc_s, m_s, l_s):
    pair = pl.program_id(0)
    i = pl.program_id(1)
    T = q_ref.shape[0]

    @pl.when(i == 0)
    def _():
        kbuf[...] = jnp.zeros_like(kbuf)
        vbuf[...] = jnp.zeros_like(vbuf)

    kbuf[0:T, :] = kbuf[T:2 * T, :]
    vbuf[0:T, :] = vbuf[T:2 * T, :]
    kbuf[T:2 * T, :] = k_ref[...]
    vbuf[T:2 * T, :] = v_ref[...]

    ii = lax.broadcasted_iota(jnp.int32, (DIL_SPAN, 2 * DIL_SPAN), 0)
    jj = lax.broadcasted_iota(jnp.int32, (DIL_SPAN, 2 * DIL_SPAN), 1)
    dist = ii + DIL_SPAN - jj
    band = (dist >= 0) & (dist <= DIL_SPAN)
    dist_f = dist.astype(F32)
    upper = lax.broadcasted_iota(jnp.int32, (1, LANES), 1) >= HEAD_DIM

    for pi, (window, r) in enumerate(DIL_PATTERNS):
        assert window // r == DIL_SPAN
        sub = DIL_SPAN * r

        def body(idx, carry, r=r, sub=sub, pi=pi):
            n = idx // r
            rho = idx - n * r
            base = n * sub + rho
            q = q_ref[pl.ds(base, DIL_SPAN, stride=r), :]
            kc = kbuf[pl.ds(T + base - sub, 2 * DIL_SPAN, stride=r), :].astype(BF16)
            vc = vbuf[pl.ds(T + base - sub, 2 * DIL_SPAN, stride=r), :].astype(BF16)
            first_key = jnp.where((i == 0) & (n == 0), DIL_SPAN, 0)
            valid = band & (jj >= first_key)
            stats = []
            for hh in range(2):
                slope = jnp.where(pair == 0, 2.0 ** (-(2 * hh + 1)), 2.0 ** (-(2 * hh + 5)))
                qh = (jnp.where(upper, q, 0.0) if hh else jnp.where(upper, 0.0, q)).astype(BF16)
                s = _dot_nt(qh, kc) - (slope * r) * dist_f
                s = jnp.where(valid, s, NEG)
                m = jnp.max(s, axis=-1, keepdims=True)
                p = jnp.exp(s - m)
                l = jnp.sum(p, axis=-1, keepdims=True)
                stats.append((_dot(p.astype(BF16), vc), m, l))
            rows = pl.ds(pi * T + base, DIL_SPAN, stride=r)
            acc_s[rows, :] = jnp.where(upper, stats[1][0], stats[0][0])
            m_s[rows, :] = jnp.where(upper, stats[1][1], stats[0][1])
            l_s[rows, :] = jnp.where(upper, stats[1][2], stats[0][2])
            return carry

        lax.fori_loop(0, T // DIL_SPAN, body, 0, unroll=True)

    npat = len(DIL_PATTERNS)
    ms = [m_s[pi * T:(pi + 1) * T, :] for pi in range(npat)]
    m_top = functools.reduce(jnp.maximum, ms)
    num = jnp.zeros((T, LANES), F32)
    den = jnp.zeros((T, LANES), F32)
    for pi in range(npat):
        w = jnp.exp(ms[pi] - m_top)
        num = num + w * acc_s[pi * T:(pi + 1) * T, :]
        den = den + w * l_s[pi * T:(pi + 1) * T, :]
    o_ref[...] = (num / den).astype(o_ref.dtype)


def _dilated(q, k, v):
    seq = q.shape[0]
    T = DIL_TILE
    spec = pl.BlockSpec((T, LANES), lambda p, i: (i, p))
    npat = len(DIL_PATTERNS)
    return pl.pallas_call(
        _dilated_kernel,
        grid=(HEADS // 2, seq // T),
        in_specs=[spec, spec, spec],
        out_specs=spec,
        out_shape=jax.ShapeDtypeStruct((seq, GROUP), BF16),
        scratch_shapes=[pltpu.VMEM((2 * T, LANES), F32), pltpu.VMEM((2 * T, LANES), F32),
                        pltpu.VMEM((npat * T, LANES), F32), pltpu.VMEM((npat * T, LANES), F32),
                        pltpu.VMEM((npat * T, LANES), F32)],
        compiler_params=_params(("arbitrary", "arbitrary")),
        name="dilated",
    )(q, k, v)


def _post_kernel(x_ref, oa_ref, ob_ref, oc_ref, od_ref, wout_ref, mod_ref, g2_ref,
                 rw_ref, rb_ref, tri_ref,
                 x1_ref, h2_ref, eidx_ref, gate_ref, rank_ref, cnt_ref, carry):
    i = pl.program_id(0)
    tm = x_ref.shape[0]

    @pl.when(i == 0)
    def _():
        carry[...] = jnp.zeros_like(carry)

    o = (_dot(oa_ref[...], wout_ref[0]) + _dot(ob_ref[...], wout_ref[1])
         + _dot(oc_ref[...], wout_ref[2]) + _dot(od_ref[...], wout_ref[3]))
    x1 = x_ref[...] + mod_ref[2] * o
    x1_ref[...] = x1
    y = x1 * lax.rsqrt(jnp.mean(x1 * x1, axis=-1, keepdims=True) + EPS) * g2_ref[...]
    h2 = y * (1.0 + mod_ref[4]) + mod_ref[3]
    _store_token_tiles(h2_ref, h2)

    h_hi, h_lo = _split2(h2)
    hh = _dot(h_hi, rw_ref[...])
    logits = (hh[:, :LANES] + hh[:, LANES:] + _dot(h_lo, rw_ref[:, :LANES])
              + rb_ref[...])
    lane = lax.broadcasted_iota(jnp.int32, (tm, LANES), 1)
    lane_f = lane.astype(F32)
    g = logits
    chosen = jnp.zeros((tm, LANES), F32)
    vals, idxs = [], []
    for _ in range(TOP_K):
        m = jnp.max(g, axis=-1, keepdims=True)
        first = jnp.min(jnp.where(g == m, lane_f, 1e9), axis=-1, keepdims=True)
        pick = lane_f == first
        chosen = jnp.where(pick, 1.0, chosen)
        g = jnp.where(pick, -jnp.inf, g)
        vals.append(m)
        idxs.append(first)
    exps = [jnp.exp(v - vals[0]) for v in vals]
    den = exps[0] + exps[1] + exps[2] + exps[3]
    before = _dot(tri_ref[...], chosen.astype(BF16)) + carry[...]
    carry[...] = carry[...] + jnp.sum(chosen, axis=0, keepdims=True)
    cnt_ref[...] = carry[...]
    e_out = jnp.zeros((tm, LANES), F32)
    g_out = jnp.zeros((tm, LANES), F32)
    r_out = jnp.zeros((tm, LANES), F32)
    for k in range(TOP_K):
        rank_k = jnp.sum(jnp.where(lane_f == idxs[k], before, 0.0), axis=-1, keepdims=True)
        e_out = jnp.where(lane == k, idxs[k], e_out)
        g_out = jnp.where(lane == k, exps[k] / den, g_out)
        r_out = jnp.where(lane == k, rank_k, r_out)
    eidx_ref[...] = e_out.astype(jnp.int32)
    gate_ref[...] = g_out
    rank_ref[...] = r_out.astype(jnp.int32)


def _post(x2, oa, ob, oc, od, mod_l, consts, p):
    seq = x2.shape[0]
    tm = TOKEN_TILE
    row = lambda n: pl.BlockSpec((tm, n), lambda i: (i, 0))
    full = [p["w_out"], mod_l, p["g2"], p["rw"], p["r_b"], consts["tri_strict"]]
    f32 = lambda n: jax.ShapeDtypeStruct((seq, n), F32)
    i32 = lambda n: jax.ShapeDtypeStruct((seq, n), jnp.int32)
    return pl.pallas_call(
        _post_kernel,
        grid=(seq // tm,),
        in_specs=[row(D_MODEL)] + [row(GROUP)] * 4 + [_full_spec(a.shape) for a in full],
        out_specs=[row(D_MODEL), pl.BlockSpec((tm * CHUNKS, LANES), lambda i: (i, 0)),
                   row(LANES), row(LANES), row(LANES), _full_spec((1, LANES))],
        out_shape=[f32(D_MODEL), jax.ShapeDtypeStruct((seq * CHUNKS, LANES), F32),
                   i32(LANES), f32(LANES), i32(LANES), jax.ShapeDtypeStruct((1, LANES), F32)],
        scratch_shapes=[pltpu.VMEM((1, LANES), F32)],
        compiler_params=_params(("arbitrary",)),
        name="post",
    )(x2, oa, ob, oc, od, *full)


def _tile_copy(src, s, dst, d, sem):
    return pltpu.make_async_copy(src.at[pl.ds(pl.multiple_of(s * CHUNKS, CHUNKS), CHUNKS), :],
                                 dst.at[pl.ds(pl.multiple_of(d * CHUNKS, CHUNKS), CHUNKS), :], sem)


def _dispatch_kernel(dest_ref, padlo_ref, padn_ref, nu_ref, h_ref, xs_ref, zbuf, sem, zsem):
    i = pl.program_id(0)
    n = ROUTE_TILE * TOP_K
    base = i * n
    block_rows = EXPERT_ROWS * CHUNKS

    @pl.when(i == 0)
    def _():
        zbuf[...] = jnp.zeros_like(zbuf)

        def fill(wait):
            def go(copy):
                copy.wait() if wait else copy.start()

            def per_expert(e, carry):
                lo, cnt = padlo_ref[e], padn_ref[e]
                off = lo
                p = EXPERT_ROWS // 2
                while p >= 1:
                    rows = p * CHUNKS

                    @pl.when((cnt & p) != 0)
                    def _(off=off, rows=rows):
                        go(pltpu.make_async_copy(
                            zbuf.at[pl.ds(0, rows), :],
                            xs_ref.at[pl.ds(pl.multiple_of(off * CHUNKS, CHUNKS), rows), :], zsem))

                    off = off + (cnt & p)
                    p //= 2
                return carry

            lax.fori_loop(0, N_EXPERTS, per_expert, 0)

            def per_block(b, carry):
                go(pltpu.make_async_copy(
                    zbuf, xs_ref.at[pl.ds(pl.multiple_of(b * block_rows, block_rows), block_rows), :],
                    zsem))
                return carry

            lax.fori_loop(nu_ref[0], xs_ref.shape[0] // block_rows, per_block, 0)

        fill(False)
        fill(True)

    def issue(r, carry):
        for k in range(TOP_K):
            _tile_copy(h_ref, r, xs_ref, dest_ref[base + r * TOP_K + k], sem).start(priority=k % 2)
        return carry

    lax.fori_loop(0, ROUTE_TILE, issue, 0, unroll=2)
    rows = pl.ds(0, n * CHUNKS)
    pltpu.make_async_copy(xs_ref.at[rows, :], xs_ref.at[rows, :], sem).wait()


def _dispatch(dest, pad_lo, pad_n, n_used, h2_tiles, m_pad):
    seq = h2_tiles.shape[0] // CHUNKS
    return pl.pallas_call(
        _dispatch_kernel,
        grid_spec=pltpu.PrefetchScalarGridSpec(
            num_scalar_prefetch=4,
            grid=(seq // ROUTE_TILE,),
            in_specs=[pl.BlockSpec((ROUTE_TILE * CHUNKS, LANES), lambda i, *_: (i, 0))],
            out_specs=pl.BlockSpec(memory_space=pl.ANY),
            scratch_shapes=[pltpu.VMEM((EXPERT_ROWS * CHUNKS, LANES), F32),
                            pltpu.SemaphoreType.DMA(()), pltpu.SemaphoreType.DMA(())],
        ),
        out_shape=jax.ShapeDtypeStruct((m_pad * CHUNKS, LANES), F32),
        compiler_params=_params(("arbitrary",)),
        name="dispatch",
    )(dest, pad_lo, pad_n, n_used, h2_tiles)


def _expert_kernel(be_ref, nu_ref, nxt_ref, ord_ref, xs_ref, w1_hbm, b1_ref, w2_hbm, b2_ref, ys_ref,
                   w1f, w2f, w1b, w2b, sems):
    b = pl.program_id(0)
    e = be_ref[b]
    prev = be_ref[jnp.maximum(b - 1, 0)]
    fresh = ((b == 0) | (e != prev)) & (b < nu_ref[0])
    slot = ord_ref[b] % 2

    def fetch(expert, to_slot):
        return (pltpu.make_async_copy(w1_hbm.at[expert], w1f.at[to_slot], sems.at[0, to_slot]),
                pltpu.make_async_copy(w2_hbm.at[expert], w2f.at[to_slot], sems.at[1, to_slot]))

    @pl.when(b == 0)
    def _():
        for copy in fetch(e, slot):
            copy.start()

    @pl.when(fresh)
    def _():
        for copy in fetch(e, slot):
            copy.wait()
        w1b[...] = w1f[slot].astype(BF16)
        w2b[...] = w2f[slot].astype(BF16)

        @pl.when(nxt_ref[b] >= 0)
        def _():
            for copy in fetch(nxt_ref[b], 1 - slot):
                copy.start()

    @pl.when(b < nu_ref[0])
    def _():
        half = EXPERT_ROWS // 2
        gus = []
        for r in range(2):
            xb = jnp.concatenate(
                [_load_token_chunk(xs_ref, half, c, offset=r * half * CHUNKS) for c in range(CHUNKS)],
                axis=1).astype(BF16)
            gus.append(_dot(xb, w1b[...]) + b1_ref[0])
        for r in range(2):
            g = jnp.minimum(gus[r][:, :D_EXPERT], SWIGLU_LIMIT)
            u = jnp.clip(gus[r][:, D_EXPERT:], -SWIGLU_LIMIT, SWIGLU_LIMIT)
            y = (u + 1.0) * g * (1.0 / (1.0 + jnp.exp(-SWIGLU_ALPHA * g)))
            _store_token_tiles(ys_ref, _dot(y.astype(BF16), w2b[...]) + b2_ref[0],
                               offset=r * half * CHUNKS)

    @pl.when(b >= nu_ref[0])
    def _():
        ys_ref[...] = jnp.zeros_like(ys_ref)


def _experts(blk_expert, n_used, blk_next, blk_ord, xs, w1, b1, w2, b2):
    m_pad = xs.shape[0] // CHUNKS
    bm = EXPERT_ROWS
    n_all = w1.shape[0] * w1.shape[1]
    rows = lambda b, be, nu, *_: (jnp.minimum(b, nu[0] - 1), 0)
    ex = lambda b, be, nu, *_: (be[jnp.minimum(b, nu[0] - 1)], 0, 0)
    return pl.pallas_call(
        _expert_kernel,
        grid_spec=pltpu.PrefetchScalarGridSpec(
            num_scalar_prefetch=4,
            grid=(m_pad // bm,),
            in_specs=[
                pl.BlockSpec((bm * CHUNKS, LANES), rows),
                pl.BlockSpec(memory_space=pl.ANY),
                pl.BlockSpec((1, 1, 2 * D_EXPERT), ex),
                pl.BlockSpec(memory_space=pl.ANY),
                pl.BlockSpec((1, 1, D_MODEL), ex),
            ],
            out_specs=pl.BlockSpec((bm * CHUNKS, LANES), lambda b, *_: (b, 0)),
            scratch_shapes=[pltpu.VMEM((2, D_MODEL, 2 * D_EXPERT), F32),
                            pltpu.VMEM((2, D_EXPERT, D_MODEL), F32),
                            pltpu.VMEM((D_MODEL, 2 * D_EXPERT), BF16),
                            pltpu.VMEM((D_EXPERT, D_MODEL), BF16),
                            pltpu.SemaphoreType.DMA((2, 2))],
        ),
        out_shape=jax.ShapeDtypeStruct((m_pad * CHUNKS, LANES), F32),
        compiler_params=_params(("arbitrary",)),
        name="experts",
    )(blk_expert, n_used, blk_next, blk_ord, xs,
      w1.reshape(n_all, D_MODEL, 2 * D_EXPERT), b1.reshape(n_all, 1, -1),
      w2.reshape(n_all, D_EXPERT, D_MODEL), b2.reshape(n_all, 1, -1))


def _combine_kernel(dest_ref, ys_ref, x1_ref, gate_ref, mod_ref, o_ref, buf, sems):
    i = pl.program_id(0)
    tm = x1_ref.shape[0]
    n = tm * TOP_K
    slot = i % 2

    def gather(step, to_slot):
        base = step * n

        def issue(r, carry):
            for k in range(TOP_K):
                _tile_copy(ys_ref, dest_ref[base + r * TOP_K + k], buf, to_slot * n + k * tm + r,
                           sems.at[to_slot]).start(priority=k % 2)
            return carry

        lax.fori_loop(0, tm, issue, 0, unroll=2)

    @pl.when(i == 0)
    def _():
        gather(0, 0)

    @pl.when(i + 1 < pl.num_programs(0))
    def _():
        gather(i + 1, 1 - slot)

    mine = pl.ds(pl.multiple_of(slot * n * CHUNKS, n * CHUNKS), n * CHUNKS)
    pltpu.make_async_copy(ys_ref.at[pl.ds(0, n * CHUNKS), :], buf.at[mine, :], sems.at[slot]).wait()
    gates = gate_ref[...]
    g2 = mod_ref[5]
    for c in range(CHUNKS):
        cols = slice(c * LANES, (c + 1) * LANES)
        mix = jnp.zeros((tm, LANES), F32)
        for k in range(TOP_K):
            mix = mix + gates[:, k:k + 1] * _load_token_chunk(
                buf, tm, c, offset=(slot * n + k * tm) * CHUNKS)
        o_ref[:, cols] = x1_ref[:, cols] + g2[:, cols] * mix


def _combine(dest, ys, x1, gates, mod_l):
    seq = x1.shape[0]
    tm = ROUTE_TILE
    return pl.pallas_call(
        _combine_kernel,
        grid_spec=pltpu.PrefetchScalarGridSpec(
            num_scalar_prefetch=1,
            grid=(seq // tm,),
            in_specs=[
                pl.BlockSpec(memory_space=pl.ANY),
                pl.BlockSpec((tm, D_MODEL), lambda i, d: (i, 0)),
                pl.BlockSpec((tm, LANES), lambda i, d: (i, 0)),
                pl.BlockSpec(mod_l.shape, lambda i, d: (0, 0, 0)),
            ],
            out_specs=pl.BlockSpec((tm, D_MODEL), lambda i, d: (i, 0)),
            scratch_shapes=[pltpu.VMEM((2 * TOP_K * tm * CHUNKS, LANES), F32),
                            pltpu.SemaphoreType.DMA((2,))],
        ),
        out_shape=jax.ShapeDtypeStruct((seq, D_MODEL), F32),
        compiler_params=_params(("arbitrary",)),
        name="combine",
    )(dest, ys, x1, gates, mod_l)


def _moe(l, x1, h2_tiles, eidx, gates, rank, counts, mod_l, w1, b1, w2, b2):
    seq = x1.shape[0]
    bm = EXPERT_ROWS
    m_pad = seq * TOP_K + N_EXPERTS * bm
    cnt = counts[0, :N_EXPERTS].astype(jnp.int32)
    padded = (cnt + bm - 1) // bm * bm
    pad_end = jnp.cumsum(padded)
    pad_start = pad_end - padded
    onehot = eidx[:, :TOP_K, None] == jnp.arange(N_EXPERTS, dtype=jnp.int32)
    start_of = jnp.sum(jnp.where(onehot, pad_start, 0), axis=-1)
    dest = (start_of + rank[:, :TOP_K]).reshape(seq * TOP_K).astype(jnp.int32)
    nblk = m_pad // bm
    blk_start = jnp.arange(nblk, dtype=jnp.int32) * bm
    local = jnp.minimum(jnp.sum(pad_end[None, :] <= blk_start[:, None], axis=1), N_EXPERTS - 1)
    blk_expert = (local + l * N_EXPERTS).astype(jnp.int32)
    n_used = (pad_end[-1:] // bm).astype(jnp.int32)
    ids = jnp.arange(N_EXPERTS, dtype=jnp.int32)
    live = padded > 0
    ordinal = jnp.cumsum(live.astype(jnp.int32)) - 1
    later = live[None, :] & (ids[None, :] > ids[:, None])
    nxt = jnp.min(jnp.where(later, ids[None, :], N_EXPERTS), axis=1)
    nxt = jnp.where(nxt < N_EXPERTS, nxt + l * N_EXPERTS, -1)
    of_block = local[:, None] == ids[None, :]
    blk_next = jnp.sum(jnp.where(of_block, nxt[None, :], 0), axis=1).astype(jnp.int32)
    blk_ord = jnp.sum(jnp.where(of_block, ordinal[None, :], 0), axis=1).astype(jnp.int32)
    xs = _dispatch(dest, (pad_start + cnt).astype(jnp.int32), (padded - cnt).astype(jnp.int32),
                   n_used, h2_tiles, m_pad)
    ys = _experts(blk_expert, n_used, blk_next, blk_ord, xs, w1, b1, w2, b2)
    return _combine(dest, ys, x1, gates, mod_l)


def _pad_cols(a, n):
    return jnp.pad(a, ((0, 0), (0, n - a.shape[1])))


def _layer_params(l, w_in, mla_cq_g, mla_w_uq, mla_ckv_g, mla_w_ukv, mla_q_g, mla_k_g,
                  fox_q_g, fox_k_g, fox_b_f, moba_q_g, moba_k_g, dil_q_g, dil_k_g, w_out,
                  norm1_g, norm2_g, router_w, router_b):
    w = w_in[l]
    sizes = [MLA_Q_RANK, MLA_KV_RANK, MLA_ROPE, GROUP, GROUP, GROUP, HEADS] + [GROUP] * 6
    offs = np.concatenate([[0], np.cumsum(sizes)])
    part = [w[:, offs[j]:offs[j + 1]] for j in range(len(sizes))]
    zeros = lambda n: jnp.zeros((D_MODEL, n), F32)
    w_in_r = jnp.concatenate(
        [part[0], part[1], zeros(MLA_NOPE), part[2], zeros(LANES - MLA_QK),
         part[3], part[4], part[6], zeros(LANES - HEADS),
         part[7], part[8], part[10], part[11], part[12]], axis=1).astype(BF16)
    assert w_in_r.shape[1] == COLS_IN
    w_vt = jnp.stack([part[5].T, part[9].T]).astype(BF16)
    w_uq = jnp.pad(mla_w_uq[l].reshape(MLA_Q_RANK, HEADS, MLA_QK),
                   ((0, 0), (0, 0), (0, LANES - MLA_QK))).reshape(MLA_Q_RANK, HEADS * LANES)
    w_ukv = mla_w_ukv[l].reshape(MLA_KV_RANK, HEADS, MLA_NOPE + HEAD_DIM)
    w_uk = jnp.pad(w_ukv[:, :, :MLA_NOPE], ((0, 0), (0, 0), (0, LANES - MLA_NOPE)))
    w_uv = w_ukv[:, :, MLA_NOPE:]
    tile4 = lambda g: jnp.tile(g, HEADS)[None, :]
    rw = _pad_cols(router_w[l], LANES)
    rw_hi = rw.astype(BF16)
    rw_lo = (rw - rw_hi.astype(F32)).astype(BF16)
    r_b = jnp.concatenate([router_b[l], jnp.full((LANES - N_EXPERTS,), NEG, F32)])[None, :]
    return dict(
        g1=norm1_g[l][None, :], g2=norm2_g[l][None, :], w_in=w_in_r, w_vt=w_vt,
        cq_g=mla_cq_g[l][None, :], w_uq=w_uq.astype(BF16), ckv_g=mla_ckv_g[l][None, :],
        w_uk=w_uk.reshape(MLA_KV_RANK, HEADS * LANES).astype(BF16),
        w_uvt=w_uv.reshape(MLA_KV_RANK, GROUP).T.astype(BF16),
        fox_bound=FOX_NORM_SLACK * HEAD_DIM ** 0.5 * jnp.max(jnp.abs(fox_q_g[l]))
        * jnp.max(jnp.abs(fox_k_g[l])),
        q_g=_pad_cols(mla_q_g[l][None, :], LANES), k_g=_pad_cols(mla_k_g[l][None, :], LANES),
        fq_g=tile4(fox_q_g[l]), fk_g=tile4(fox_k_g[l]), f_b=_pad_cols(fox_b_f[l][None, :], LANES),
        mq_g=tile4(moba_q_g[l]), mk_g=tile4(moba_k_g[l]), dq_g=tile4(dil_q_g[l]), dk_g=tile4(dil_k_g[l]),
        w_out=w_out[l].reshape(HEADS, GROUP, D_MODEL).astype(BF16),
        rw=jnp.concatenate([rw_hi, rw_lo], axis=1), r_b=r_b,
    )


def kernel(x, c, w_mod, b_mod, norm1_g, norm2_g, w_in, mla_cq_g, mla_w_uq, mla_ckv_g, mla_w_ukv, mla_q_g, mla_k_g, fox_q_g, fox_k_g, fox_b_f, moba_q_g, moba_k_g, dil_q_g, dil_k_g, w_out, router_w, router_b, exp_w1, exp_b1, exp_w2, exp_b2):
    batch, seq, d = x.shape
    assert batch == 1 and d == D_MODEL
    assert seq % DIL_TILE == 0 and seq // MOBA_BLOCK <= MOBA_MAX_BLOCKS
    depth = w_mod.shape[0]
    consts = _prep_constants(seq, TOKEN_TILE)
    mod = _modulation(c, w_mod, b_mod)
    slopes_c = 2.0 ** (-(2.0 * np.arange(HEADS) + 2.0))
    in_block = np.arange(FLASH_TILE) % MOBA_BLOCK
    kbias_c = jnp.asarray(np.broadcast_to(slopes_c[:, None, None] * in_block[None, :, None],
                                          (HEADS, FLASH_TILE, LANES)), F32)
    kbias_0 = jnp.zeros((HEADS, FLASH_TILE, LANES), F32)
    all_tiles = jnp.zeros((seq // FLASH_TILE,), jnp.int32)
    x2 = x.reshape(seq, d)
    for l in range(depth):
        p = _layer_params(l, w_in, mla_cq_g, mla_w_uq, mla_ckv_g, mla_w_ukv, mla_q_g, mla_k_g,
                          fox_q_g, fox_k_g, fox_b_f, moba_q_g, moba_k_g, dil_q_g, dil_k_g, w_out,
                          norm1_g, norm2_g, router_w, router_b)
        mod_l = mod[l]
        qa, ka, va, qf, kf, vf, qm, km, vm, qd, kd, vd, decay = _prep(x2, mod_l, consts, p)
        oa = _flash(qa, ka, va, kbias_0, all_tiles, False)
        ob = _flash(qf, kf, vf, kbias_0, _fox_first_tile(decay, p["fox_bound"]), False)
        oc = _flash(qm, km, vm, kbias_c, all_tiles, True)
        od = _dilated(qd, kd, vd)
        x1, h2, eidx, gates, rank, counts = _post(x2, oa, ob, oc, od, mod_l, consts, p)
        x2 = _moe(l, x1, h2, eidx, gates, rank, counts, mod_l, exp_w1, exp_b1, exp_w2, exp_b2)
    return x2.reshape(batch, seq, d)
```

```python
import functools

import numpy as np
import jax
import jax.numpy as jnp
from jax import lax
from jax.experimental import pallas as pl
from jax.experimental.pallas import tpu as pltpu

F32 = jnp.float32
BF16 = jnp.bfloat16

D_MODEL = 1024
HEAD_DIM = 64
HEADS = 4
GROUP = HEADS * HEAD_DIM
LANES = 128
CHUNKS = D_MODEL // LANES
MLA_Q_RANK = 256
MLA_KV_RANK = 128
MLA_NOPE = 64
MLA_ROPE = 32
MLA_QK = MLA_NOPE + MLA_ROPE
ROPE_THETA = 10000.0
MOBA_BLOCK = 256
MOBA_TOPK = 3
MOBA_MAX_BLOCKS = 64
DIL_PATTERNS = ((128, 1), (512, 4), (2048, 16))
DIL_SPAN = 128
DIL_TILE = 2048
N_EXPERTS = 32
TOP_K = 4
D_EXPERT = 1024
SWIGLU_LIMIT = 7.0
SWIGLU_ALPHA = 1.702
EPS = 1e-6
NEG = -1e30

FLASH_TILE = 512
ONES_ROWS = 16
FOX_SKIP_LOG = -106.0
FOX_NORM_SLACK = 1.02
TOKEN_TILE = 256
ROUTE_TILE = 512
EXPERT_ROWS = 512
VMEM_LIMIT = 56 * 1024 * 1024

COLS_MLA = 512
COLS_FOX = 640
COLS_MOBA = 512
COLS_DIL = 768
COLS_IN = COLS_MLA + COLS_FOX + COLS_MOBA + COLS_DIL


def _in_column_map():
    sizes = [MLA_Q_RANK, MLA_KV_RANK, MLA_ROPE, GROUP, GROUP, GROUP, HEADS] + [GROUP] * 6
    src = np.concatenate([[0], np.cumsum(sizes)])
    cq, ckv, kr, fq, fk, fv, flog, mq, mk, mv, dq, dk, dv = range(13)
    layout = [(cq, 0), (ckv, MLA_Q_RANK), (kr, MLA_Q_RANK + MLA_KV_RANK + MLA_NOPE),
              (fq, COLS_MLA), (fk, COLS_MLA + GROUP), (flog, COLS_MLA + 2 * GROUP),
              (mq, COLS_MLA + COLS_FOX), (mk, COLS_MLA + COLS_FOX + GROUP),
              (dq, COLS_IN - 3 * GROUP), (dk, COLS_IN - 2 * GROUP), (dv, COLS_IN - GROUP)]
    moves = tuple((dst, int(src[piece]), sizes[piece]) for piece, dst in layout)
    return moves, (int(src[fv]), int(src[mv]))


IN_COLUMN_MAP, IN_VALUE_COLUMNS = _in_column_map()


def _dot(a, b):
    return jnp.dot(a, b, preferred_element_type=F32)


def _dot_nt(a, b):
    return lax.dot_general(a, b, (((1,), (1,)), ((), ())), preferred_element_type=F32)


def _split2(x):
    hi = x.astype(BF16)
    lo = (x - hi.astype(F32)).astype(BF16)
    return hi, lo


def _split3(x):
    a = x.astype(BF16)
    r = x - a.astype(F32)
    b = r.astype(BF16)
    c = (r - b.astype(F32)).astype(BF16)
    return a, b, c


def _head_of_lane():
    return jnp.right_shift(lax.broadcasted_iota(jnp.int32, (1, GROUP), 1), 6)


def _store_token_tiles(ref, x, offset=0):
    n = x.shape[0]
    for c in range(CHUNKS):
        ref[pl.ds(offset + c, n, stride=CHUNKS), :] = x[:, c * LANES:(c + 1) * LANES]


def _load_token_chunk(ref, n, c, offset=0):
    return ref[pl.ds(offset + c, n, stride=CHUNKS), :]


def _full_spec(shape):
    nd = len(shape)
    return pl.BlockSpec(shape, lambda *_: (0,) * nd)


def _params(sem):
    return pltpu.CompilerParams(dimension_semantics=sem, vmem_limit_bytes=VMEM_LIMIT)


def _mod_kernel(c_ref, w_ref, b_ref, o_ref):
    c = c_ref[...]
    s = c * (1.0 / (1.0 + jnp.exp(-c)))
    s8 = jnp.broadcast_to(s, (8, D_MODEL))
    r = jnp.dot(s8, w_ref[0], preferred_element_type=F32, precision=lax.Precision.HIGHEST)
    o_ref[0, 0] = r[0:1, :] + b_ref[0, 0]


def _modulation(c, w_mod, b_mod):
    depth = w_mod.shape[0]
    b4 = b_mod.reshape(depth, 6, 1, D_MODEL)
    return pl.pallas_call(
        _mod_kernel,
        grid=(depth, 6),
        in_specs=[
            pl.BlockSpec((1, D_MODEL), lambda l, j: (0, 0)),
            pl.BlockSpec((1, D_MODEL, D_MODEL), lambda l, j: (l, 0, j)),
            pl.BlockSpec((1, 1, 1, D_MODEL), lambda l, j: (l, j, 0, 0)),
        ],
        out_specs=pl.BlockSpec((1, 1, 1, D_MODEL), lambda l, j: (l, j, 0, 0)),
        out_shape=jax.ShapeDtypeStruct((depth, 6, 1, D_MODEL), F32),
        compiler_params=_params(("arbitrary", "arbitrary")),
        name="modulation",
    )(c, w_mod, b4)


def _head_norm(x, g, bd):
    hi, lo = _split2(x * x)
    ss = _dot(hi, bd) + _dot(lo, bd)
    return x * lax.rsqrt(ss * (1.0 / HEAD_DIM) + EPS) * g


def _prep_kernel(x_ref, mod_ref, g1_ref, wraw_ref, cqg_ref, wuq_ref, ckvg_ref, wuk_ref,
                 wuvt_ref, qg_ref, kg_ref, rc_ref, rs1_ref, rs2_ref,
                 fqg_ref, fkg_ref, fb_ref, mqg_ref, mkg_ref, dqg_ref, dkg_ref,
                 bd_ref, tri_ref, eq_ref, ek_ref, sel_ref,
                 qa_ref, ka_ref, va_ref, qf_ref, kf_ref, vf_ref, qm_ref, km_ref, vm_ref,
                 qd_ref, kd_ref, vd_ref, f_ref,
                 fcarry, kmean, win_ref, wvt_ref):
    i = pl.program_id(0)
    tm = x_ref.shape[0]

    @pl.when(i == 0)
    def _():
        fcarry[...] = jnp.zeros_like(fcarry)
        kmean[...] = jnp.zeros_like(kmean)
        win_ref[...] = jnp.zeros_like(win_ref)
        for dst, src, width in IN_COLUMN_MAP:
            win_ref[:, dst:dst + width] = wraw_ref[0, :, src:src + width].astype(BF16)
        for slot, src in enumerate(IN_VALUE_COLUMNS):
            wvt_ref[slot] = wraw_ref[0, :, src:src + GROUP].T.astype(BF16)

    x = x_ref[...]
    y = x * lax.rsqrt(jnp.mean(x * x, axis=-1, keepdims=True) + EPS) * g1_ref[...]
    hb = (y * (1.0 + mod_ref[1]) + mod_ref[0]).astype(BF16)
    bd = bd_ref[...]
    lane = lax.broadcasted_iota(jnp.int32, (1, LANES), 1)
    lane_f = lane.astype(F32)
    head_of_lane = _head_of_lane()

    pa = _dot(hb, win_ref[:, 0:COLS_MLA])
    cq = pa[:, 0:MLA_Q_RANK]
    ckv = pa[:, MLA_Q_RANK:MLA_Q_RANK + MLA_KV_RANK]
    kr = pa[:, MLA_Q_RANK + MLA_KV_RANK:COLS_MLA]
    cqn = (cq * lax.rsqrt(jnp.mean(cq * cq, axis=-1, keepdims=True) + EPS) * cqg_ref[...]).astype(BF16)
    ckvn = (ckv * lax.rsqrt(jnp.mean(ckv * ckv, axis=-1, keepdims=True) + EPS) * ckvg_ref[...]).astype(BF16)
    q_all = _dot(cqn, wuq_ref[...])
    k_all = _dot(ckvn, wuk_ref[...])
    va_ref[...] = _dot_nt(wuvt_ref[...], ckvn).astype(BF16)
    rc, rs1, rs2 = rc_ref[...], rs1_ref[...], rs2_ref[...]

    def rope(t):
        return t * rc + pltpu.roll(t, LANES - MLA_ROPE // 2, 1) * rs1 + pltpu.roll(t, MLA_ROPE // 2, 1) * rs2

    for h in range(HEADS):
        q = q_all[:, h * LANES:(h + 1) * LANES]
        q = q * lax.rsqrt(jnp.sum(q * q, axis=-1, keepdims=True) * (1.0 / MLA_QK) + EPS) * qg_ref[...]
        qa_ref[h] = (rope(q) * (MLA_QK ** -0.5)).astype(BF16)
        k = k_all[:, h * LANES:(h + 1) * LANES] + kr
        k = k * lax.rsqrt(jnp.sum(k * k, axis=-1, keepdims=True) * (1.0 / MLA_QK) + EPS) * kg_ref[...]
        ka_ref[h] = rope(k).astype(BF16)

    pf = _dot(hb, win_ref[:, COLS_MLA:COLS_MLA + COLS_FOX])
    fqn = (_head_norm(pf[:, 0:GROUP], fqg_ref[...], bd) * (HEAD_DIM ** -0.5)).astype(BF16)
    fkn = _head_norm(pf[:, GROUP:2 * GROUP], fkg_ref[...], bd).astype(BF16)
    vf_ref[...] = _dot_nt(wvt_ref[0], hb).astype(BF16)
    z = pf[:, 2 * GROUP:2 * GROUP + LANES] + fb_ref[...]
    log_f = jnp.minimum(z, 0.0) - jnp.log(1.0 + jnp.exp(-jnp.abs(z)))
    tri = tri_ref[...]
    a1, a2, a3 = _split3(log_f)
    cum = fcarry[...] + (_dot(tri, a1) + _dot(tri, a2) + _dot(tri, a3))
    fcarry[...] = cum[tm - 1:tm, :]
    f_ref[...] = cum
    f1, f2, f3 = _split3(cum)
    xq = jnp.concatenate([fqn, f1, f2, f3], axis=1)
    xk = jnp.concatenate([fkn, f1, f2, f3], axis=1)
    ones_q = jnp.where((lane >= HEAD_DIM + 3) & (lane < HEAD_DIM + 6), 1.0, 0.0)
    ones_k = jnp.where((lane >= HEAD_DIM) & (lane < HEAD_DIM + 3), 1.0, 0.0)
    qf_all = _dot(xq, eq_ref[...])
    kf_all = _dot(xk, ek_ref[...])
    for h in range(HEADS):
        qf_ref[h] = (qf_all[:, h * LANES:(h + 1) * LANES] + ones_q).astype(BF16)
        kf_ref[h] = (kf_all[:, h * LANES:(h + 1) * LANES] + ones_k).astype(BF16)

    pm = _dot(hb, win_ref[:, COLS_MLA + COLS_FOX:COLS_MLA + COLS_FOX + COLS_MOBA])
    mqn = _head_norm(pm[:, 0:GROUP], mqg_ref[...], bd) * (HEAD_DIM ** -0.5)
    mkn = _head_norm(pm[:, GROUP:2 * GROUP], mkg_ref[...], bd)
    vm_ref[...] = _dot_nt(wvt_ref[1], hb).astype(BF16)
    col_mean = jnp.mean(mkn, axis=0, keepdims=True)
    mqb = mqn.astype(BF16)
    mkb = mkn.astype(BF16)
    blk = lane - HEAD_DIM
    blk_f = blk.astype(F32)
    past = (blk >= 0) & (blk < i)
    i_f = i.astype(F32)
    for h in range(HEADS):
        kmean[pl.ds(h * LANES + HEAD_DIM + i, 1), :] = jnp.where(head_of_lane == h, col_mean, 0.0)
    q_hi, q_lo = _split2(mqn)
    km_hi, km_lo = _split2(kmean[...])
    gate_all = _dot_nt(q_hi, km_hi) + _dot_nt(q_hi, km_lo) + _dot_nt(q_lo, km_hi)
    qm_all = _dot(mqb, sel_ref[...])
    km_all = _dot(mkb, sel_ref[...])
    for h in range(HEADS):
        g = jnp.where(past, gate_all[:, h * LANES:(h + 1) * LANES], NEG)
        chosen = jnp.zeros((tm, LANES), F32)
        for _ in range(MOBA_TOPK):
            m = jnp.max(g, axis=-1, keepdims=True)
            first = jnp.min(jnp.where(g == m, lane_f, 1e9), axis=-1, keepdims=True)
            pick = (lane_f == first) & (m > NEG)
            chosen = jnp.where(pick, 1.0, chosen)
            g = jnp.where(pick, NEG, g)
        slope = 2.0 ** (-(2 * h + 2))
        keep = (chosen > 0.0) | (blk == i)
        bias = jnp.where(keep, (slope * MOBA_BLOCK) * (blk_f - i_f), NEG)
        bias = jnp.where(blk >= 0, bias, 0.0)
        qm_ref[h] = (qm_all[:, h * LANES:(h + 1) * LANES] + bias).astype(BF16)
        onehot = jnp.where(blk == i, 1.0, 0.0)
        km_ref[h] = (km_all[:, h * LANES:(h + 1) * LANES] + onehot).astype(BF16)

    pd = _dot(hb, win_ref[:, COLS_MLA + COLS_FOX + COLS_MOBA:COLS_IN])
    qd_ref[...] = _head_norm(pd[:, 0:GROUP], dqg_ref[...], bd) * (HEAD_DIM ** -0.5)
    kd_ref[...] = _head_norm(pd[:, GROUP:2 * GROUP], dkg_ref[...], bd)
    vd_ref[...] = pd[:, 2 * GROUP:3 * GROUP]


def _prep_constants(seq, tm):
    half = MLA_ROPE // 2
    inv = 1.0 / (ROPE_THETA ** (jnp.arange(half, dtype=F32) / half))
    ang = jnp.arange(seq, dtype=F32)[:, None] * inv[None, :]
    cos, sin = jnp.cos(ang), jnp.sin(ang)
    z = lambda n: jnp.zeros((seq, n), F32)
    rc = jnp.concatenate([jnp.ones((seq, MLA_NOPE), F32), cos, cos, z(LANES - MLA_QK)], axis=1)
    rs1 = jnp.concatenate([z(MLA_NOPE), -sin, z(LANES - MLA_NOPE - half)], axis=1)
    rs2 = jnp.concatenate([z(MLA_NOPE + half), sin, z(LANES - MLA_QK)], axis=1)
    bd =np.kron(np.eye(HEADS, dtype=np.float32), np.ones((HEAD_DIM, HEAD_DIM), np.float32))
    tri = np.tril(np.ones((tm, tm), np.float32))
    tri_strict = np.tril(np.ones((tm, tm), np.float32), -1)
    sel = np.zeros((GROUP, HEADS * LANES), np.float32)
    eq = np.zeros((GROUP + 3 * LANES, HEADS * LANES), np.float32)
    ek = np.zeros((GROUP + 3 * LANES, HEADS * LANES), np.float32)
    for h in range(HEADS):
        for d in range(HEAD_DIM):
            sel[h * HEAD_DIM + d, h * LANES + d] = 1.0
        for piece in range(3):
            eq[GROUP + piece * LANES + h, h * LANES + HEAD_DIM + piece] = 1.0
            ek[GROUP + piece * LANES + h, h * LANES + HEAD_DIM + 3 + piece] = -1.0
    eq[:GROUP] = sel
    ek[:GROUP] = sel
    as_bf = lambda a: jnp.asarray(a, BF16)
    return dict(rc=rc, rs1=rs1, rs2=rs2, bd=as_bf(bd),
                tri=as_bf(tri), tri_strict=as_bf(tri_strict), sel=as_bf(sel), eq=as_bf(eq), ek=as_bf(ek))


def _prep(x2, mod_l, consts, p):
    seq = x2.shape[0]
    tm = TOKEN_TILE
    row = lambda n: pl.BlockSpec((tm, n), lambda i: (i, 0))
    heads = pl.BlockSpec((HEADS, tm, LANES), lambda i: (0, i, 0))
    in_arrays = [
        (x2, row(D_MODEL)), (mod_l, _full_spec(mod_l.shape)), (p["g1"], None),
        (p["w_in"], pl.BlockSpec((1,) + p["w_in"].shape[1:], lambda i: (p["layer"], 0, 0),
                                 pipeline_mode=pl.Buffered(1))),
        (p["cq_g"], None), (p["w_uq"], None), (p["ckv_g"], None), (p["w_uk"], None), (p["w_uvt"], None),
        (p["q_g"], None), (p["k_g"], None),
        (consts["rc"], row(LANES)), (consts["rs1"], row(LANES)), (consts["rs2"], row(LANES)),
        (p["fq_g"], None), (p["fk_g"], None), (p["f_b"], None), (p["mq_g"], None), (p["mk_g"], None),
        (p["dq_g"], None), (p["dk_g"], None),
        (consts["bd"], None), (consts["tri"], None), (consts["eq"], None), (consts["ek"], None),
        (consts["sel"], None),
    ]
    args = [a for a, _ in in_arrays]
    specs = [s if s is not None else _full_spec(a.shape) for a, s in in_arrays]
    hshape = jax.ShapeDtypeStruct((HEADS, seq, LANES), BF16)
    vshape = jax.ShapeDtypeStruct((GROUP, seq), BF16)
    dshape = jax.ShapeDtypeStruct((seq, GROUP), F32)
    vt = pl.BlockSpec((GROUP, tm), lambda i: (0, i))
    return pl.pallas_call(
        _prep_kernel,
        grid=(seq // tm,),
        in_specs=specs,
        out_specs=[heads, heads, vt] * 3 + [row(GROUP)] * 3 + [row(LANES)],
        out_shape=[hshape, hshape, vshape] * 3 + [dshape] * 3
                  + [jax.ShapeDtypeStruct((seq, LANES), F32)],
        scratch_shapes=[pltpu.VMEM((1, LANES), F32), pltpu.VMEM((HEADS * LANES, GROUP), F32),
                        pltpu.VMEM((D_MODEL, COLS_IN), BF16),
                        pltpu.VMEM((len(IN_VALUE_COLUMNS), GROUP, D_MODEL), BF16)],
        compiler_params=_params(("arbitrary",)),
        name="prep",
    )(*args)


def _flash_kernel(jlo_ref, q_ref, k_ref, vt_ref, kbias_ref, o_ref, m_sc, acc_sc, sa_sc, sb_sc,
                  *, tile, use_kbias):
    i = pl.program_id(0)
    m_sc[...] = jnp.full_like(m_sc, -jnp.inf)
    acc_sc[...] = jnp.zeros_like(acc_sc)

    def score(hh, j, buf):
        start = pl.multiple_of(j * tile, tile)
        buf[hh] = _dot_nt(k_ref[hh, pl.ds(start, tile), :], q_ref[hh])

    def absorb(hh, j, buf, causal):
        start = pl.multiple_of(j * tile, tile)
        s = buf[hh]
        if use_kbias:
            s = s + jnp.concatenate([kbias_ref[hh]] * (tile // LANES), axis=1)
        if causal:
            key = lax.broadcasted_iota(jnp.int32, (tile, tile), 0)
            qry = lax.broadcasted_iota(jnp.int32, (tile, tile), 1)
            s = jnp.where(key <= qry, s, NEG)
        m_prev = m_sc[hh]
        m_cur = jnp.max(jnp.max(s.reshape(8, tile // 8, tile), axis=0), axis=0, keepdims=True)
        m_new = jnp.maximum(m_prev, m_cur)
        alpha = jnp.exp(m_prev - m_new)
        p = jnp.exp((s - m_new).astype(BF16))
        vt = vt_ref[hh * HEAD_DIM:(hh + 1) * HEAD_DIM, pl.ds(start, tile)]
        vt = jnp.concatenate([vt, jnp.ones((ONES_ROWS, tile), BF16)], axis=0)
        acc_sc[hh] = alpha * acc_sc[hh] + _dot(vt, p)
        m_sc[hh] = m_new

    first = jlo_ref[i]
    n_off = i - first
    for hh in range(HEADS):
        score(hh, first, sa_sc)

    def body(t, carry):
        j = first + 2 * t
        for hh in range(HEADS):
            score(hh, j + 1, sb_sc)
            absorb(hh, j, sa_sc, False)
        for hh in range(HEADS):
            score(hh, j + 2, sa_sc)
            absorb(hh, j + 1, sb_sc, False)
        return carry

    lax.fori_loop(0, n_off // 2, body, 0)

    @pl.when(n_off % 2 == 1)
    def _():
        for hh in range(HEADS):
            score(hh, i, sb_sc)
            absorb(hh, i - 1, sa_sc, False)
        for hh in range(HEADS):
            absorb(hh, i, sb_sc, True)

    @pl.when(n_off % 2 == 0)
    def _():
        for hh in range(HEADS):
            absorb(hh, i, sa_sc, True)

    o_t = jnp.concatenate([acc_sc[hh, 0:HEAD_DIM, :] / acc_sc[hh, HEAD_DIM:HEAD_DIM + 1, :]
                           for hh in range(HEADS)], axis=0)
    o_ref[...] = o_t.T.astype(o_ref.dtype)


def _flash(q, k, v_t, kbias, first_tile, use_kbias):
    seq = v_t.shape[1]
    tile = FLASH_TILE
    kern = functools.partial(_flash_kernel, tile=tile, use_kbias=use_kbias)
    resident = pl.Buffered(1)
    return pl.pallas_call(
        kern,
        grid_spec=pltpu.PrefetchScalarGridSpec(
            num_scalar_prefetch=1,
            grid=(seq // tile,),
            in_specs=[
                pl.BlockSpec((HEADS, tile, LANES), lambda i, f: (0, i, 0)),
                pl.BlockSpec((HEADS, seq, LANES), lambda i, f: (0, 0, 0), pipeline_mode=resident),
                pl.BlockSpec((GROUP, seq), lambda i, f: (0, 0), pipeline_mode=resident),
                pl.BlockSpec((HEADS, tile, LANES), lambda i, f: (0, 0, 0)),
            ],
            out_specs=pl.BlockSpec((tile, GROUP), lambda i, f: (i, 0)),
            scratch_shapes=[pltpu.VMEM((HEADS, 1, tile), F32),
                            pltpu.VMEM((HEADS, HEAD_DIM + ONES_ROWS, tile), F32),
                            pltpu.VMEM((HEADS, tile, tile), F32), pltpu.VMEM((HEADS, tile, tile), F32)],
        ),
        out_shape=jax.ShapeDtypeStruct((seq, GROUP), BF16),
        compiler_params=_params(("arbitrary",)),
        name="flash_kbias" if use_kbias else "flash",
    )(first_tile, q, k, v_t, kbias)


def _fox_first_tile(decay, qk_bound):
    seq = decay.shape[0]
    nq = seq // FLASH_TILE
    f = decay[:, :HEADS]
    f_first = f[0::FLASH_TILE]
    f_last = f[FLASH_TILE - 1::FLASH_TILE]
    gap = f_first[:, None, :] - f_last[None, :, :] + 2.0 * qk_bound
    jj = jnp.arange(nq, dtype=jnp.int32)
    needed = (gap >= FOX_SKIP_LOG) | (jj[None, :, None] >= jj[:, None, None])
    first = jnp.min(jnp.where(needed, jj[None, :, None], nq), axis=1)
    return jnp.min(first, axis=1).astype(jnp.int32)


def _dilated_kernel(q_ref, k_ref, v_ref, o_ref, kbuf, vbuf, acc_s, m_s, l_s):
    pair = pl.program_id(0)
    i = pl.program_id(1)
    T = q_ref.shape[0]

    @pl.when(i == 0)
    def _():
        kbuf[...] = jnp.zeros_like(kbuf)
        vbuf[...] = jnp.zeros_like(vbuf)

    kbuf[0:T, :] = kbuf[T:2 * T, :]
    vbuf[0:T, :] = vbuf[T:2 * T, :]
    kbuf[T:2 * T, :] = k_ref[...]
    vbuf[T:2 * T, :] = v_ref[...]

    ii = lax.broadcasted_iota(jnp.int32, (DIL_SPAN, 2 * DIL_SPAN), 0)
    jj = lax.broadcasted_iota(jnp.int32, (DIL_SPAN, 2 * DIL_SPAN), 1)
    dist = ii + DIL_SPAN - jj
    band = (dist >= 0) & (dist <= DIL_SPAN)
    dist_f = dist.astype(F32)
    upper = lax.broadcasted_iota(jnp.int32, (1, LANES), 1) >= HEAD_DIM

    for pi, (window, r) in enumerate(DIL_PATTERNS):
        assert window // r == DIL_SPAN
        sub = DIL_SPAN * r

        def body(idx, carry, r=r, sub=sub, pi=pi):
            n = idx // r
            rho = idx - n * r
            base = n * sub + rho
            q = q_ref[pl.ds(base, DIL_SPAN, stride=r), :]
            kc = kbuf[pl.ds(T + base - sub, 2 * DIL_SPAN, stride=r), :].astype(BF16)
            vc = vbuf[pl.ds(T + base - sub, 2 * DIL_SPAN, stride=r), :].astype(BF16)
            first_key = jnp.where((i == 0) & (n == 0), DIL_SPAN, 0)
            valid = band & (jj >= first_key)
            stats = []
            for hh in range(2):
                slope = jnp.where(pair == 0, 2.0 ** (-(2 * hh + 1)), 2.0 ** (-(2 * hh + 5)))
                qh = (jnp.where(upper, q, 0.0) if hh else jnp.where(upper, 0.0, q)).astype(BF16)
                s = _dot_nt(qh, kc) - (slope * r) * dist_f
                s = jnp.where(valid, s, NEG)
                m = jnp.max(s, axis=-1, keepdims=True)
                p = jnp.exp(s - m)
                l = jnp.sum(p, axis=-1, keepdims=True)
                stats.append((_dot(p.astype(BF16), vc), m, l))
            rows = pl.ds(pi * T + base, DIL_SPAN, stride=r)
            acc_s[rows, :] = jnp.where(upper, stats[1][0], stats[0][0])
            m_s[rows, :] = jnp.where(upper, stats[1][1], stats[0][1])
            l_s[rows, :] = jnp.where(upper, stats[1][2], stats[0][2])
            return carry

        lax.fori_loop(0, T // DIL_SPAN, body, 0, unroll=True)

    npat = len(DIL_PATTERNS)
    ms = [m_s[pi * T:(pi + 1) * T, :] for pi in range(npat)]
    m_top = functools.reduce(jnp.maximum, ms)
    num = jnp.zeros((T, LANES), F32)
    den = jnp.zeros((T, LANES), F32)
    for pi in range(npat):
        w = jnp.exp(ms[pi] - m_top)
        num = num + w * acc_s[pi * T:(pi + 1) * T, :]
        den = den + w * l_s[pi * T:(pi + 1) * T, :]
    o_ref[...] = (num / den).astype(o_ref.dtype)


def _dilated(q, k, v):
    seq = q.shape[0]
    T = DIL_TILE
    spec = pl.BlockSpec((T, LANES), lambda p, i: (i, p))
    npat = len(DIL_PATTERNS)
    return pl.pallas_call(
        _dilated_kernel,
        grid=(HEADS // 2, seq // T),
        in_specs=[spec, spec, spec],
        out_specs=spec,
        out_shape=jax.ShapeDtypeStruct((seq, GROUP), BF16),
        scratch_shapes=[pltpu.VMEM((2 * T, LANES), F32), pltpu.VMEM((2 * T, LANES), F32),
                        pltpu.VMEM((npat * T, LANES), F32), pltpu.VMEM((npat * T, LANES), F32),
                        pltpu.VMEM((npat * T, LANES), F32)],
        compiler_params=_params(("arbitrary", "arbitrary")),
        name="dilated",
    )(q, k, v)


def _post_kernel(x_ref, oa_ref, ob_ref, oc_ref, od_ref, wout_ref, mod_ref, g2_ref,
                 rw_ref, rb_ref, tri_ref,
                 x1_ref, h2_ref, eidx_ref, gate_ref, rank_ref, cnt_ref, carry):
    i = pl.program_id(0)
    tm = x_ref.shape[0]

    @pl.when(i == 0)
    def _():
        carry[...] = jnp.zeros_like(carry)

    o = (_dot(oa_ref[...], wout_ref[0]) + _dot(ob_ref[...], wout_ref[1])
         + _dot(oc_ref[...], wout_ref[2]) + _dot(od_ref[...], wout_ref[3]))
    x1 = x_ref[...] + mod_ref[2] * o
    x1_ref[...] = x1
    y = x1 * lax.rsqrt(jnp.mean(x1 * x1, axis=-1, keepdims=True) + EPS) * g2_ref[...]
    h2 = y * (1.0 + mod_ref[4]) + mod_ref[3]
    _store_token_tiles(h2_ref, h2)

    h_hi, h_lo = _split2(h2)
    hh = _dot(h_hi, rw_ref[...])
    logits = (hh[:, :LANES] + hh[:, LANES:] + _dot(h_lo, rw_ref[:, :LANES])
              + rb_ref[...])
    lane = lax.broadcasted_iota(jnp.int32, (tm, LANES), 1)
    lane_f = lane.astype(F32)
    g = logits
    chosen = jnp.zeros((tm, LANES), F32)
    vals, idxs = [], []
    for _ in range(TOP_K):
        m = jnp.max(g, axis=-1, keepdims=True)
        first = jnp.min(jnp.where(g == m, lane_f, 1e9), axis=-1, keepdims=True)
        pick = lane_f == first
        chosen = jnp.where(pick, 1.0, chosen)
        g = jnp.where(pick, -jnp.inf, g)
        vals.append(m)
        idxs.append(first)
    exps = [jnp.exp(v - vals[0]) for v in vals]
    den = exps[0] + exps[1] + exps[2] + exps[3]
    before = _dot(tri_ref[...], chosen.astype(BF16)) + carry[...]
    carry[...] = carry[...] + jnp.sum(chosen, axis=0, keepdims=True)
    cnt_ref[...] = carry[...]
    e_out = jnp.zeros((tm, LANES), F32)
    g_out = jnp.zeros((tm, LANES), F32)
    r_out = jnp.zeros((tm, LANES), F32)
    for k in range(TOP_K):
        rank_k = jnp.sum(jnp.where(lane_f == idxs[k], before, 0.0), axis=-1, keepdims=True)
        e_out = jnp.where(lane == k, idxs[k], e_out)
        g_out = jnp.where(lane == k, exps[k] / den, g_out)
        r_out = jnp.where(lane == k, rank_k, r_out)
    eidx_ref[...] = e_out.astype(jnp.int32)
    gate_ref[...] = g_out
    rank_ref[...] = r_out.astype(jnp.int32)


def _post(x2, oa, ob, oc, od, mod_l, consts, p):
    seq = x2.shape[0]
    tm = TOKEN_TILE
    row = lambda n: pl.BlockSpec((tm, n), lambda i: (i, 0))
    full = [p["w_out"], mod_l, p["g2"], p["rw"], p["r_b"], consts["tri_strict"]]
    f32 = lambda n: jax.ShapeDtypeStruct((seq, n), F32)
    i32 = lambda n: jax.ShapeDtypeStruct((seq, n), jnp.int32)
    return pl.pallas_call(
        _post_kernel,
        grid=(seq // tm,),
        in_specs=[row(D_MODEL)] + [row(GROUP)] * 4 + [_full_spec(a.shape) for a in full],
        out_specs=[row(D_MODEL), pl.BlockSpec((tm * CHUNKS, LANES), lambda i: (i, 0)),
                   row(LANES), row(LANES), row(LANES), _full_spec((1, LANES))],
        out_shape=[f32(D_MODEL), jax.ShapeDtypeStruct((seq * CHUNKS, LANES), F32),
                   i32(LANES), f32(LANES), i32(LANES), jax.ShapeDtypeStruct((1, LANES), F32)],
        scratch_shapes=[pltpu.VMEM((1, LANES), F32)],
        compiler_params=_params(("arbitrary",)),
        name="post",
    )(x2, oa, ob, oc, od, *full)


def _tile_copy(src, s, dst, d, sem):
    return pltpu.make_async_copy(src.at[pl.ds(pl.multiple_of(s * CHUNKS, CHUNKS), CHUNKS), :],
                                 dst.at[pl.ds(pl.multiple_of(d * CHUNKS, CHUNKS), CHUNKS), :], sem)


def _dispatch_kernel(dest_ref, padlo_ref, padn_ref, nu_ref, h_ref, xs_ref, zbuf, sem, zsem):
    i = pl.program_id(0)
    n = ROUTE_TILE * TOP_K
    base = i * n
    block_rows = EXPERT_ROWS * CHUNKS

    @pl.when(i == 0)
    def _():
        zbuf[...] = jnp.zeros_like(zbuf)

        def fill(wait):
            def go(copy):
                copy.wait() if wait else copy.start()

            def per_expert(e, carry):
                lo, cnt = padlo_ref[e], padn_ref[e]
                off = lo
                p = EXPERT_ROWS // 2
                while p >= 1:
                    rows = p * CHUNKS

                    @pl.when((cnt & p) != 0)
                    def _(off=off, rows=rows):
                        go(pltpu.make_async_copy(
                            zbuf.at[pl.ds(0, rows), :],
                            xs_ref.at[pl.ds(pl.multiple_of(off * CHUNKS, CHUNKS), rows), :], zsem))

                    off = off + (cnt & p)
                    p //= 2
                return carry

            lax.fori_loop(0, N_EXPERTS, per_expert, 0)

            def per_block(b, carry):
                go(pltpu.make_async_copy(
                    zbuf, xs_ref.at[pl.ds(pl.multiple_of(b * block_rows, block_rows), block_rows), :],
                    zsem))
                return carry

            lax.fori_loop(nu_ref[0], xs_ref.shape[0] // block_rows, per_block, 0)

        fill(False)
        fill(True)

    def issue(r, carry):
        for k in range(TOP_K):
            _tile_copy(h_ref, r, xs_ref, dest_ref[base + r * TOP_K + k], sem).start(priority=k % 2)
        return carry

    lax.fori_loop(0, ROUTE_TILE, issue, 0, unroll=2)
    rows = pl.ds(0, n * CHUNKS)
    pltpu.make_async_copy(xs_ref.at[rows, :], xs_ref.at[rows, :], sem).wait()


def _dispatch(dest, pad_lo, pad_n, n_used, h2_tiles, m_pad):
    seq = h2_tiles.shape[0] // CHUNKS
    return pl.pallas_call(
        _dispatch_kernel,
        grid_spec=pltpu.PrefetchScalarGridSpec(
            num_scalar_prefetch=4,
            grid=(seq // ROUTE_TILE,),
            in_specs=[pl.BlockSpec((ROUTE_TILE * CHUNKS, LANES), lambda i, *_: (i, 0))],
            out_specs=pl.BlockSpec(memory_space=pl.ANY),
            scratch_shapes=[pltpu.VMEM((EXPERT_ROWS * CHUNKS, LANES), F32),
                            pltpu.SemaphoreType.DMA(()), pltpu.SemaphoreType.DMA(())],
        ),
        out_shape=jax.ShapeDtypeStruct((m_pad * CHUNKS, LANES), F32),
        compiler_params=_params(("arbitrary",)),
        name="dispatch",
    )(dest, pad_lo, pad_n, n_used, h2_tiles)


def _expert_kernel(be_ref, nu_ref, nxt_ref, ord_ref, xs_ref, w1_hbm, b1_ref, w2_hbm, b2_ref, ys_ref,
                   w1f, w2f, w1b, w2b, sems):
    b = pl.program_id(0)
    e = be_ref[b]
    prev = be_ref[jnp.maximum(b - 1, 0)]
    fresh = ((b == 0) | (e != prev)) & (b < nu_ref[0])
    slot = ord_ref[b] % 2

    def fetch(expert, to_slot):
        return (pltpu.make_async_copy(w1_hbm.at[expert], w1f.at[to_slot], sems.at[0, to_slot]),
                pltpu.make_async_copy(w2_hbm.at[expert], w2f.at[to_slot], sems.at[1, to_slot]))

    @pl.when(b == 0)
    def _():
        for copy in fetch(e, slot):
            copy.start()

    @pl.when(fresh)
    def _():
        for copy in fetch(e, slot):
            copy.wait()
        w1b[...] = w1f[slot].astype(BF16)
        w2b[...] = w2f[slot].astype(BF16)

        @pl.when(nxt_ref[b] >= 0)
        def _():
            for copy in fetch(nxt_ref[b], 1 - slot):
                copy.start()

    @pl.when(b < nu_ref[0])
    def _():
        half = EXPERT_ROWS // 2
        gus = []
        for r in range(2):
            xb = jnp.concatenate(
                [_load_token_chunk(xs_ref, half, c, offset=r * half * CHUNKS) for c in range(CHUNKS)],
                axis=1).astype(BF16)
            gus.append(_dot(xb, w1b[...]) + b1_ref[0])
        for r in range(2):
            g = jnp.minimum(gus[r][:, :D_EXPERT], SWIGLU_LIMIT)
            u = jnp.clip(gus[r][:, D_EXPERT:], -SWIGLU_LIMIT, SWIGLU_LIMIT)
            y = (u + 1.0) * g * (1.0 / (1.0 + jnp.exp(-SWIGLU_ALPHA * g)))
            _store_token_tiles(ys_ref, _dot(y.astype(BF16), w2b[...]) + b2_ref[0],
                               offset=r * half * CHUNKS)

    @pl.when(b >= nu_ref[0])
    def _():
        ys_ref[...] = jnp.zeros_like(ys_ref)


def _experts(blk_expert, n_used, blk_next, blk_ord, xs, w1, b1, w2, b2):
    m_pad = xs.shape[0] // CHUNKS
    bm = EXPERT_ROWS
    n_all = w1.shape[0] * w1.shape[1]
    rows = lambda b, be, nu, *_: (jnp.minimum(b, nu[0] - 1), 0)
    ex = lambda b, be, nu, *_: (be[jnp.minimum(b, nu[0] - 1)], 0, 0)
    return pl.pallas_call(
        _expert_kernel,
        grid_spec=pltpu.PrefetchScalarGridSpec(
            num_scalar_prefetch=4,
            grid=(m_pad // bm,),
            in_specs=[
                pl.BlockSpec((bm * CHUNKS, LANES), rows),
                pl.BlockSpec(memory_space=pl.ANY),
                pl.BlockSpec((1, 1, 2 * D_EXPERT), ex),
                pl.BlockSpec(memory_space=pl.ANY),
                pl.BlockSpec((1, 1, D_MODEL), ex),
            ],
            out_specs=pl.BlockSpec((bm * CHUNKS, LANES), lambda b, *_: (b, 0)),
            scratch_shapes=[pltpu.VMEM((2, D_MODEL, 2 * D_EXPERT), F32),
                            pltpu.VMEM((2, D_EXPERT, D_MODEL), F32),
                            pltpu.VMEM((D_MODEL, 2 * D_EXPERT), BF16),
                            pltpu.VMEM((D_EXPERT, D_MODEL), BF16),
                            pltpu.SemaphoreType.DMA((2, 2))],
        ),
        out_shape=jax.ShapeDtypeStruct((m_pad * CHUNKS, LANES), F32),
        compiler_params=_params(("arbitrary",)),
        name="experts",
    )(blk_expert, n_used, blk_next, blk_ord, xs,
      w1.reshape(n_all, D_MODEL, 2 * D_EXPERT), b1.reshape(n_all, 1, -1),
      w2.reshape(n_all, D_EXPERT, D_MODEL), b2.reshape(n_all, 1, -1))


def _combine_kernel(dest_ref, ys_ref, x1_ref, gate_ref, mod_ref, o_ref, buf, sems):
    i = pl.program_id(0)
    tm = x1_ref.shape[0]
    n = tm * TOP_K
    slot = i % 2

    def gather(step, to_slot):
        base = step * n

        def issue(r, carry):
            for k in range(TOP_K):
                _tile_copy(ys_ref, dest_ref[base + r * TOP_K + k], buf, to_slot * n + k * tm + r,
                           sems.at[to_slot]).start(priority=k % 2)
            return carry

        lax.fori_loop(0, tm, issue, 0, unroll=2)

    @pl.when(i == 0)
    def _():
        gather(0, 0)

    @pl.when(i + 1 < pl.num_programs(0))
    def _():
        gather(i + 1, 1 - slot)

    mine = pl.ds(pl.multiple_of(slot * n * CHUNKS, n * CHUNKS), n * CHUNKS)
    pltpu.make_async_copy(ys_ref.at[pl.ds(0, n * CHUNKS), :], buf.at[mine, :], sems.at[slot]).wait()
    gates = gate_ref[...]
    g2 = mod_ref[5]
    for c in range(CHUNKS):
        cols = slice(c * LANES, (c + 1) * LANES)
        mix = jnp.zeros((tm, LANES), F32)
        for k in range(TOP_K):
            mix = mix + gates[:, k:k + 1] * _load_token_chunk(
                buf, tm, c, offset=(slot * n + k * tm) * CHUNKS)
        o_ref[:, cols] = x1_ref[:, cols] + g2[:, cols] * mix


def _combine(dest, ys, x1, gates, mod_l):
    seq = x1.shape[0]
    tm = ROUTE_TILE
    return pl.pallas_call(
        _combine_kernel,
        grid_spec=pltpu.PrefetchScalarGridSpec(
            num_scalar_prefetch=1,
            grid=(seq // tm,),
            in_specs=[
                pl.BlockSpec(memory_space=pl.ANY),
                pl.BlockSpec((tm, D_MODEL), lambda i, d: (i, 0)),
                pl.BlockSpec((tm, LANES), lambda i, d: (i, 0)),
                pl.BlockSpec(mod_l.shape, lambda i, d: (0, 0, 0)),
            ],
            out_specs=pl.BlockSpec((tm, D_MODEL), lambda i, d: (i, 0)),
            scratch_shapes=[pltpu.VMEM((2 * TOP_K * tm * CHUNKS, LANES), F32),
                            pltpu.SemaphoreType.DMA((2,))],
        ),
        out_shape=jax.ShapeDtypeStruct((seq, D_MODEL), F32),
        compiler_params=_params(("arbitrary",)),
        name="combine",
    )(dest, ys, x1, gates, mod_l)


def _moe(l, x1, h2_tiles, eidx, gates, rank, counts, mod_l, w1, b1, w2, b2):
    seq = x1.shape[0]
    bm = EXPERT_ROWS
    m_pad = seq * TOP_K + N_EXPERTS * bm
    cnt = counts[0, :N_EXPERTS].astype(jnp.int32)
    padded = (cnt + bm - 1) // bm * bm
    pad_end = jnp.cumsum(padded)
    pad_start = pad_end - padded
    onehot = eidx[:, :TOP_K, None] == jnp.arange(N_EXPERTS, dtype=jnp.int32)
    start_of = jnp.sum(jnp.where(onehot, pad_start, 0), axis=-1)
    dest = (start_of + rank[:, :TOP_K]).reshape(seq * TOP_K).astype(jnp.int32)
    nblk = m_pad // bm
    blk_start = jnp.arange(nblk, dtype=jnp.int32) * bm
    local = jnp.minimum(jnp.sum(pad_end[None, :] <= blk_start[:, None], axis=1), N_EXPERTS - 1)
    blk_expert = (local + l * N_EXPERTS).astype(jnp.int32)
    n_used = (pad_end[-1:] // bm).astype(jnp.int32)
    ids = jnp.arange(N_EXPERTS, dtype=jnp.int32)
    live = padded > 0
    ordinal = jnp.cumsum(live.astype(jnp.int32)) - 1
    later = live[None, :] & (ids[None, :] > ids[:, None])
    nxt = jnp.min(jnp.where(later, ids[None, :], N_EXPERTS), axis=1)
    nxt = jnp.where(nxt < N_EXPERTS, nxt + l * N_EXPERTS, -1)
    of_block = local[:, None] == ids[None, :]
    blk_next = jnp.sum(jnp.where(of_block, nxt[None, :], 0), axis=1).astype(jnp.int32)
    blk_ord = jnp.sum(jnp.where(of_block, ordinal[None, :], 0), axis=1).astype(jnp.int32)
    xs = _dispatch(dest, (pad_start + cnt).astype(jnp.int32), (padded - cnt).astype(jnp.int32),
                   n_used, h2_tiles, m_pad)
    ys = _experts(blk_expert, n_used, blk_next, blk_ord, xs, w1, b1, w2, b2)
    return _combine(dest, ys, x1, gates, mod_l)


def _pad_cols(a, n):
    return jnp.pad(a, ((0, 0), (0, n - a.shape[1])))


def _layer_params(l, w_in, mla_cq_g, mla_w_uq, mla_ckv_g, mla_w_ukv, mla_q_g, mla_k_g,
                  fox_q_g, fox_k_g, fox_b_f, moba_q_g, moba_k_g, dil_q_g, dil_k_g, w_out,
                  norm1_g, norm2_g, router_w, router_b):
    w_uq = jnp.pad(mla_w_uq[l].reshape(MLA_Q_RANK, HEADS, MLA_QK),
                   ((0, 0), (0, 0), (0, LANES - MLA_QK))).reshape(MLA_Q_RANK, HEADS * LANES)
    w_ukv = mla_w_ukv[l].reshape(MLA_KV_RANK, HEADS, MLA_NOPE + HEAD_DIM)
    w_uk = jnp.pad(w_ukv[:, :, :MLA_NOPE], ((0, 0), (0, 0), (0, LANES - MLA_NOPE)))
    w_uv = w_ukv[:, :, MLA_NOPE:]
    tile4 = lambda g: jnp.tile(g, HEADS)[None, :]
    rw = _pad_cols(router_w[l], LANES)
    rw_hi = rw.astype(BF16)
    rw_lo = (rw - rw_hi.astype(F32)).astype(BF16)
    r_b = jnp.concatenate([router_b[l], jnp.full((LANES - N_EXPERTS,), NEG, F32)])[None, :]
    return dict(
        g1=norm1_g[l][None, :], g2=norm2_g[l][None, :], w_in=w_in, layer=l,
        cq_g=mla_cq_g[l][None, :], w_uq=w_uq.astype(BF16), ckv_g=mla_ckv_g[l][None, :],
        w_uk=w_uk.reshape(MLA_KV_RANK, HEADS * LANES).astype(BF16),
        w_uvt=w_uv.reshape(MLA_KV_RANK, GROUP).T.astype(BF16),
        fox_bound=FOX_NORM_SLACK * HEAD_DIM ** 0.5 * jnp.max(jnp.abs(fox_q_g[l]))
        * jnp.max(jnp.abs(fox_k_g[l])),
        q_g=_pad_cols(mla_q_g[l][None, :], LANES), k_g=_pad_cols(mla_k_g[l][None, :], LANES),
        fq_g=tile4(fox_q_g[l]), fk_g=tile4(fox_k_g[l]), f_b=_pad_cols(fox_b_f[l][None, :], LANES),
        mq_g=tile4(moba_q_g[l]), mk_g=tile4(moba_k_g[l]), dq_g=tile4(dil_q_g[l]), dk_g=tile4(dil_k_g[l]),
        w_out=w_out[l].reshape(HEADS, GROUP, D_MODEL).astype(BF16),
        rw=jnp.concatenate([rw_hi, rw_lo], axis=1), r_b=r_b,
    )


def kernel(x, c, w_mod, b_mod, norm1_g, norm2_g, w_in, mla_cq_g, mla_w_uq, mla_ckv_g, mla_w_ukv, mla_q_g, mla_k_g, fox_q_g, fox_k_g, fox_b_f, moba_q_g, moba_k_g, dil_q_g, dil_k_g, w_out, router_w, router_b, exp_w1, exp_b1, exp_w2, exp_b2):
    batch, seq, d = x.shape
    assert batch == 1 and d == D_MODEL
    assert seq % DIL_TILE == 0 and seq // MOBA_BLOCK <= MOBA_MAX_BLOCKS
    depth = w_mod.shape[0]
    consts = _prep_constants(seq, TOKEN_TILE)
    mod = _modulation(c, w_mod, b_mod)
    slopes_c = 2.0 ** (-(2.0 * np.arange(HEADS) + 2.0))
    in_block = np.arange(FLASH_TILE) % MOBA_BLOCK
    kbias_c = jnp.asarray(np.broadcast_to(slopes_c[:, None, None] * in_block[None, :, None],
                                          (HEADS, FLASH_TILE, LANES)), F32)
    kbias_0 = jnp.zeros((HEADS, FLASH_TILE, LANES), F32)
    all_tiles = jnp.zeros((seq // FLASH_TILE,), jnp.int32)
    x2 = x.reshape(seq, d)
    for l in range(depth):
        p = _layer_params(l, w_in, mla_cq_g, mla_w_uq, mla_ckv_g, mla_w_ukv, mla_q_g, mla_k_g,
                          fox_q_g, fox_k_g, fox_b_f, moba_q_g, moba_k_g, dil_q_g, dil_k_g, w_out,
                          norm1_g, norm2_g, router_w, router_b)
        mod_l = mod[l]
        qa, ka, va, qf, kf, vf, qm, km, vm, qd, kd, vd, decay = _prep(x2, mod_l, consts, p)
        oa = _flash(qa, ka, va, kbias_0, all_tiles, False)
        ob = _flash(qf, kf, vf, kbias_0, _fox_first_tile(decay, p["fox_bound"]), False)
        oc = _flash(qm, km, vm, kbias_c, all_tiles, True)
        od = _dilated(qd, kd, vd)
        x1, h2, eidx, gates, rank, counts = _post(x2, oa, ob, oc, od, mod_l, consts, p)
        x2 = _moe(l, x1, h2, eidx, gates, rank, counts, mod_l, exp_w1, exp_b1, exp_w2, exp_b2)
    return x2.reshape(batch, seq, d)
```

```python
import functools

import numpy as np
import jax
import jax.numpy as jnp
from jax import lax
from jax.experimental import pallas as pl
from jax.experimental.pallas import tpu as pltpu

F32 = jnp.float32
BF16 = jnp.bfloat16

D_MODEL = 1024
HEAD_DIM = 64
HEADS = 4
GROUP = HEADS * HEAD_DIM
LANES = 128
CHUNKS = D_MODEL // LANES
MLA_Q_RANK = 256
MLA_KV_RANK = 128
MLA_NOPE = 64
MLA_ROPE = 32
MLA_QK = MLA_NOPE + MLA_ROPE
ROPE_THETA = 10000.0
MOBA_BLOCK = 256
MOBA_TOPK = 3
MOBA_MAX_BLOCKS = 64
DIL_PATTERNS = ((128, 1), (512, 4), (2048, 16))
DIL_SPAN = 128
DIL_TILE = 2048
N_EXPERTS = 32
TOP_K = 4
D_EXPERT = 1024
SWIGLU_LIMIT = 7.0
SWIGLU_ALPHA = 1.702
EPS = 1e-6
NEG = -1e30

FLASH_TILE = 512
ONES_ROWS = 16
FOX_SKIP_LOG = -106.0
FOX_NORM_SLACK = 1.02
TOKEN_TILE = 256
ROUTE_TILE = 512
EXPERT_ROWS = 512
VMEM_LIMIT = 56 * 1024 * 1024

COLS_MLA = 512
COLS_FOX = 640
COLS_MOBA = 512
COLS_DIL = 768
COLS_IN = COLS_MLA + COLS_FOX + COLS_MOBA + COLS_DIL


def _in_column_map():
    sizes = [MLA_Q_RANK, MLA_KV_RANK, MLA_ROPE, GROUP, GROUP, GROUP, HEADS] + [GROUP] * 6
    src = np.concatenate([[0], np.cumsum(sizes)])
    cq, ckv, kr, fq, fk, fv, flog, mq, mk, mv, dq, dk, dv = range(13)
    layout = [(cq, 0), (ckv, MLA_Q_RANK), (kr, MLA_Q_RANK + MLA_KV_RANK + MLA_NOPE),
              (fq, COLS_MLA), (fk, COLS_MLA + GROUP), (flog, COLS_MLA + 2 * GROUP),
              (mq, COLS_MLA + COLS_FOX), (mk, COLS_MLA + COLS_FOX + GROUP),
              (dq, COLS_IN - 3 * GROUP), (dk, COLS_IN - 2 * GROUP), (dv, COLS_IN - GROUP)]
    moves = tuple((dst, int(src[piece]), sizes[piece]) for piece, dst in layout)
    return moves, (int(src[fv]), int(src[mv]))


IN_COLUMN_MAP, IN_VALUE_COLUMNS = _in_column_map()


def _dot(a, b):
    return jnp.dot(a, b, preferred_element_type=F32)


def _dot_nt(a, b):
    return lax.dot_general(a, b, (((1,), (1,)), ((), ())), preferred_element_type=F32)


def _split2(x):
    hi = x.astype(BF16)
    lo = (x - hi.astype(F32)).astype(BF16)
    return hi, lo


def _split3(x):
    a = x.astype(BF16)
    r = x - a.astype(F32)
    b = r.astype(BF16)
    c = (r - b.astype(F32)).astype(BF16)
    return a, b, c


def _head_of_lane():
    return jnp.right_shift(lax.broadcasted_iota(jnp.int32, (1, GROUP), 1), 6)


def _store_token_tiles(ref, x, offset=0):
    n = x.shape[0]
    for c in range(CHUNKS):
        ref[pl.ds(offset + c, n, stride=CHUNKS), :] = x[:, c * LANES:(c + 1) * LANES]


def _load_token_chunk(ref, n, c, offset=0):
    return ref[pl.ds(offset + c, n, stride=CHUNKS), :]


def _full_spec(shape):
    nd = len(shape)
    return pl.BlockSpec(shape, lambda *_: (0,) * nd)


def _params(sem):
    return pltpu.CompilerParams(dimension_semantics=sem, vmem_limit_bytes=VMEM_LIMIT)


def _mod_kernel(c_ref, w_ref, b_ref, o_ref):
    c = c_ref[...]
    s = c * (1.0 / (1.0 + jnp.exp(-c)))
    s8 = jnp.broadcast_to(s, (8, D_MODEL))
    r = jnp.dot(s8, w_ref[0], preferred_element_type=F32, precision=lax.Precision.HIGHEST)
    o_ref[0, 0] = r[0:1, :] + b_ref[0, 0]


def _modulation(c, w_mod, b_mod):
    depth = w_mod.shape[0]
    b4 = b_mod.reshape(depth, 6, 1, D_MODEL)
    return pl.pallas_call(
        _mod_kernel,
        grid=(depth, 6),
        in_specs=[
            pl.BlockSpec((1, D_MODEL), lambda l, j: (0, 0)),
            pl.BlockSpec((1, D_MODEL, D_MODEL), lambda l, j: (l, 0, j)),
            pl.BlockSpec((1, 1, 1, D_MODEL), lambda l, j: (l, j, 0, 0)),
        ],
        out_specs=pl.BlockSpec((1, 1, 1, D_MODEL), lambda l, j: (l, j, 0, 0)),
        out_shape=jax.ShapeDtypeStruct((depth, 6, 1, D_MODEL), F32),
        compiler_params=_params(("arbitrary", "arbitrary")),
        name="modulation",
    )(c, w_mod, b4)


def _head_norm(x, g, bd):
    hi, lo = _split2(x * x)
    ss = _dot(hi, bd) + _dot(lo, bd)
    return x * lax.rsqrt(ss * (1.0 / HEAD_DIM) + EPS) * g


def _prep_kernel(x_ref, mod_ref, g1_ref, win_hbm, cqg_ref, wuq_ref, ckvg_ref, wuk_ref,
                 wuvt_ref, qg_ref, kg_ref, cs_ref,
                 fqg_ref, fkg_ref, fb_ref, mqg_ref, mkg_ref, dqg_ref, dkg_ref,
                 bd_ref, tri_ref, eq_ref, ek_ref, sel_ref,
                 qa_ref, ka_ref, va_ref, qf_ref, kf_ref, vf_ref, qm_ref, km_ref, vm_ref,
                 qd_ref, kd_ref, vd_ref, f_ref,
                 fcarry, kmean, wraw_ref, win_ref, wvt_ref, rope_scr, wsem, *, layer):
    i = pl.program_id(0)
    tm = x_ref.shape[0]

    @pl.when(i == 0)
    def _():
        fetch = pltpu.make_async_copy(win_hbm.at[layer], wraw_ref, wsem)
        fetch.start()
        fcarry[...] = jnp.zeros_like(fcarry)
        kmean[...] = jnp.zeros_like(kmean)
        win_ref[...] = jnp.zeros_like(win_ref)
        rope_scr[...] = jnp.zeros_like(rope_scr)
        rope_scr[0, :, 0:MLA_NOPE] = jnp.ones((tm, MLA_NOPE), F32)
        fetch.wait()
        for dst, src, width in IN_COLUMN_MAP:
            win_ref[:, dst:dst + width] = wraw_ref[:, src:src + width].astype(BF16)
        for slot, src in enumerate(IN_VALUE_COLUMNS):
            wvt_ref[slot] = wraw_ref[:, src:src + GROUP].T.astype(BF16)

    x = x_ref[...]
    y = x * lax.rsqrt(jnp.mean(x * x, axis=-1, keepdims=True) + EPS) * g1_ref[...]
    hb = (y * (1.0 + mod_ref[1]) + mod_ref[0]).astype(BF16)
    bd = bd_ref[...]
    lane = lax.broadcasted_iota(jnp.int32, (1, LANES), 1)
    lane_f = lane.astype(F32)
    head_of_lane = _head_of_lane()

    pa = _dot(hb, win_ref[:, 0:COLS_MLA])
    cq = pa[:, 0:MLA_Q_RANK]
    ckv = pa[:, MLA_Q_RANK:MLA_Q_RANK + MLA_KV_RANK]
    kr = pa[:, MLA_Q_RANK + MLA_KV_RANK:COLS_MLA]
    cqn = (cq * lax.rsqrt(jnp.mean(cq * cq, axis=-1, keepdims=True) + EPS) * cqg_ref[...]).astype(BF16)
    ckvn = (ckv * lax.rsqrt(jnp.mean(ckv * ckv, axis=-1, keepdims=True) + EPS) * ckvg_ref[...]).astype(BF16)
    q_all = _dot(cqn, wuq_ref[...])
    k_all = _dot(ckvn, wuk_ref[...])
    va_ref[...] = _dot_nt(wuvt_ref[...], ckvn).astype(BF16)
    half = MLA_ROPE // 2
    cos, sin = cs_ref[:, 0:half], cs_ref[:, half:MLA_ROPE]
    rope_scr[0, :, MLA_NOPE:MLA_NOPE + half] = cos
    rope_scr[0, :, MLA_NOPE + half:MLA_QK] = cos
    rope_scr[1, :, MLA_NOPE:MLA_NOPE + half] = -sin
    rope_scr[2, :, MLA_NOPE + half:MLA_QK] = sin
    rc, rs1, rs2 = rope_scr[0], rope_scr[1], rope_scr[2]

    def rope(t):
        return t * rc + pltpu.roll(t, LANES - MLA_ROPE // 2, 1) * rs1 + pltpu.roll(t, MLA_ROPE // 2, 1) * rs2

    for h in range(HEADS):
        q = q_all[:, h * LANES:(h + 1) * LANES]
        q = q * lax.rsqrt(jnp.sum(q * q, axis=-1, keepdims=True) * (1.0 / MLA_QK) + EPS) * qg_ref[...]
        qa_ref[h] = (rope(q) * (MLA_QK ** -0.5)).astype(BF16)
        k = k_all[:, h * LANES:(h + 1) * LANES] + kr
        k = k * lax.rsqrt(jnp.sum(k * k, axis=-1, keepdims=True) * (1.0 / MLA_QK) + EPS) * kg_ref[...]
        ka_ref[h] = rope(k).astype(BF16)

    pf = _dot(hb, win_ref[:, COLS_MLA:COLS_MLA + COLS_FOX])
    fqn = (_head_norm(pf[:, 0:GROUP], fqg_ref[...], bd) * (HEAD_DIM ** -0.5)).astype(BF16)
    fkn = _head_norm(pf[:, GROUP:2 * GROUP], fkg_ref[...], bd).astype(BF16)
    vf_ref[...] = _dot_nt(wvt_ref[0], hb).astype(BF16)
    z = pf[:, 2 * GROUP:2 * GROUP + LANES] + fb_ref[...]
    log_f = jnp.minimum(z, 0.0) - jnp.log(1.0 + jnp.exp(-jnp.abs(z)))
    tri = tri_ref[...]
    a1, a2, a3 = _split3(log_f)
    cum = fcarry[...] + (_dot(tri, a1) + _dot(tri, a2) + _dot(tri, a3))
    fcarry[...] = cum[tm - 1:tm, :]
    f_ref[...] = cum
    f1, f2, f3 = _split3(cum)
    xq = jnp.concatenate([fqn, f1, f2, f3], axis=1)
    xk = jnp.concatenate([fkn, f1, f2, f3], axis=1)
    ones_q = jnp.where((lane >= HEAD_DIM + 3) & (lane < HEAD_DIM + 6), 1.0, 0.0)
    ones_k = jnp.where((lane >= HEAD_DIM) & (lane < HEAD_DIM + 3), 1.0, 0.0)
    qf_all = _dot(xq, eq_ref[...])
    kf_all = _dot(xk, ek_ref[...])
    for h in range(HEADS):
        qf_ref[h] = (qf_all[:, h * LANES:(h + 1) * LANES] + ones_q).astype(BF16)
        kf_ref[h] = (kf_all[:, h * LANES:(h + 1) * LANES] + ones_k).astype(BF16)

    pm = _dot(hb, win_ref[:, COLS_MLA + COLS_FOX:COLS_MLA + COLS_FOX + COLS_MOBA])
    mqn = _head_norm(pm[:, 0:GROUP], mqg_ref[...], bd) * (HEAD_DIM ** -0.5)
    mkn = _head_norm(pm[:, GROUP:2 * GROUP], mkg_ref[...], bd)
    vm_ref[...] = _dot_nt(wvt_ref[1], hb).astype(BF16)
    col_mean = jnp.mean(mkn, axis=0, keepdims=True)
    mqb = mqn.astype(BF16)
    mkb = mkn.astype(BF16)
    blk = lane - HEAD_DIM
    blk_f = blk.astype(F32)
    past = (blk >= 0) & (blk < i)
    i_f = i.astype(F32)
    for h in range(HEADS):
        kmean[pl.ds(h * LANES + HEAD_DIM + i, 1), :] = jnp.where(head_of_lane == h, col_mean, 0.0)
    q_hi, q_lo = _split2(mqn)
    km_hi, km_lo = _split2(kmean[...])
    gate_all = _dot_nt(q_hi, km_hi) + _dot_nt(q_hi, km_lo) + _dot_nt(q_lo, km_hi)
    qm_all = _dot(mqb, sel_ref[...])
    km_all = _dot(mkb, sel_ref[...])
    for h in range(HEADS):
        g = jnp.where(past, gate_all[:, h * LANES:(h + 1) * LANES], NEG)
        chosen = jnp.zeros((tm, LANES), F32)
        for _ in range(MOBA_TOPK):
            m = jnp.max(g, axis=-1, keepdims=True)
            first = jnp.min(jnp.where(g == m, lane_f, 1e9), axis=-1, keepdims=True)
            pick = (lane_f == first) & (m > NEG)
            chosen = jnp.where(pick, 1.0, chosen)
            g = jnp.where(pick, NEG, g)
        slope = 2.0 ** (-(2 * h + 2))
        keep = (chosen > 0.0) | (blk == i)
        bias = jnp.where(keep, (slope * MOBA_BLOCK) * (blk_f - i_f), NEG)
        bias = jnp.where(blk >= 0, bias, 0.0)
        qm_ref[h] = (qm_all[:, h * LANES:(h + 1) * LANES] + bias).astype(BF16)
        onehot = jnp.where(blk == i, 1.0, 0.0)
        km_ref[h] = (km_all[:, h * LANES:(h + 1) * LANES] + onehot).astype(BF16)

    pd = _dot(hb, win_ref[:, COLS_MLA + COLS_FOX + COLS_MOBA:COLS_IN])
    qd_ref[...] = _head_norm(pd[:, 0:GROUP], dqg_ref[...], bd) * (HEAD_DIM ** -0.5)
    kd_ref[...] = _head_norm(pd[:, GROUP:2 * GROUP], dkg_ref[...], bd)
    vd_ref[...] = pd[:, 2 * GROUP:3 * GROUP]


def _prep_constants(seq, tm):
    half = MLA_ROPE // 2
    inv = 1.0 / (ROPE_THETA ** (jnp.arange(half, dtype=F32) / half))
    ang = jnp.arange(seq, dtype=F32)[:, None] * inv[None, :]
    cs = jnp.concatenate([jnp.cos(ang), jnp.sin(ang)], axis=1)
    bd =np.kron(np.eye(HEADS, dtype=np.float32), np.ones((HEAD_DIM, HEAD_DIM), np.float32))
    tri = np.tril(np.ones((tm, tm), np.float32))
    tri_strict = np.tril(np.ones((tm, tm), np.float32), -1)
    sel = np.zeros((GROUP, HEADS * LANES), np.float32)
    eq = np.zeros((GROUP + 3 * LANES, HEADS * LANES), np.float32)
    ek = np.zeros((GROUP + 3 * LANES, HEADS * LANES), np.float32)
    for h in range(HEADS):
        for d in range(HEAD_DIM):
            sel[h * HEAD_DIM + d, h * LANES + d] = 1.0
        for piece in range(3):
            eq[GROUP + piece * LANES + h, h * LANES + HEAD_DIM + piece] = 1.0
            ek[GROUP + piece * LANES + h, h * LANES + HEAD_DIM + 3 + piece] = -1.0
    eq[:GROUP] = sel
    ek[:GROUP] = sel
    as_bf = lambda a: jnp.asarray(a, BF16)
    return dict(cs=cs, bd=as_bf(bd),
                tri=as_bf(tri), tri_strict=as_bf(tri_strict), sel=as_bf(sel), eq=as_bf(eq), ek=as_bf(ek))


def _prep(x2, mod_l, consts, p):
    seq = x2.shape[0]
    tm = TOKEN_TILE
    row = lambda n: pl.BlockSpec((tm, n), lambda i: (i, 0))
    heads = pl.BlockSpec((HEADS, tm, LANES), lambda i: (0, i, 0))
    in_arrays = [
        (x2, row(D_MODEL)), (mod_l, _full_spec(mod_l.shape)), (p["g1"], None),
        (p["w_in"], pl.BlockSpec(memory_space=pl.ANY)),
        (p["cq_g"], None), (p["w_uq"], None), (p["ckv_g"], None), (p["w_uk"], None), (p["w_uvt"], None),
        (p["q_g"], None), (p["k_g"], None),
        (consts["cs"], row(MLA_ROPE)),
        (p["fq_g"], None), (p["fk_g"], None), (p["f_b"], None), (p["mq_g"], None), (p["mk_g"], None),
        (p["dq_g"], None), (p["dk_g"], None),
        (consts["bd"], None), (consts["tri"], None), (consts["eq"], None), (consts["ek"], None),
        (consts["sel"], None),
    ]
    args = [a for a, _ in in_arrays]
    specs = [s if s is not None else _full_spec(a.shape) for a, s in in_arrays]
    hshape = jax.ShapeDtypeStruct((HEADS, seq, LANES), BF16)
    vshape = jax.ShapeDtypeStruct((GROUP, seq), BF16)
    dshape = jax.ShapeDtypeStruct((seq, GROUP), F32)
    vt = pl.BlockSpec((GROUP, tm), lambda i: (0, i))
    return pl.pallas_call(
        functools.partial(_prep_kernel, layer=p["layer"]),
        grid=(seq // tm,),
        in_specs=specs,
        out_specs=[heads, heads, vt] * 3 + [row(GROUP)] * 3 + [row(LANES)],
        out_shape=[hshape, hshape, vshape] * 3 + [dshape] * 3
                  + [jax.ShapeDtypeStruct((seq, LANES), F32)],
        scratch_shapes=[pltpu.VMEM((1, LANES), F32), pltpu.VMEM((HEADS * LANES, GROUP), F32),
                        pltpu.VMEM(p["w_in"].shape[1:], F32),
                        pltpu.VMEM((D_MODEL, COLS_IN), BF16),
                        pltpu.VMEM((len(IN_VALUE_COLUMNS), GROUP, D_MODEL), BF16),
                        pltpu.VMEM((3, tm, LANES), F32),
                        pltpu.SemaphoreType.DMA(())],
        compiler_params=_params(("arbitrary",)),
        name="prep",
    )(*args)


def _flash_kernel(jlo_ref, q_ref, k_ref, vt_ref, kbias_ref, o_ref, m_sc, acc_sc, sa_sc, sb_sc,
                  *, tile, use_kbias):
    i = pl.program_id(0)
    m_sc[...] = jnp.full_like(m_sc, -jnp.inf)
    acc_sc[...] = jnp.zeros_like(acc_sc)

    def score(hh, j, buf):
        start = pl.multiple_of(j * tile, tile)
        buf[hh] = _dot_nt(k_ref[hh, pl.ds(start, tile), :], q_ref[hh])

    def absorb(hh, j, buf, causal):
        start = pl.multiple_of(j * tile, tile)
        s = buf[hh]
        if use_kbias:
            s = s + jnp.concatenate([kbias_ref[hh]] * (tile // LANES), axis=1)
        if causal:
            key = lax.broadcasted_iota(jnp.int32, (tile, tile), 0)
            qry = lax.broadcasted_iota(jnp.int32, (tile, tile), 1)
            s = jnp.where(key <= qry, s, NEG)
        m_prev = m_sc[hh]
        m_cur = jnp.max(jnp.max(s.reshape(8, tile // 8, tile), axis=0), axis=0, keepdims=True)
        m_new = jnp.maximum(m_prev, m_cur)
        alpha = jnp.exp(m_prev - m_new)
        p = jnp.exp((s - m_new).astype(BF16))
        vt = vt_ref[hh * HEAD_DIM:(hh + 1) * HEAD_DIM, pl.ds(start, tile)]
        vt = jnp.concatenate([vt, jnp.ones((ONES_ROWS, tile), BF16)], axis=0)
        acc_sc[hh] = alpha * acc_sc[hh] + _dot(vt, p)
        m_sc[hh] = m_new

    first = jlo_ref[i]
    n_off = i - first
    for hh in range(HEADS):
        score(hh, first, sa_sc)

    def body(t, carry):
        j = first + 2 * t
        for hh in range(HEADS):
            score(hh, j + 1, sb_sc)
            absorb(hh, j, sa_sc, False)
        for hh in range(HEADS):
            score(hh, j + 2, sa_sc)
            absorb(hh, j + 1, sb_sc, False)
        return carry

    lax.fori_loop(0, n_off // 2, body, 0)

    @pl.when(n_off % 2 == 1)
    def _():
        for hh in range(HEADS):
            score(hh, i, sb_sc)
            absorb(hh, i - 1, sa_sc, False)
        for hh in range(HEADS):
            absorb(hh, i, sb_sc, True)

    @pl.when(n_off % 2 == 0)
    def _():
        for hh in range(HEADS):
            absorb(hh, i, sa_sc, True)

    o_t = jnp.concatenate([acc_sc[hh, 0:HEAD_DIM, :] / acc_sc[hh, HEAD_DIM:HEAD_DIM + 1, :]
                           for hh in range(HEADS)], axis=0)
    o_ref[...] = o_t.T.astype(o_ref.dtype)


def _flash(q, k, v_t, kbias, first_tile, use_kbias):
    seq = v_t.shape[1]
    tile = FLASH_TILE
    kern = functools.partial(_flash_kernel, tile=tile, use_kbias=use_kbias)
    resident = pl.Buffered(1)
    return pl.pallas_call(
        kern,
        grid_spec=pltpu.PrefetchScalarGridSpec(
            num_scalar_prefetch=1,
            grid=(seq // tile,),
            in_specs=[
                pl.BlockSpec((HEADS, tile, LANES), lambda i, f: (0, i, 0)),
                pl.BlockSpec((HEADS, seq, LANES), lambda i, f: (0, 0, 0), pipeline_mode=resident),
                pl.BlockSpec((GROUP, seq), lambda i, f: (0, 0), pipeline_mode=resident),
                pl.BlockSpec((HEADS, tile, LANES), lambda i, f: (0, 0, 0)),
            ],
            out_specs=pl.BlockSpec((tile, GROUP), lambda i, f: (i, 0)),
            scratch_shapes=[pltpu.VMEM((HEADS, 1, tile), F32),
                            pltpu.VMEM((HEADS, HEAD_DIM + ONES_ROWS, tile), F32),
                            pltpu.VMEM((HEADS, tile, tile), F32), pltpu.VMEM((HEADS, tile, tile), F32)],
        ),
        out_shape=jax.ShapeDtypeStruct((seq, GROUP), BF16),
        compiler_params=_params(("arbitrary",)),
        name="flash_kbias" if use_kbias else "flash",
    )(first_tile, q, k, v_t, kbias)


def _fox_first_tile(decay, qk_bound):
    seq = decay.shape[0]
    nq = seq // FLASH_TILE
    f = decay[:, :HEADS]
    f_first = f[0::FLASH_TILE]
    f_last = f[FLASH_TILE - 1::FLASH_TILE]
    gap = f_first[:, None, :] - f_last[None, :, :] + 2.0 * qk_bound
    jj = jnp.arange(nq, dtype=jnp.int32)
    needed = (gap >= FOX_SKIP_LOG) | (jj[None, :, None] >= jj[:, None, None])
    first = jnp.min(jnp.where(needed, jj[None, :, None], nq), axis=1)
    return jnp.min(first, axis=1).astype(jnp.int32)


def _dilated_kernel(q_ref, k_ref, v_ref, o_ref, kbuf, vbuf, acc_s, m_s, l_s):
    pair = pl.program_id(0)
    i = pl.program_id(1)
    T = q_ref.shape[0]

    @pl.when(i == 0)
    def _():
        kbuf[...] = jnp.zeros_like(kbuf)
        vbuf[...] = jnp.zeros_like(vbuf)

    kbuf[0:T, :] = kbuf[T:2 * T, :]
    vbuf[0:T, :] = vbuf[T:2 * T, :]
    kbuf[T:2 * T, :] = k_ref[...]
    vbuf[T:2 * T, :] = v_ref[...]

    ii = lax.broadcasted_iota(jnp.int32, (DIL_SPAN, 2 * DIL_SPAN), 0)
    jj = lax.broadcasted_iota(jnp.int32, (DIL_SPAN, 2 * DIL_SPAN), 1)
    dist = ii + DIL_SPAN - jj
    band = (dist >= 0) & (dist <= DIL_SPAN)
    dist_f = dist.astype(F32)
    upper = lax.broadcasted_iota(jnp.int32, (1, LANES), 1) >= HEAD_DIM

    for pi, (window, r) in enumerate(DIL_PATTERNS):
        assert window // r == DIL_SPAN
        sub = DIL_SPAN * r

        def body(idx, carry, r=r, sub=sub, pi=pi):
            n = idx // r
            rho = idx - n * r
            base = n * sub + rho
            q = q_ref[pl.ds(base, DIL_SPAN, stride=r), :]
            kc = kbuf[pl.ds(T + base - sub, 2 * DIL_SPAN, stride=r), :].astype(BF16)
            vc = vbuf[pl.ds(T + base - sub, 2 * DIL_SPAN, stride=r), :].astype(BF16)
            first_key = jnp.where((i == 0) & (n == 0), DIL_SPAN, 0)
            valid = band & (jj >= first_key)
            stats = []
            for hh in range(2):
                slope = jnp.where(pair == 0, 2.0 ** (-(2 * hh + 1)), 2.0 ** (-(2 * hh + 5)))
                qh = (jnp.where(upper, q, 0.0) if hh else jnp.where(upper, 0.0, q)).astype(BF16)
                s = _dot_nt(qh, kc) - (slope * r) * dist_f
                s = jnp.where(valid, s, NEG)
                m = jnp.max(s, axis=-1, keepdims=True)
                p = jnp.exp(s - m)
                l = jnp.sum(p, axis=-1, keepdims=True)
                stats.append((_dot(p.astype(BF16), vc), m, l))
            rows = pl.ds(pi * T + base, DIL_SPAN, stride=r)
            acc_s[rows, :] = jnp.where(upper, stats[1][0], stats[0][0])
            m_s[rows, :] = jnp.where(upper, stats[1][1], stats[0][1])
            l_s[rows, :] = jnp.where(upper, stats[1][2], stats[0][2])
            return carry

        lax.fori_loop(0, T // DIL_SPAN, body, 0, unroll=True)

    npat = len(DIL_PATTERNS)
    ms = [m_s[pi * T:(pi + 1) * T, :] for pi in range(npat)]
    m_top = functools.reduce(jnp.maximum, ms)
    num = jnp.zeros((T, LANES), F32)
    den = jnp.zeros((T, LANES), F32)
    for pi in range(npat):
        w = jnp.exp(ms[pi] - m_top)
        num = num + w * acc_s[pi * T:(pi + 1) * T, :]
        den = den + w * l_s[pi * T:(pi + 1) * T, :]
    o_ref[...] = (num / den).astype(o_ref.dtype)


def _dilated(q, k, v):
    seq = q.shape[0]
    T = DIL_TILE
    spec = pl.BlockSpec((T, LANES), lambda p, i: (i, p))
    npat = len(DIL_PATTERNS)
    return pl.pallas_call(
        _dilated_kernel,
        grid=(HEADS // 2, seq // T),
        in_specs=[spec, spec, spec],
        out_specs=spec,
        out_shape=jax.ShapeDtypeStruct((seq, GROUP), BF16),
        scratch_shapes=[pltpu.VMEM((2 * T, LANES), F32), pltpu.VMEM((2 * T, LANES), F32),
                        pltpu.VMEM((npat * T, LANES), F32), pltpu.VMEM((npat * T, LANES), F32),
                        pltpu.VMEM((npat * T, LANES), F32)],
        compiler_params=_params(("arbitrary", "arbitrary")),
        name="dilated",
    )(q, k, v)


def _post_kernel(x_ref, oa_ref, ob_ref, oc_ref, od_ref, wout_ref, mod_ref, g2_ref,
                 rw_ref, rb_ref, tri_ref,
                 x1_ref, h2_ref, eidx_ref, gate_ref, rank_ref, cnt_ref, carry):
    i = pl.program_id(0)
    tm = x_ref.shape[0]

    @pl.when(i == 0)
    def _():
        carry[...] = jnp.zeros_like(carry)

    o = (_dot(oa_ref[...], wout_ref[0]) + _dot(ob_ref[...], wout_ref[1])
         + _dot(oc_ref[...], wout_ref[2]) + _dot(od_ref[...], wout_ref[3]))
    x1 = x_ref[...] + mod_ref[2] * o
    x1_ref[...] = x1
    y = x1 * lax.rsqrt(jnp.mean(x1 * x1, axis=-1, keepdims=True) + EPS) * g2_ref[...]
    h2 = y * (1.0 + mod_ref[4]) + mod_ref[3]
    _store_token_tiles(h2_ref, h2)

    h_hi, h_lo = _split2(h2)
    hh = _dot(h_hi, rw_ref[...])
    logits = (hh[:, :LANES] + hh[:, LANES:] + _dot(h_lo, rw_ref[:, :LANES])
              + rb_ref[...])
    lane = lax.broadcasted_iota(jnp.int32, (tm, LANES), 1)
    lane_f = lane.astype(F32)
    g = logits
    chosen = jnp.zeros((tm, LANES), F32)
    vals, idxs = [], []
    for _ in range(TOP_K):
        m = jnp.max(g, axis=-1, keepdims=True)
        first = jnp.min(jnp.where(g == m, lane_f, 1e9), axis=-1, keepdims=True)
        pick = lane_f == first
        chosen = jnp.where(pick, 1.0, chosen)
        g = jnp.where(pick, -jnp.inf, g)
        vals.append(m)
        idxs.append(first)
    exps = [jnp.exp(v - vals[0]) for v in vals]
    den = exps[0] + exps[1] + exps[2] + exps[3]
    before = _dot(tri_ref[...], chosen.astype(BF16)) + carry[...]
    carry[...] = carry[...] + jnp.sum(chosen, axis=0, keepdims=True)
    cnt_ref[...] = carry[...]
    e_out = jnp.zeros((tm, LANES), F32)
    g_out = jnp.zeros((tm, LANES), F32)
    r_out = jnp.zeros((tm, LANES), F32)
    for k in range(TOP_K):
        rank_k = jnp.sum(jnp.where(lane_f == idxs[k], before, 0.0), axis=-1, keepdims=True)
        e_out = jnp.where(lane == k, idxs[k], e_out)
        g_out = jnp.where(lane == k, exps[k] / den, g_out)
        r_out = jnp.where(lane == k, rank_k, r_out)
    eidx_ref[...] = e_out.astype(jnp.int32)
    gate_ref[...] = g_out
    rank_ref[...] = r_out.astype(jnp.int32)


def _post(x2, oa, ob, oc, od, mod_l, consts, p):
    seq = x2.shape[0]
    tm = TOKEN_TILE
    row = lambda n: pl.BlockSpec((tm, n), lambda i: (i, 0))
    full = [p["w_out"], mod_l, p["g2"], p["rw"], p["r_b"], consts["tri_strict"]]
    f32 = lambda n: jax.ShapeDtypeStruct((seq, n), F32)
    i32 = lambda n: jax.ShapeDtypeStruct((seq, n), jnp.int32)
    return pl.pallas_call(
        _post_kernel,
        grid=(seq // tm,),
        in_specs=[row(D_MODEL)] + [row(GROUP)] * 4 + [_full_spec(a.shape) for a in full],
        out_specs=[row(D_MODEL), pl.BlockSpec((tm * CHUNKS, LANES), lambda i: (i, 0)),
                   row(LANES), row(LANES), row(LANES), _full_spec((1, LANES))],
        out_shape=[f32(D_MODEL), jax.ShapeDtypeStruct((seq * CHUNKS, LANES), F32),
                   i32(LANES), f32(LANES), i32(LANES), jax.ShapeDtypeStruct((1, LANES), F32)],
        scratch_shapes=[pltpu.VMEM((1, LANES), F32)],
        compiler_params=_params(("arbitrary",)),
        name="post",
    )(x2, oa, ob, oc, od, *full)


def _tile_copy(src, s, dst, d, sem):
    return pltpu.make_async_copy(src.at[pl.ds(pl.multiple_of(s * CHUNKS, CHUNKS), CHUNKS), :],
                                 dst.at[pl.ds(pl.multiple_of(d * CHUNKS, CHUNKS), CHUNKS), :], sem)


def _dispatch_kernel(dest_ref, padlo_ref, padn_ref, nu_ref, h_ref, xs_ref, zbuf, sem, zsem):
    i = pl.program_id(0)
    n = ROUTE_TILE * TOP_K
    base = i * n
    block_rows = EXPERT_ROWS * CHUNKS

    @pl.when(i == 0)
    def _():
        zbuf[...] = jnp.zeros_like(zbuf)

        def fill(wait):
            def go(copy):
                copy.wait() if wait else copy.start()

            def per_expert(e, carry):
                lo, cnt = padlo_ref[e], padn_ref[e]
                off = lo
                p = EXPERT_ROWS // 2
                while p >= 1:
                    rows = p * CHUNKS

                    @pl.when((cnt & p) != 0)
                    def _(off=off, rows=rows):
                        go(pltpu.make_async_copy(
                            zbuf.at[pl.ds(0, rows), :],
                            xs_ref.at[pl.ds(pl.multiple_of(off * CHUNKS, CHUNKS), rows), :], zsem))

                    off = off + (cnt & p)
                    p //= 2
                return carry

            lax.fori_loop(0, N_EXPERTS, per_expert, 0)

            def per_block(b, carry):
                go(pltpu.make_async_copy(
                    zbuf, xs_ref.at[pl.ds(pl.multiple_of(b * block_rows, block_rows), block_rows), :],
                    zsem))
                return carry

            lax.fori_loop(nu_ref[0], xs_ref.shape[0] // block_rows, per_block, 0)

        fill(False)
        fill(True)

    def issue(r, carry):
        for k in range(TOP_K):
            _tile_copy(h_ref, r, xs_ref, dest_ref[base + r * TOP_K + k], sem).start(priority=k % 2)
        return carry

    lax.fori_loop(0, ROUTE_TILE, issue, 0, unroll=2)
    rows = pl.ds(0, n * CHUNKS)
    pltpu.make_async_copy(xs_ref.at[rows, :], xs_ref.at[rows, :], sem).wait()


def _dispatch(dest, pad_lo, pad_n, n_used, h2_tiles, m_pad):
    seq = h2_tiles.shape[0] // CHUNKS
    return pl.pallas_call(
        _dispatch_kernel,
        grid_spec=pltpu.PrefetchScalarGridSpec(
            num_scalar_prefetch=4,
            grid=(seq // ROUTE_TILE,),
            in_specs=[pl.BlockSpec((ROUTE_TILE * CHUNKS, LANES), lambda i, *_: (i, 0))],
            out_specs=pl.BlockSpec(memory_space=pl.ANY),
            scratch_shapes=[pltpu.VMEM((EXPERT_ROWS * CHUNKS, LANES), F32),
                            pltpu.SemaphoreType.DMA(()), pltpu.SemaphoreType.DMA(())],
        ),
        out_shape=jax.ShapeDtypeStruct((m_pad * CHUNKS, LANES), F32),
        compiler_params=_params(("arbitrary",)),
        name="dispatch",
    )(dest, pad_lo, pad_n, n_used, h2_tiles)


def _expert_kernel(be_ref, nu_ref, nxt_ref, ord_ref, xs_ref, w1_hbm, b1_ref, w2_hbm, b2_ref, ys_ref,
                   w1f, w2f, w1b, w2b, sems):
    b = pl.program_id(0)
    e = be_ref[b]
    prev = be_ref[jnp.maximum(b - 1, 0)]
    fresh = ((b == 0) | (e != prev)) & (b < nu_ref[0])
    slot = ord_ref[b] % 2

    def fetch(expert, to_slot):
        return (pltpu.make_async_copy(w1_hbm.at[expert], w1f.at[to_slot], sems.at[0, to_slot]),
                pltpu.make_async_copy(w2_hbm.at[expert], w2f.at[to_slot], sems.at[1, to_slot]))

    @pl.when(b == 0)
    def _():
        for copy in fetch(e, slot):
            copy.start()

    @pl.when(fresh)
    def _():
        for copy in fetch(e, slot):
            copy.wait()
        w1b[...] = w1f[slot].astype(BF16)
        w2b[...] = w2f[slot].astype(BF16)

        @pl.when(nxt_ref[b] >= 0)
        def _():
            for copy in fetch(nxt_ref[b], 1 - slot):
                copy.start()

    @pl.when(b < nu_ref[0])
    def _():
        half = EXPERT_ROWS // 2
        gus = []
        for r in range(2):
            xb = jnp.concatenate(
                [_load_token_chunk(xs_ref, half, c, offset=r * half * CHUNKS) for c in range(CHUNKS)],
                axis=1).astype(BF16)
            gus.append(_dot(xb, w1b[...]) + b1_ref[0])
        for r in range(2):
            g = jnp.minimum(gus[r][:, :D_EXPERT], SWIGLU_LIMIT)
            u = jnp.clip(gus[r][:, D_EXPERT:], -SWIGLU_LIMIT, SWIGLU_LIMIT)
            y = (u + 1.0) * g * (1.0 / (1.0 + jnp.exp(-SWIGLU_ALPHA * g)))
            _store_token_tiles(ys_ref, _dot(y.astype(BF16), w2b[...]) + b2_ref[0],
                               offset=r * half * CHUNKS)

    @pl.when(b >= nu_ref[0])
    def _():
        ys_ref[...] = jnp.zeros_like(ys_ref)


def _experts(blk_expert, n_used, blk_next, blk_ord, xs, w1, b1, w2, b2):
    m_pad = xs.shape[0] // CHUNKS
    bm = EXPERT_ROWS
    n_all = w1.shape[0] * w1.shape[1]
    rows = lambda b, be, nu, *_: (jnp.minimum(b, nu[0] - 1), 0)
    ex = lambda b, be, nu, *_: (be[jnp.minimum(b, nu[0] - 1)], 0, 0)
    return pl.pallas_call(
        _expert_kernel,
        grid_spec=pltpu.PrefetchScalarGridSpec(
            num_scalar_prefetch=4,
            grid=(m_pad // bm,),
            in_specs=[
                pl.BlockSpec((bm * CHUNKS, LANES), rows),
                pl.BlockSpec(memory_space=pl.ANY),
                pl.BlockSpec((1, 1, 2 * D_EXPERT), ex),
                pl.BlockSpec(memory_space=pl.ANY),
                pl.BlockSpec((1, 1, D_MODEL), ex),
            ],
            out_specs=pl.BlockSpec((bm * CHUNKS, LANES), lambda b, *_: (b, 0)),
            scratch_shapes=[pltpu.VMEM((2, D_MODEL, 2 * D_EXPERT), F32),
                            pltpu.VMEM((2, D_EXPERT, D_MODEL), F32),
                            pltpu.VMEM((D_MODEL, 2 * D_EXPERT), BF16),
                            pltpu.VMEM((D_EXPERT, D_MODEL), BF16),
                            pltpu.SemaphoreType.DMA((2, 2))],
        ),
        out_shape=jax.ShapeDtypeStruct((m_pad * CHUNKS, LANES), F32),
        compiler_params=_params(("arbitrary",)),
        name="experts",
    )(blk_expert, n_used, blk_next, blk_ord, xs,
      w1.reshape(n_all, D_MODEL, 2 * D_EXPERT), b1.reshape(n_all, 1, -1),
      w2.reshape(n_all, D_EXPERT, D_MODEL), b2.reshape(n_all, 1, -1))


def _combine_kernel(dest_ref, ys_ref, x1_ref, gate_ref, mod_ref, o_ref, buf, sems):
    i = pl.program_id(0)
    tm = x1_ref.shape[0]
    n = tm * TOP_K
    slot = i % 2

    def gather(step, to_slot):
        base = step * n

        def issue(r, carry):
            for k in range(TOP_K):
                _tile_copy(ys_ref, dest_ref[base + r * TOP_K + k], buf, to_slot * n + k * tm + r,
                           sems.at[to_slot]).start(priority=k % 2)
            return carry

        lax.fori_loop(0, tm, issue, 0, unroll=2)

    @pl.when(i == 0)
    def _():
        gather(0, 0)

    @pl.when(i + 1 < pl.num_programs(0))
    def _():
        gather(i + 1, 1 - slot)

    mine = pl.ds(pl.multiple_of(slot * n * CHUNKS, n * CHUNKS), n * CHUNKS)
    pltpu.make_async_copy(ys_ref.at[pl.ds(0, n * CHUNKS), :], buf.at[mine, :], sems.at[slot]).wait()
    gates = gate_ref[...]
    g2 = mod_ref[5]
    for c in range(CHUNKS):
        cols = slice(c * LANES, (c + 1) * LANES)
        mix = jnp.zeros((tm, LANES), F32)
        for k in range(TOP_K):
            mix = mix + gates[:, k:k + 1] * _load_token_chunk(
                buf, tm, c, offset=(slot * n + k * tm) * CHUNKS)
        o_ref[:, cols] = x1_ref[:, cols] + g2[:, cols] * mix


def _combine(dest, ys, x1, gates, mod_l):
    seq = x1.shape[0]
    tm = ROUTE_TILE
    return pl.pallas_call(
        _combine_kernel,
        grid_spec=pltpu.PrefetchScalarGridSpec(
            num_scalar_prefetch=1,
            grid=(seq // tm,),
            in_specs=[
                pl.BlockSpec(memory_space=pl.ANY),
                pl.BlockSpec((tm, D_MODEL), lambda i, d: (i, 0)),
                pl.BlockSpec((tm, LANES), lambda i, d: (i, 0)),
                pl.BlockSpec(mod_l.shape, lambda i, d: (0, 0, 0)),
            ],
            out_specs=pl.BlockSpec((tm, D_MODEL), lambda i, d: (i, 0)),
            scratch_shapes=[pltpu.VMEM((2 * TOP_K * tm * CHUNKS, LANES), F32),
                            pltpu.SemaphoreType.DMA((2,))],
        ),
        out_shape=jax.ShapeDtypeStruct((seq, D_MODEL), F32),
        compiler_params=_params(("arbitrary",)),
        name="combine",
    )(dest, ys, x1, gates, mod_l)


def _moe(l, x1, h2_tiles, eidx, gates, rank, counts, mod_l, w1, b1, w2, b2):
    seq = x1.shape[0]
    bm = EXPERT_ROWS
    m_pad = seq * TOP_K + N_EXPERTS * bm
    cnt = counts[0, :N_EXPERTS].astype(jnp.int32)
    padded = (cnt + bm - 1) // bm * bm
    pad_end = jnp.cumsum(padded)
    pad_start = pad_end - padded
    onehot = eidx[:, :TOP_K, None] == jnp.arange(N_EXPERTS, dtype=jnp.int32)
    start_of = jnp.sum(jnp.where(onehot, pad_start, 0), axis=-1)
    dest = (start_of + rank[:, :TOP_K]).reshape(seq * TOP_K).astype(jnp.int32)
    nblk = m_pad // bm
    blk_start = jnp.arange(nblk, dtype=jnp.int32) * bm
    local = jnp.minimum(jnp.sum(pad_end[None, :] <= blk_start[:, None], axis=1), N_EXPERTS - 1)
    blk_expert = (local + l * N_EXPERTS).astype(jnp.int32)
    n_used = (pad_end[-1:] // bm).astype(jnp.int32)
    ids = jnp.arange(N_EXPERTS, dtype=jnp.int32)
    live = padded > 0
    ordinal = jnp.cumsum(live.astype(jnp.int32)) - 1
    later = live[None, :] & (ids[None, :] > ids[:, None])
    nxt = jnp.min(jnp.where(later, ids[None, :], N_EXPERTS), axis=1)
    nxt = jnp.where(nxt < N_EXPERTS, nxt + l * N_EXPERTS, -1)
    of_block = local[:, None] == ids[None, :]
    blk_next = jnp.sum(jnp.where(of_block, nxt[None, :], 0), axis=1).astype(jnp.int32)
    blk_ord = jnp.sum(jnp.where(of_block, ordinal[None, :], 0), axis=1).astype(jnp.int32)
    xs = _dispatch(dest, (pad_start + cnt).astype(jnp.int32), (padded - cnt).astype(jnp.int32),
                   n_used, h2_tiles, m_pad)
    ys = _experts(blk_expert, n_used, blk_next, blk_ord, xs, w1, b1, w2, b2)
    return _combine(dest, ys, x1, gates, mod_l)


def _pad_cols(a, n):
    return jnp.pad(a, ((0, 0), (0, n - a.shape[1])))


def _layer_params(l, w_in, mla_cq_g, mla_w_uq, mla_ckv_g, mla_w_ukv, mla_q_g, mla_k_g,
                  fox_q_g, fox_k_g, fox_b_f, moba_q_g, moba_k_g, dil_q_g, dil_k_g, w_out,
                  norm1_g, norm2_g, router_w, router_b):
    w_uq = jnp.pad(mla_w_uq[l].reshape(MLA_Q_RANK, HEADS, MLA_QK),
                   ((0, 0), (0, 0), (0, LANES - MLA_QK))).reshape(MLA_Q_RANK, HEADS * LANES)
    w_ukv = mla_w_ukv[l].reshape(MLA_KV_RANK, HEADS, MLA_NOPE + HEAD_DIM)
    w_uk = jnp.pad(w_ukv[:, :, :MLA_NOPE], ((0, 0), (0, 0), (0, LANES - MLA_NOPE)))
    w_uv = w_ukv[:, :, MLA_NOPE:]
    tile4 = lambda g: jnp.tile(g, HEADS)[None, :]
    rw = _pad_cols(router_w[l], LANES)
    rw_hi = rw.astype(BF16)
    rw_lo = (rw - rw_hi.astype(F32)).astype(BF16)
    r_b = jnp.concatenate([router_b[l], jnp.full((LANES - N_EXPERTS,), NEG, F32)])[None, :]
    return dict(
        g1=norm1_g[l][None, :], g2=norm2_g[l][None, :], w_in=w_in, layer=l,
        cq_g=mla_cq_g[l][None, :], w_uq=w_uq.astype(BF16), ckv_g=mla_ckv_g[l][None, :],
        w_uk=w_uk.reshape(MLA_KV_RANK, HEADS * LANES).astype(BF16),
        w_uvt=w_uv.reshape(MLA_KV_RANK, GROUP).T.astype(BF16),
        fox_bound=FOX_NORM_SLACK * HEAD_DIM ** 0.5 * jnp.max(jnp.abs(fox_q_g[l]))
        * jnp.max(jnp.abs(fox_k_g[l])),
        q_g=_pad_cols(mla_q_g[l][None, :], LANES), k_g=_pad_cols(mla_k_g[l][None, :], LANES),
        fq_g=tile4(fox_q_g[l]), fk_g=tile4(fox_k_g[l]), f_b=_pad_cols(fox_b_f[l][None, :], LANES),
        mq_g=tile4(moba_q_g[l]), mk_g=tile4(moba_k_g[l]), dq_g=tile4(dil_q_g[l]), dk_g=tile4(dil_k_g[l]),
        w_out=w_out[l].reshape(HEADS, GROUP, D_MODEL).astype(BF16),
        rw=jnp.concatenate([rw_hi, rw_lo], axis=1), r_b=r_b,
    )


def kernel(x, c, w_mod, b_mod, norm1_g, norm2_g, w_in, mla_cq_g, mla_w_uq, mla_ckv_g, mla_w_ukv, mla_q_g, mla_k_g, fox_q_g, fox_k_g, fox_b_f, moba_q_g, moba_k_g, dil_q_g, dil_k_g, w_out, router_w, router_b, exp_w1, exp_b1, exp_w2, exp_b2):
    batch, seq, d = x.shape
    assert batch == 1 and d == D_MODEL
    assert seq % DIL_TILE == 0 and seq // MOBA_BLOCK <= MOBA_MAX_BLOCKS
    depth = w_mod.shape[0]
    consts = _prep_constants(seq, TOKEN_TILE)
    mod = _modulation(c, w_mod, b_mod)
    slopes_c = 2.0 ** (-(2.0 * np.arange(HEADS) + 2.0))
    in_block = np.arange(FLASH_TILE) % MOBA_BLOCK
    kbias_c = jnp.asarray(np.broadcast_to(slopes_c[:, None, None] * in_block[None, :, None],
                                          (HEADS, FLASH_TILE, LANES)), F32)
    kbias_0 = jnp.zeros((HEADS, FLASH_TILE, LANES), F32)
    all_tiles = jnp.zeros((seq // FLASH_TILE,), jnp.int32)
    x2 = x.reshape(seq, d)
    for l in range(depth):
        p = _layer_params(l, w_in, mla_cq_g, mla_w_uq, mla_ckv_g, mla_w_ukv, mla_q_g, mla_k_g,
                          fox_q_g, fox_k_g, fox_b_f, moba_q_g, moba_k_g, dil_q_g, dil_k_g, w_out,
                          norm1_g, norm2_g, router_w, router_b)
        mod_l = mod[l]
        qa, ka, va, qf, kf, vf, qm, km, vm, qd, kd, vd, decay = _prep(x2, mod_l, consts, p)
        oa = _flash(qa, ka, va, kbias_0, all_tiles, False)
        ob = _flash(qf, kf, vf, kbias_0, _fox_first_tile(decay, p["fox_bound"]), False)
        oc = _flash(qm, km, vm, kbias_c, all_tiles, True)
        od = _dilated(qd, kd, vd)
        x1, h2, eidx, gates, rank, counts = _post(x2, oa, ob, oc, od, mod_l, consts, p)
        x2 = _moe(l, x1, h2, eidx, gates, rank, counts, mod_l, exp_w1, exp_b1, exp_w2, exp_b2)
    return x2.reshape(batch, seq, d)
```

```python
import functools

import numpy as np
import jax
import jax.numpy as jnp
from jax import lax
from jax.experimental import pallas as pl
from jax.experimental.pallas import tpu as pltpu

F32 = jnp.float32
BF16 = jnp.bfloat16

D_MODEL = 1024
HEAD_DIM = 64
HEADS = 4
GROUP = HEADS * HEAD_DIM
LANES = 128
CHUNKS = D_MODEL // LANES
MLA_Q_RANK = 256
MLA_KV_RANK = 128
MLA_NOPE = 64
MLA_ROPE = 32
MLA_QK = MLA_NOPE + MLA_ROPE
ROPE_THETA = 10000.0
MOBA_BLOCK = 256
MOBA_TOPK = 3
MOBA_MAX_BLOCKS = 64
DIL_PATTERNS = ((128, 1), (512, 4), (2048, 16))
DIL_SPAN = 128
DIL_TILE = 2048
N_EXPERTS = 32
TOP_K = 4
D_EXPERT = 1024
SWIGLU_LIMIT = 7.0
SWIGLU_ALPHA = 1.702
EPS = 1e-6
NEG = -1e30

FLASH_TILE = 512
ONES_ROWS = 16
FOX_SKIP_LOG = -106.0
FOX_NORM_SLACK = 1.02
TOKEN_TILE = 256
POST_TILE = 512
ROUTE_TILE = 512
EXPERT_ROWS = 512
VMEM_LIMIT = 56 * 1024 * 1024

COLS_MLA = 512
COLS_FOX = 640
COLS_MOBA = 512
COLS_DIL = 768
COLS_IN = COLS_MLA + COLS_FOX + COLS_MOBA + COLS_DIL


def _in_column_map():
    sizes = [MLA_Q_RANK, MLA_KV_RANK, MLA_ROPE, GROUP, GROUP, GROUP, HEADS] + [GROUP] * 6
    src = np.concatenate([[0], np.cumsum(sizes)])
    cq, ckv, kr, fq, fk, fv, flog, mq, mk, mv, dq, dk, dv = range(13)
    layout = [(cq, 0), (ckv, MLA_Q_RANK), (kr, MLA_Q_RANK + MLA_KV_RANK + MLA_NOPE),
              (fq, COLS_MLA), (fk, COLS_MLA + GROUP), (flog, COLS_MLA + 2 * GROUP),
              (mq, COLS_MLA + COLS_FOX), (mk, COLS_MLA + COLS_FOX + GROUP),
              (dq, COLS_IN - 3 * GROUP), (dk, COLS_IN - 2 * GROUP), (dv, COLS_IN - GROUP)]
    moves = tuple((dst, int(src[piece]), sizes[piece]) for piece, dst in layout)
    return moves, (int(src[fv]), int(src[mv]))


IN_COLUMN_MAP, IN_VALUE_COLUMNS = _in_column_map()


def _dot(a, b):
    return jnp.dot(a, b, preferred_element_type=F32)


def _dot_nt(a, b):
    return lax.dot_general(a, b, (((1,), (1,)), ((), ())), preferred_element_type=F32)


def _split2(x):
    hi = x.astype(BF16)
    lo = (x - hi.astype(F32)).astype(BF16)
    return hi, lo


def _split3(x):
    a = x.astype(BF16)
    r = x - a.astype(F32)
    b = r.astype(BF16)
    c = (r - b.astype(F32)).astype(BF16)
    return a, b, c


def _head_of_lane():
    return jnp.right_shift(lax.broadcasted_iota(jnp.int32, (1, GROUP), 1), 6)


def _store_token_tiles(ref, x, offset=0):
    n = x.shape[0]
    for c in range(CHUNKS):
        ref[pl.ds(offset + c, n, stride=CHUNKS), :] = x[:, c * LANES:(c + 1) * LANES]


def _load_token_chunk(ref, n, c, offset=0):
    return ref[pl.ds(offset + c, n, stride=CHUNKS), :]


def _full_spec(shape):
    nd = len(shape)
    return pl.BlockSpec(shape, lambda *_: (0,) * nd)


def _params(sem):
    return pltpu.CompilerParams(dimension_semantics=sem, vmem_limit_bytes=VMEM_LIMIT)


def _mod_kernel(c_ref, w_ref, b_ref, o_ref):
    c = c_ref[...]
    s = c * (1.0 / (1.0 + jnp.exp(-c)))
    s8 = jnp.broadcast_to(s, (8, D_MODEL))
    r = jnp.dot(s8, w_ref[0], preferred_element_type=F32, precision=lax.Precision.HIGHEST)
    o_ref[0, 0] = r[0:1, :] + b_ref[0, 0]


def _modulation(c, w_mod, b_mod):
    depth = w_mod.shape[0]
    b4 = b_mod.reshape(depth, 6, 1, D_MODEL)
    return pl.pallas_call(
        _mod_kernel,
        grid=(depth, 6),
        in_specs=[
            pl.BlockSpec((1, D_MODEL), lambda l, j: (0, 0)),
            pl.BlockSpec((1, D_MODEL, D_MODEL), lambda l, j: (l, 0, j)),
            pl.BlockSpec((1, 1, 1, D_MODEL), lambda l, j: (l, j, 0, 0)),
        ],
        out_specs=pl.BlockSpec((1, 1, 1, D_MODEL), lambda l, j: (l, j, 0, 0)),
        out_shape=jax.ShapeDtypeStruct((depth, 6, 1, D_MODEL), F32),
        compiler_params=_params(("arbitrary", "arbitrary")),
        name="modulation",
    )(c, w_mod, b4)


def _head_norm(x, g, bd):
    hi, lo = _split2(x * x)
    ss = _dot(hi, bd) + _dot(lo, bd)
    return x * lax.rsqrt(ss * (1.0 / HEAD_DIM) + EPS) * g


def _prep_kernel(x_ref, mod_ref, g1_ref, win_hbm, cqg_ref, wuq_ref, ckvg_ref, wuk_ref,
                 wuvt_ref, qg_ref, kg_ref, cs_ref,
                 fqg_ref, fkg_ref, fb_ref, mqg_ref, mkg_ref, dqg_ref, dkg_ref,
                 bd_ref, tri_ref, eq_ref, ek_ref, sel_ref,
                 qa_ref, ka_ref, va_ref, qf_ref, kf_ref, vf_ref, qm_ref, km_ref, vm_ref,
                 qd_ref, kd_ref, vd_ref, f_ref,
                 fcarry, kmean, wraw_ref, win_ref, wvt_ref, rope_scr, wsem, *, layer):
    i = pl.program_id(0)
    tm = x_ref.shape[0]

    @pl.when(i == 0)
    def _():
        fetch = pltpu.make_async_copy(win_hbm.at[layer], wraw_ref, wsem)
        fetch.start()
        fcarry[...] = jnp.zeros_like(fcarry)
        kmean[...] = jnp.zeros_like(kmean)
        win_ref[...] = jnp.zeros_like(win_ref)
        rope_scr[...] = jnp.zeros_like(rope_scr)
        rope_scr[0, :, 0:MLA_NOPE] = jnp.ones((tm, MLA_NOPE), F32)
        fetch.wait()
        for dst, src, width in IN_COLUMN_MAP:
            win_ref[:, dst:dst + width] = wraw_ref[:, src:src + width].astype(BF16)
        for slot, src in enumerate(IN_VALUE_COLUMNS):
            wvt_ref[slot] = wraw_ref[:, src:src + GROUP].T.astype(BF16)

    x = x_ref[...]
    y = x * lax.rsqrt(jnp.mean(x * x, axis=-1, keepdims=True) + EPS) * g1_ref[...]
    hb = (y * (1.0 + mod_ref[1]) + mod_ref[0]).astype(BF16)
    bd = bd_ref[...]
    lane = lax.broadcasted_iota(jnp.int32, (1, LANES), 1)
    lane_f = lane.astype(F32)
    head_of_lane = _head_of_lane()

    pa = _dot(hb, win_ref[:, 0:COLS_MLA])
    cq = pa[:, 0:MLA_Q_RANK]
    ckv = pa[:, MLA_Q_RANK:MLA_Q_RANK + MLA_KV_RANK]
    kr = pa[:, MLA_Q_RANK + MLA_KV_RANK:COLS_MLA]
    cqn = (cq * lax.rsqrt(jnp.mean(cq * cq, axis=-1, keepdims=True) + EPS) * cqg_ref[...]).astype(BF16)
    ckvn = (ckv * lax.rsqrt(jnp.mean(ckv * ckv, axis=-1, keepdims=True) + EPS) * ckvg_ref[...]).astype(BF16)
    q_all = _dot(cqn, wuq_ref[...])
    k_all = _dot(ckvn, wuk_ref[...])
    va_ref[...] = _dot_nt(wuvt_ref[...], ckvn).astype(BF16)
    half = MLA_ROPE // 2
    cos, sin = cs_ref[:, 0:half], cs_ref[:, half:MLA_ROPE]
    rope_scr[0, :, MLA_NOPE:MLA_NOPE + half] = cos
    rope_scr[0, :, MLA_NOPE + half:MLA_QK] = cos
    rope_scr[1, :, MLA_NOPE:MLA_NOPE + half] = -sin
    rope_scr[2, :, MLA_NOPE + half:MLA_QK] = sin
    rc, rs1, rs2 = rope_scr[0], rope_scr[1], rope_scr[2]

    def rope(t):
        return t * rc + pltpu.roll(t, LANES - MLA_ROPE // 2, 1) * rs1 + pltpu.roll(t, MLA_ROPE // 2, 1) * rs2

    for h in range(HEADS):
        q = q_all[:, h * LANES:(h + 1) * LANES]
        q = q * lax.rsqrt(jnp.sum(q * q, axis=-1, keepdims=True) * (1.0 / MLA_QK) + EPS) * qg_ref[...]
        qa_ref[h] = (rope(q) * (MLA_QK ** -0.5)).astype(BF16)
        k = k_all[:, h * LANES:(h + 1) * LANES] + kr
        k = k * lax.rsqrt(jnp.sum(k * k, axis=-1, keepdims=True) * (1.0 / MLA_QK) + EPS) * kg_ref[...]
        ka_ref[h] = rope(k).astype(BF16)

    pf = _dot(hb, win_ref[:, COLS_MLA:COLS_MLA + COLS_FOX])
    fqn = (_head_norm(pf[:, 0:GROUP], fqg_ref[...], bd) * (HEAD_DIM ** -0.5)).astype(BF16)
    fkn = _head_norm(pf[:, GROUP:2 * GROUP], fkg_ref[...], bd).astype(BF16)
    vf_ref[...] = _dot_nt(wvt_ref[0], hb).astype(BF16)
    z = pf[:, 2 * GROUP:2 * GROUP + LANES] + fb_ref[...]
    log_f = jnp.minimum(z, 0.0) - jnp.log(1.0 + jnp.exp(-jnp.abs(z)))
    tri = tri_ref[...]
    a1, a2, a3 = _split3(log_f)
    cum = fcarry[...] + (_dot(tri, a1) + _dot(tri, a2) + _dot(tri, a3))
    fcarry[...] = cum[tm - 1:tm, :]
    f_ref[...] = cum
    f1, f2, f3 = _split3(cum)
    xq = jnp.concatenate([fqn, f1, f2, f3], axis=1)
    xk = jnp.concatenate([fkn, f1, f2, f3], axis=1)
    ones_q = jnp.where((lane >= HEAD_DIM + 3) & (lane < HEAD_DIM + 6), 1.0, 0.0)
    ones_k = jnp.where((lane >= HEAD_DIM) & (lane < HEAD_DIM + 3), 1.0, 0.0)
    qf_all = _dot(xq, eq_ref[...])
    kf_all = _dot(xk, ek_ref[...])
    for h in range(HEADS):
        qf_ref[h] = (qf_all[:, h * LANES:(h + 1) * LANES] + ones_q).astype(BF16)
        kf_ref[h] = (kf_all[:, h * LANES:(h + 1) * LANES] + ones_k).astype(BF16)

    pm = _dot(hb, win_ref[:, COLS_MLA + COLS_FOX:COLS_MLA + COLS_FOX + COLS_MOBA])
    mqn = _head_norm(pm[:, 0:GROUP], mqg_ref[...], bd) * (HEAD_DIM ** -0.5)
    mkn = _head_norm(pm[:, GROUP:2 * GROUP], mkg_ref[...], bd)
    vm_ref[...] = _dot_nt(wvt_ref[1], hb).astype(BF16)
    col_mean = jnp.mean(mkn, axis=0, keepdims=True)
    mqb = mqn.astype(BF16)
    mkb = mkn.astype(BF16)
    blk = lane - HEAD_DIM
    blk_f = blk.astype(F32)
    past = (blk >= 0) & (blk < i)
    i_f = i.astype(F32)
    for h in range(HEADS):
        kmean[pl.ds(h * LANES + HEAD_DIM + i, 1), :] = jnp.where(head_of_lane == h, col_mean, 0.0)
    q_hi, q_lo = _split2(mqn)
    km_hi, km_lo = _split2(kmean[...])
    gate_all = _dot_nt(q_hi, km_hi) + _dot_nt(q_hi, km_lo) + _dot_nt(q_lo, km_hi)
    qm_all = _dot(mqb, sel_ref[...])
    km_all = _dot(mkb, sel_ref[...])
    for h in range(HEADS):
        g = jnp.where(past, gate_all[:, h * LANES:(h + 1) * LANES], NEG)
        chosen = jnp.zeros((tm, LANES), F32)
        for _ in range(MOBA_TOPK):
            m = jnp.max(g, axis=-1, keepdims=True)
            first = jnp.min(jnp.where(g == m, lane_f, 1e9), axis=-1, keepdims=True)
            pick = (lane_f == first) & (m > NEG)
            chosen = jnp.where(pick, 1.0, chosen)
            g = jnp.where(pick, NEG, g)
        slope = 2.0 ** (-(2 * h + 2))
        keep = (chosen > 0.0) | (blk == i)
        bias = jnp.where(keep, (slope * MOBA_BLOCK) * (blk_f - i_f), NEG)
        bias = jnp.where(blk >= 0, bias, 0.0)
        qm_ref[h] = (qm_all[:, h * LANES:(h + 1) * LANES] + bias).astype(BF16)
        onehot = jnp.where(blk == i, 1.0, 0.0)
        km_ref[h] = (km_all[:, h * LANES:(h + 1) * LANES] + onehot).astype(BF16)

    pd = _dot(hb, win_ref[:, COLS_MLA + COLS_FOX + COLS_MOBA:COLS_IN])
    qd_ref[...] = _head_norm(pd[:, 0:GROUP], dqg_ref[...], bd) * (HEAD_DIM ** -0.5)
    kd_ref[...] = _head_norm(pd[:, GROUP:2 * GROUP], dkg_ref[...], bd)
    vd_ref[...] = pd[:, 2 * GROUP:3 * GROUP]


def _prep_constants(seq, tm):
    half = MLA_ROPE // 2
    inv = 1.0 / (ROPE_THETA ** (jnp.arange(half, dtype=F32) / half))
    ang = jnp.arange(seq, dtype=F32)[:, None] * inv[None, :]
    cs = jnp.concatenate([jnp.cos(ang), jnp.sin(ang)], axis=1)
    bd =np.kron(np.eye(HEADS, dtype=np.float32), np.ones((HEAD_DIM, HEAD_DIM), np.float32))
    tri = np.tril(np.ones((tm, tm), np.float32))
    tri_strict = np.tril(np.ones((POST_TILE, POST_TILE), np.float32), -1)
    sel = np.zeros((GROUP, HEADS * LANES), np.float32)
    eq = np.zeros((GROUP + 3 * LANES, HEADS * LANES), np.float32)
    ek = np.zeros((GROUP + 3 * LANES, HEADS * LANES), np.float32)
    for h in range(HEADS):
        for d in range(HEAD_DIM):
            sel[h * HEAD_DIM + d, h * LANES + d] = 1.0
        for piece in range(3):
            eq[GROUP + piece * LANES + h, h * LANES + HEAD_DIM + piece] = 1.0
            ek[GROUP + piece * LANES + h, h * LANES + HEAD_DIM + 3 + piece] = -1.0
    eq[:GROUP] = sel
    ek[:GROUP] = sel
    as_bf = lambda a: jnp.asarray(a, BF16)
    return dict(cs=cs, bd=as_bf(bd),
                tri=as_bf(tri), tri_strict=as_bf(tri_strict), sel=as_bf(sel), eq=as_bf(eq), ek=as_bf(ek))


def _prep(x2, mod_l, consts, p):
    seq = x2.shape[0]
    tm = TOKEN_TILE
    row = lambda n: pl.BlockSpec((tm, n), lambda i: (i, 0))
    heads = pl.BlockSpec((HEADS, tm, LANES), lambda i: (0, i, 0))
    in_arrays = [
        (x2, row(D_MODEL)), (mod_l, _full_spec(mod_l.shape)), (p["g1"], None),
        (p["w_in"], pl.BlockSpec(memory_space=pl.ANY)),
        (p["cq_g"], None), (p["w_uq"], None), (p["ckv_g"], None), (p["w_uk"], None), (p["w_uvt"], None),
        (p["q_g"], None), (p["k_g"], None),
        (consts["cs"], row(MLA_ROPE)),
        (p["fq_g"], None), (p["fk_g"], None), (p["f_b"], None), (p["mq_g"], None), (p["mk_g"], None),
        (p["dq_g"], None), (p["dk_g"], None),
        (consts["bd"], None), (consts["tri"], None), (consts["eq"], None), (consts["ek"], None),
        (consts["sel"], None),
    ]
    args = [a for a, _ in in_arrays]
    specs = [s if s is not None else _full_spec(a.shape) for a, s in in_arrays]
    hshape = jax.ShapeDtypeStruct((HEADS, seq, LANES), BF16)
    vshape = jax.ShapeDtypeStruct((GROUP, seq), BF16)
    dshape = jax.ShapeDtypeStruct((seq, GROUP), F32)
    vt = pl.BlockSpec((GROUP, tm), lambda i: (0, i))
    return pl.pallas_call(
        functools.partial(_prep_kernel, layer=p["layer"]),
        grid=(seq // tm,),
        in_specs=specs,
        out_specs=[heads, heads, vt] * 3 + [row(GROUP)] * 3 + [row(LANES)],
        out_shape=[hshape, hshape, vshape] * 3 + [dshape] * 3
                  + [jax.ShapeDtypeStruct((seq, LANES), F32)],
        scratch_shapes=[pltpu.VMEM((1, LANES), F32), pltpu.VMEM((HEADS * LANES, GROUP), F32),
                        pltpu.VMEM(p["w_in"].shape[1:], F32),
                        pltpu.VMEM((D_MODEL, COLS_IN), BF16),
                        pltpu.VMEM((len(IN_VALUE_COLUMNS), GROUP, D_MODEL), BF16),
                        pltpu.VMEM((3, tm, LANES), F32),
                        pltpu.SemaphoreType.DMA(())],
        compiler_params=_params(("arbitrary",)),
        name="prep",
    )(*args)


def _flash_kernel(jlo_ref, q_ref, k_ref, vt_ref, kbias_ref, o_ref, m_sc, acc_sc, sa_sc, sb_sc,
                  *, tile, use_kbias):
    i = pl.program_id(0)
    m_sc[...] = jnp.full_like(m_sc, -jnp.inf)
    acc_sc[...] = jnp.zeros_like(acc_sc)

    def score(hh, j, buf):
        start = pl.multiple_of(j * tile, tile)
        buf[hh] = _dot_nt(k_ref[hh, pl.ds(start, tile), :], q_ref[hh])

    def absorb(hh, j, buf, causal):
        start = pl.multiple_of(j * tile, tile)
        s = buf[hh]
        if use_kbias:
            s = s + jnp.concatenate([kbias_ref[hh]] * (tile // LANES), axis=1)
        if causal:
            key = lax.broadcasted_iota(jnp.int32, (tile, tile), 0)
            qry = lax.broadcasted_iota(jnp.int32, (tile, tile), 1)
            s = jnp.where(key <= qry, s, NEG)
        m_prev = m_sc[hh]
        m_cur = jnp.max(jnp.max(s.reshape(8, tile // 8, tile), axis=0), axis=0, keepdims=True)
        m_new = jnp.maximum(m_prev, m_cur)
        alpha = jnp.exp(m_prev - m_new)
        p = jnp.exp((s - m_new).astype(BF16))
        vt = vt_ref[hh * HEAD_DIM:(hh + 1) * HEAD_DIM, pl.ds(start, tile)]
        vt = jnp.concatenate([vt, jnp.ones((ONES_ROWS, tile), BF16)], axis=0)
        acc_sc[hh] = alpha * acc_sc[hh] + _dot(vt, p)
        m_sc[hh] = m_new

    first = jlo_ref[i]
    n_off = i - first
    for hh in range(HEADS):
        score(hh, first, sa_sc)

    def body(t, carry):
        j = first + 2 * t
        for hh in range(HEADS):
            score(hh, j + 1, sb_sc)
            absorb(hh, j, sa_sc, False)
        for hh in range(HEADS):
            score(hh, j + 2, sa_sc)
            absorb(hh, j + 1, sb_sc, False)
        return carry

    lax.fori_loop(0, n_off // 2, body, 0)

    @pl.when(n_off % 2 == 1)
    def _():
        for hh in range(HEADS):
            score(hh, i, sb_sc)
            absorb(hh, i - 1, sa_sc, False)
        for hh in range(HEADS):
            absorb(hh, i, sb_sc, True)

    @pl.when(n_off % 2 == 0)
    def _():
        for hh in range(HEADS):
            absorb(hh, i, sa_sc, True)

    o_t = jnp.concatenate([acc_sc[hh, 0:HEAD_DIM, :] / acc_sc[hh, HEAD_DIM:HEAD_DIM + 1, :]
                           for hh in range(HEADS)], axis=0)
    o_ref[...] = o_t.T.astype(o_ref.dtype)


def _flash(q, k, v_t, kbias, first_tile, use_kbias):
    seq = v_t.shape[1]
    tile = FLASH_TILE
    kern = functools.partial(_flash_kernel, tile=tile, use_kbias=use_kbias)
    resident = pl.Buffered(1)
    return pl.pallas_call(
        kern,
        grid_spec=pltpu.PrefetchScalarGridSpec(
            num_scalar_prefetch=1,
            grid=(seq // tile,),
            in_specs=[
                pl.BlockSpec((HEADS, tile, LANES), lambda i, f: (0, i, 0)),
                pl.BlockSpec((HEADS, seq, LANES), lambda i, f: (0, 0, 0), pipeline_mode=resident),
                pl.BlockSpec((GROUP, seq), lambda i, f: (0, 0), pipeline_mode=resident),
                pl.BlockSpec((HEADS, tile, LANES), lambda i, f: (0, 0, 0)),
            ],
            out_specs=pl.BlockSpec((tile, GROUP), lambda i, f: (i, 0)),
            scratch_shapes=[pltpu.VMEM((HEADS, 1, tile), F32),
                            pltpu.VMEM((HEADS, HEAD_DIM + ONES_ROWS, tile), F32),
                            pltpu.VMEM((HEADS, tile, tile), F32), pltpu.VMEM((HEADS, tile, tile), F32)],
        ),
        out_shape=jax.ShapeDtypeStruct((seq, GROUP), BF16),
        compiler_params=_params(("arbitrary",)),
        name="flash_kbias" if use_kbias else "flash",
    )(first_tile, q, k, v_t, kbias)


def _fox_first_tile(decay, qk_bound):
    seq = decay.shape[0]
    nq = seq // FLASH_TILE
    f = decay[:, :HEADS]
    f_first = f[0::FLASH_TILE]
    f_last = f[FLASH_TILE - 1::FLASH_TILE]
    gap = f_first[:, None, :] - f_last[None, :, :] + 2.0 * qk_bound
    jj = jnp.arange(nq, dtype=jnp.int32)
    needed = (gap >= FOX_SKIP_LOG) | (jj[None, :, None] >= jj[:, None, None])
    first = jnp.min(jnp.where(needed, jj[None, :, None], nq), axis=1)
    return jnp.min(first, axis=1).astype(jnp.int32)


def _dilated_kernel(q_ref, k_ref, v_ref, o_ref, kbuf, vbuf, acc_s, m_s, l_s):
    pair = pl.program_id(0)
    i = pl.program_id(1)
    T = q_ref.shape[0]

    @pl.when(i == 0)
    def _():
        kbuf[...] = jnp.zeros_like(kbuf)
        vbuf[...] = jnp.zeros_like(vbuf)

    kbuf[0:T, :] = kbuf[T:2 * T, :]
    vbuf[0:T, :] = vbuf[T:2 * T, :]
    kbuf[T:2 * T, :] = k_ref[...]
    vbuf[T:2 * T, :] = v_ref[...]

    ii = lax.broadcasted_iota(jnp.int32, (DIL_SPAN, 2 * DIL_SPAN), 0)
    jj = lax.broadcasted_iota(jnp.int32, (DIL_SPAN, 2 * DIL_SPAN), 1)
    dist = ii + DIL_SPAN - jj
    band = (dist >= 0) & (dist <= DIL_SPAN)
    dist_f = dist.astype(F32)
    upper = lax.broadcasted_iota(jnp.int32, (1, LANES), 1) >= HEAD_DIM

    for pi, (window, r) in enumerate(DIL_PATTERNS):
        assert window // r == DIL_SPAN
        sub = DIL_SPAN * r

        def body(idx, carry, r=r, sub=sub, pi=pi):
            n = idx // r
            rho = idx - n * r
            base = n * sub + rho
            q = q_ref[pl.ds(base, DIL_SPAN, stride=r), :]
            kc = kbuf[pl.ds(T + base - sub, 2 * DIL_SPAN, stride=r), :].astype(BF16)
            vc = vbuf[pl.ds(T + base - sub, 2 * DIL_SPAN, stride=r), :].astype(BF16)
            first_key = jnp.where((i == 0) & (n == 0), DIL_SPAN, 0)
            valid = band & (jj >= first_key)
            stats = []
            for hh in range(2):
                slope = jnp.where(pair == 0, 2.0 ** (-(2 * hh + 1)), 2.0 ** (-(2 * hh + 5)))
                qh = (jnp.where(upper, q, 0.0) if hh else jnp.where(upper, 0.0, q)).astype(BF16)
                s = _dot_nt(qh, kc) - (slope * r) * dist_f
                s = jnp.where(valid, s, NEG)
                m = jnp.max(s, axis=-1, keepdims=True)
                p = jnp.exp(s - m)
                l = jnp.sum(p, axis=-1, keepdims=True)
                stats.append((_dot(p.astype(BF16), vc), m, l))
            rows = pl.ds(pi * T + base, DIL_SPAN, stride=r)
            acc_s[rows, :] = jnp.where(upper, stats[1][0], stats[0][0])
            m_s[rows, :] = jnp.where(upper, stats[1][1], stats[0][1])
            l_s[rows, :] = jnp.where(upper, stats[1][2], stats[0][2])
            return carry

        lax.fori_loop(0, T // DIL_SPAN, body, 0, unroll=True)

    npat = len(DIL_PATTERNS)
    ms = [m_s[pi * T:(pi + 1) * T, :] for pi in range(npat)]
    m_top = functools.reduce(jnp.maximum, ms)
    num = jnp.zeros((T, LANES), F32)
    den = jnp.zeros((T, LANES), F32)
    for pi in range(npat):
        w = jnp.exp(ms[pi] - m_top)
        num = num + w * acc_s[pi * T:(pi + 1) * T, :]
        den = den + w * l_s[pi * T:(pi + 1) * T, :]
    o_ref[...] = (num / den).astype(o_ref.dtype)


def _dilated(q, k, v):
    seq = q.shape[0]
    T = DIL_TILE
    spec = pl.BlockSpec((T, LANES), lambda p, i: (i, p))
    npat = len(DIL_PATTERNS)
    return pl.pallas_call(
        _dilated_kernel,
        grid=(HEADS // 2, seq // T),
        in_specs=[spec, spec, spec],
        out_specs=spec,
        out_shape=jax.ShapeDtypeStruct((seq, GROUP), BF16),
        scratch_shapes=[pltpu.VMEM((2 * T, LANES), F32), pltpu.VMEM((2 * T, LANES), F32),
                        pltpu.VMEM((npat * T, LANES), F32), pltpu.VMEM((npat * T, LANES), F32),
                        pltpu.VMEM((npat * T, LANES), F32)],
        compiler_params=_params(("arbitrary", "arbitrary")),
        name="dilated",
    )(q, k, v)


def _post_kernel(x_ref, oa_ref, ob_ref, oc_ref, od_ref, wout_ref, mod_ref, g2_ref,
                 rw_ref, rb_ref, tri_ref,
                 x1_ref, h2_ref, eidx_ref, gate_ref, rank_ref, cnt_ref, carry):
    i = pl.program_id(0)
    tm = x_ref.shape[0]

    @pl.when(i == 0)
    def _():
        carry[...] = jnp.zeros_like(carry)

    o = (_dot(oa_ref[...], wout_ref[0]) + _dot(ob_ref[...], wout_ref[1])
         + _dot(oc_ref[...], wout_ref[2]) + _dot(od_ref[...], wout_ref[3]))
    x1 = x_ref[...] + mod_ref[2] * o
    x1_ref[...] = x1
    y = x1 * lax.rsqrt(jnp.mean(x1 * x1, axis=-1, keepdims=True) + EPS) * g2_ref[...]
    h2 = y * (1.0 + mod_ref[4]) + mod_ref[3]
    _store_token_tiles(h2_ref, h2)

    h_hi, h_lo = _split2(h2)
    hh = _dot(h_hi, rw_ref[...])
    logits = (hh[:, :LANES] + hh[:, LANES:] + _dot(h_lo, rw_ref[:, :LANES])
              + rb_ref[...])
    lane = lax.broadcasted_iota(jnp.int32, (tm, LANES), 1)
    lane_f = lane.astype(F32)
    g = logits
    chosen = jnp.zeros((tm, LANES), F32)
    vals, idxs = [], []
    for _ in range(TOP_K):
        m = jnp.max(g, axis=-1, keepdims=True)
        first = jnp.min(jnp.where(g == m, lane_f, 1e9), axis=-1, keepdims=True)
        pick = lane_f == first
        chosen = jnp.where(pick, 1.0, chosen)
        g = jnp.where(pick, -jnp.inf, g)
        vals.append(m)
        idxs.append(first)
    exps = [jnp.exp(v - vals[0]) for v in vals]
    den = exps[0] + exps[1] + exps[2] + exps[3]
    before = _dot(tri_ref[...], chosen.astype(BF16)) + carry[...]
    carry[...] = carry[...] + jnp.sum(chosen, axis=0, keepdims=True)
    cnt_ref[...] = carry[...]
    e_out = jnp.zeros((tm, LANES), F32)
    g_out = jnp.zeros((tm, LANES), F32)
    r_out = jnp.zeros((tm, LANES), F32)
    for k in range(TOP_K):
        rank_k = jnp.sum(jnp.where(lane_f == idxs[k], before, 0.0), axis=-1, keepdims=True)
        e_out = jnp.where(lane == k, idxs[k], e_out)
        g_out = jnp.where(lane == k, exps[k] / den, g_out)
        r_out = jnp.where(lane == k, rank_k, r_out)
    eidx_ref[...] = e_out.astype(jnp.int32)
    gate_ref[...] = g_out
    rank_ref[...] = r_out.astype(jnp.int32)


def _post(x2, oa, ob, oc, od, mod_l, consts, p):
    seq = x2.shape[0]
    tm = POST_TILE
    row = lambda n: pl.BlockSpec((tm, n), lambda i: (i, 0))
    full = [p["w_out"], mod_l, p["g2"], p["rw"], p["r_b"], consts["tri_strict"]]
    f32 = lambda n: jax.ShapeDtypeStruct((seq, n), F32)
    i32 = lambda n: jax.ShapeDtypeStruct((seq, n), jnp.int32)
    return pl.pallas_call(
        _post_kernel,
        grid=(seq // tm,),
        in_specs=[row(D_MODEL)] + [row(GROUP)] * 4 + [_full_spec(a.shape) for a in full],
        out_specs=[row(D_MODEL), pl.BlockSpec((tm * CHUNKS, LANES), lambda i: (i, 0)),
                   row(LANES), row(LANES), row(LANES), _full_spec((1, LANES))],
        out_shape=[f32(D_MODEL), jax.ShapeDtypeStruct((seq * CHUNKS, LANES), F32),
                   i32(LANES), f32(LANES), i32(LANES), jax.ShapeDtypeStruct((1, LANES), F32)],
        scratch_shapes=[pltpu.VMEM((1, LANES), F32)],
        compiler_params=_params(("arbitrary",)),
        name="post",
    )(x2, oa, ob, oc, od, *full)


def _tile_copy(src, s, dst, d, sem):
    return pltpu.make_async_copy(src.at[pl.ds(pl.multiple_of(s * CHUNKS, CHUNKS), CHUNKS), :],
                                 dst.at[pl.ds(pl.multiple_of(d * CHUNKS, CHUNKS), CHUNKS), :], sem)


def _dispatch_kernel(dest_ref, padlo_ref, padn_ref, nu_ref, h_ref, xs_ref, zbuf, sem, zsem):
    i = pl.program_id(0)
    n = ROUTE_TILE * TOP_K
    base = i * n
    block_rows = EXPERT_ROWS * CHUNKS

    @pl.when(i == 0)
    def _():
        zbuf[...] = jnp.zeros_like(zbuf)

        def fill(wait):
            def go(copy):
                copy.wait() if wait else copy.start()

            def per_expert(e, carry):
                lo, cnt = padlo_ref[e], padn_ref[e]
                off = lo
                p = EXPERT_ROWS // 2
                while p >= 1:
                    rows = p * CHUNKS

                    @pl.when((cnt & p) != 0)
                    def _(off=off, rows=rows):
                        go(pltpu.make_async_copy(
                            zbuf.at[pl.ds(0, rows), :],
                            xs_ref.at[pl.ds(pl.multiple_of(off * CHUNKS, CHUNKS), rows), :], zsem))

                    off = off + (cnt & p)
                    p //= 2
                return carry

            lax.fori_loop(0, N_EXPERTS, per_expert, 0)

            def per_block(b, carry):
                go(pltpu.make_async_copy(
                    zbuf, xs_ref.at[pl.ds(pl.multiple_of(b * block_rows, block_rows), block_rows), :],
                    zsem))
                return carry

            lax.fori_loop(nu_ref[0], xs_ref.shape[0] // block_rows, per_block, 0)

        fill(False)
        fill(True)

    def issue(r, carry):
        for k in range(TOP_K):
            _tile_copy(h_ref, r, xs_ref, dest_ref[base + r * TOP_K + k], sem).start(priority=k % 2)
        return carry

    lax.fori_loop(0, ROUTE_TILE, issue, 0, unroll=4)
    rows = pl.ds(0, n * CHUNKS)
    pltpu.make_async_copy(xs_ref.at[rows, :], xs_ref.at[rows, :], sem).wait()


def _dispatch(dest, pad_lo, pad_n, n_used, h2_tiles, m_pad):
    seq = h2_tiles.shape[0] // CHUNKS
    return pl.pallas_call(
        _dispatch_kernel,
        grid_spec=pltpu.PrefetchScalarGridSpec(
            num_scalar_prefetch=4,
            grid=(seq // ROUTE_TILE,),
            in_specs=[pl.BlockSpec((ROUTE_TILE * CHUNKS, LANES), lambda i, *_: (i, 0))],
            out_specs=pl.BlockSpec(memory_space=pl.ANY),
            scratch_shapes=[pltpu.VMEM((EXPERT_ROWS * CHUNKS, LANES), F32),
                            pltpu.SemaphoreType.DMA(()), pltpu.SemaphoreType.DMA(())],
        ),
        out_shape=jax.ShapeDtypeStruct((m_pad * CHUNKS, LANES), F32),
        compiler_params=_params(("arbitrary",)),
        name="dispatch",
    )(dest, pad_lo, pad_n, n_used, h2_tiles)


def _expert_kernel(be_ref, nu_ref, nxt_ref, ord_ref, xs_ref, w1_hbm, b1_ref, w2_hbm, b2_ref, ys_ref,
                   w1f, w2f, w1b, w2b, sems):
    b = pl.program_id(0)
    e = be_ref[b]
    prev = be_ref[jnp.maximum(b - 1, 0)]
    fresh = ((b == 0) | (e != prev)) & (b < nu_ref[0])
    slot = ord_ref[b] % 2

    def fetch(expert, to_slot):
        return (pltpu.make_async_copy(w1_hbm.at[expert], w1f.at[to_slot], sems.at[0, to_slot]),
                pltpu.make_async_copy(w2_hbm.at[expert], w2f.at[to_slot], sems.at[1, to_slot]))

    @pl.when(b == 0)
    def _():
        for copy in fetch(e, slot):
            copy.start()

    @pl.when(fresh)
    def _():
        for copy in fetch(e, slot):
            copy.wait()
        w1b[...] = w1f[slot].astype(BF16)
        w2b[...] = w2f[slot].astype(BF16)

        @pl.when(nxt_ref[b] >= 0)
        def _():
            for copy in fetch(nxt_ref[b], 1 - slot):
                copy.start()

    @pl.when(b < nu_ref[0])
    def _():
        half = EXPERT_ROWS // 2
        gus = []
        for r in range(2):
            xb = jnp.concatenate(
                [_load_token_chunk(xs_ref, half, c, offset=r * half * CHUNKS) for c in range(CHUNKS)],
                axis=1).astype(BF16)
            gus.append(_dot(xb, w1b[...]) + b1_ref[0])
        for r in range(2):
            g = jnp.minimum(gus[r][:, :D_EXPERT], SWIGLU_LIMIT)
            u = jnp.clip(gus[r][:, D_EXPERT:], -SWIGLU_LIMIT, SWIGLU_LIMIT)
            y = (u + 1.0) * g * (1.0 / (1.0 + jnp.exp(-SWIGLU_ALPHA * g)))
            _store_token_tiles(ys_ref, _dot(y.astype(BF16), w2b[...]) + b2_ref[0],
                               offset=r * half * CHUNKS)

    @pl.when(b >= nu_ref[0])
    def _():
        ys_ref[...] = jnp.zeros_like(ys_ref)


def _experts(blk_expert, n_used, blk_next, blk_ord, xs, w1, b1, w2, b2):
    m_pad = xs.shape[0] // CHUNKS
    bm = EXPERT_ROWS
    n_all = w1.shape[0] * w1.shape[1]
    rows = lambda b, be, nu, *_: (jnp.minimum(b, nu[0] - 1), 0)
    ex = lambda b, be, nu, *_: (be[jnp.minimum(b, nu[0] - 1)], 0, 0)
    return pl.pallas_call(
        _expert_kernel,
        grid_spec=pltpu.PrefetchScalarGridSpec(
            num_scalar_prefetch=4,
            grid=(m_pad // bm,),
            in_specs=[
                pl.BlockSpec((bm * CHUNKS, LANES), rows),
                pl.BlockSpec(memory_space=pl.ANY),
                pl.BlockSpec((1, 1, 2 * D_EXPERT), ex),
                pl.BlockSpec(memory_space=pl.ANY),
                pl.BlockSpec((1, 1, D_MODEL), ex),
            ],
            out_specs=pl.BlockSpec((bm * CHUNKS, LANES), lambda b, *_: (b, 0)),
            scratch_shapes=[pltpu.VMEM((2, D_MODEL, 2 * D_EXPERT), F32),
                            pltpu.VMEM((2, D_EXPERT, D_MODEL), F32),
                            pltpu.VMEM((D_MODEL, 2 * D_EXPERT), BF16),
                            pltpu.VMEM((D_EXPERT, D_MODEL), BF16),
                            pltpu.SemaphoreType.DMA((2, 2))],
        ),
        out_shape=jax.ShapeDtypeStruct((m_pad * CHUNKS, LANES), F32),
        compiler_params=_params(("arbitrary",)),
        name="experts",
    )(blk_expert, n_used, blk_next, blk_ord, xs,
      w1.reshape(n_all, D_MODEL, 2 * D_EXPERT), b1.reshape(n_all, 1, -1),
      w2.reshape(n_all, D_EXPERT, D_MODEL), b2.reshape(n_all, 1, -1))


def _combine_kernel(dest_ref, ys_ref, x1_ref, gate_ref, mod_ref, o_ref, buf, sems):
    i = pl.program_id(0)
    tm = x1_ref.shape[0]
    n = tm * TOP_K
    slot = i % 2

    def gather(step, to_slot):
        base = step * n

        def issue(r, carry):
            for k in range(TOP_K):
                _tile_copy(ys_ref, dest_ref[base + r * TOP_K + k], buf, to_slot * n + k * tm + r,
                           sems.at[to_slot]).start(priority=k % 2)
            return carry

        lax.fori_loop(0, tm, issue, 0, unroll=4)

    @pl.when(i == 0)
    def _():
        gather(0, 0)

    @pl.when(i + 1 < pl.num_programs(0))
    def _():
        gather(i + 1, 1 - slot)

    mine = pl.ds(pl.multiple_of(slot * n * CHUNKS, n * CHUNKS), n * CHUNKS)
    pltpu.make_async_copy(ys_ref.at[pl.ds(0, n * CHUNKS), :], buf.at[mine, :], sems.at[slot]).wait()
    gates = gate_ref[...]
    g2 = mod_ref[5]
    for c in range(CHUNKS):
        cols = slice(c * LANES, (c + 1) * LANES)
        mix = jnp.zeros((tm, LANES), F32)
        for k in range(TOP_K):
            mix = mix + gates[:, k:k + 1] * _load_token_chunk(
                buf, tm, c, offset=(slot * n + k * tm) * CHUNKS)
        o_ref[:, cols] = x1_ref[:, cols] + g2[:, cols] * mix


def _combine(dest, ys, x1, gates, mod_l):
    seq = x1.shape[0]
    tm = ROUTE_TILE
    return pl.pallas_call(
        _combine_kernel,
        grid_spec=pltpu.PrefetchScalarGridSpec(
            num_scalar_prefetch=1,
            grid=(seq // tm,),
            in_specs=[
                pl.BlockSpec(memory_space=pl.ANY),
                pl.BlockSpec((tm, D_MODEL), lambda i, d: (i, 0)),
                pl.BlockSpec((tm, LANES), lambda i, d: (i, 0)),
                pl.BlockSpec(mod_l.shape, lambda i, d: (0, 0, 0)),
            ],
            out_specs=pl.BlockSpec((tm, D_MODEL), lambda i, d: (i, 0)),
            scratch_shapes=[pltpu.VMEM((2 * TOP_K * tm * CHUNKS, LANES), F32),
                            pltpu.SemaphoreType.DMA((2,))],
        ),
        out_shape=jax.ShapeDtypeStruct((seq, D_MODEL), F32),
        compiler_params=_params(("arbitrary",)),
        name="combine",
    )(dest, ys, x1, gates, mod_l)


def _moe(l, x1, h2_tiles, eidx, gates, rank, counts, mod_l, w1, b1, w2, b2):
    seq = x1.shape[0]
    bm = EXPERT_ROWS
    m_pad = seq * TOP_K + N_EXPERTS * bm
    cnt = counts[0, :N_EXPERTS].astype(jnp.int32)
    padded = (cnt + bm - 1) // bm * bm
    pad_end = jnp.cumsum(padded)
    pad_start = pad_end - padded
    onehot = eidx[:, :TOP_K, None] == jnp.arange(N_EXPERTS, dtype=jnp.int32)
    start_of = jnp.sum(jnp.where(onehot, pad_start, 0), axis=-1)
    dest = (start_of + rank[:, :TOP_K]).reshape(seq * TOP_K).astype(jnp.int32)
    nblk = m_pad // bm
    blk_start = jnp.arange(nblk, dtype=jnp.int32) * bm
    local = jnp.minimum(jnp.sum(pad_end[None, :] <= blk_start[:, None], axis=1), N_EXPERTS - 1)
    blk_expert = (local + l * N_EXPERTS).astype(jnp.int32)
    n_used = (pad_end[-1:] // bm).astype(jnp.int32)
    ids = jnp.arange(N_EXPERTS, dtype=jnp.int32)
    live = padded > 0
    ordinal = jnp.cumsum(live.astype(jnp.int32)) - 1
    later = live[None, :] & (ids[None, :] > ids[:, None])
    nxt = jnp.min(jnp.where(later, ids[None, :], N_EXPERTS), axis=1)
    nxt = jnp.where(nxt < N_EXPERTS, nxt + l * N_EXPERTS, -1)
    of_block = local[:, None] == ids[None, :]
    blk_next = jnp.sum(jnp.where(of_block, nxt[None, :], 0), axis=1).astype(jnp.int32)
    blk_ord = jnp.sum(jnp.where(of_block, ordinal[None, :], 0), axis=1).astype(jnp.int32)
    xs = _dispatch(dest, (pad_start + cnt).astype(jnp.int32), (padded - cnt).astype(jnp.int32),
                   n_used, h2_tiles, m_pad)
    ys = _experts(blk_expert, n_used, blk_next, blk_ord, xs, w1, b1, w2, b2)
    return _combine(dest, ys, x1, gates, mod_l)


def _pad_cols(a, n):
    return jnp.pad(a, ((0, 0), (0, n - a.shape[1])))


def _layer_params(l, w_in, mla_cq_g, mla_w_uq, mla_ckv_g, mla_w_ukv, mla_q_g, mla_k_g,
                  fox_q_g, fox_k_g, fox_b_f, moba_q_g, moba_k_g, dil_q_g, dil_k_g, w_out,
                  norm1_g, norm2_g, router_w, router_b):
    w_uq = jnp.pad(mla_w_uq[l].reshape(MLA_Q_RANK, HEADS, MLA_QK),
                   ((0, 0), (0, 0), (0, LANES - MLA_QK))).reshape(MLA_Q_RANK, HEADS * LANES)
    w_ukv = mla_w_ukv[l].reshape(MLA_KV_RANK, HEADS, MLA_NOPE + HEAD_DIM)
    w_uk = jnp.pad(w_ukv[:, :, :MLA_NOPE], ((0, 0), (0, 0), (0, LANES - MLA_NOPE)))
    w_uv = w_ukv[:, :, MLA_NOPE:]
    tile4 = lambda g: jnp.tile(g, HEADS)[None, :]
    rw = _pad_cols(router_w[l], LANES)
    rw_hi = rw.astype(BF16)
    rw_lo = (rw - rw_hi.astype(F32)).astype(BF16)
    r_b = jnp.concatenate([router_b[l], jnp.full((LANES - N_EXPERTS,), NEG, F32)])[None, :]
    return dict(
        g1=norm1_g[l][None, :], g2=norm2_g[l][None, :], w_in=w_in, layer=l,
        cq_g=mla_cq_g[l][None, :], w_uq=w_uq.astype(BF16), ckv_g=mla_ckv_g[l][None, :],
        w_uk=w_uk.reshape(MLA_KV_RANK, HEADS * LANES).astype(BF16),
        w_uvt=w_uv.reshape(MLA_KV_RANK, GROUP).T.astype(BF16),
        fox_bound=FOX_NORM_SLACK * HEAD_DIM ** 0.5 * jnp.max(jnp.abs(fox_q_g[l]))
        * jnp.max(jnp.abs(fox_k_g[l])),
        q_g=_pad_cols(mla_q_g[l][None, :], LANES), k_g=_pad_cols(mla_k_g[l][None, :], LANES),
        fq_g=tile4(fox_q_g[l]), fk_g=tile4(fox_k_g[l]), f_b=_pad_cols(fox_b_f[l][None, :], LANES),
        mq_g=tile4(moba_q_g[l]), mk_g=tile4(moba_k_g[l]), dq_g=tile4(dil_q_g[l]), dk_g=tile4(dil_k_g[l]),
        w_out=w_out[l].reshape(HEADS, GROUP, D_MODEL).astype(BF16),
        rw=jnp.concatenate([rw_hi, rw_lo], axis=1), r_b=r_b,
    )


def kernel(x, c, w_mod, b_mod, norm1_g, norm2_g, w_in, mla_cq_g, mla_w_uq, mla_ckv_g, mla_w_ukv, mla_q_g, mla_k_g, fox_q_g, fox_k_g, fox_b_f, moba_q_g, moba_k_g, dil_q_g, dil_k_g, w_out, router_w, router_b, exp_w1, exp_b1, exp_w2, exp_b2):
    batch, seq, d = x.shape
    assert batch == 1 and d == D_MODEL
    assert seq % DIL_TILE == 0 and seq // MOBA_BLOCK <= MOBA_MAX_BLOCKS
    depth = w_mod.shape[0]
    consts = _prep_constants(seq, TOKEN_TILE)
    mod = _modulation(c, w_mod, b_mod)
    slopes_c = 2.0 ** (-(2.0 * np.arange(HEADS) + 2.0))
    in_block = np.arange(FLASH_TILE) % MOBA_BLOCK
    kbias_c = jnp.asarray(np.broadcast_to(slopes_c[:, None, None] * in_block[None, :, None],
                                          (HEADS, FLASH_TILE, LANES)), F32)
    kbias_0 = jnp.zeros((HEADS, FLASH_TILE, LANES), F32)
    all_tiles = jnp.zeros((seq // FLASH_TILE,), jnp.int32)
    x2 = x.reshape(seq, d)
    for l in range(depth):
        p = _layer_params(l, w_in, mla_cq_g, mla_w_uq, mla_ckv_g, mla_w_ukv, mla_q_g, mla_k_g,
                          fox_q_g, fox_k_g, fox_b_f, moba_q_g, moba_k_g, dil_q_g, dil_k_g, w_out,
                          norm1_g, norm2_g, router_w, router_b)
        mod_l = mod[l]
        qa, ka, va, qf, kf, vf, qm, km, vm, qd, kd, vd, decay = _prep(x2, mod_l, consts, p)
        oa = _flash(qa, ka, va, kbias_0, all_tiles, False)
        ob = _flash(qf, kf, vf, kbias_0, _fox_first_tile(decay, p["fox_bound"]), False)
        oc = _flash(qm, km, vm, kbias_c, all_tiles, True)
        od = _dilated(qd, kd, vd)
        x1, h2, eidx, gates, rank, counts = _post(x2, oa, ob, oc, od, mod_l, consts, p)
        x2 = _moe(l, x1, h2, eidx, gates, rank, counts, mod_l, exp_w1, exp_b1, exp_w2, exp_b2)
    return x2.reshape(batch, seq, d)
```

```python
import functools

import numpy as np
import jax
import jax.numpy as jnp
from jax import lax
from jax.experimental import pallas as pl
from jax.experimental.pallas import tpu as pltpu

F32 = jnp.float32
BF16 = jnp.bfloat16

D_MODEL = 1024
HEAD_DIM = 64
HEADS = 4
GROUP = HEADS * HEAD_DIM
LANES = 128
CHUNKS = D_MODEL // LANES
MLA_Q_RANK = 256
MLA_KV_RANK = 128
MLA_NOPE = 64
MLA_ROPE = 32
MLA_QK = MLA_NOPE + MLA_ROPE
ROPE_THETA = 10000.0
MOBA_BLOCK = 256
MOBA_TOPK = 3
MOBA_MAX_BLOCKS = 64
DIL_PATTERNS = ((128, 1), (512, 4), (2048, 16))
DIL_SPAN = 128
DIL_TILE = 2048
N_EXPERTS = 32
TOP_K = 4
D_EXPERT = 1024
SWIGLU_LIMIT = 7.0
SWIGLU_ALPHA = 1.702
EPS = 1e-6
NEG = -1e30

FLASH_TILE = 512
ONES_ROWS = 16
FOX_SKIP_LOG = -106.0
FOX_NORM_SLACK = 1.02
TOKEN_TILE = 512
POST_TILE = 512
ROUTE_TILE = 512
EXPERT_ROWS = 512
VMEM_LIMIT = 56 * 1024 * 1024

COLS_MLA = 512
COLS_FOX = 640
COLS_MOBA = 512
COLS_DIL = 768
COLS_IN = COLS_MLA + COLS_FOX + COLS_MOBA + COLS_DIL


def _in_column_map():
    sizes = [MLA_Q_RANK, MLA_KV_RANK, MLA_ROPE, GROUP, GROUP, GROUP, HEADS] + [GROUP] * 6
    src = np.concatenate([[0], np.cumsum(sizes)])
    cq, ckv, kr, fq, fk, fv, flog, mq, mk, mv, dq, dk, dv = range(13)
    layout = [(cq, 0), (ckv, MLA_Q_RANK), (kr, MLA_Q_RANK + MLA_KV_RANK + MLA_NOPE),
              (fq, COLS_MLA), (fk, COLS_MLA + GROUP), (flog, COLS_MLA + 2 * GROUP),
              (mq, COLS_MLA + COLS_FOX), (mk, COLS_MLA + COLS_FOX + GROUP),
              (dq, COLS_IN - 3 * GROUP), (dk, COLS_IN - 2 * GROUP), (dv, COLS_IN - GROUP)]
    moves = tuple((dst, int(src[piece]), sizes[piece]) for piece, dst in layout)
    return moves, (int(src[fv]), int(src[mv]))


IN_COLUMN_MAP, IN_VALUE_COLUMNS = _in_column_map()


def _dot(a, b):
    return jnp.dot(a, b, preferred_element_type=F32)


def _dot_nt(a, b):
    return lax.dot_general(a, b, (((1,), (1,)), ((), ())), preferred_element_type=F32)


def _split2(x):
    hi = x.astype(BF16)
    lo = (x - hi.astype(F32)).astype(BF16)
    return hi, lo


def _split3(x):
    a = x.astype(BF16)
    r = x - a.astype(F32)
    b = r.astype(BF16)
    c = (r - b.astype(F32)).astype(BF16)
    return a, b, c


def _head_of_lane():
    return jnp.right_shift(lax.broadcasted_iota(jnp.int32, (1, GROUP), 1), 6)


def _store_token_tiles(ref, x, offset=0):
    n = x.shape[0]
    for c in range(CHUNKS):
        ref[pl.ds(offset + c, n, stride=CHUNKS), :] = x[:, c * LANES:(c + 1) * LANES]


def _load_token_chunk(ref, n, c, offset=0):
    return ref[pl.ds(offset + c, n, stride=CHUNKS), :]


def _full_spec(shape):
    nd = len(shape)
    return pl.BlockSpec(shape, lambda *_: (0,) * nd)


def _params(sem):
    return pltpu.CompilerParams(dimension_semantics=sem, vmem_limit_bytes=VMEM_LIMIT)


def _mod_kernel(c_ref, w_ref, b_ref, o_ref):
    c = c_ref[...]
    s = c * (1.0 / (1.0 + jnp.exp(-c)))
    s8 = jnp.broadcast_to(s, (8, D_MODEL))
    r = jnp.dot(s8, w_ref[0], preferred_element_type=F32, precision=lax.Precision.HIGHEST)
    o_ref[0, 0] = r[0:1, :] + b_ref[0, 0]


def _modulation(c, w_mod, b_mod):
    depth = w_mod.shape[0]
    b4 = b_mod.reshape(depth, 6, 1, D_MODEL)
    return pl.pallas_call(
        _mod_kernel,
        grid=(depth, 6),
        in_specs=[
            pl.BlockSpec((1, D_MODEL), lambda l, j: (0, 0)),
            pl.BlockSpec((1, D_MODEL, D_MODEL), lambda l, j: (l, 0, j)),
            pl.BlockSpec((1, 1, 1, D_MODEL), lambda l, j: (l, j, 0, 0)),
        ],
        out_specs=pl.BlockSpec((1, 1, 1, D_MODEL), lambda l, j: (l, j, 0, 0)),
        out_shape=jax.ShapeDtypeStruct((depth, 6, 1, D_MODEL), F32),
        compiler_params=_params(("arbitrary", "arbitrary")),
        name="modulation",
    )(c, w_mod, b4)


def _head_norm(x, g, bd):
    hi, lo = _split2(x * x)
    ss = _dot(hi, bd) + _dot(lo, bd)
    return x * lax.rsqrt(ss * (1.0 / HEAD_DIM) + EPS) * g


def _prep_kernel(x_ref, mod_ref, g1_ref, win_hbm, cqg_ref, wuq_ref, ckvg_ref, wuk_ref,
                 wuvt_ref, qg_ref, kg_ref, cs_ref,
                 fqg_ref, fkg_ref, fb_ref, mqg_ref, mkg_ref, dqg_ref, dkg_ref,
                 bd_ref, tri_ref, eq_ref, ek_ref, sel_ref,
                 qa_ref, ka_ref, va_ref, qf_ref, kf_ref, vf_ref, qm_ref, km_ref, vm_ref,
                 qd_ref, kd_ref, vd_ref, f_ref,
                 fcarry, kmean, wraw_ref, win_ref, wvt_ref, rope_scr, wsem, *, layer):
    i = pl.program_id(0)
    tm = x_ref.shape[0]

    @pl.when(i == 0)
    def _():
        fetch = pltpu.make_async_copy(win_hbm.at[layer], wraw_ref, wsem)
        fetch.start()
        fcarry[...] = jnp.zeros_like(fcarry)
        kmean[...] = jnp.zeros_like(kmean)
        win_ref[...] = jnp.zeros_like(win_ref)
        rope_scr[...] = jnp.zeros_like(rope_scr)
        rope_scr[0, :, 0:MLA_NOPE] = jnp.ones((tm, MLA_NOPE), F32)
        fetch.wait()
        for dst, src, width in IN_COLUMN_MAP:
            win_ref[:, dst:dst + width] = wraw_ref[:, src:src + width].astype(BF16)
        for slot, src in enumerate(IN_VALUE_COLUMNS):
            wvt_ref[slot] = wraw_ref[:, src:src + GROUP].T.astype(BF16)

    x = x_ref[...]
    y = x * lax.rsqrt(jnp.mean(x * x, axis=-1, keepdims=True) + EPS) * g1_ref[...]
    hb = (y * (1.0 + mod_ref[1]) + mod_ref[0]).astype(BF16)
    bd = bd_ref[...]
    lane = lax.broadcasted_iota(jnp.int32, (1, LANES), 1)
    lane_f = lane.astype(F32)
    head_of_lane = _head_of_lane()

    pa = _dot(hb, win_ref[:, 0:COLS_MLA])
    cq = pa[:, 0:MLA_Q_RANK]
    ckv = pa[:, MLA_Q_RANK:MLA_Q_RANK + MLA_KV_RANK]
    kr = pa[:, MLA_Q_RANK + MLA_KV_RANK:COLS_MLA]
    cqn = (cq * lax.rsqrt(jnp.mean(cq * cq, axis=-1, keepdims=True) + EPS) * cqg_ref[...]).astype(BF16)
    ckvn = (ckv * lax.rsqrt(jnp.mean(ckv * ckv, axis=-1, keepdims=True) + EPS) * ckvg_ref[...]).astype(BF16)
    q_all = _dot(cqn, wuq_ref[...])
    k_all = _dot(ckvn, wuk_ref[...])
    va_ref[...] = _dot_nt(wuvt_ref[...], ckvn).astype(BF16)
    half = MLA_ROPE // 2
    cos, sin = cs_ref[:, 0:half], cs_ref[:, half:MLA_ROPE]
    rope_scr[0, :, MLA_NOPE:MLA_NOPE + half] = cos
    rope_scr[0, :, MLA_NOPE + half:MLA_QK] = cos
    rope_scr[1, :, MLA_NOPE:MLA_NOPE + half] = -sin
    rope_scr[2, :, MLA_NOPE + half:MLA_QK] = sin
    rc, rs1, rs2 = rope_scr[0], rope_scr[1], rope_scr[2]

    def rope(t):
        return t * rc + pltpu.roll(t, LANES - MLA_ROPE // 2, 1) * rs1 + pltpu.roll(t, MLA_ROPE // 2, 1) * rs2

    for h in range(HEADS):
        q = q_all[:, h * LANES:(h + 1) * LANES]
        q = q * lax.rsqrt(jnp.sum(q * q, axis=-1, keepdims=True) * (1.0 / MLA_QK) + EPS) * qg_ref[...]
        qa_ref[h] = (rope(q) * (MLA_QK ** -0.5)).astype(BF16)
        k = k_all[:, h * LANES:(h + 1) * LANES] + kr
        k = k * lax.rsqrt(jnp.sum(k * k, axis=-1, keepdims=True) * (1.0 / MLA_QK) + EPS) * kg_ref[...]
        ka_ref[h] = rope(k).astype(BF16)

    pf = _dot(hb, win_ref[:, COLS_MLA:COLS_MLA + COLS_FOX])
    fqn = (_head_norm(pf[:, 0:GROUP], fqg_ref[...], bd) * (HEAD_DIM ** -0.5)).astype(BF16)
    fkn = _head_norm(pf[:, GROUP:2 * GROUP], fkg_ref[...], bd).astype(BF16)
    vf_ref[...] = _dot_nt(wvt_ref[0], hb).astype(BF16)
    z = pf[:, 2 * GROUP:2 * GROUP + LANES] + fb_ref[...]
    log_f = jnp.minimum(z, 0.0) - jnp.log(1.0 + jnp.exp(-jnp.abs(z)))
    tri = tri_ref[...]
    a1, a2, a3 = _split3(log_f)
    cum = fcarry[...] + (_dot(tri, a1) + _dot(tri, a2) + _dot(tri, a3))
    fcarry[...] = cum[tm - 1:tm, :]
    f_ref[...] = cum
    f1, f2, f3 = _split3(cum)
    xq = jnp.concatenate([fqn, f1, f2, f3], axis=1)
    xk = jnp.concatenate([fkn, f1, f2, f3], axis=1)
    ones_q = jnp.where((lane >= HEAD_DIM + 3) & (lane < HEAD_DIM + 6), 1.0, 0.0)
    ones_k = jnp.where((lane >= HEAD_DIM) & (lane < HEAD_DIM + 3), 1.0, 0.0)
    qf_all = _dot(xq, eq_ref[...])
    kf_all = _dot(xk, ek_ref[...])
    for h in range(HEADS):
        qf_ref[h] = (qf_all[:, h * LANES:(h + 1) * LANES] + ones_q).astype(BF16)
        kf_ref[h] = (kf_all[:, h * LANES:(h + 1) * LANES] + ones_k).astype(BF16)

    pm = _dot(hb, win_ref[:, COLS_MLA + COLS_FOX:COLS_MLA + COLS_FOX + COLS_MOBA])
    mqn = _head_norm(pm[:, 0:GROUP], mqg_ref[...], bd) * (HEAD_DIM ** -0.5)
    mkn = _head_norm(pm[:, GROUP:2 * GROUP], mkg_ref[...], bd)
    vm_ref[...] = _dot_nt(wvt_ref[1], hb).astype(BF16)
    mqb = mqn.astype(BF16)
    mkb = mkn.astype(BF16)
    blocks = tm // MOBA_BLOCK
    blk = lane - HEAD_DIM
    blk_f = blk.astype(F32)
    row_block = jnp.right_shift(lax.broadcasted_iota(jnp.int32, (tm, 1), 0),
                                MOBA_BLOCK.bit_length() - 1)
    own = i * blocks + row_block
    own_f = own.astype(F32)
    past = (blk >= 0) & (blk < own)
    for b in range(blocks):
        col_mean = jnp.mean(mkn[b * MOBA_BLOCK:(b + 1) * MOBA_BLOCK, :], axis=0, keepdims=True)
        for h in range(HEADS):
            kmean[pl.ds(h * LANES + HEAD_DIM + i * blocks + b, 1), :] = jnp.where(
                head_of_lane == h, col_mean, 0.0)
    q_hi, q_lo = _split2(mqn)
    km_hi, km_lo = _split2(kmean[...])
    gate_all = _dot_nt(q_hi, km_hi) + _dot_nt(q_hi, km_lo) + _dot_nt(q_lo, km_hi)
    qm_all = _dot(mqb, sel_ref[...])
    km_all = _dot(mkb, sel_ref[...])
    for h in range(HEADS):
        g = jnp.where(past, gate_all[:, h * LANES:(h + 1) * LANES], NEG)
        chosen = jnp.zeros((tm, LANES), F32)
        for _ in range(MOBA_TOPK):
            m = jnp.max(g, axis=-1, keepdims=True)
            first = jnp.min(jnp.where(g == m, lane_f, 1e9), axis=-1, keepdims=True)
            pick = (lane_f == first) & (m > NEG)
            chosen = jnp.where(pick, 1.0, chosen)
            g = jnp.where(pick, NEG, g)
        slope = 2.0 ** (-(2 * h + 2))
        keep = (chosen > 0.0) | (blk == own)
        bias = jnp.where(keep, (slope * MOBA_BLOCK) * (blk_f - own_f), NEG)
        bias = jnp.where(blk >= 0, bias, 0.0)
        qm_ref[h] = (qm_all[:, h * LANES:(h + 1) * LANES] + bias).astype(BF16)
        onehot = jnp.where(blk == own, 1.0, 0.0)
        km_ref[h] = (km_all[:, h * LANES:(h + 1) * LANES] + onehot).astype(BF16)

    pd = _dot(hb, win_ref[:, COLS_MLA + COLS_FOX + COLS_MOBA:COLS_IN])
    qd_ref[...] = _head_norm(pd[:, 0:GROUP], dqg_ref[...], bd) * (HEAD_DIM ** -0.5)
    kd_ref[...] = _head_norm(pd[:, GROUP:2 * GROUP], dkg_ref[...], bd)
    vd_ref[...] = pd[:, 2 * GROUP:3 * GROUP]


def _prep_constants(seq, tm):
    half = MLA_ROPE // 2
    inv = 1.0 / (ROPE_THETA ** (jnp.arange(half, dtype=F32) / half))
    ang = jnp.arange(seq, dtype=F32)[:, None] * inv[None, :]
    cs = jnp.concatenate([jnp.cos(ang), jnp.sin(ang)], axis=1)
    bd =np.kron(np.eye(HEADS, dtype=np.float32), np.ones((HEAD_DIM, HEAD_DIM), np.float32))
    tri = np.tril(np.ones((tm, tm), np.float32))
    tri_strict = np.tril(np.ones((POST_TILE, POST_TILE), np.float32), -1)
    sel = np.zeros((GROUP, HEADS * LANES), np.float32)
    eq = np.zeros((GROUP + 3 * LANES, HEADS * LANES), np.float32)
    ek = np.zeros((GROUP + 3 * LANES, HEADS * LANES), np.float32)
    for h in range(HEADS):
        for d in range(HEAD_DIM):
            sel[h * HEAD_DIM + d, h * LANES + d] = 1.0
        for piece in range(3):
            eq[GROUP + piece * LANES + h, h * LANES + HEAD_DIM + piece] = 1.0
            ek[GROUP + piece * LANES + h, h * LANES + HEAD_DIM + 3 + piece] = -1.0
    eq[:GROUP] = sel
    ek[:GROUP] = sel
    as_bf = lambda a: jnp.asarray(a, BF16)
    return dict(cs=cs, bd=as_bf(bd),
                tri=as_bf(tri), tri_strict=as_bf(tri_strict), sel=as_bf(sel), eq=as_bf(eq), ek=as_bf(ek))


def _prep(x2, mod_l, consts, p):
    seq = x2.shape[0]
    tm = TOKEN_TILE
    row = lambda n: pl.BlockSpec((tm, n), lambda i: (i, 0))
    heads = pl.BlockSpec((HEADS, tm, LANES), lambda i: (0, i, 0))
    in_arrays = [
        (x2, row(D_MODEL)), (mod_l, _full_spec(mod_l.shape)), (p["g1"], None),
        (p["w_in"], pl.BlockSpec(memory_space=pl.ANY)),
        (p["cq_g"], None), (p["w_uq"], None), (p["ckv_g"], None), (p["w_uk"], None), (p["w_uvt"], None),
        (p["q_g"], None), (p["k_g"], None),
        (consts["cs"], row(MLA_ROPE)),
        (p["fq_g"], None), (p["fk_g"], None), (p["f_b"], None), (p["mq_g"], None), (p["mk_g"], None),
        (p["dq_g"], None), (p["dk_g"], None),
        (consts["bd"], None), (consts["tri"], None), (consts["eq"], None), (consts["ek"], None),
        (consts["sel"], None),
    ]
    args = [a for a, _ in in_arrays]
    specs = [s if s is not None else _full_spec(a.shape) for a, s in in_arrays]
    hshape = jax.ShapeDtypeStruct((HEADS, seq, LANES), BF16)
    vshape = jax.ShapeDtypeStruct((GROUP, seq), BF16)
    dshape = jax.ShapeDtypeStruct((seq, GROUP), F32)
    vt = pl.BlockSpec((GROUP, tm), lambda i: (0, i))
    return pl.pallas_call(
        functools.partial(_prep_kernel, layer=p["layer"]),
        grid=(seq // tm,),
        in_specs=specs,
        out_specs=[heads, heads, vt] * 3 + [row(GROUP)] * 3 + [row(LANES)],
        out_shape=[hshape, hshape, vshape] * 3 + [dshape] * 3
                  + [jax.ShapeDtypeStruct((seq, LANES), F32)],
        scratch_shapes=[pltpu.VMEM((1, LANES), F32), pltpu.VMEM((HEADS * LANES, GROUP), F32),
                        pltpu.VMEM(p["w_in"].shape[1:], F32),
                        pltpu.VMEM((D_MODEL, COLS_IN), BF16),
                        pltpu.VMEM((len(IN_VALUE_COLUMNS), GROUP, D_MODEL), BF16),
                        pltpu.VMEM((3, tm, LANES), F32),
                        pltpu.SemaphoreType.DMA(())],
        compiler_params=_params(("arbitrary",)),
        name="prep",
    )(*args)


def _flash_kernel(jlo_ref, q_ref, k_ref, vt_ref, kbias_ref, o_ref, m_sc, acc_sc, sa_sc, sb_sc,
                  *, tile, use_kbias):
    i = pl.program_id(0)
    m_sc[...] = jnp.full_like(m_sc, -jnp.inf)
    acc_sc[...] = jnp.zeros_like(acc_sc)

    def score(hh, j, buf):
        start = pl.multiple_of(j * tile, tile)
        buf[hh] = _dot_nt(k_ref[hh, pl.ds(start, tile), :], q_ref[hh])

    def absorb(hh, j, buf, causal):
        start = pl.multiple_of(j * tile, tile)
        s = buf[hh]
        if use_kbias:
            s = s + jnp.concatenate([kbias_ref[hh]] * (tile // LANES), axis=1)
        if causal:
            key = lax.broadcasted_iota(jnp.int32, (tile, tile), 0)
            qry = lax.broadcasted_iota(jnp.int32, (tile, tile), 1)
            s = jnp.where(key <= qry, s, NEG)
        m_prev = m_sc[hh]
        m_cur = jnp.max(jnp.max(s.reshape(8, tile // 8, tile), axis=0), axis=0, keepdims=True)
        m_new = jnp.maximum(m_prev, m_cur)
        alpha = jnp.exp(m_prev - m_new)
        p = jnp.exp((s - m_new).astype(BF16))
        vt = vt_ref[hh * HEAD_DIM:(hh + 1) * HEAD_DIM, pl.ds(start, tile)]
        vt = jnp.concatenate([vt, jnp.ones((ONES_ROWS, tile), BF16)], axis=0)
        acc_sc[hh] = alpha * acc_sc[hh] + _dot(vt, p)
        m_sc[hh] = m_new

    first = jlo_ref[i]
    n_off = i - first
    for hh in range(HEADS):
        score(hh, first, sa_sc)

    def body(t, carry):
        j = first + 2 * t
        for hh in range(HEADS):
            score(hh, j + 1, sb_sc)
            absorb(hh, j, sa_sc, False)
        for hh in range(HEADS):
            score(hh, j + 2, sa_sc)
            absorb(hh, j + 1, sb_sc, False)
        return carry

    lax.fori_loop(0, n_off // 2, body, 0)

    @pl.when(n_off % 2 == 1)
    def _():
        for hh in range(HEADS):
            score(hh, i, sb_sc)
            absorb(hh, i - 1, sa_sc, False)
        for hh in range(HEADS):
            absorb(hh, i, sb_sc, True)

    @pl.when(n_off % 2 == 0)
    def _():
        for hh in range(HEADS):
            absorb(hh, i, sa_sc, True)

    o_t = jnp.concatenate([acc_sc[hh, 0:HEAD_DIM, :] / acc_sc[hh, HEAD_DIM:HEAD_DIM + 1, :]
                           for hh in range(HEADS)], axis=0)
    o_ref[...] = o_t.T.astype(o_ref.dtype)


def _flash(q, k, v_t, kbias, first_tile, use_kbias):
    seq = v_t.shape[1]
    tile = FLASH_TILE
    kern = functools.partial(_flash_kernel, tile=tile, use_kbias=use_kbias)
    resident = pl.Buffered(1)
    return pl.pallas_call(
        kern,
        grid_spec=pltpu.PrefetchScalarGridSpec(
            num_scalar_prefetch=1,
            grid=(seq // tile,),
            in_specs=[
                pl.BlockSpec((HEADS, tile, LANES), lambda i, f: (0, i, 0)),
                pl.BlockSpec((HEADS, seq, LANES), lambda i, f: (0, 0, 0), pipeline_mode=resident),
                pl.BlockSpec((GROUP, seq), lambda i, f: (0, 0), pipeline_mode=resident),
                pl.BlockSpec((HEADS, tile, LANES), lambda i, f: (0, 0, 0)),
            ],
            out_specs=pl.BlockSpec((tile, GROUP), lambda i, f: (i, 0)),
            scratch_shapes=[pltpu.VMEM((HEADS, 1, tile), F32),
                            pltpu.VMEM((HEADS, HEAD_DIM + ONES_ROWS, tile), F32),
                            pltpu.VMEM((HEADS, tile, tile), F32), pltpu.VMEM((HEADS, tile, tile), F32)],
        ),
        out_shape=jax.ShapeDtypeStruct((seq, GROUP), BF16),
        compiler_params=_params(("arbitrary",)),
        name="flash_kbias" if use_kbias else "flash",
    )(first_tile, q, k, v_t, kbias)


def _fox_first_tile(decay, qk_bound):
    seq = decay.shape[0]
    nq = seq // FLASH_TILE
    f = decay[:, :HEADS]
    f_first = f[0::FLASH_TILE]
    f_last = f[FLASH_TILE - 1::FLASH_TILE]
    gap = f_first[:, None, :] - f_last[None, :, :] + 2.0 * qk_bound
    jj = jnp.arange(nq, dtype=jnp.int32)
    needed = (gap >= FOX_SKIP_LOG) | (jj[None, :, None] >= jj[:, None, None])
    first = jnp.min(jnp.where(needed, jj[None, :, None], nq), axis=1)
    return jnp.min(first, axis=1).astype(jnp.int32)


def _dilated_kernel(q_ref, k_ref, v_ref, o_ref, kbuf, vbuf, acc_s, m_s, l_s):
    pair = pl.program_id(0)
    i = pl.program_id(1)
    T = q_ref.shape[0]

    @pl.when(i == 0)
    def _():
        kbuf[...] = jnp.zeros_like(kbuf)
        vbuf[...] = jnp.zeros_like(vbuf)

    kbuf[0:T, :] = kbuf[T:2 * T, :]
    vbuf[0:T, :] = vbuf[T:2 * T, :]
    kbuf[T:2 * T, :] = k_ref[...]
    vbuf[T:2 * T, :] = v_ref[...]

    ii = lax.broadcasted_iota(jnp.int32, (DIL_SPAN, 2 * DIL_SPAN), 0)
    jj = lax.broadcasted_iota(jnp.int32, (DIL_SPAN, 2 * DIL_SPAN), 1)
    dist = ii + DIL_SPAN - jj
    band = (dist >= 0) & (dist <= DIL_SPAN)
    dist_f = dist.astype(F32)
    upper = lax.broadcasted_iota(jnp.int32, (1, LANES), 1) >= HEAD_DIM

    for pi, (window, r) in enumerate(DIL_PATTERNS):
        assert window // r == DIL_SPAN
        sub = DIL_SPAN * r

        def body(idx, carry, r=r, sub=sub, pi=pi):
            n = idx // r
            rho = idx - n * r
            base = n * sub + rho
            q = q_ref[pl.ds(base, DIL_SPAN, stride=r), :]
            kc = kbuf[pl.ds(T + base - sub, 2 * DIL_SPAN, stride=r), :].astype(BF16)
            vc = vbuf[pl.ds(T + base - sub, 2 * DIL_SPAN, stride=r), :].astype(BF16)
            first_key = jnp.where((i == 0) & (n == 0), DIL_SPAN, 0)
            valid = band & (jj >= first_key)
            stats = []
            for hh in range(2):
                slope = jnp.where(pair == 0, 2.0 ** (-(2 * hh + 1)), 2.0 ** (-(2 * hh + 5)))
                qh = (jnp.where(upper, q, 0.0) if hh else jnp.where(upper, 0.0, q)).astype(BF16)
                s = _dot_nt(qh, kc) - (slope * r) * dist_f
                s = jnp.where(valid, s, NEG)
                m = jnp.max(s, axis=-1, keepdims=True)
                p = jnp.exp(s - m)
                l = jnp.sum(p, axis=-1, keepdims=True)
                stats.append((_dot(p.astype(BF16), vc), m, l))
            rows = pl.ds(pi * T + base, DIL_SPAN, stride=r)
            acc_s[rows, :] = jnp.where(upper, stats[1][0], stats[0][0])
            m_s[rows, :] = jnp.where(upper, stats[1][1], stats[0][1])
            l_s[rows, :] = jnp.where(upper, stats[1][2], stats[0][2])
            return carry

        lax.fori_loop(0, T // DIL_SPAN, body, 0, unroll=True)

    npat = len(DIL_PATTERNS)
    ms = [m_s[pi * T:(pi + 1) * T, :] for pi in range(npat)]
    m_top = functools.reduce(jnp.maximum, ms)
    num = jnp.zeros((T, LANES), F32)
    den = jnp.zeros((T, LANES), F32)
    for pi in range(npat):
        w = jnp.exp(ms[pi] - m_top)
        num = num + w * acc_s[pi * T:(pi + 1) * T, :]
        den = den + w * l_s[pi * T:(pi + 1) * T, :]
    o_ref[...] = (num / den).astype(o_ref.dtype)


def _dilated(q, k, v):
    seq = q.shape[0]
    T = DIL_TILE
    spec = pl.BlockSpec((T, LANES), lambda p, i: (i, p))
    npat = len(DIL_PATTERNS)
    return pl.pallas_call(
        _dilated_kernel,
        grid=(HEADS // 2, seq // T),
        in_specs=[spec, spec, spec],
        out_specs=spec,
        out_shape=jax.ShapeDtypeStruct((seq, GROUP), BF16),
        scratch_shapes=[pltpu.VMEM((2 * T, LANES), F32), pltpu.VMEM((2 * T, LANES), F32),
                        pltpu.VMEM((npat * T, LANES), F32), pltpu.VMEM((npat * T, LANES), F32),
                        pltpu.VMEM((npat * T, LANES), F32)],
        compiler_params=_params(("arbitrary", "arbitrary")),
        name="dilated",
    )(q, k, v)


def _post_kernel(x_ref, oa_ref, ob_ref, oc_ref, od_ref, wout_ref, mod_ref, g2_ref,
                 rw_ref, rb_ref, tri_ref,
                 x1_ref, h2_ref, eidx_ref, gate_ref, rank_ref, cnt_ref, carry):
    i = pl.program_id(0)
    tm = x_ref.shape[0]

    @pl.when(i == 0)
    def _():
        carry[...] = jnp.zeros_like(carry)

    o = (_dot(oa_ref[...], wout_ref[0]) + _dot(ob_ref[...], wout_ref[1])
         + _dot(oc_ref[...], wout_ref[2]) + _dot(od_ref[...], wout_ref[3]))
    x1 = x_ref[...] + mod_ref[2] * o
    x1_ref[...] = x1
    y = x1 * lax.rsqrt(jnp.mean(x1 * x1, axis=-1, keepdims=True) + EPS) * g2_ref[...]
    h2 = y * (1.0 + mod_ref[4]) + mod_ref[3]
    _store_token_tiles(h2_ref, h2)

    h_hi, h_lo = _split2(h2)
    hh = _dot(h_hi, rw_ref[...])
    logits = (hh[:, :LANES] + hh[:, LANES:] + _dot(h_lo, rw_ref[:, :LANES])
              + rb_ref[...])
    lane = lax.broadcasted_iota(jnp.int32, (tm, LANES), 1)
    lane_f = lane.astype(F32)
    g = logits
    chosen = jnp.zeros((tm, LANES), F32)
    vals, idxs = [], []
    for _ in range(TOP_K):
        m = jnp.max(g, axis=-1, keepdims=True)
        first = jnp.min(jnp.where(g == m, lane_f, 1e9), axis=-1, keepdims=True)
        pick = lane_f == first
        chosen = jnp.where(pick, 1.0, chosen)
        g = jnp.where(pick, -jnp.inf, g)
        vals.append(m)
        idxs.append(first)
    exps = [jnp.exp(v - vals[0]) for v in vals]
    den = exps[0] + exps[1] + exps[2] + exps[3]
    before = _dot(tri_ref[...], chosen.astype(BF16)) + carry[...]
    carry[...] = carry[...] + jnp.sum(chosen, axis=0, keepdims=True)
    cnt_ref[...] = carry[...]
    e_out = jnp.zeros((tm, LANES), F32)
    g_out = jnp.zeros((tm, LANES), F32)
    r_out = jnp.zeros((tm, LANES), F32)
    for k in range(TOP_K):
        rank_k = jnp.sum(jnp.where(lane_f == idxs[k], before, 0.0), axis=-1, keepdims=True)
        e_out = jnp.where(lane == k, idxs[k], e_out)
        g_out = jnp.where(lane == k, exps[k] / den, g_out)
        r_out = jnp.where(lane == k, rank_k, r_out)
    eidx_ref[...] = e_out.T[0:8, :].astype(jnp.int32)
    gate_ref[...] = g_out
    rank_ref[...] = r_out.T[0:8, :].astype(jnp.int32)


def _post(x2, oa, ob, oc, od, mod_l, consts, p):
    seq = x2.shape[0]
    tm = POST_TILE
    row = lambda n: pl.BlockSpec((tm, n), lambda i: (i, 0))
    full = [p["w_out"], mod_l, p["g2"], p["rw"], p["r_b"], consts["tri_strict"]]
    f32 = lambda n: jax.ShapeDtypeStruct((seq, n), F32)
    slots = pl.BlockSpec((8, tm), lambda i: (0, i))
    slots_shape = jax.ShapeDtypeStruct((8, seq), jnp.int32)
    return pl.pallas_call(
        _post_kernel,
        grid=(seq // tm,),
        in_specs=[row(D_MODEL)] + [row(GROUP)] * 4 + [_full_spec(a.shape) for a in full],
        out_specs=[row(D_MODEL), pl.BlockSpec((tm * CHUNKS, LANES), lambda i: (i, 0)),
                   slots, row(LANES), slots, _full_spec((1, LANES))],
        out_shape=[f32(D_MODEL), jax.ShapeDtypeStruct((seq * CHUNKS, LANES), F32),
                   slots_shape, f32(LANES), slots_shape, jax.ShapeDtypeStruct((1, LANES), F32)],
        scratch_shapes=[pltpu.VMEM((1, LANES), F32)],
        compiler_params=_params(("arbitrary",)),
        name="post",
    )(x2, oa, ob, oc, od, *full)


def _tile_copy(src, s, dst, d, sem):
    return pltpu.make_async_copy(src.at[pl.ds(pl.multiple_of(s * CHUNKS, CHUNKS), CHUNKS), :],
                                 dst.at[pl.ds(pl.multiple_of(d * CHUNKS, CHUNKS), CHUNKS), :], sem)


def _dispatch_kernel(dest_ref, padlo_ref, padn_ref, nu_ref, h_ref, xs_ref, zbuf, sem, zsem):
    i = pl.program_id(0)
    n = ROUTE_TILE * TOP_K
    base = i * ROUTE_TILE
    seq = pl.num_programs(0) * ROUTE_TILE
    block_rows = EXPERT_ROWS * CHUNKS

    @pl.when(i == 0)
    def _():
        zbuf[...] = jnp.zeros_like(zbuf)

        def fill(wait):
            def go(copy):
                copy.wait() if wait else copy.start()

            def per_expert(e, carry):
                lo, cnt = padlo_ref[e], padn_ref[e]
                off = lo
                p = EXPERT_ROWS // 2
                while p >= 1:
                    rows = p * CHUNKS

                    @pl.when((cnt & p) != 0)
                    def _(off=off, rows=rows):
                        go(pltpu.make_async_copy(
                            zbuf.at[pl.ds(0, rows), :],
                            xs_ref.at[pl.ds(pl.multiple_of(off * CHUNKS, CHUNKS), rows), :], zsem))

                    off = off + (cnt & p)
                    p //= 2
                return carry

            lax.fori_loop(0, N_EXPERTS, per_expert, 0)

            def per_block(b, carry):
                go(pltpu.make_async_copy(
                    zbuf, xs_ref.at[pl.ds(pl.multiple_of(b * block_rows, block_rows), block_rows), :],
                    zsem))
                return carry

            lax.fori_loop(nu_ref[0], xs_ref.shape[0] // block_rows, per_block, 0)

        fill(False)
        fill(True)

    def issue(r, carry):
        for k in range(TOP_K):
            _tile_copy(h_ref, r, xs_ref, dest_ref[k * seq + base + r], sem).start(priority=k % 2)
        return carry

    lax.fori_loop(0, ROUTE_TILE, issue, 0, unroll=4)
    rows = pl.ds(0, n * CHUNKS)
    pltpu.make_async_copy(xs_ref.at[rows, :], xs_ref.at[rows, :], sem).wait()


def _dispatch(dest, pad_lo, pad_n, n_used, h2_tiles, m_pad):
    seq = h2_tiles.shape[0] // CHUNKS
    return pl.pallas_call(
        _dispatch_kernel,
        grid_spec=pltpu.PrefetchScalarGridSpec(
            num_scalar_prefetch=4,
            grid=(seq // ROUTE_TILE,),
            in_specs=[pl.BlockSpec((ROUTE_TILE * CHUNKS, LANES), lambda i, *_: (i, 0))],
            out_specs=pl.BlockSpec(memory_space=pl.ANY),
            scratch_shapes=[pltpu.VMEM((EXPERT_ROWS * CHUNKS, LANES), F32),
                            pltpu.SemaphoreType.DMA(()), pltpu.SemaphoreType.DMA(())],
        ),
        out_shape=jax.ShapeDtypeStruct((m_pad * CHUNKS, LANES), F32),
        compiler_params=_params(("arbitrary",)),
        name="dispatch",
    )(dest, pad_lo, pad_n, n_used, h2_tiles)


def _expert_kernel(be_ref, nu_ref, nxt_ref, ord_ref, xs_ref, w1_hbm, b1_ref, w2_hbm, b2_ref, ys_ref,
                   w1f, w2f, w1b, w2b, sems):
    b = pl.program_id(0)
    e = be_ref[b]
    prev = be_ref[jnp.maximum(b - 1, 0)]
    fresh = ((b == 0) | (e != prev)) & (b < nu_ref[0])
    slot = ord_ref[b] % 2

    def fetch(expert, to_slot):
        return (pltpu.make_async_copy(w1_hbm.at[expert], w1f.at[to_slot], sems.at[0, to_slot]),
                pltpu.make_async_copy(w2_hbm.at[expert], w2f.at[to_slot], sems.at[1, to_slot]))

    @pl.when(b == 0)
    def _():
        for copy in fetch(e, slot):
            copy.start()

    @pl.when(fresh)
    def _():
        for copy in fetch(e, slot):
            copy.wait()
        w1b[...] = w1f[slot].astype(BF16)
        w2b[...] = w2f[slot].astype(BF16)

        @pl.when(nxt_ref[b] >= 0)
        def _():
            for copy in fetch(nxt_ref[b], 1 - slot):
                copy.start()

    @pl.when(b < nu_ref[0])
    def _():
        half = EXPERT_ROWS // 2
        gus = []
        for r in range(2):
            xb = jnp.concatenate(
                [_load_token_chunk(xs_ref, half, c, offset=r * half * CHUNKS) for c in range(CHUNKS)],
                axis=1).astype(BF16)
            gus.append(_dot(xb, w1b[...]) + b1_ref[0])
        for r in range(2):
            g = jnp.minimum(gus[r][:, :D_EXPERT], SWIGLU_LIMIT)
            u = jnp.clip(gus[r][:, D_EXPERT:], -SWIGLU_LIMIT, SWIGLU_LIMIT)
            y = (u + 1.0) * g * (1.0 / (1.0 + jnp.exp(-SWIGLU_ALPHA * g)))
            _store_token_tiles(ys_ref, _dot(y.astype(BF16), w2b[...]) + b2_ref[0],
                               offset=r * half * CHUNKS)

    @pl.when(b >= nu_ref[0])
    def _():
        ys_ref[...] = jnp.zeros_like(ys_ref)


def _experts(blk_expert, n_used, blk_next, blk_ord, xs, w1, b1, w2, b2):
    m_pad = xs.shape[0] // CHUNKS
    bm = EXPERT_ROWS
    n_all = w1.shape[0] * w1.shape[1]
    rows = lambda b, be, nu, *_: (jnp.minimum(b, nu[0] - 1), 0)
    ex = lambda b, be, nu, *_: (be[jnp.minimum(b, nu[0] - 1)], 0, 0)
    return pl.pallas_call(
        _expert_kernel,
        grid_spec=pltpu.PrefetchScalarGridSpec(
            num_scalar_prefetch=4,
            grid=(m_pad // bm,),
            in_specs=[
                pl.BlockSpec((bm * CHUNKS, LANES), rows),
                pl.BlockSpec(memory_space=pl.ANY),
                pl.BlockSpec((1, 1, 2 * D_EXPERT), ex),
                pl.BlockSpec(memory_space=pl.ANY),
                pl.BlockSpec((1, 1, D_MODEL), ex),
            ],
            out_specs=pl.BlockSpec((bm * CHUNKS, LANES), lambda b, *_: (b, 0)),
            scratch_shapes=[pltpu.VMEM((2, D_MODEL, 2 * D_EXPERT), F32),
                            pltpu.VMEM((2, D_EXPERT, D_MODEL), F32),
                            pltpu.VMEM((D_MODEL, 2 * D_EXPERT), BF16),
                            pltpu.VMEM((D_EXPERT, D_MODEL), BF16),
                            pltpu.SemaphoreType.DMA((2, 2))],
        ),
        out_shape=jax.ShapeDtypeStruct((m_pad * CHUNKS, LANES), F32),
        compiler_params=_params(("arbitrary",)),
        name="experts",
    )(blk_expert, n_used, blk_next, blk_ord, xs,
      w1.reshape(n_all, D_MODEL, 2 * D_EXPERT), b1.reshape(n_all, 1, -1),
      w2.reshape(n_all, D_EXPERT, D_MODEL), b2.reshape(n_all, 1, -1))


def _combine_kernel(dest_ref, ys_ref, x1_ref, gate_ref, mod_ref, o_ref, buf, sems):
    i = pl.program_id(0)
    tm = x1_ref.shape[0]
    n = tm * TOP_K
    slot = i % 2

    seq = pl.num_programs(0) * tm

    def gather(step, to_slot):
        base = step * tm

        def issue(r, carry):
            for k in range(TOP_K):
                _tile_copy(ys_ref, dest_ref[k * seq + base + r], buf, to_slot * n + k * tm + r,
                           sems.at[to_slot]).start(priority=k % 2)
            return carry

        lax.fori_loop(0, tm, issue, 0, unroll=4)

    @pl.when(i == 0)
    def _():
        gather(0, 0)

    @pl.when(i + 1 < pl.num_programs(0))
    def _():
        gather(i + 1, 1 - slot)

    mine = pl.ds(pl.multiple_of(slot * n * CHUNKS, n * CHUNKS), n * CHUNKS)
    pltpu.make_async_copy(ys_ref.at[pl.ds(0, n * CHUNKS), :], buf.at[mine, :], sems.at[slot]).wait()
    gates = gate_ref[...]
    g2 = mod_ref[5]
    for c in range(CHUNKS):
        cols = slice(c * LANES, (c + 1) * LANES)
        mix = jnp.zeros((tm, LANES), F32)
        for k in range(TOP_K):
            mix = mix + gates[:, k:k + 1] * _load_token_chunk(
                buf, tm, c, offset=(slot * n + k * tm) * CHUNKS)
        o_ref[:, cols] = x1_ref[:, cols] + g2[:, cols] * mix


def _combine(dest, ys, x1, gates, mod_l):
    seq = x1.shape[0]
    tm = ROUTE_TILE
    return pl.pallas_call(
        _combine_kernel,
        grid_spec=pltpu.PrefetchScalarGridSpec(
            num_scalar_prefetch=1,
            grid=(seq // tm,),
            in_specs=[
                pl.BlockSpec(memory_space=pl.ANY),
                pl.BlockSpec((tm, D_MODEL), lambda i, d: (i, 0)),
                pl.BlockSpec((tm, LANES), lambda i, d: (i, 0)),
                pl.BlockSpec(mod_l.shape, lambda i, d: (0, 0, 0)),
            ],
            out_specs=pl.BlockSpec((tm, D_MODEL), lambda i, d: (i, 0)),
            scratch_shapes=[pltpu.VMEM((2 * TOP_K * tm * CHUNKS, LANES), F32),
                            pltpu.SemaphoreType.DMA((2,))],
        ),
        out_shape=jax.ShapeDtypeStruct((seq, D_MODEL), F32),
        compiler_params=_params(("arbitrary",)),
        name="combine",
    )(dest, ys, x1, gates, mod_l)


def _moe(l, x1, h2_tiles, eidx, gates, rank, counts, mod_l, w1, b1, w2, b2):
    seq = x1.shape[0]
    bm = EXPERT_ROWS
    m_pad = seq * TOP_K + N_EXPERTS * bm
    cnt = counts[0, :N_EXPERTS].astype(jnp.int32)
    padded = (cnt + bm - 1) // bm * bm
    pad_end = jnp.cumsum(padded)
    pad_start = pad_end - padded
    onehot = eidx[:TOP_K, :, None] == jnp.arange(N_EXPERTS, dtype=jnp.int32)
    start_of = jnp.sum(jnp.where(onehot, pad_start, 0), axis=-1)
    dest = (start_of + rank[:TOP_K]).reshape(TOP_K * seq).astype(jnp.int32)
    nblk = m_pad // bm
    blk_start = jnp.arange(nblk, dtype=jnp.int32) * bm
    local = jnp.minimum(jnp.sum(pad_end[None, :] <= blk_start[:, None], axis=1), N_EXPERTS - 1)
    blk_expert = (local + l * N_EXPERTS).astype(jnp.int32)
    n_used = (pad_end[-1:] // bm).astype(jnp.int32)
    ids = jnp.arange(N_EXPERTS, dtype=jnp.int32)
    live = padded > 0
    ordinal = jnp.cumsum(live.astype(jnp.int32)) - 1
    later = live[None, :] & (ids[None, :] > ids[:, None])
    nxt = jnp.min(jnp.where(later, ids[None, :], N_EXPERTS), axis=1)
    nxt = jnp.where(nxt < N_EXPERTS, nxt + l * N_EXPERTS, -1)
    of_block = local[:, None] == ids[None, :]
    blk_next = jnp.sum(jnp.where(of_block, nxt[None, :], 0), axis=1).astype(jnp.int32)
    blk_ord = jnp.sum(jnp.where(of_block, ordinal[None, :], 0), axis=1).astype(jnp.int32)
    xs = _dispatch(dest, (pad_start + cnt).astype(jnp.int32), (padded - cnt).astype(jnp.int32),
                   n_used, h2_tiles, m_pad)
    ys = _experts(blk_expert, n_used, blk_next, blk_ord, xs, w1, b1, w2, b2)
    return _combine(dest, ys, x1, gates, mod_l)


def _pad_cols(a, n):
    return jnp.pad(a, ((0, 0), (0, n - a.shape[1])))


def _layer_params(l, w_in, mla_cq_g, mla_w_uq, mla_ckv_g, mla_w_ukv, mla_q_g, mla_k_g,
                  fox_q_g, fox_k_g, fox_b_f, moba_q_g, moba_k_g, dil_q_g, dil_k_g, w_out,
                  norm1_g, norm2_g, router_w, router_b):
    w_uq = jnp.pad(mla_w_uq[l].reshape(MLA_Q_RANK, HEADS, MLA_QK),
                   ((0, 0), (0, 0), (0, LANES - MLA_QK))).reshape(MLA_Q_RANK, HEADS * LANES)
    w_ukv = mla_w_ukv[l].reshape(MLA_KV_RANK, HEADS, MLA_NOPE + HEAD_DIM)
    w_uk = jnp.pad(w_ukv[:, :, :MLA_NOPE], ((0, 0), (0, 0), (0, LANES - MLA_NOPE)))
    w_uv = w_ukv[:, :, MLA_NOPE:]
    tile4 = lambda g: jnp.tile(g, HEADS)[None, :]
    rw = _pad_cols(router_w[l], LANES)
    rw_hi = rw.astype(BF16)
    rw_lo = (rw - rw_hi.astype(F32)).astype(BF16)
    r_b = jnp.concatenate([router_b[l], jnp.full((LANES - N_EXPERTS,), NEG, F32)])[None, :]
    return dict(
        g1=norm1_g[l][None, :], g2=norm2_g[l][None, :], w_in=w_in, layer=l,
        cq_g=mla_cq_g[l][None, :], w_uq=w_uq.astype(BF16), ckv_g=mla_ckv_g[l][None, :],
        w_uk=w_uk.reshape(MLA_KV_RANK, HEADS * LANES).astype(BF16),
        w_uvt=w_uv.reshape(MLA_KV_RANK, GROUP).T.astype(BF16),
        fox_bound=FOX_NORM_SLACK * HEAD_DIM ** 0.5 * jnp.max(jnp.abs(fox_q_g[l]))
        * jnp.max(jnp.abs(fox_k_g[l])),
        q_g=_pad_cols(mla_q_g[l][None, :], LANES), k_g=_pad_cols(mla_k_g[l][None, :], LANES),
        fq_g=tile4(fox_q_g[l]), fk_g=tile4(fox_k_g[l]), f_b=_pad_cols(fox_b_f[l][None, :], LANES),
        mq_g=tile4(moba_q_g[l]), mk_g=tile4(moba_k_g[l]), dq_g=tile4(dil_q_g[l]), dk_g=tile4(dil_k_g[l]),
        w_out=w_out[l].reshape(HEADS, GROUP, D_MODEL).astype(BF16),
        rw=jnp.concatenate([rw_hi, rw_lo], axis=1), r_b=r_b,
    )


def kernel(x, c, w_mod, b_mod, norm1_g, norm2_g, w_in, mla_cq_g, mla_w_uq, mla_ckv_g, mla_w_ukv, mla_q_g, mla_k_g, fox_q_g, fox_k_g, fox_b_f, moba_q_g, moba_k_g, dil_q_g, dil_k_g, w_out, router_w, router_b, exp_w1, exp_b1, exp_w2, exp_b2):
    batch, seq, d = x.shape
    assert batch == 1 and d == D_MODEL
    assert seq % DIL_TILE == 0 and seq // MOBA_BLOCK <= MOBA_MAX_BLOCKS
    depth = w_mod.shape[0]
    consts = _prep_constants(seq, TOKEN_TILE)
    mod = _modulation(c, w_mod, b_mod)
    slopes_c = 2.0 ** (-(2.0 * np.arange(HEADS) + 2.0))
    in_block = np.arange(FLASH_TILE) % MOBA_BLOCK
    kbias_c = jnp.asarray(np.broadcast_to(slopes_c[:, None, None] * in_block[None, :, None],
                                          (HEADS, FLASH_TILE, LANES)), F32)
    kbias_0 = jnp.zeros((HEADS, FLASH_TILE, LANES), F32)
    all_tiles = jnp.zeros((seq // FLASH_TILE,), jnp.int32)
    x2 = x.reshape(seq, d)
    for l in range(depth):
        p = _layer_params(l, w_in, mla_cq_g, mla_w_uq, mla_ckv_g, mla_w_ukv, mla_q_g, mla_k_g,
                          fox_q_g, fox_k_g, fox_b_f, moba_q_g, moba_k_g, dil_q_g, dil_k_g, w_out,
                          norm1_g, norm2_g, router_w, router_b)
        mod_l = mod[l]
        qa, ka, va, qf, kf, vf, qm, km, vm, qd, kd, vd, decay = _prep(x2, mod_l, consts, p)
        oa = _flash(qa, ka, va, kbias_0, all_tiles, False)
        ob = _flash(qf, kf, vf, kbias_0, _fox_first_tile(decay, p["fox_bound"]), False)
        oc = _flash(qm, km, vm, kbias_c, all_tiles, True)
        od = _dilated(qd, kd, vd)
        x1, h2, eidx, gates, rank, counts = _post(x2, oa, ob, oc, od, mod_l, consts, p)
        x2 = _moe(l, x1, h2, eidx, gates, rank, counts, mod_l, exp_w1, exp_b1, exp_w2, exp_b2)
    return x2.reshape(batch, seq, d)
```

```python
import functools

import numpy as np
import jax
import jax.numpy as jnp
from jax import lax
from jax.experimental import pallas as pl
from jax.experimental.pallas import tpu as pltpu

F32 = jnp.float32
BF16 = jnp.bfloat16

D_MODEL = 1024
HEAD_DIM = 64
HEADS = 4
GROUP = HEADS * HEAD_DIM
LANES = 128
CHUNKS = D_MODEL // LANES
MLA_Q_RANK = 256
MLA_KV_RANK = 128
MLA_NOPE = 64
MLA_ROPE = 32
MLA_QK = MLA_NOPE + MLA_ROPE
ROPE_THETA = 10000.0
MOBA_BLOCK = 256
MOBA_TOPK = 3
MOBA_MAX_BLOCKS = 64
DIL_PATTERNS = ((128, 1), (512, 4), (2048, 16))
DIL_SPAN = 128
DIL_TILE = 2048
N_EXPERTS = 32
TOP_K = 4
D_EXPERT = 1024
SWIGLU_LIMIT = 7.0
SWIGLU_ALPHA = 1.702
EPS = 1e-6
NEG = -1e30

FLASH_TILE = 512
ONES_ROWS = 16
FOX_SKIP_LOG = -106.0
FOX_NORM_SLACK = 1.02
TOKEN_TILE = 512
POST_TILE = 512
ROUTE_TILE = 512
EXPERT_ROWS = 512
VMEM_LIMIT = 56 * 1024 * 1024

COLS_MLA = 512
COLS_FOX = 640
COLS_MOBA = 512
COLS_DIL = 768
COLS_IN = COLS_MLA + COLS_FOX + COLS_MOBA + COLS_DIL


def _in_column_map():
    sizes = [MLA_Q_RANK, MLA_KV_RANK, MLA_ROPE, GROUP, GROUP, GROUP, HEADS] + [GROUP] * 6
    src = np.concatenate([[0], np.cumsum(sizes)])
    cq, ckv, kr, fq, fk, fv, flog, mq, mk, mv, dq, dk, dv = range(13)
    layout = [(cq, 0), (ckv, MLA_Q_RANK), (kr, MLA_Q_RANK + MLA_KV_RANK + MLA_NOPE),
              (fq, COLS_MLA), (fk, COLS_MLA + GROUP), (flog, COLS_MLA + 2 * GROUP),
              (mq, COLS_MLA + COLS_FOX), (mk, COLS_MLA + COLS_FOX + GROUP),
              (dq, COLS_IN - 3 * GROUP), (dk, COLS_IN - 2 * GROUP), (dv, COLS_IN - GROUP)]
    moves = tuple((dst, int(src[piece]), sizes[piece]) for piece, dst in layout)
    return moves, (int(src[fv]), int(src[mv]))


IN_COLUMN_MAP, IN_VALUE_COLUMNS = _in_column_map()


def _dot(a, b):
    return jnp.dot(a, b, preferred_element_type=F32)


def _dot_nt(a, b):
    return lax.dot_general(a, b, (((1,), (1,)), ((), ())), preferred_element_type=F32)


def _split2(x):
    hi = x.astype(BF16)
    lo = (x - hi.astype(F32)).astype(BF16)
    return hi, lo


def _split3(x):
    a = x.astype(BF16)
    r = x - a.astype(F32)
    b = r.astype(BF16)
    c = (r - b.astype(F32)).astype(BF16)
    return a, b, c


def _head_of_lane():
    return jnp.right_shift(lax.broadcasted_iota(jnp.int32, (1, GROUP), 1), 6)


def _store_token_tiles(ref, x, offset=0):
    n = x.shape[0]
    for c in range(CHUNKS):
        ref[pl.ds(offset + c, n, stride=CHUNKS), :] = x[:, c * LANES:(c + 1) * LANES]


def _load_token_chunk(ref, n, c, offset=0):
    return ref[pl.ds(offset + c, n, stride=CHUNKS), :]


def _full_spec(shape):
    nd = len(shape)
    return pl.BlockSpec(shape, lambda *_: (0,) * nd)


def _params(sem):
    return pltpu.CompilerParams(dimension_semantics=sem, vmem_limit_bytes=VMEM_LIMIT)


def _mod_kernel(c_ref, w_ref, b_ref, o_ref):
    c = c_ref[...]
    s = c * (1.0 / (1.0 + jnp.exp(-c)))
    s8 = jnp.broadcast_to(s, (8, D_MODEL))
    r = jnp.dot(s8, w_ref[0], preferred_element_type=F32, precision=lax.Precision.HIGHEST)
    o_ref[0, 0] = r[0:1, :] + b_ref[0, 0]


def _modulation(c, w_mod, b_mod):
    depth = w_mod.shape[0]
    b4 = b_mod.reshape(depth, 6, 1, D_MODEL)
    return pl.pallas_call(
        _mod_kernel,
        grid=(depth, 6),
        in_specs=[
            pl.BlockSpec((1, D_MODEL), lambda l, j: (0, 0)),
            pl.BlockSpec((1, D_MODEL, D_MODEL), lambda l, j: (l, 0, j)),
            pl.BlockSpec((1, 1, 1, D_MODEL), lambda l, j: (l, j, 0, 0)),
        ],
        out_specs=pl.BlockSpec((1, 1, 1, D_MODEL), lambda l, j: (l, j, 0, 0)),
        out_shape=jax.ShapeDtypeStruct((depth, 6, 1, D_MODEL), F32),
        compiler_params=_params(("arbitrary", "arbitrary")),
        name="modulation",
    )(c, w_mod, b4)


def _head_norm(x, g, bd):
    hi, lo = _split2(x * x)
    ss = _dot(hi, bd) + _dot(lo, bd)
    return x * lax.rsqrt(ss * (1.0 / HEAD_DIM) + EPS) * g


def _prep_kernel(x_ref, mod_ref, g1_ref, win_hbm, cqg_ref, wuq_ref, ckvg_ref, wuk_ref,
                 wuvt_ref, qg_ref, kg_ref, cs_ref,
                 fqg_ref, fkg_ref, fb_ref, mqg_ref, mkg_ref, dqg_ref, dkg_ref,
                 bd_ref, tri_ref, eq_ref, ek_ref, sel_ref,
                 qa_ref, ka_ref, va_ref, qf_ref, kf_ref, vf_ref, qm_ref, km_ref, vm_ref,
                 qd_ref, kd_ref, vd_ref, f_ref,
                 fcarry, kmean, wraw_ref, win_ref, wvt_ref, rope_scr, wsem, *, layer):
    i = pl.program_id(0)
    tm = x_ref.shape[0]

    @pl.when(i == 0)
    def _():
        fetch = pltpu.make_async_copy(win_hbm.at[pl.ds(layer * D_MODEL, D_MODEL), :], wraw_ref, wsem)
        fetch.start()
        fcarry[...] = jnp.zeros_like(fcarry)
        kmean[...] = jnp.zeros_like(kmean)
        win_ref[...] = jnp.zeros_like(win_ref)
        rope_scr[...] = jnp.zeros_like(rope_scr)
        rope_scr[0, :, 0:MLA_NOPE] = jnp.ones((tm, MLA_NOPE), F32)
        fetch.wait()
        for dst, src, width in IN_COLUMN_MAP:
            win_ref[:, dst:dst + width] = wraw_ref[:, src:src + width].astype(BF16)
        for slot, src in enumerate(IN_VALUE_COLUMNS):
            wvt_ref[slot] = wraw_ref[:, src:src + GROUP].T.astype(BF16)

    x = x_ref[...]
    y = x * lax.rsqrt(jnp.mean(x * x, axis=-1, keepdims=True) + EPS) * g1_ref[...]
    hb = (y * (1.0 + mod_ref[1]) + mod_ref[0]).astype(BF16)
    bd = bd_ref[...]
    lane = lax.broadcasted_iota(jnp.int32, (1, LANES), 1)
    lane_f = lane.astype(F32)
    head_of_lane = _head_of_lane()

    pa = _dot(hb, win_ref[:, 0:COLS_MLA])
    cq = pa[:, 0:MLA_Q_RANK]
    ckv = pa[:, MLA_Q_RANK:MLA_Q_RANK + MLA_KV_RANK]
    kr = pa[:, MLA_Q_RANK + MLA_KV_RANK:COLS_MLA]
    cqn = (cq * lax.rsqrt(jnp.mean(cq * cq, axis=-1, keepdims=True) + EPS) * cqg_ref[...]).astype(BF16)
    ckvn = (ckv * lax.rsqrt(jnp.mean(ckv * ckv, axis=-1, keepdims=True) + EPS) * ckvg_ref[...]).astype(BF16)
    q_all = _dot(cqn, wuq_ref[...])
    k_all = _dot(ckvn, wuk_ref[...])
    va_ref[...] = _dot_nt(wuvt_ref[...], ckvn).astype(BF16)
    half = MLA_ROPE // 2
    cos, sin = cs_ref[:, 0:half], cs_ref[:, half:MLA_ROPE]
    rope_scr[0, :, MLA_NOPE:MLA_NOPE + half] = cos
    rope_scr[0, :, MLA_NOPE + half:MLA_QK] = cos
    rope_scr[1, :, MLA_NOPE:MLA_NOPE + half] = -sin
    rope_scr[2, :, MLA_NOPE + half:MLA_QK] = sin
    rc, rs1, rs2 = rope_scr[0], rope_scr[1], rope_scr[2]

    def rope(t):
        return t * rc + pltpu.roll(t, LANES - MLA_ROPE // 2, 1) * rs1 + pltpu.roll(t, MLA_ROPE // 2, 1) * rs2

    for h in range(HEADS):
        q = q_all[:, h * LANES:(h + 1) * LANES]
        q = q * lax.rsqrt(jnp.sum(q * q, axis=-1, keepdims=True) * (1.0 / MLA_QK) + EPS) * qg_ref[...]
        qa_ref[h] = (rope(q) * (MLA_QK ** -0.5)).astype(BF16)
        k = k_all[:, h * LANES:(h + 1) * LANES] + kr
        k = k * lax.rsqrt(jnp.sum(k * k, axis=-1, keepdims=True) * (1.0 / MLA_QK) + EPS) * kg_ref[...]
        ka_ref[h] = rope(k).astype(BF16)

    pf = _dot(hb, win_ref[:, COLS_MLA:COLS_MLA + COLS_FOX])
    fqn = (_head_norm(pf[:, 0:GROUP], fqg_ref[...], bd) * (HEAD_DIM ** -0.5)).astype(BF16)
    fkn = _head_norm(pf[:, GROUP:2 * GROUP], fkg_ref[...], bd).astype(BF16)
    vf_ref[...] = _dot_nt(wvt_ref[0], hb).astype(BF16)
    z = pf[:, 2 * GROUP:2 * GROUP + LANES] + fb_ref[...]
    log_f = jnp.minimum(z, 0.0) - jnp.log(1.0 + jnp.exp(-jnp.abs(z)))
    tri = tri_ref[...]
    a1, a2, a3 = _split3(log_f)
    cum = fcarry[...] + (_dot(tri, a1) + _dot(tri, a2) + _dot(tri, a3))
    fcarry[...] = cum[tm - 1:tm, :]
    f_ref[...] = cum
    f1, f2, f3 = _split3(cum)
    xq = jnp.concatenate([fqn, f1, f2, f3], axis=1)
    xk = jnp.concatenate([fkn, f1, f2, f3], axis=1)
    ones_q = jnp.where((lane >= HEAD_DIM + 3) & (lane < HEAD_DIM + 6), 1.0, 0.0)
    ones_k = jnp.where((lane >= HEAD_DIM) & (lane < HEAD_DIM + 3), 1.0, 0.0)
    qf_all = _dot(xq, eq_ref[...])
    kf_all = _dot(xk, ek_ref[...])
    for h in range(HEADS):
        qf_ref[h] = (qf_all[:, h * LANES:(h + 1) * LANES] + ones_q).astype(BF16)
        kf_ref[h] = (kf_all[:, h * LANES:(h + 1) * LANES] + ones_k).astype(BF16)

    pm = _dot(hb, win_ref[:, COLS_MLA + COLS_FOX:COLS_MLA + COLS_FOX + COLS_MOBA])
    mqn = _head_norm(pm[:, 0:GROUP], mqg_ref[...], bd) * (HEAD_DIM ** -0.5)
    mkn = _head_norm(pm[:, GROUP:2 * GROUP], mkg_ref[...], bd)
    vm_ref[...] = _dot_nt(wvt_ref[1], hb).astype(BF16)
    mqb = mqn.astype(BF16)
    mkb = mkn.astype(BF16)
    blocks = tm // MOBA_BLOCK
    blk = lane - HEAD_DIM
    blk_f = blk.astype(F32)
    row_block = jnp.right_shift(lax.broadcasted_iota(jnp.int32, (tm, 1), 0),
                                MOBA_BLOCK.bit_length() - 1)
    own = i * blocks + row_block
    own_f = own.astype(F32)
    past = (blk >= 0) & (blk < own)
    for b in range(blocks):
        col_mean = jnp.mean(mkn[b * MOBA_BLOCK:(b + 1) * MOBA_BLOCK, :], axis=0, keepdims=True)
        for h in range(HEADS):
            kmean[pl.ds(h * LANES + HEAD_DIM + i * blocks + b, 1), :] = jnp.where(
                head_of_lane == h, col_mean, 0.0)
    q_hi, q_lo = _split2(mqn)
    km_hi, km_lo = _split2(kmean[...])
    gate_all = _dot_nt(q_hi, km_hi) + _dot_nt(q_hi, km_lo) + _dot_nt(q_lo, km_hi)
    qm_all = _dot(mqb, sel_ref[...])
    km_all = _dot(mkb, sel_ref[...])
    for h in range(HEADS):
        g = jnp.where(past, gate_all[:, h * LANES:(h + 1) * LANES], NEG)
        chosen = jnp.zeros((tm, LANES), F32)
        for _ in range(MOBA_TOPK):
            m = jnp.max(g, axis=-1, keepdims=True)
            first = jnp.min(jnp.where(g == m, lane_f, 1e9), axis=-1, keepdims=True)
            pick = (lane_f == first) & (m > NEG)
            chosen = jnp.where(pick, 1.0, chosen)
            g = jnp.where(pick, NEG, g)
        slope = 2.0 ** (-(2 * h + 2))
        keep = (chosen > 0.0) | (blk == own)
        bias = jnp.where(keep, (slope * MOBA_BLOCK) * (blk_f - own_f), NEG)
        bias = jnp.where(blk >= 0, bias, 0.0)
        qm_ref[h] = (qm_all[:, h * LANES:(h + 1) * LANES] + bias).astype(BF16)
        onehot = jnp.where(blk == own, 1.0, 0.0)
        km_ref[h] = (km_all[:, h * LANES:(h + 1) * LANES] + onehot).astype(BF16)

    pd = _dot(hb, win_ref[:, COLS_MLA + COLS_FOX + COLS_MOBA:COLS_IN])
    qd_ref[...] = _head_norm(pd[:, 0:GROUP], dqg_ref[...], bd) * (HEAD_DIM ** -0.5)
    kd_ref[...] = _head_norm(pd[:, GROUP:2 * GROUP], dkg_ref[...], bd)
    vd_ref[...] = pd[:, 2 * GROUP:3 * GROUP]


def _prep_constants(seq, tm):
    half = MLA_ROPE // 2
    inv = 1.0 / (ROPE_THETA ** (jnp.arange(half, dtype=F32) / half))
    ang = jnp.arange(seq, dtype=F32)[:, None] * inv[None, :]
    cs = jnp.concatenate([jnp.cos(ang), jnp.sin(ang)], axis=1)
    bd =np.kron(np.eye(HEADS, dtype=np.float32), np.ones((HEAD_DIM, HEAD_DIM), np.float32))
    tri = np.tril(np.ones((tm, tm), np.float32))
    tri_strict = np.tril(np.ones((POST_TILE, POST_TILE), np.float32), -1)
    sel = np.zeros((GROUP, HEADS * LANES), np.float32)
    eq = np.zeros((GROUP + 3 * LANES, HEADS * LANES), np.float32)
    ek = np.zeros((GROUP + 3 * LANES, HEADS * LANES), np.float32)
    for h in range(HEADS):
        for d in range(HEAD_DIM):
            sel[h * HEAD_DIM + d, h * LANES + d] = 1.0
        for piece in range(3):
            eq[GROUP + piece * LANES + h, h * LANES + HEAD_DIM + piece] = 1.0
            ek[GROUP + piece * LANES + h, h * LANES + HEAD_DIM + 3 + piece] = -1.0
    eq[:GROUP] = sel
    ek[:GROUP] = sel
    as_bf = lambda a: jnp.asarray(a, BF16)
    return dict(cs=cs, bd=as_bf(bd),
                tri=as_bf(tri), tri_strict=as_bf(tri_strict), sel=as_bf(sel), eq=as_bf(eq), ek=as_bf(ek))


def _prep(x2, mod_l, consts, p):
    seq = x2.shape[0]
    tm = TOKEN_TILE
    row = lambda n: pl.BlockSpec((tm, n), lambda i: (i, 0))
    heads = pl.BlockSpec((HEADS, tm, LANES), lambda i: (0, i, 0))
    in_arrays = [
        (x2, row(D_MODEL)), (mod_l, _full_spec(mod_l.shape)), (p["g1"], None),
        (p["w_in"], pl.BlockSpec(memory_space=pl.ANY)),
        (p["cq_g"], None), (p["w_uq"], None), (p["ckv_g"], None), (p["w_uk"], None), (p["w_uvt"], None),
        (p["q_g"], None), (p["k_g"], None),
        (consts["cs"], row(MLA_ROPE)),
        (p["fq_g"], None), (p["fk_g"], None), (p["f_b"], None), (p["mq_g"], None), (p["mk_g"], None),
        (p["dq_g"], None), (p["dk_g"], None),
        (consts["bd"], None), (consts["tri"], None), (consts["eq"], None), (consts["ek"], None),
        (consts["sel"], None),
    ]
    args = [a for a, _ in in_arrays]
    specs = [s if s is not None else _full_spec(a.shape) for a, s in in_arrays]
    hshape = jax.ShapeDtypeStruct((HEADS, seq, LANES), BF16)
    vshape = jax.ShapeDtypeStruct((GROUP, seq), BF16)
    dshape = jax.ShapeDtypeStruct((seq, GROUP), F32)
    vt = pl.BlockSpec((GROUP, tm), lambda i: (0, i))
    return pl.pallas_call(
        functools.partial(_prep_kernel, layer=p["layer"]),
        grid=(seq // tm,),
        in_specs=specs,
        out_specs=[heads, heads, vt] * 3 + [row(GROUP)] * 3 + [row(LANES)],
        out_shape=[hshape, hshape, vshape] * 3 + [dshape] * 3
                  + [jax.ShapeDtypeStruct((seq, LANES), F32)],
        scratch_shapes=[pltpu.VMEM((1, LANES), F32), pltpu.VMEM((HEADS * LANES, GROUP), F32),
                        pltpu.VMEM((D_MODEL, p["w_in"].shape[-1]), F32),
                        pltpu.VMEM((D_MODEL, COLS_IN), BF16),
                        pltpu.VMEM((len(IN_VALUE_COLUMNS), GROUP, D_MODEL), BF16),
                        pltpu.VMEM((3, tm, LANES), F32),
                        pltpu.SemaphoreType.DMA(())],
        compiler_params=_params(("arbitrary",)),
        name="prep",
    )(*args)


def _flash_kernel(jlo_ref, q_ref, k_ref, vt_ref, kbias_ref, o_ref, m_sc, acc_sc, sa_sc, sb_sc,
                  *, tile, use_kbias):
    i = pl.program_id(0)
    m_sc[...] = jnp.full_like(m_sc, -jnp.inf)
    acc_sc[...] = jnp.zeros_like(acc_sc)

    def score(hh, j, buf):
        start = pl.multiple_of(j * tile, tile)
        buf[hh] = _dot_nt(k_ref[hh, pl.ds(start, tile), :], q_ref[hh])

    def absorb(hh, j, buf, causal):
        start = pl.multiple_of(j * tile, tile)
        s = buf[hh]
        if use_kbias:
            s = s + jnp.concatenate([kbias_ref[hh]] * (tile // LANES), axis=1)
        if causal:
            key = lax.broadcasted_iota(jnp.int32, (tile, tile), 0)
            qry = lax.broadcasted_iota(jnp.int32, (tile, tile), 1)
            s = jnp.where(key <= qry, s, NEG)
        m_prev = m_sc[hh]
        m_cur = jnp.max(jnp.max(s.reshape(8, tile // 8, tile), axis=0), axis=0, keepdims=True)
        m_new = jnp.maximum(m_prev, m_cur)
        alpha = jnp.exp(m_prev - m_new)
        p = jnp.exp((s - m_new).astype(BF16))
        vt = vt_ref[hh * HEAD_DIM:(hh + 1) * HEAD_DIM, pl.ds(start, tile)]
        vt = jnp.concatenate([vt, jnp.ones((ONES_ROWS, tile), BF16)], axis=0)
        acc_sc[hh] = alpha * acc_sc[hh] + _dot(vt, p)
        m_sc[hh] = m_new

    first = jlo_ref[i]
    n_off = i - first
    for hh in range(HEADS):
        score(hh, first, sa_sc)

    def body(t, carry):
        j = first + 2 * t
        for hh in range(HEADS):
            score(hh, j + 1, sb_sc)
            absorb(hh, j, sa_sc, False)
        for hh in range(HEADS):
            score(hh, j + 2, sa_sc)
            absorb(hh, j + 1, sb_sc, False)
        return carry

    lax.fori_loop(0, n_off // 2, body, 0)

    @pl.when(n_off % 2 == 1)
    def _():
        for hh in range(HEADS):
            score(hh, i, sb_sc)
            absorb(hh, i - 1, sa_sc, False)
        for hh in range(HEADS):
            absorb(hh, i, sb_sc, True)

    @pl.when(n_off % 2 == 0)
    def _():
        for hh in range(HEADS):
            absorb(hh, i, sa_sc, True)

    o_t = jnp.concatenate([acc_sc[hh, 0:HEAD_DIM, :] / acc_sc[hh, HEAD_DIM:HEAD_DIM + 1, :]
                           for hh in range(HEADS)], axis=0)
    o_ref[...] = o_t.T.astype(o_ref.dtype)


def _flash(q, k, v_t, kbias, first_tile, use_kbias):
    seq = v_t.shape[1]
    tile = FLASH_TILE
    kern = functools.partial(_flash_kernel, tile=tile, use_kbias=use_kbias)
    resident = pl.Buffered(1)
    return pl.pallas_call(
        kern,
        grid_spec=pltpu.PrefetchScalarGridSpec(
            num_scalar_prefetch=1,
            grid=(seq // tile,),
            in_specs=[
                pl.BlockSpec((HEADS, tile, LANES), lambda i, f: (0, i, 0)),
                pl.BlockSpec((HEADS, seq, LANES), lambda i, f: (0, 0, 0), pipeline_mode=resident),
                pl.BlockSpec((GROUP, seq), lambda i, f: (0, 0), pipeline_mode=resident),
                pl.BlockSpec((HEADS, tile, LANES), lambda i, f: (0, 0, 0)),
            ],
            out_specs=pl.BlockSpec((tile, GROUP), lambda i, f: (i, 0)),
            scratch_shapes=[pltpu.VMEM((HEADS, 1, tile), F32),
                            pltpu.VMEM((HEADS, HEAD_DIM + ONES_ROWS, tile), F32),
                            pltpu.VMEM((HEADS, tile, tile), F32), pltpu.VMEM((HEADS, tile, tile), F32)],
        ),
        out_shape=jax.ShapeDtypeStruct((seq, GROUP), BF16),
        compiler_params=_params(("arbitrary",)),
        name="flash_kbias" if use_kbias else "flash",
    )(first_tile, q, k, v_t, kbias)


def _fox_first_tile(decay, qk_bound):
    seq = decay.shape[0]
    nq = seq // FLASH_TILE
    f = decay[:, :HEADS]
    f_first = f[0::FLASH_TILE]
    f_last = f[FLASH_TILE - 1::FLASH_TILE]
    gap = f_first[:, None, :] - f_last[None, :, :] + 2.0 * qk_bound
    jj = jnp.arange(nq, dtype=jnp.int32)
    needed = (gap >= FOX_SKIP_LOG) | (jj[None, :, None] >= jj[:, None, None])
    first = jnp.min(jnp.where(needed, jj[None, :, None], nq), axis=1)
    return jnp.min(first, axis=1).astype(jnp.int32)


def _dilated_kernel(q_ref, k_ref, v_ref, o_ref, kbuf, vbuf, acc_s, m_s, l_s):
    pair = pl.program_id(0)
    i = pl.program_id(1)
    T = q_ref.shape[0]

    @pl.when(i == 0)
    def _():
        kbuf[...] = jnp.zeros_like(kbuf)
        vbuf[...] = jnp.zeros_like(vbuf)

    kbuf[0:T, :] = kbuf[T:2 * T, :]
    vbuf[0:T, :] = vbuf[T:2 * T, :]
    kbuf[T:2 * T, :] = k_ref[...]
    vbuf[T:2 * T, :] = v_ref[...]

    ii = lax.broadcasted_iota(jnp.int32, (DIL_SPAN, 2 * DIL_SPAN), 0)
    jj = lax.broadcasted_iota(jnp.int32, (DIL_SPAN, 2 * DIL_SPAN), 1)
    dist = ii + DIL_SPAN - jj
    band = (dist >= 0) & (dist <= DIL_SPAN)
    dist_f = dist.astype(F32)
    upper = lax.broadcasted_iota(jnp.int32, (1, LANES), 1) >= HEAD_DIM

    for pi, (window, r) in enumerate(DIL_PATTERNS):
        assert window // r == DIL_SPAN
        sub = DIL_SPAN * r

        def body(idx, carry, r=r, sub=sub, pi=pi):
            n = idx // r
            rho = idx - n * r
            base = n * sub + rho
            q = q_ref[pl.ds(base, DIL_SPAN, stride=r), :]
            kc = kbuf[pl.ds(T + base - sub, 2 * DIL_SPAN, stride=r), :].astype(BF16)
            vc = vbuf[pl.ds(T + base - sub, 2 * DIL_SPAN, stride=r), :].astype(BF16)
            first_key = jnp.where((i == 0) & (n == 0), DIL_SPAN, 0)
            valid = band & (jj >= first_key)
            stats = []
            for hh in range(2):
                slope = jnp.where(pair == 0, 2.0 ** (-(2 * hh + 1)), 2.0 ** (-(2 * hh + 5)))
                qh = (jnp.where(upper, q, 0.0) if hh else jnp.where(upper, 0.0, q)).astype(BF16)
                s = _dot_nt(qh, kc) - (slope * r) * dist_f
                s = jnp.where(valid, s, NEG)
                m = jnp.max(s, axis=-1, keepdims=True)
                p = jnp.exp(s - m)
                l = jnp.sum(p, axis=-1, keepdims=True)
                stats.append((_dot(p.astype(BF16), vc), m, l))
            rows = pl.ds(pi * T + base, DIL_SPAN, stride=r)
            acc_s[rows, :] = jnp.where(upper, stats[1][0], stats[0][0])
            m_s[rows, :] = jnp.where(upper, stats[1][1], stats[0][1])
            l_s[rows, :] = jnp.where(upper, stats[1][2], stats[0][2])
            return carry

        lax.fori_loop(0, T // DIL_SPAN, body, 0, unroll=True)

    npat = len(DIL_PATTERNS)
    ms = [m_s[pi * T:(pi + 1) * T, :] for pi in range(npat)]
    m_top = functools.reduce(jnp.maximum, ms)
    num = jnp.zeros((T, LANES), F32)
    den = jnp.zeros((T, LANES), F32)
    for pi in range(npat):
        w = jnp.exp(ms[pi] - m_top)
        num = num + w * acc_s[pi * T:(pi + 1) * T, :]
        den = den + w * l_s[pi * T:(pi + 1) * T, :]
    o_ref[...] = (num / den).astype(o_ref.dtype)


def _dilated(q, k, v):
    seq = q.shape[0]
    T = DIL_TILE
    spec = pl.BlockSpec((T, LANES), lambda p, i: (i, p))
    npat = len(DIL_PATTERNS)
    return pl.pallas_call(
        _dilated_kernel,
        grid=(HEADS // 2, seq // T),
        in_specs=[spec, spec, spec],
        out_specs=spec,
        out_shape=jax.ShapeDtypeStruct((seq, GROUP), BF16),
        scratch_shapes=[pltpu.VMEM((2 * T, LANES), F32), pltpu.VMEM((2 * T, LANES), F32),
                        pltpu.VMEM((npat * T, LANES), F32), pltpu.VMEM((npat * T, LANES), F32),
                        pltpu.VMEM((npat * T, LANES), F32)],
        compiler_params=_params(("arbitrary", "arbitrary")),
        name="dilated",
    )(q, k, v)


def _post_kernel(x_ref, oa_ref, ob_ref, oc_ref, od_ref, wout_ref, mod_ref, g2_ref,
                 rw_ref, rb_ref, tri_ref,
                 x1_ref, h2_ref, eidx_ref, gate_ref, rank_ref, cnt_ref, carry):
    i = pl.program_id(0)
    tm = x_ref.shape[0]

    @pl.when(i == 0)
    def _():
        carry[...] = jnp.zeros_like(carry)

    o = (_dot(oa_ref[...], wout_ref[0]) + _dot(ob_ref[...], wout_ref[1])
         + _dot(oc_ref[...], wout_ref[2]) + _dot(od_ref[...], wout_ref[3]))
    x1 = x_ref[...] + mod_ref[2] * o
    x1_ref[...] = x1
    y = x1 * lax.rsqrt(jnp.mean(x1 * x1, axis=-1, keepdims=True) + EPS) * g2_ref[...]
    h2 = y * (1.0 + mod_ref[4]) + mod_ref[3]
    _store_token_tiles(h2_ref, h2)

    h_hi, h_lo = _split2(h2)
    hh = _dot(h_hi, rw_ref[...])
    logits = (hh[:, :LANES] + hh[:, LANES:] + _dot(h_lo, rw_ref[:, :LANES])
              + rb_ref[...])
    lane = lax.broadcasted_iota(jnp.int32, (tm, LANES), 1)
    lane_f = lane.astype(F32)
    g = logits
    chosen = jnp.zeros((tm, LANES), F32)
    vals, idxs = [], []
    for _ in range(TOP_K):
        m = jnp.max(g, axis=-1, keepdims=True)
        first = jnp.min(jnp.where(g == m, lane_f, 1e9), axis=-1, keepdims=True)
        pick = lane_f == first
        chosen = jnp.where(pick, 1.0, chosen)
        g = jnp.where(pick, -jnp.inf, g)
        vals.append(m)
        idxs.append(first)
    exps = [jnp.exp(v - vals[0]) for v in vals]
    den = exps[0] + exps[1] + exps[2] + exps[3]
    before = _dot(tri_ref[...], chosen.astype(BF16)) + carry[...]
    carry[...] = carry[...] + jnp.sum(chosen, axis=0, keepdims=True)
    cnt_ref[...] = carry[...]
    e_out = jnp.zeros((tm, LANES), F32)
    g_out = jnp.zeros((tm, LANES), F32)
    r_out = jnp.zeros((tm, LANES), F32)
    for k in range(TOP_K):
        rank_k = jnp.sum(jnp.where(lane_f == idxs[k], before, 0.0), axis=-1, keepdims=True)
        e_out = jnp.where(lane == k, idxs[k], e_out)
        g_out = jnp.where(lane == k, exps[k] / den, g_out)
        r_out = jnp.where(lane == k, rank_k, r_out)
    eidx_ref[...] = e_out.T[0:8, :].astype(jnp.int32)
    gate_ref[...] = g_out
    rank_ref[...] = r_out.T[0:8, :].astype(jnp.int32)


def _post(x2, oa, ob, oc, od, mod_l, consts, p):
    seq = x2.shape[0]
    tm = POST_TILE
    row = lambda n: pl.BlockSpec((tm, n), lambda i: (i, 0))
    full = [p["w_out"], mod_l, p["g2"], p["rw"], p["r_b"], consts["tri_strict"]]
    f32 = lambda n: jax.ShapeDtypeStruct((seq, n), F32)
    slots = pl.BlockSpec((8, tm), lambda i: (0, i))
    slots_shape = jax.ShapeDtypeStruct((8, seq), jnp.int32)
    return pl.pallas_call(
        _post_kernel,
        grid=(seq // tm,),
        in_specs=[row(D_MODEL)] + [row(GROUP)] * 4 + [_full_spec(a.shape) for a in full],
        out_specs=[row(D_MODEL), pl.BlockSpec((tm * CHUNKS, LANES), lambda i: (i, 0)),
                   slots, row(LANES), slots, _full_spec((1, LANES))],
        out_shape=[f32(D_MODEL), jax.ShapeDtypeStruct((seq * CHUNKS, LANES), F32),
                   slots_shape, f32(LANES), slots_shape, jax.ShapeDtypeStruct((1, LANES), F32)],
        scratch_shapes=[pltpu.VMEM((1, LANES), F32)],
        compiler_params=_params(("arbitrary",)),
        name="post",
    )(x2, oa, ob, oc, od, *full)


def _tile_copy(src, s, dst, d, sem):
    return pltpu.make_async_copy(src.at[pl.ds(pl.multiple_of(s * CHUNKS, CHUNKS), CHUNKS), :],
                                 dst.at[pl.ds(pl.multiple_of(d * CHUNKS, CHUNKS), CHUNKS), :], sem)


def _dispatch_kernel(dest_ref, padlo_ref, padn_ref, nu_ref, h_ref, xs_ref, zbuf, sem, zsem):
    i = pl.program_id(0)
    n = ROUTE_TILE * TOP_K
    base = i * ROUTE_TILE
    seq = pl.num_programs(0) * ROUTE_TILE
    block_rows = EXPERT_ROWS * CHUNKS

    @pl.when(i == 0)
    def _():
        zbuf[...] = jnp.zeros_like(zbuf)

        def fill(wait):
            def go(copy):
                copy.wait() if wait else copy.start()

            def per_expert(e, carry):
                lo, cnt = padlo_ref[e], padn_ref[e]
                off = lo
                p = EXPERT_ROWS // 2
                while p >= 1:
                    rows = p * CHUNKS

                    @pl.when((cnt & p) != 0)
                    def _(off=off, rows=rows):
                        go(pltpu.make_async_copy(
                            zbuf.at[pl.ds(0, rows), :],
                            xs_ref.at[pl.ds(pl.multiple_of(off * CHUNKS, CHUNKS), rows), :], zsem))

                    off = off + (cnt & p)
                    p //= 2
                return carry

            lax.fori_loop(0, N_EXPERTS, per_expert, 0)

            def per_block(b, carry):
                go(pltpu.make_async_copy(
                    zbuf, xs_ref.at[pl.ds(pl.multiple_of(b * block_rows, block_rows), block_rows), :],
                    zsem))
                return carry

            lax.fori_loop(nu_ref[0], xs_ref.shape[0] // block_rows, per_block, 0)

        fill(False)
        fill(True)

    def issue(r, carry):
        for k in range(TOP_K):
            _tile_copy(h_ref, r, xs_ref, dest_ref[k * seq + base + r], sem).start(priority=k % 2)
        return carry

    lax.fori_loop(0, ROUTE_TILE, issue, 0, unroll=4)
    rows = pl.ds(0, n * CHUNKS)
    pltpu.make_async_copy(xs_ref.at[rows, :], xs_ref.at[rows, :], sem).wait()


def _dispatch(dest, pad_lo, pad_n, n_used, h2_tiles, m_pad):
    seq = h2_tiles.shape[0] // CHUNKS
    return pl.pallas_call(
        _dispatch_kernel,
        grid_spec=pltpu.PrefetchScalarGridSpec(
            num_scalar_prefetch=4,
            grid=(seq // ROUTE_TILE,),
            in_specs=[pl.BlockSpec((ROUTE_TILE * CHUNKS, LANES), lambda i, *_: (i, 0))],
            out_specs=pl.BlockSpec(memory_space=pl.ANY),
            scratch_shapes=[pltpu.VMEM((EXPERT_ROWS * CHUNKS, LANES), F32),
                            pltpu.SemaphoreType.DMA(()), pltpu.SemaphoreType.DMA(())],
        ),
        out_shape=jax.ShapeDtypeStruct((m_pad * CHUNKS, LANES), F32),
        compiler_params=_params(("arbitrary",)),
        name="dispatch",
    )(dest, pad_lo, pad_n, n_used, h2_tiles)


def _expert_kernel(be_ref, nu_ref, nxt_ref, ord_ref, xs_ref, w1_hbm, b1_ref, w2_hbm, b2_ref, ys_ref,
                   w1f, w2f, w1b, w2b, sems):
    b = pl.program_id(0)
    e = be_ref[b]
    prev = be_ref[jnp.maximum(b - 1, 0)]
    fresh = ((b == 0) | (e != prev)) & (b < nu_ref[0])
    slot = ord_ref[b] % 2

    def fetch(expert, to_slot):
        return (pltpu.make_async_copy(w1_hbm.at[expert], w1f.at[to_slot], sems.at[0, to_slot]),
                pltpu.make_async_copy(w2_hbm.at[expert], w2f.at[to_slot], sems.at[1, to_slot]))

    @pl.when(b == 0)
    def _():
        for copy in fetch(e, slot):
            copy.start()

    @pl.when(fresh)
    def _():
        for copy in fetch(e, slot):
            copy.wait()
        w1b[...] = w1f[slot].astype(BF16)
        w2b[...] = w2f[slot].astype(BF16)

        @pl.when(nxt_ref[b] >= 0)
        def _():
            for copy in fetch(nxt_ref[b], 1 - slot):
                copy.start()

    @pl.when(b < nu_ref[0])
    def _():
        half = EXPERT_ROWS // 2
        gus = []
        for r in range(2):
            xb = jnp.concatenate(
                [_load_token_chunk(xs_ref, half, c, offset=r * half * CHUNKS) for c in range(CHUNKS)],
                axis=1).astype(BF16)
            gus.append(_dot(xb, w1b[...]) + b1_ref[0])
        for r in range(2):
            g = jnp.minimum(gus[r][:, :D_EXPERT], SWIGLU_LIMIT)
            u = jnp.clip(gus[r][:, D_EXPERT:], -SWIGLU_LIMIT, SWIGLU_LIMIT)
            y = (u + 1.0) * g * (1.0 / (1.0 + jnp.exp(-SWIGLU_ALPHA * g)))
            _store_token_tiles(ys_ref, _dot(y.astype(BF16), w2b[...]) + b2_ref[0],
                               offset=r * half * CHUNKS)

    @pl.when(b >= nu_ref[0])
    def _():
        ys_ref[...] = jnp.zeros_like(ys_ref)


def _experts(blk_expert, n_used, blk_next, blk_ord, xs, w1, b1, w2, b2):
    m_pad = xs.shape[0] // CHUNKS
    bm = EXPERT_ROWS
    n_all = w1.shape[0] * w1.shape[1]
    rows = lambda b, be, nu, *_: (jnp.minimum(b, nu[0] - 1), 0)
    ex = lambda b, be, nu, *_: (be[jnp.minimum(b, nu[0] - 1)], 0, 0)
    return pl.pallas_call(
        _expert_kernel,
        grid_spec=pltpu.PrefetchScalarGridSpec(
            num_scalar_prefetch=4,
            grid=(m_pad // bm,),
            in_specs=[
                pl.BlockSpec((bm * CHUNKS, LANES), rows),
                pl.BlockSpec(memory_space=pl.ANY),
                pl.BlockSpec((1, 1, 2 * D_EXPERT), ex),
                pl.BlockSpec(memory_space=pl.ANY),
                pl.BlockSpec((1, 1, D_MODEL), ex),
            ],
            out_specs=pl.BlockSpec((bm * CHUNKS, LANES), lambda b, *_: (b, 0)),
            scratch_shapes=[pltpu.VMEM((2, D_MODEL, 2 * D_EXPERT), F32),
                            pltpu.VMEM((2, D_EXPERT, D_MODEL), F32),
                            pltpu.VMEM((D_MODEL, 2 * D_EXPERT), BF16),
                            pltpu.VMEM((D_EXPERT, D_MODEL), BF16),
                            pltpu.SemaphoreType.DMA((2, 2))],
        ),
        out_shape=jax.ShapeDtypeStruct((m_pad * CHUNKS, LANES), F32),
        compiler_params=_params(("arbitrary",)),
        name="experts",
    )(blk_expert, n_used, blk_next, blk_ord, xs,
      w1.reshape(n_all, D_MODEL, 2 * D_EXPERT), b1.reshape(n_all, 1, -1),
      w2.reshape(n_all, D_EXPERT, D_MODEL), b2.reshape(n_all, 1, -1))


def _combine_kernel(dest_ref, ys_ref, x1_ref, gate_ref, mod_ref, o_ref, buf, sems):
    i = pl.program_id(0)
    tm = x1_ref.shape[0]
    n = tm * TOP_K
    slot = i % 2

    seq = pl.num_programs(0) * tm

    def gather(step, to_slot):
        base = step * tm

        def issue(r, carry):
            for k in range(TOP_K):
                _tile_copy(ys_ref, dest_ref[k * seq + base + r], buf, to_slot * n + k * tm + r,
                           sems.at[to_slot]).start(priority=k % 2)
            return carry

        lax.fori_loop(0, tm, issue, 0, unroll=4)

    @pl.when(i == 0)
    def _():
        gather(0, 0)

    @pl.when(i + 1 < pl.num_programs(0))
    def _():
        gather(i + 1, 1 - slot)

    mine = pl.ds(pl.multiple_of(slot * n * CHUNKS, n * CHUNKS), n * CHUNKS)
    pltpu.make_async_copy(ys_ref.at[pl.ds(0, n * CHUNKS), :], buf.at[mine, :], sems.at[slot]).wait()
    gates = gate_ref[...]
    g2 = mod_ref[5]
    for c in range(CHUNKS):
        cols = slice(c * LANES, (c + 1) * LANES)
        mix = jnp.zeros((tm, LANES), F32)
        for k in range(TOP_K):
            mix = mix + gates[:, k:k + 1] * _load_token_chunk(
                buf, tm, c, offset=(slot * n + k * tm) * CHUNKS)
        o_ref[:, cols] = x1_ref[:, cols] + g2[:, cols] * mix


def _combine(dest, ys, x1, gates, mod_l):
    seq = x1.shape[0]
    tm = ROUTE_TILE
    return pl.pallas_call(
        _combine_kernel,
        grid_spec=pltpu.PrefetchScalarGridSpec(
            num_scalar_prefetch=1,
            grid=(seq // tm,),
            in_specs=[
                pl.BlockSpec(memory_space=pl.ANY),
                pl.BlockSpec((tm, D_MODEL), lambda i, d: (i, 0)),
                pl.BlockSpec((tm, LANES), lambda i, d: (i, 0)),
                pl.BlockSpec(mod_l.shape, lambda i, d: (0, 0, 0)),
            ],
            out_specs=pl.BlockSpec((tm, D_MODEL), lambda i, d: (i, 0)),
            scratch_shapes=[pltpu.VMEM((2 * TOP_K * tm * CHUNKS, LANES), F32),
                            pltpu.SemaphoreType.DMA((2,))],
        ),
        out_shape=jax.ShapeDtypeStruct((seq, D_MODEL), F32),
        compiler_params=_params(("arbitrary",)),
        name="combine",
    )(dest, ys, x1, gates, mod_l)


def _moe(l, x1, h2_tiles, eidx, gates, rank, counts, mod_l, w1, b1, w2, b2):
    seq = x1.shape[0]
    bm = EXPERT_ROWS
    m_pad = seq * TOP_K + N_EXPERTS * bm
    cnt = counts[0, :N_EXPERTS].astype(jnp.int32)
    padded = (cnt + bm - 1) // bm * bm
    pad_end = jnp.cumsum(padded)
    pad_start = pad_end - padded
    onehot = eidx[:TOP_K, :, None] == jnp.arange(N_EXPERTS, dtype=jnp.int32)
    start_of = jnp.sum(jnp.where(onehot, pad_start, 0), axis=-1)
    dest = (start_of + rank[:TOP_K]).reshape(TOP_K * seq).astype(jnp.int32)
    nblk = m_pad // bm
    blk_start = jnp.arange(nblk, dtype=jnp.int32) * bm
    local = jnp.minimum(jnp.sum(pad_end[None, :] <= blk_start[:, None], axis=1), N_EXPERTS - 1)
    blk_expert = (local + l * N_EXPERTS).astype(jnp.int32)
    n_used = (pad_end[-1:] // bm).astype(jnp.int32)
    ids = jnp.arange(N_EXPERTS, dtype=jnp.int32)
    live = padded > 0
    ordinal = jnp.cumsum(live.astype(jnp.int32)) - 1
    later = live[None, :] & (ids[None, :] > ids[:, None])
    nxt = jnp.min(jnp.where(later, ids[None, :], N_EXPERTS), axis=1)
    nxt = jnp.where(nxt < N_EXPERTS, nxt + l * N_EXPERTS, -1)
    of_block = local[:, None] == ids[None, :]
    blk_next = jnp.sum(jnp.where(of_block, nxt[None, :], 0), axis=1).astype(jnp.int32)
    blk_ord = jnp.sum(jnp.where(of_block, ordinal[None, :], 0), axis=1).astype(jnp.int32)
    xs = _dispatch(dest, (pad_start + cnt).astype(jnp.int32), (padded - cnt).astype(jnp.int32),
                   n_used, h2_tiles, m_pad)
    ys = _experts(blk_expert, n_used, blk_next, blk_ord, xs, w1, b1, w2, b2)
    return _combine(dest, ys, x1, gates, mod_l)


def _pad_cols(a, n):
    return jnp.pad(a, ((0, 0), (0, n - a.shape[1])))


def _layer_params(l, w_in, mla_cq_g, mla_w_uq, mla_ckv_g, mla_w_ukv, mla_q_g, mla_k_g,
                  fox_q_g, fox_k_g, fox_b_f, moba_q_g, moba_k_g, dil_q_g, dil_k_g, w_out,
                  norm1_g, norm2_g, router_w, router_b):
    w_uq = jnp.pad(mla_w_uq[l].reshape(MLA_Q_RANK, HEADS, MLA_QK),
                   ((0, 0), (0, 0), (0, LANES - MLA_QK))).reshape(MLA_Q_RANK, HEADS * LANES)
    w_ukv = mla_w_ukv[l].reshape(MLA_KV_RANK, HEADS, MLA_NOPE + HEAD_DIM)
    w_uk = jnp.pad(w_ukv[:, :, :MLA_NOPE], ((0, 0), (0, 0), (0, LANES - MLA_NOPE)))
    w_uv = w_ukv[:, :, MLA_NOPE:]
    tile4 = lambda g: jnp.tile(g, HEADS)[None, :]
    rw = _pad_cols(router_w[l], LANES)
    rw_hi = rw.astype(BF16)
    rw_lo = (rw - rw_hi.astype(F32)).astype(BF16)
    r_b = jnp.concatenate([router_b[l], jnp.full((LANES - N_EXPERTS,), NEG, F32)])[None, :]
    return dict(
        g1=norm1_g[l][None, :], g2=norm2_g[l][None, :], layer=l,
        w_in=w_in.reshape(-1, w_in.shape[-1]),
        cq_g=mla_cq_g[l][None, :], w_uq=w_uq.astype(BF16), ckv_g=mla_ckv_g[l][None, :],
        w_uk=w_uk.reshape(MLA_KV_RANK, HEADS * LANES).astype(BF16),
        w_uvt=w_uv.reshape(MLA_KV_RANK, GROUP).T.astype(BF16),
        fox_bound=FOX_NORM_SLACK * HEAD_DIM ** 0.5 * jnp.max(jnp.abs(fox_q_g[l]))
        * jnp.max(jnp.abs(fox_k_g[l])),
        q_g=_pad_cols(mla_q_g[l][None, :], LANES), k_g=_pad_cols(mla_k_g[l][None, :], LANES),
        fq_g=tile4(fox_q_g[l]), fk_g=tile4(fox_k_g[l]), f_b=_pad_cols(fox_b_f[l][None, :], LANES),
        mq_g=tile4(moba_q_g[l]), mk_g=tile4(moba_k_g[l]), dq_g=tile4(dil_q_g[l]), dk_g=tile4(dil_k_g[l]),
        w_out=w_out[l].reshape(HEADS, GROUP, D_MODEL).astype(BF16),
        rw=jnp.concatenate([rw_hi, rw_lo], axis=1), r_b=r_b,
    )


def kernel(x, c, w_mod, b_mod, norm1_g, norm2_g, w_in, mla_cq_g, mla_w_uq, mla_ckv_g, mla_w_ukv, mla_q_g, mla_k_g, fox_q_g, fox_k_g, fox_b_f, moba_q_g, moba_k_g, dil_q_g, dil_k_g, w_out, router_w, router_b, exp_w1, exp_b1, exp_w2, exp_b2):
    batch, seq, d = x.shape
    assert batch == 1 and d == D_MODEL
    assert seq % DIL_TILE == 0 and seq // MOBA_BLOCK <= MOBA_MAX_BLOCKS
    depth = w_mod.shape[0]
    consts = _prep_constants(seq, TOKEN_TILE)
    mod = _modulation(c, w_mod, b_mod)
    slopes_c = 2.0 ** (-(2.0 * np.arange(HEADS) + 2.0))
    in_block = np.arange(FLASH_TILE) % MOBA_BLOCK
    kbias_c = jnp.asarray(np.broadcast_to(slopes_c[:, None, None] * in_block[None, :, None],
                                          (HEADS, FLASH_TILE, LANES)), F32)
    kbias_0 = jnp.zeros((HEADS, FLASH_TILE, LANES), F32)
    all_tiles = jnp.zeros((seq // FLASH_TILE,), jnp.int32)
    x2 = x.reshape(seq, d)
    for l in range(depth):
        p = _layer_params(l, w_in, mla_cq_g, mla_w_uq, mla_ckv_g, mla_w_ukv, mla_q_g, mla_k_g,
                          fox_q_g, fox_k_g, fox_b_f, moba_q_g, moba_k_g, dil_q_g, dil_k_g, w_out,
                          norm1_g, norm2_g, router_w, router_b)
        mod_l = mod[l]
        qa, ka, va, qf, kf, vf, qm, km, vm, qd, kd, vd, decay = _prep(x2, mod_l, consts, p)
        oa = _flash(qa, ka, va, kbias_0, all_tiles, False)
        ob = _flash(qf, kf, vf, kbias_0, _fox_first_tile(decay, p["fox_bound"]), False)
        oc = _flash(qm, km, vm, kbias_c, all_tiles, True)
        od = _dilated(qd, kd, vd)
        x1, h2, eidx, gates, rank, counts = _post(x2, oa, ob, oc, od, mod_l, consts, p)
        x2 = _moe(l, x1, h2, eidx, gates, rank, counts, mod_l, exp_w1, exp_b1, exp_w2, exp_b2)
    return x2.reshape(batch, seq, d)
```

```python
import functools

import numpy as np
import jax
import jax.numpy as jnp
from jax import lax
from jax.experimental import pallas as pl
from jax.experimental.pallas import tpu as pltpu

F32 = jnp.float32
BF16 = jnp.bfloat16

D_MODEL = 1024
HEAD_DIM = 64
HEADS = 4
GROUP = HEADS * HEAD_DIM
LANES = 128
CHUNKS = D_MODEL // LANES
MLA_Q_RANK = 256
MLA_KV_RANK = 128
MLA_NOPE = 64
MLA_ROPE = 32
MLA_QK = MLA_NOPE + MLA_ROPE
ROPE_THETA = 10000.0
MOBA_BLOCK = 256
MOBA_TOPK = 3
MOBA_MAX_BLOCKS = 64
DIL_PATTERNS = ((128, 1), (512, 4), (2048, 16))
DIL_SPAN = 128
DIL_TILE = 2048
N_EXPERTS = 32
TOP_K = 4
D_EXPERT = 1024
SWIGLU_LIMIT = 7.0
SWIGLU_ALPHA = 1.702
EPS = 1e-6
NEG = -1e30

FLASH_TILE = 512
ONES_ROWS = 16
FOX_SKIP_LOG = -106.0
FOX_NORM_SLACK = 1.02
TOKEN_TILE = 512
POST_TILE = 512
DISPATCH_TILE = 1024
COMBINE_TILE = 512
EXPERT_ROWS = 512
VMEM_LIMIT = 56 * 1024 * 1024

COLS_MLA = 512
COLS_FOX = 640
COLS_MOBA = 512
COLS_DIL = 768
COLS_IN = COLS_MLA + COLS_FOX + COLS_MOBA + COLS_DIL


def _in_column_map():
    sizes = [MLA_Q_RANK, MLA_KV_RANK, MLA_ROPE, GROUP, GROUP, GROUP, HEADS] + [GROUP] * 6
    src = np.concatenate([[0], np.cumsum(sizes)])
    cq, ckv, kr, fq, fk, fv, flog, mq, mk, mv, dq, dk, dv = range(13)
    layout = [(cq, 0), (ckv, MLA_Q_RANK), (kr, MLA_Q_RANK + MLA_KV_RANK + MLA_NOPE),
              (fq, COLS_MLA), (fk, COLS_MLA + GROUP), (flog, COLS_MLA + 2 * GROUP),
              (mq, COLS_MLA + COLS_FOX), (mk, COLS_MLA + COLS_FOX + GROUP),
              (dq, COLS_IN - 3 * GROUP), (dk, COLS_IN - 2 * GROUP), (dv, COLS_IN - GROUP)]
    moves = tuple((dst, int(src[piece]), sizes[piece]) for piece, dst in layout)
    return moves, (int(src[fv]), int(src[mv]))


IN_COLUMN_MAP, IN_VALUE_COLUMNS = _in_column_map()


def _dot(a, b):
    return jnp.dot(a, b, preferred_element_type=F32)


def _dot_nt(a, b):
    return lax.dot_general(a, b, (((1,), (1,)), ((), ())), preferred_element_type=F32)


def _split2(x):
    hi = x.astype(BF16)
    lo = (x - hi.astype(F32)).astype(BF16)
    return hi, lo


def _split3(x):
    a = x.astype(BF16)
    r = x - a.astype(F32)
    b = r.astype(BF16)
    c = (r - b.astype(F32)).astype(BF16)
    return a, b, c


def _head_of_lane():
    return jnp.right_shift(lax.broadcasted_iota(jnp.int32, (1, GROUP), 1), 6)


def _store_token_tiles(ref, x, offset=0):
    n = x.shape[0]
    for c in range(CHUNKS):
        ref[pl.ds(offset + c, n, stride=CHUNKS), :] = x[:, c * LANES:(c + 1) * LANES]


def _load_token_chunk(ref, n, c, offset=0):
    return ref[pl.ds(offset + c, n, stride=CHUNKS), :]


def _full_spec(shape):
    nd = len(shape)
    return pl.BlockSpec(shape, lambda *_: (0,) * nd)


def _params(sem):
    return pltpu.CompilerParams(dimension_semantics=sem, vmem_limit_bytes=VMEM_LIMIT)


def _mod_kernel(c_ref, w_ref, b_ref, o_ref):
    c = c_ref[...]
    s = c * (1.0 / (1.0 + jnp.exp(-c)))
    s8 = jnp.broadcast_to(s, (8, D_MODEL))
    r = jnp.dot(s8, w_ref[0], preferred_element_type=F32, precision=lax.Precision.HIGHEST)
    o_ref[0, 0] = r[0:1, :] + b_ref[0, 0]


def _modulation(c, w_mod, b_mod):
    depth = w_mod.shape[0]
    b4 = b_mod.reshape(depth, 6, 1, D_MODEL)
    return pl.pallas_call(
        _mod_kernel,
        grid=(depth, 6),
        in_specs=[
            pl.BlockSpec((1, D_MODEL), lambda l, j: (0, 0)),
            pl.BlockSpec((1, D_MODEL, D_MODEL), lambda l, j: (l, 0, j)),
            pl.BlockSpec((1, 1, 1, D_MODEL), lambda l, j: (l, j, 0, 0)),
        ],
        out_specs=pl.BlockSpec((1, 1, 1, D_MODEL), lambda l, j: (l, j, 0, 0)),
        out_shape=jax.ShapeDtypeStruct((depth, 6, 1, D_MODEL), F32),
        compiler_params=_params(("arbitrary", "arbitrary")),
        name="modulation",
    )(c, w_mod, b4)


def _head_norm(x, g, bd):
    hi, lo = _split2(x * x)
    ss = _dot(hi, bd) + _dot(lo, bd)
    return x * lax.rsqrt(ss * (1.0 / HEAD_DIM) + EPS) * g


def _prep_kernel(x_ref, mod_ref, g1_ref, win_hbm, cqg_ref, wuq_ref, ckvg_ref, wuk_ref,
                 wuvt_ref, qg_ref, kg_ref, cs_ref,
                 fqg_ref, fkg_ref, fb_ref, mqg_ref, mkg_ref, dqg_ref, dkg_ref,
                 bd_ref, tri_ref, eq_ref, ek_ref, sel_ref,
                 qa_ref, ka_ref, va_ref, qf_ref, kf_ref, vf_ref, qm_ref, km_ref, vm_ref,
                 qd_ref, kd_ref, vd_ref, f_ref,
                 fcarry, kmean, wraw_ref, win_ref, wvt_ref, rope_scr, wsem, *, layer):
    i = pl.program_id(0)
    tm = x_ref.shape[0]

    @pl.when(i == 0)
    def _():
        fetch = pltpu.make_async_copy(win_hbm.at[pl.ds(layer * D_MODEL, D_MODEL), :], wraw_ref, wsem)
        fetch.start()
        fcarry[...] = jnp.zeros_like(fcarry)
        kmean[...] = jnp.zeros_like(kmean)
        win_ref[...] = jnp.zeros_like(win_ref)
        rope_scr[...] = jnp.zeros_like(rope_scr)
        rope_scr[0, :, 0:MLA_NOPE] = jnp.ones((tm, MLA_NOPE), F32)
        fetch.wait()
        for dst, src, width in IN_COLUMN_MAP:
            win_ref[:, dst:dst + width] = wraw_ref[:, src:src + width].astype(BF16)
        for slot, src in enumerate(IN_VALUE_COLUMNS):
            wvt_ref[slot] = wraw_ref[:, src:src + GROUP].T.astype(BF16)

    x = x_ref[...]
    y = x * lax.rsqrt(jnp.mean(x * x, axis=-1, keepdims=True) + EPS) * g1_ref[...]
    hb = (y * (1.0 + mod_ref[1]) + mod_ref[0]).astype(BF16)
    bd = bd_ref[...]
    lane = lax.broadcasted_iota(jnp.int32, (1, LANES), 1)
    lane_f = lane.astype(F32)
    head_of_lane = _head_of_lane()

    pa = _dot(hb, win_ref[:, 0:COLS_MLA])
    cq = pa[:, 0:MLA_Q_RANK]
    ckv = pa[:, MLA_Q_RANK:MLA_Q_RANK + MLA_KV_RANK]
    kr = pa[:, MLA_Q_RANK + MLA_KV_RANK:COLS_MLA]
    cqn = (cq * lax.rsqrt(jnp.mean(cq * cq, axis=-1, keepdims=True) + EPS) * cqg_ref[...]).astype(BF16)
    ckvn = (ckv * lax.rsqrt(jnp.mean(ckv * ckv, axis=-1, keepdims=True) + EPS) * ckvg_ref[...]).astype(BF16)
    q_all = _dot(cqn, wuq_ref[...])
    k_all = _dot(ckvn, wuk_ref[...])
    va_ref[...] = _dot_nt(wuvt_ref[...], ckvn).astype(BF16)
    half = MLA_ROPE // 2
    cos, sin = cs_ref[:, 0:half], cs_ref[:, half:MLA_ROPE]
    rope_scr[0, :, MLA_NOPE:MLA_NOPE + half] = cos
    rope_scr[0, :, MLA_NOPE + half:MLA_QK] = cos
    rope_scr[1, :, MLA_NOPE:MLA_NOPE + half] = -sin
    rope_scr[2, :, MLA_NOPE + half:MLA_QK] = sin
    rc, rs1, rs2 = rope_scr[0], rope_scr[1], rope_scr[2]

    def rope(t):
        return t * rc + pltpu.roll(t, LANES - MLA_ROPE // 2, 1) * rs1 + pltpu.roll(t, MLA_ROPE // 2, 1) * rs2

    for h in range(HEADS):
        q = q_all[:, h * LANES:(h + 1) * LANES]
        q = q * lax.rsqrt(jnp.sum(q * q, axis=-1, keepdims=True) * (1.0 / MLA_QK) + EPS) * qg_ref[...]
        qa_ref[h] = (rope(q) * (MLA_QK ** -0.5)).astype(BF16)
        k = k_all[:, h * LANES:(h + 1) * LANES] + kr
        k = k * lax.rsqrt(jnp.sum(k * k, axis=-1, keepdims=True) * (1.0 / MLA_QK) + EPS) * kg_ref[...]
        ka_ref[h] = rope(k).astype(BF16)

    pf = _dot(hb, win_ref[:, COLS_MLA:COLS_MLA + COLS_FOX])
    fqn = (_head_norm(pf[:, 0:GROUP], fqg_ref[...], bd) * (HEAD_DIM ** -0.5)).astype(BF16)
    fkn = _head_norm(pf[:, GROUP:2 * GROUP], fkg_ref[...], bd).astype(BF16)
    vf_ref[...] = _dot_nt(wvt_ref[0], hb).astype(BF16)
    z = pf[:, 2 * GROUP:2 * GROUP + LANES] + fb_ref[...]
    log_f = jnp.minimum(z, 0.0) - jnp.log(1.0 + jnp.exp(-jnp.abs(z)))
    tri = tri_ref[...]
    a1, a2, a3 = _split3(log_f)
    cum = fcarry[...] + (_dot(tri, a1) + _dot(tri, a2) + _dot(tri, a3))
    fcarry[...] = cum[tm - 1:tm, :]
    f_ref[...] = cum
    f1, f2, f3 = _split3(cum)
    xq = jnp.concatenate([fqn, f1, f2, f3], axis=1)
    xk = jnp.concatenate([fkn, f1, f2, f3], axis=1)
    ones_q = jnp.where((lane >= HEAD_DIM + 3) & (lane < HEAD_DIM + 6), 1.0, 0.0)
    ones_k = jnp.where((lane >= HEAD_DIM) & (lane < HEAD_DIM + 3), 1.0, 0.0)
    qf_all = _dot(xq, eq_ref[...])
    kf_all = _dot(xk, ek_ref[...])
    for h in range(HEADS):
        qf_ref[h] = (qf_all[:, h * LANES:(h + 1) * LANES] + ones_q).astype(BF16)
        kf_ref[h] = (kf_all[:, h * LANES:(h + 1) * LANES] + ones_k).astype(BF16)

    pm = _dot(hb, win_ref[:, COLS_MLA + COLS_FOX:COLS_MLA + COLS_FOX + COLS_MOBA])
    mqn = _head_norm(pm[:, 0:GROUP], mqg_ref[...], bd) * (HEAD_DIM ** -0.5)
    mkn = _head_norm(pm[:, GROUP:2 * GROUP], mkg_ref[...], bd)
    vm_ref[...] = _dot_nt(wvt_ref[1], hb).astype(BF16)
    mqb = mqn.astype(BF16)
    mkb = mkn.astype(BF16)
    blocks = tm // MOBA_BLOCK
    blk = lane - HEAD_DIM
    blk_f = blk.astype(F32)
    row_block = jnp.right_shift(lax.broadcasted_iota(jnp.int32, (tm, 1), 0),
                                MOBA_BLOCK.bit_length() - 1)
    own = i * blocks + row_block
    own_f = own.astype(F32)
    past = (blk >= 0) & (blk < own)
    for b in range(blocks):
        col_mean = jnp.mean(mkn[b * MOBA_BLOCK:(b + 1) * MOBA_BLOCK, :], axis=0, keepdims=True)
        for h in range(HEADS):
            kmean[pl.ds(h * LANES + HEAD_DIM + i * blocks + b, 1), :] = jnp.where(
                head_of_lane == h, col_mean, 0.0)
    q_hi, q_lo = _split2(mqn)
    km_hi, km_lo = _split2(kmean[...])
    gate_all = _dot_nt(q_hi, km_hi) + _dot_nt(q_hi, km_lo) + _dot_nt(q_lo, km_hi)
    qm_all = _dot(mqb, sel_ref[...])
    km_all = _dot(mkb, sel_ref[...])
    for h in range(HEADS):
        g = jnp.where(past, gate_all[:, h * LANES:(h + 1) * LANES], NEG)
        chosen = jnp.zeros((tm, LANES), F32)
        for _ in range(MOBA_TOPK):
            m = jnp.max(g, axis=-1, keepdims=True)
            first = jnp.min(jnp.where(g == m, lane_f, 1e9), axis=-1, keepdims=True)
            pick = (lane_f == first) & (m > NEG)
            chosen = jnp.where(pick, 1.0, chosen)
            g = jnp.where(pick, NEG, g)
        slope = 2.0 ** (-(2 * h + 2))
        keep = (chosen > 0.0) | (blk == own)
        bias = jnp.where(keep, (slope * MOBA_BLOCK) * (blk_f - own_f), NEG)
        bias = jnp.where(blk >= 0, bias, 0.0)
        qm_ref[h] = (qm_all[:, h * LANES:(h + 1) * LANES] + bias).astype(BF16)
        onehot = jnp.where(blk == own, 1.0, 0.0)
        km_ref[h] = (km_all[:, h * LANES:(h + 1) * LANES] + onehot).astype(BF16)

    pd = _dot(hb, win_ref[:, COLS_MLA + COLS_FOX + COLS_MOBA:COLS_IN])
    qd_ref[...] = _head_norm(pd[:, 0:GROUP], dqg_ref[...], bd) * (HEAD_DIM ** -0.5)
    kd_ref[...] = _head_norm(pd[:, GROUP:2 * GROUP], dkg_ref[...], bd)
    vd_ref[...] = pd[:, 2 * GROUP:3 * GROUP]


def _prep_constants(seq, tm):
    half = MLA_ROPE // 2
    inv = 1.0 / (ROPE_THETA ** (jnp.arange(half, dtype=F32) / half))
    ang = jnp.arange(seq, dtype=F32)[:, None] * inv[None, :]
    cs = jnp.concatenate([jnp.cos(ang), jnp.sin(ang)], axis=1)
    bd =np.kron(np.eye(HEADS, dtype=np.float32), np.ones((HEAD_DIM, HEAD_DIM), np.float32))
    tri = np.tril(np.ones((tm, tm), np.float32))
    tri_strict = np.tril(np.ones((POST_TILE, POST_TILE), np.float32), -1)
    sel = np.zeros((GROUP, HEADS * LANES), np.float32)
    eq = np.zeros((GROUP + 3 * LANES, HEADS * LANES), np.float32)
    ek = np.zeros((GROUP + 3 * LANES, HEADS * LANES), np.float32)
    for h in range(HEADS):
        for d in range(HEAD_DIM):
            sel[h * HEAD_DIM + d, h * LANES + d] = 1.0
        for piece in range(3):
            eq[GROUP + piece * LANES + h, h * LANES + HEAD_DIM + piece] = 1.0
            ek[GROUP + piece * LANES + h, h * LANES + HEAD_DIM + 3 + piece] = -1.0
    eq[:GROUP] = sel
    ek[:GROUP] = sel
    as_bf = lambda a: jnp.asarray(a, BF16)
    return dict(cs=cs, bd=as_bf(bd),
                tri=as_bf(tri), tri_strict=as_bf(tri_strict), sel=as_bf(sel), eq=as_bf(eq), ek=as_bf(ek))


def _prep(x2, mod_l, consts, p):
    seq = x2.shape[0]
    tm = TOKEN_TILE
    row = lambda n: pl.BlockSpec((tm, n), lambda i: (i, 0))
    heads = pl.BlockSpec((HEADS, tm, LANES), lambda i: (0, i, 0))
    in_arrays = [
        (x2, row(D_MODEL)), (mod_l, _full_spec(mod_l.shape)), (p["g1"], None),
        (p["w_in"], pl.BlockSpec(memory_space=pl.ANY)),
        (p["cq_g"], None), (p["w_uq"], None), (p["ckv_g"], None), (p["w_uk"], None), (p["w_uvt"], None),
        (p["q_g"], None), (p["k_g"], None),
        (consts["cs"], row(MLA_ROPE)),
        (p["fq_g"], None), (p["fk_g"], None), (p["f_b"], None), (p["mq_g"], None), (p["mk_g"], None),
        (p["dq_g"], None), (p["dk_g"], None),
        (consts["bd"], None), (consts["tri"], None), (consts["eq"], None), (consts["ek"], None),
        (consts["sel"], None),
    ]
    args = [a for a, _ in in_arrays]
    specs = [s if s is not None else _full_spec(a.shape) for a, s in in_arrays]
    hshape = jax.ShapeDtypeStruct((HEADS, seq, LANES), BF16)
    vshape = jax.ShapeDtypeStruct((GROUP, seq), BF16)
    dshape = jax.ShapeDtypeStruct((seq, GROUP), F32)
    vt = pl.BlockSpec((GROUP, tm), lambda i: (0, i))
    return pl.pallas_call(
        functools.partial(_prep_kernel, layer=p["layer"]),
        grid=(seq // tm,),
        in_specs=specs,
        out_specs=[heads, heads, vt] * 3 + [row(GROUP)] * 3 + [row(LANES)],
        out_shape=[hshape, hshape, vshape] * 3 + [dshape] * 3
                  + [jax.ShapeDtypeStruct((seq, LANES), F32)],
        scratch_shapes=[pltpu.VMEM((1, LANES), F32), pltpu.VMEM((HEADS * LANES, GROUP), F32),
                        pltpu.VMEM((D_MODEL, p["w_in"].shape[-1]), F32),
                        pltpu.VMEM((D_MODEL, COLS_IN), BF16),
                        pltpu.VMEM((len(IN_VALUE_COLUMNS), GROUP, D_MODEL), BF16),
                        pltpu.VMEM((3, tm, LANES), F32),
                        pltpu.SemaphoreType.DMA(())],
        compiler_params=_params(("arbitrary",)),
        name="prep",
    )(*args)


def _flash_kernel(jlo_ref, q_ref, k_ref, vt_ref, kbias_ref, o_ref, m_sc, acc_sc, sa_sc, sb_sc,
                  *, tile, use_kbias):
    i = pl.program_id(0)
    m_sc[...] = jnp.full_like(m_sc, -jnp.inf)
    acc_sc[...] = jnp.zeros_like(acc_sc)

    def score(hh, j, buf):
        start = pl.multiple_of(j * tile, tile)
        buf[hh] = _dot_nt(k_ref[hh, pl.ds(start, tile), :], q_ref[hh])

    def absorb(hh, j, buf, causal):
        start = pl.multiple_of(j * tile, tile)
        s = buf[hh]
        if use_kbias:
            s = s + jnp.concatenate([kbias_ref[hh]] * (tile // LANES), axis=1)
        if causal:
            key = lax.broadcasted_iota(jnp.int32, (tile, tile), 0)
            qry = lax.broadcasted_iota(jnp.int32, (tile, tile), 1)
            s = jnp.where(key <= qry, s, NEG)
        m_prev = m_sc[hh]
        m_cur = jnp.max(jnp.max(s.reshape(8, tile // 8, tile), axis=0), axis=0, keepdims=True)
        m_new = jnp.maximum(m_prev, m_cur)
        alpha = jnp.exp(m_prev - m_new)
        p = jnp.exp((s - m_new).astype(BF16))
        vt = vt_ref[hh * HEAD_DIM:(hh + 1) * HEAD_DIM, pl.ds(start, tile)]
        vt = jnp.concatenate([vt, jnp.ones((ONES_ROWS, tile), BF16)], axis=0)
        acc_sc[hh] = alpha * acc_sc[hh] + _dot(vt, p)
        m_sc[hh] = m_new

    first = jlo_ref[i]
    n_off = i - first
    for hh in range(HEADS):
        score(hh, first, sa_sc)

    def body(t, carry):
        j = first + 2 * t
        for hh in range(HEADS):
            score(hh, j + 1, sb_sc)
            absorb(hh, j, sa_sc, False)
        for hh in range(HEADS):
            score(hh, j + 2, sa_sc)
            absorb(hh, j + 1, sb_sc, False)
        return carry

    lax.fori_loop(0, n_off // 2, body, 0)

    @pl.when(n_off % 2 == 1)
    def _():
        for hh in range(HEADS):
            score(hh, i, sb_sc)
            absorb(hh, i - 1, sa_sc, False)
        for hh in range(HEADS):
            absorb(hh, i, sb_sc, True)

    @pl.when(n_off % 2 == 0)
    def _():
        for hh in range(HEADS):
            absorb(hh, i, sa_sc, True)

    o_t = jnp.concatenate([acc_sc[hh, 0:HEAD_DIM, :] / acc_sc[hh, HEAD_DIM:HEAD_DIM + 1, :]
                           for hh in range(HEADS)], axis=0)
    o_ref[...] = o_t.T.astype(o_ref.dtype)


def _flash(q, k, v_t, kbias, first_tile, use_kbias):
    seq = v_t.shape[1]
    tile = FLASH_TILE
    kern = functools.partial(_flash_kernel, tile=tile, use_kbias=use_kbias)
    resident = pl.Buffered(1)
    return pl.pallas_call(
        kern,
        grid_spec=pltpu.PrefetchScalarGridSpec(
            num_scalar_prefetch=1,
            grid=(seq // tile,),
            in_specs=[
                pl.BlockSpec((HEADS, tile, LANES), lambda i, f: (0, i, 0)),
                pl.BlockSpec((HEADS, seq, LANES), lambda i, f: (0, 0, 0), pipeline_mode=resident),
                pl.BlockSpec((GROUP, seq), lambda i, f: (0, 0), pipeline_mode=resident),
                pl.BlockSpec((HEADS, tile, LANES), lambda i, f: (0, 0, 0)),
            ],
            out_specs=pl.BlockSpec((tile, GROUP), lambda i, f: (i, 0)),
            scratch_shapes=[pltpu.VMEM((HEADS, 1, tile), F32),
                            pltpu.VMEM((HEADS, HEAD_DIM + ONES_ROWS, tile), F32),
                            pltpu.VMEM((HEADS, tile, tile), F32), pltpu.VMEM((HEADS, tile, tile), F32)],
        ),
        out_shape=jax.ShapeDtypeStruct((seq, GROUP), BF16),
        compiler_params=_params(("arbitrary",)),
        name="flash_kbias" if use_kbias else "flash",
    )(first_tile, q, k, v_t, kbias)


def _fox_first_tile(decay, qk_bound):
    seq = decay.shape[0]
    nq = seq // FLASH_TILE
    f = decay[:, :HEADS]
    f_first = f[0::FLASH_TILE]
    f_last = f[FLASH_TILE - 1::FLASH_TILE]
    gap = f_first[:, None, :] - f_last[None, :, :] + 2.0 * qk_bound
    jj = jnp.arange(nq, dtype=jnp.int32)
    needed = (gap >= FOX_SKIP_LOG) | (jj[None, :, None] >= jj[:, None, None])
    first = jnp.min(jnp.where(needed, jj[None, :, None], nq), axis=1)
    return jnp.min(first, axis=1).astype(jnp.int32)


def _dilated_kernel(q_ref, k_ref, v_ref, o_ref, kbuf, vbuf, acc_s, m_s, l_s):
    pair = pl.program_id(0)
    i = pl.program_id(1)
    T = q_ref.shape[0]

    @pl.when(i == 0)
    def _():
        kbuf[...] = jnp.zeros_like(kbuf)
        vbuf[...] = jnp.zeros_like(vbuf)

    kbuf[0:T, :] = kbuf[T:2 * T, :]
    vbuf[0:T, :] = vbuf[T:2 * T, :]
    kbuf[T:2 * T, :] = k_ref[...]
    vbuf[T:2 * T, :] = v_ref[...]

    ii = lax.broadcasted_iota(jnp.int32, (DIL_SPAN, 2 * DIL_SPAN), 0)
    jj = lax.broadcasted_iota(jnp.int32, (DIL_SPAN, 2 * DIL_SPAN), 1)
    dist = ii + DIL_SPAN - jj
    band = (dist >= 0) & (dist <= DIL_SPAN)
    dist_f = dist.astype(F32)
    upper = lax.broadcasted_iota(jnp.int32, (1, LANES), 1) >= HEAD_DIM

    for pi, (window, r) in enumerate(DIL_PATTERNS):
        assert window // r == DIL_SPAN
        sub = DIL_SPAN * r

        def body(idx, carry, r=r, sub=sub, pi=pi):
            n = idx // r
            rho = idx - n * r
            base = n * sub + rho
            q = q_ref[pl.ds(base, DIL_SPAN, stride=r), :]
            kc = kbuf[pl.ds(T + base - sub, 2 * DIL_SPAN, stride=r), :].astype(BF16)
            vc = vbuf[pl.ds(T + base - sub, 2 * DIL_SPAN, stride=r), :].astype(BF16)
            first_key = jnp.where((i == 0) & (n == 0), DIL_SPAN, 0)
            valid = band & (jj >= first_key)
            stats = []
            for hh in range(2):
                slope = jnp.where(pair == 0, 2.0 ** (-(2 * hh + 1)), 2.0 ** (-(2 * hh + 5)))
                qh = (jnp.where(upper, q, 0.0) if hh else jnp.where(upper, 0.0, q)).astype(BF16)
                s = _dot_nt(qh, kc) - (slope * r) * dist_f
                s = jnp.where(valid, s, NEG)
                m = jnp.max(s, axis=-1, keepdims=True)
                p = jnp.exp(s - m)
                l = jnp.sum(p, axis=-1, keepdims=True)
                stats.append((_dot(p.astype(BF16), vc), m, l))
            rows = pl.ds(pi * T + base, DIL_SPAN, stride=r)
            acc_s[rows, :] = jnp.where(upper, stats[1][0], stats[0][0])
            m_s[rows, :] = jnp.where(upper, stats[1][1], stats[0][1])
            l_s[rows, :] = jnp.where(upper, stats[1][2], stats[0][2])
            return carry

        lax.fori_loop(0, T // DIL_SPAN, body, 0, unroll=True)

    npat = len(DIL_PATTERNS)
    ms = [m_s[pi * T:(pi + 1) * T, :] for pi in range(npat)]
    m_top = functools.reduce(jnp.maximum, ms)
    num = jnp.zeros((T, LANES), F32)
    den = jnp.zeros((T, LANES), F32)
    for pi in range(npat):
        w = jnp.exp(ms[pi] - m_top)
        num = num + w * acc_s[pi * T:(pi + 1) * T, :]
        den = den + w * l_s[pi * T:(pi + 1) * T, :]
    o_ref[...] = (num / den).astype(o_ref.dtype)


def _dilated(q, k, v):
    seq = q.shape[0]
    T = DIL_TILE
    spec = pl.BlockSpec((T, LANES), lambda p, i: (i, p))
    npat = len(DIL_PATTERNS)
    return pl.pallas_call(
        _dilated_kernel,
        grid=(HEADS // 2, seq // T),
        in_specs=[spec, spec, spec],
        out_specs=spec,
        out_shape=jax.ShapeDtypeStruct((seq, GROUP), BF16),
        scratch_shapes=[pltpu.VMEM((2 * T, LANES), F32), pltpu.VMEM((2 * T, LANES), F32),
                        pltpu.VMEM((npat * T, LANES), F32), pltpu.VMEM((npat * T, LANES), F32),
                        pltpu.VMEM((npat * T, LANES), F32)],
        compiler_params=_params(("arbitrary", "arbitrary")),
        name="dilated",
    )(q, k, v)


def _post_kernel(x_ref, oa_ref, ob_ref, oc_ref, od_ref, wout_ref, mod_ref, g2_ref,
                 rw_ref, rb_ref, tri_ref,
                 x1_ref, h2_ref, eidx_ref, gate_ref, rank_ref, cnt_ref, carry):
    i = pl.program_id(0)
    tm = x_ref.shape[0]

    @pl.when(i == 0)
    def _():
        carry[...] = jnp.zeros_like(carry)

    o = (_dot(oa_ref[...], wout_ref[0]) + _dot(ob_ref[...], wout_ref[1])
         + _dot(oc_ref[...], wout_ref[2]) + _dot(od_ref[...], wout_ref[3]))
    x1 = x_ref[...] + mod_ref[2] * o
    x1_ref[...] = x1
    y = x1 * lax.rsqrt(jnp.mean(x1 * x1, axis=-1, keepdims=True) + EPS) * g2_ref[...]
    h2 = y * (1.0 + mod_ref[4]) + mod_ref[3]
    _store_token_tiles(h2_ref, h2)

    h_hi, h_lo = _split2(h2)
    hh = _dot(h_hi, rw_ref[...])
    logits = (hh[:, :LANES] + hh[:, LANES:] + _dot(h_lo, rw_ref[:, :LANES])
              + rb_ref[...])
    lane = lax.broadcasted_iota(jnp.int32, (tm, LANES), 1)
    lane_f = lane.astype(F32)
    g = logits
    chosen = jnp.zeros((tm, LANES), F32)
    vals, idxs = [], []
    for _ in range(TOP_K):
        m = jnp.max(g, axis=-1, keepdims=True)
        first = jnp.min(jnp.where(g == m, lane_f, 1e9), axis=-1, keepdims=True)
        pick = lane_f == first
        chosen = jnp.where(pick, 1.0, chosen)
        g = jnp.where(pick, -jnp.inf, g)
        vals.append(m)
        idxs.append(first)
    exps = [jnp.exp(v - vals[0]) for v in vals]
    den = exps[0] + exps[1] + exps[2] + exps[3]
    before = _dot(tri_ref[...], chosen.astype(BF16)) + carry[...]
    carry[...] = carry[...] + jnp.sum(chosen, axis=0, keepdims=True)
    cnt_ref[...] = carry[...]
    e_out = jnp.zeros((tm, LANES), F32)
    g_out = jnp.zeros((tm, LANES), F32)
    r_out = jnp.zeros((tm, LANES), F32)
    for k in range(TOP_K):
        rank_k = jnp.sum(jnp.where(lane_f == idxs[k], before, 0.0), axis=-1, keepdims=True)
        e_out = jnp.where(lane == k, idxs[k], e_out)
        g_out = jnp.where(lane == k, exps[k] / den, g_out)
        r_out = jnp.where(lane == k, rank_k, r_out)
    eidx_ref[...] = e_out.T[0:8, :].astype(jnp.int32)
    gate_ref[...] = g_out
    rank_ref[...] = r_out.T[0:8, :].astype(jnp.int32)


def _post(x2, oa, ob, oc, od, mod_l, consts, p):
    seq = x2.shape[0]
    tm = POST_TILE
    row = lambda n: pl.BlockSpec((tm, n), lambda i: (i, 0))
    full = [p["w_out"], mod_l, p["g2"], p["rw"], p["r_b"], consts["tri_strict"]]
    f32 = lambda n: jax.ShapeDtypeStruct((seq, n), F32)
    slots = pl.BlockSpec((8, tm), lambda i: (0, i))
    slots_shape = jax.ShapeDtypeStruct((8, seq), jnp.int32)
    return pl.pallas_call(
        _post_kernel,
        grid=(seq // tm,),
        in_specs=[row(D_MODEL)] + [row(GROUP)] * 4 + [_full_spec(a.shape) for a in full],
        out_specs=[row(D_MODEL), pl.BlockSpec((tm * CHUNKS, LANES), lambda i: (i, 0)),
                   slots, row(LANES), slots, _full_spec((1, LANES))],
        out_shape=[f32(D_MODEL), jax.ShapeDtypeStruct((seq * CHUNKS, LANES), F32),
                   slots_shape, f32(LANES), slots_shape, jax.ShapeDtypeStruct((1, LANES), F32)],
        scratch_shapes=[pltpu.VMEM((1, LANES), F32)],
        compiler_params=_params(("arbitrary",)),
        name="post",
    )(x2, oa, ob, oc, od, *full)


def _tile_copy(src, s, dst, d, sem):
    return pltpu.make_async_copy(src.at[pl.ds(pl.multiple_of(s * CHUNKS, CHUNKS), CHUNKS), :],
                                 dst.at[pl.ds(pl.multiple_of(d * CHUNKS, CHUNKS), CHUNKS), :], sem)


def _dispatch_kernel(dest_ref, padlo_ref, padn_ref, nu_ref, h_ref, xs_ref, zbuf, sem, zsem):
    i = pl.program_id(0)
    n = DISPATCH_TILE * TOP_K
    base = i * DISPATCH_TILE
    seq = pl.num_programs(0) * DISPATCH_TILE
    block_rows = EXPERT_ROWS * CHUNKS

    @pl.when(i == 0)
    def _():
        zbuf[...] = jnp.zeros_like(zbuf)

        def fill(wait):
            def go(copy):
                copy.wait() if wait else copy.start()

            def per_expert(e, carry):
                lo, cnt = padlo_ref[e], padn_ref[e]
                off = lo
                p = EXPERT_ROWS // 2
                while p >= 1:
                    rows = p * CHUNKS

                    @pl.when((cnt & p) != 0)
                    def _(off=off, rows=rows):
                        go(pltpu.make_async_copy(
                            zbuf.at[pl.ds(0, rows), :],
                            xs_ref.at[pl.ds(pl.multiple_of(off * CHUNKS, CHUNKS), rows), :], zsem))

                    off = off + (cnt & p)
                    p //= 2
                return carry

            lax.fori_loop(0, N_EXPERTS, per_expert, 0)

            def per_block(b, carry):
                go(pltpu.make_async_copy(
                    zbuf, xs_ref.at[pl.ds(pl.multiple_of(b * block_rows, block_rows), block_rows), :],
                    zsem))
                return carry

            lax.fori_loop(nu_ref[0], xs_ref.shape[0] // block_rows, per_block, 0)

        fill(False)
        fill(True)

    def issue(r, carry):
        for k in range(TOP_K):
            _tile_copy(h_ref, r, xs_ref, dest_ref[k * seq + base + r], sem).start(priority=k % 2)
        return carry

    lax.fori_loop(0, DISPATCH_TILE, issue, 0, unroll=4)
    rows = pl.ds(0, n * CHUNKS)
    pltpu.make_async_copy(xs_ref.at[rows, :], xs_ref.at[rows, :], sem).wait()


def _dispatch(dest, pad_lo, pad_n, n_used, h2_tiles, m_pad):
    seq = h2_tiles.shape[0] // CHUNKS
    return pl.pallas_call(
        _dispatch_kernel,
        grid_spec=pltpu.PrefetchScalarGridSpec(
            num_scalar_prefetch=4,
            grid=(seq // DISPATCH_TILE,),
            in_specs=[pl.BlockSpec((DISPATCH_TILE * CHUNKS, LANES), lambda i, *_: (i, 0))],
            out_specs=pl.BlockSpec(memory_space=pl.ANY),
            scratch_shapes=[pltpu.VMEM((EXPERT_ROWS * CHUNKS, LANES), F32),
                            pltpu.SemaphoreType.DMA(()), pltpu.SemaphoreType.DMA(())],
        ),
        out_shape=jax.ShapeDtypeStruct((m_pad * CHUNKS, LANES), F32),
        compiler_params=_params(("arbitrary",)),
        name="dispatch",
    )(dest, pad_lo, pad_n, n_used, h2_tiles)


def _expert_kernel(be_ref, nu_ref, nxt_ref, ord_ref, xs_ref, w1_hbm, b1_ref, w2_hbm, b2_ref, ys_ref,
                   w1f, w2f, w1b, w2b, sems):
    b = pl.program_id(0)
    e = be_ref[b]
    prev = be_ref[jnp.maximum(b - 1, 0)]
    fresh = ((b == 0) | (e != prev)) & (b < nu_ref[0])
    slot = ord_ref[b] % 2

    def fetch(expert, to_slot):
        return (pltpu.make_async_copy(w1_hbm.at[expert], w1f.at[to_slot], sems.at[0, to_slot]),
                pltpu.make_async_copy(w2_hbm.at[expert], w2f.at[to_slot], sems.at[1, to_slot]))

    @pl.when(b == 0)
    def _():
        for copy in fetch(e, slot):
            copy.start()

    @pl.when(fresh)
    def _():
        for copy in fetch(e, slot):
            copy.wait()
        w1b[...] = w1f[slot].astype(BF16)
        w2b[...] = w2f[slot].astype(BF16)

        @pl.when(nxt_ref[b] >= 0)
        def _():
            for copy in fetch(nxt_ref[b], 1 - slot):
                copy.start()

    @pl.when(b < nu_ref[0])
    def _():
        half = EXPERT_ROWS // 2
        gus = []
        for r in range(2):
            xb = jnp.concatenate(
                [_load_token_chunk(xs_ref, half, c, offset=r * half * CHUNKS) for c in range(CHUNKS)],
                axis=1).astype(BF16)
            gus.append(_dot(xb, w1b[...]) + b1_ref[0])
        for r in range(2):
            g = jnp.minimum(gus[r][:, :D_EXPERT], SWIGLU_LIMIT)
            u = jnp.clip(gus[r][:, D_EXPERT:], -SWIGLU_LIMIT, SWIGLU_LIMIT)
            y = (u + 1.0) * g * (1.0 / (1.0 + jnp.exp(-SWIGLU_ALPHA * g)))
            _store_token_tiles(ys_ref, _dot(y.astype(BF16), w2b[...]) + b2_ref[0],
                               offset=r * half * CHUNKS)

    @pl.when(b >= nu_ref[0])
    def _():
        ys_ref[...] = jnp.zeros_like(ys_ref)


def _experts(blk_expert, n_used, blk_next, blk_ord, xs, w1, b1, w2, b2):
    m_pad = xs.shape[0] // CHUNKS
    bm = EXPERT_ROWS
    n_all = w1.shape[0] * w1.shape[1]
    rows = lambda b, be, nu, *_: (jnp.minimum(b, nu[0] - 1), 0)
    ex = lambda b, be, nu, *_: (be[jnp.minimum(b, nu[0] - 1)], 0, 0)
    return pl.pallas_call(
        _expert_kernel,
        grid_spec=pltpu.PrefetchScalarGridSpec(
            num_scalar_prefetch=4,
            grid=(m_pad // bm,),
            in_specs=[
                pl.BlockSpec((bm * CHUNKS, LANES), rows),
                pl.BlockSpec(memory_space=pl.ANY),
                pl.BlockSpec((1, 1, 2 * D_EXPERT), ex),
                pl.BlockSpec(memory_space=pl.ANY),
                pl.BlockSpec((1, 1, D_MODEL), ex),
            ],
            out_specs=pl.BlockSpec((bm * CHUNKS, LANES), lambda b, *_: (b, 0)),
            scratch_shapes=[pltpu.VMEM((2, D_MODEL, 2 * D_EXPERT), F32),
                            pltpu.VMEM((2, D_EXPERT, D_MODEL), F32),
                            pltpu.VMEM((D_MODEL, 2 * D_EXPERT), BF16),
                            pltpu.VMEM((D_EXPERT, D_MODEL), BF16),
                            pltpu.SemaphoreType.DMA((2, 2))],
        ),
        out_shape=jax.ShapeDtypeStruct((m_pad * CHUNKS, LANES), F32),
        compiler_params=_params(("arbitrary",)),
        name="experts",
    )(blk_expert, n_used, blk_next, blk_ord, xs,
      w1.reshape(n_all, D_MODEL, 2 * D_EXPERT), b1.reshape(n_all, 1, -1),
      w2.reshape(n_all, D_EXPERT, D_MODEL), b2.reshape(n_all, 1, -1))


def _combine_kernel(dest_ref, ys_ref, x1_ref, gate_ref, mod_ref, o_ref, buf, sems):
    i = pl.program_id(0)
    tm = x1_ref.shape[0]
    n = tm * TOP_K
    slot = i % 2

    seq = pl.num_programs(0) * tm

    def gather(step, to_slot):
        base = step * tm

        def issue(r, carry):
            for k in range(TOP_K):
                _tile_copy(ys_ref, dest_ref[k * seq + base + r], buf, to_slot * n + k * tm + r,
                           sems.at[to_slot]).start(priority=k % 2)
            return carry

        lax.fori_loop(0, tm, issue, 0, unroll=4)

    @pl.when(i == 0)
    def _():
        gather(0, 0)

    @pl.when(i + 1 < pl.num_programs(0))
    def _():
        gather(i + 1, 1 - slot)

    mine = pl.ds(pl.multiple_of(slot * n * CHUNKS, n * CHUNKS), n * CHUNKS)
    pltpu.make_async_copy(ys_ref.at[pl.ds(0, n * CHUNKS), :], buf.at[mine, :], sems.at[slot]).wait()
    gates = gate_ref[...]
    g2 = mod_ref[5]
    for c in range(CHUNKS):
        cols = slice(c * LANES, (c + 1) * LANES)
        mix = jnp.zeros((tm, LANES), F32)
        for k in range(TOP_K):
            mix = mix + gates[:, k:k + 1] * _load_token_chunk(
                buf, tm, c, offset=(slot * n + k * tm) * CHUNKS)
        o_ref[:, cols] = x1_ref[:, cols] + g2[:, cols] * mix


def _combine(dest, ys, x1, gates, mod_l):
    seq = x1.shape[0]
    tm = COMBINE_TILE
    return pl.pallas_call(
        _combine_kernel,
        grid_spec=pltpu.PrefetchScalarGridSpec(
            num_scalar_prefetch=1,
            grid=(seq // tm,),
            in_specs=[
                pl.BlockSpec(memory_space=pl.ANY),
                pl.BlockSpec((tm, D_MODEL), lambda i, d: (i, 0)),
                pl.BlockSpec((tm, LANES), lambda i, d: (i, 0)),
                pl.BlockSpec(mod_l.shape, lambda i, d: (0, 0, 0)),
            ],
            out_specs=pl.BlockSpec((tm, D_MODEL), lambda i, d: (i, 0)),
            scratch_shapes=[pltpu.VMEM((2 * TOP_K * tm * CHUNKS, LANES), F32),
                            pltpu.SemaphoreType.DMA((2,))],
        ),
        out_shape=jax.ShapeDtypeStruct((seq, D_MODEL), F32),
        compiler_params=_params(("arbitrary",)),
        name="combine",
    )(dest, ys, x1, gates, mod_l)


def _moe(l, x1, h2_tiles, eidx, gates, rank, counts, mod_l, w1, b1, w2, b2):
    seq = x1.shape[0]
    bm = EXPERT_ROWS
    m_pad = seq * TOP_K + N_EXPERTS * bm
    cnt = counts[0, :N_EXPERTS].astype(jnp.int32)
    padded = (cnt + bm - 1) // bm * bm
    pad_end = jnp.cumsum(padded)
    pad_start = pad_end - padded
    onehot = eidx[:TOP_K, :, None] == jnp.arange(N_EXPERTS, dtype=jnp.int32)
    start_of = jnp.sum(jnp.where(onehot, pad_start, 0), axis=-1)
    dest = (start_of + rank[:TOP_K]).reshape(TOP_K * seq).astype(jnp.int32)
    nblk = m_pad // bm
    blk_start = jnp.arange(nblk, dtype=jnp.int32) * bm
    local = jnp.minimum(jnp.sum(pad_end[None, :] <= blk_start[:, None], axis=1), N_EXPERTS - 1)
    blk_expert = (local + l * N_EXPERTS).astype(jnp.int32)
    n_used = (pad_end[-1:] // bm).astype(jnp.int32)
    ids = jnp.arange(N_EXPERTS, dtype=jnp.int32)
    live = padded > 0
    ordinal = jnp.cumsum(live.astype(jnp.int32)) - 1
    later = live[None, :] & (ids[None, :] > ids[:, None])
    nxt = jnp.min(jnp.where(later, ids[None, :], N_EXPERTS), axis=1)
    nxt = jnp.where(nxt < N_EXPERTS, nxt + l * N_EXPERTS, -1)
    of_block = local[:, None] == ids[None, :]
    blk_next = jnp.sum(jnp.where(of_block, nxt[None, :], 0), axis=1).astype(jnp.int32)
    blk_ord = jnp.sum(jnp.where(of_block, ordinal[None, :], 0), axis=1).astype(jnp.int32)
    xs = _dispatch(dest, (pad_start + cnt).astype(jnp.int32), (padded - cnt).astype(jnp.int32),
                   n_used, h2_tiles, m_pad)
    ys = _experts(blk_expert, n_used, blk_next, blk_ord, xs, w1, b1, w2, b2)
    return _combine(dest, ys, x1, gates, mod_l)


def _pad_cols(a, n):
    return jnp.pad(a, ((0, 0), (0, n - a.shape[1])))


def _layer_params(l, w_in, mla_cq_g, mla_w_uq, mla_ckv_g, mla_w_ukv, mla_q_g, mla_k_g,
                  fox_q_g, fox_k_g, fox_b_f, moba_q_g, moba_k_g, dil_q_g, dil_k_g, w_out,
                  norm1_g, norm2_g, router_w, router_b):
    w_uq = jnp.pad(mla_w_uq[l].reshape(MLA_Q_RANK, HEADS, MLA_QK),
                   ((0, 0), (0, 0), (0, LANES - MLA_QK))).reshape(MLA_Q_RANK, HEADS * LANES)
    w_ukv = mla_w_ukv[l].reshape(MLA_KV_RANK, HEADS, MLA_NOPE + HEAD_DIM)
    w_uk = jnp.pad(w_ukv[:, :, :MLA_NOPE], ((0, 0), (0, 0), (0, LANES - MLA_NOPE)))
    w_uv = w_ukv[:, :, MLA_NOPE:]
    tile4 = lambda g: jnp.tile(g, HEADS)[None, :]
    rw = _pad_cols(router_w[l], LANES)
    rw_hi = rw.astype(BF16)
    rw_lo = (rw - rw_hi.astype(F32)).astype(BF16)
    r_b = jnp.concatenate([router_b[l], jnp.full((LANES - N_EXPERTS,), NEG, F32)])[None, :]
    return dict(
        g1=norm1_g[l][None, :], g2=norm2_g[l][None, :], layer=l,
        w_in=w_in.reshape(-1, w_in.shape[-1]),
        cq_g=mla_cq_g[l][None, :], w_uq=w_uq.astype(BF16), ckv_g=mla_ckv_g[l][None, :],
        w_uk=w_uk.reshape(MLA_KV_RANK, HEADS * LANES).astype(BF16),
        w_uvt=w_uv.reshape(MLA_KV_RANK, GROUP).T.astype(BF16),
        fox_bound=FOX_NORM_SLACK * HEAD_DIM ** 0.5 * jnp.max(jnp.abs(fox_q_g[l]))
        * jnp.max(jnp.abs(fox_k_g[l])),
        q_g=_pad_cols(mla_q_g[l][None, :], LANES), k_g=_pad_cols(mla_k_g[l][None, :], LANES),
        fq_g=tile4(fox_q_g[l]), fk_g=tile4(fox_k_g[l]), f_b=_pad_cols(fox_b_f[l][None, :], LANES),
        mq_g=tile4(moba_q_g[l]), mk_g=tile4(moba_k_g[l]), dq_g=tile4(dil_q_g[l]), dk_g=tile4(dil_k_g[l]),
        w_out=w_out[l].reshape(HEADS, GROUP, D_MODEL).astype(BF16),
        rw=jnp.concatenate([rw_hi, rw_lo], axis=1), r_b=r_b,
    )


def kernel(x, c, w_mod, b_mod, norm1_g, norm2_g, w_in, mla_cq_g, mla_w_uq, mla_ckv_g, mla_w_ukv, mla_q_g, mla_k_g, fox_q_g, fox_k_g, fox_b_f, moba_q_g, moba_k_g, dil_q_g, dil_k_g, w_out, router_w, router_b, exp_w1, exp_b1, exp_w2, exp_b2):
    batch, seq, d = x.shape
    assert batch == 1 and d == D_MODEL
    assert seq % DIL_TILE == 0 and seq // MOBA_BLOCK <= MOBA_MAX_BLOCKS
    depth = w_mod.shape[0]
    consts = _prep_constants(seq, TOKEN_TILE)
    mod = _modulation(c, w_mod, b_mod)
    slopes_c = 2.0 ** (-(2.0 * np.arange(HEADS) + 2.0))
    in_block = np.arange(FLASH_TILE) % MOBA_BLOCK
    kbias_c = jnp.asarray(np.broadcast_to(slopes_c[:, None, None] * in_block[None, :, None],
                                          (HEADS, FLASH_TILE, LANES)), F32)
    kbias_0 = jnp.zeros((HEADS, FLASH_TILE, LANES), F32)
    all_tiles = jnp.zeros((seq // FLASH_TILE,), jnp.int32)
    x2 = x.reshape(seq, d)
    for l in range(depth):
        p = _layer_params(l, w_in, mla_cq_g, mla_w_uq, mla_ckv_g, mla_w_ukv, mla_q_g, mla_k_g,
                          fox_q_g, fox_k_g, fox_b_f, moba_q_g, moba_k_g, dil_q_g, dil_k_g, w_out,
                          norm1_g, norm2_g, router_w, router_b)
        mod_l = mod[l]
        qa, ka, va, qf, kf, vf, qm, km, vm, qd, kd, vd, decay = _prep(x2, mod_l, consts, p)
        oa = _flash(qa, ka, va, kbias_0, all_tiles, False)
        ob = _flash(qf, kf, vf, kbias_0, _fox_first_tile(decay, p["fox_bound"]), False)
        oc = _flash(qm, km, vm, kbias_c, all_tiles, True)
        od = _dilated(qd, kd, vd)
        x1, h2, eidx, gates, rank, counts = _post(x2, oa, ob, oc, od, mod_l, consts, p)
        x2 = _moe(l, x1, h2, eidx, gates, rank, counts, mod_l, exp_w1, exp_b1, exp_w2, exp_b2)
    return x2.reshape(batch, seq, d)
```

```python
import functools

import numpy as np
import jax
import jax.numpy as jnp
from jax import lax
from jax.experimental import pallas as pl
from jax.experimental.pallas import tpu as pltpu

F32 = jnp.float32
BF16 = jnp.bfloat16

D_MODEL = 1024
HEAD_DIM = 64
HEADS = 4
GROUP = HEADS * HEAD_DIM
LANES = 128
CHUNKS = D_MODEL // LANES
MLA_Q_RANK = 256
MLA_KV_RANK = 128
MLA_NOPE = 64
MLA_ROPE = 32
MLA_QK = MLA_NOPE + MLA_ROPE
ROPE_THETA = 10000.0
MOBA_BLOCK = 256
MOBA_TOPK = 3
MOBA_MAX_BLOCKS = 64
DIL_PATTERNS = ((128, 1), (512, 4), (2048, 16))
DIL_SPAN = 128
DIL_TILE = 2048
N_EXPERTS = 32
TOP_K = 4
D_EXPERT = 1024
SWIGLU_LIMIT = 7.0
SWIGLU_ALPHA = 1.702
EPS = 1e-6
NEG = -1e30

FLASH_TILE = 512
ONES_ROWS = 16
FOX_SKIP_LOG = -106.0
FOX_NORM_SLACK = 1.02
TOKEN_TILE = 512
POST_TILE = 512
DISPATCH_TILE = 1024
COMBINE_TILE = 512
EXPERT_ROWS = 512
VMEM_LIMIT = 56 * 1024 * 1024

COLS_MLA = 512
COLS_FOX = 640
COLS_MOBA = 512
COLS_DIL = 768
COLS_IN = COLS_MLA + COLS_FOX + COLS_MOBA + COLS_DIL


def _in_column_map():
    sizes = [MLA_Q_RANK, MLA_KV_RANK, MLA_ROPE, GROUP, GROUP, GROUP, HEADS] + [GROUP] * 6
    src = np.concatenate([[0], np.cumsum(sizes)])
    cq, ckv, kr, fq, fk, fv, flog, mq, mk, mv, dq, dk, dv = range(13)
    layout = [(cq, 0), (ckv, MLA_Q_RANK), (kr, MLA_Q_RANK + MLA_KV_RANK + MLA_NOPE),
              (fq, COLS_MLA), (fk, COLS_MLA + GROUP), (flog, COLS_MLA + 2 * GROUP),
              (mq, COLS_MLA + COLS_FOX), (mk, COLS_MLA + COLS_FOX + GROUP),
              (dq, COLS_IN - 3 * GROUP), (dk, COLS_IN - 2 * GROUP), (dv, COLS_IN - GROUP)]
    moves = tuple((dst, int(src[piece]), sizes[piece]) for piece, dst in layout)
    return moves, (int(src[fv]), int(src[mv]))


IN_COLUMN_MAP, IN_VALUE_COLUMNS = _in_column_map()


def _dot(a, b):
    return jnp.dot(a, b, preferred_element_type=F32)


def _dot_nt(a, b):
    return lax.dot_general(a, b, (((1,), (1,)), ((), ())), preferred_element_type=F32)


def _split2(x):
    hi = x.astype(BF16)
    lo = (x - hi.astype(F32)).astype(BF16)
    return hi, lo


def _split3(x):
    a = x.astype(BF16)
    r = x - a.astype(F32)
    b = r.astype(BF16)
    c = (r - b.astype(F32)).astype(BF16)
    return a, b, c


def _head_of_lane():
    return jnp.right_shift(lax.broadcasted_iota(jnp.int32, (1, GROUP), 1), 6)


def _store_token_tiles(ref, x, offset=0):
    n = x.shape[0]
    for c in range(CHUNKS):
        ref[pl.ds(offset + c, n, stride=CHUNKS), :] = x[:, c * LANES:(c + 1) * LANES]


def _load_token_chunk(ref, n, c, offset=0):
    return ref[pl.ds(offset + c, n, stride=CHUNKS), :]


def _full_spec(shape):
    nd = len(shape)
    return pl.BlockSpec(shape, lambda *_: (0,) * nd)


def _params(sem):
    return pltpu.CompilerParams(dimension_semantics=sem, vmem_limit_bytes=VMEM_LIMIT)


MOD_GROUP = 3


def _mod_kernel(c_ref, w_ref, b_ref, o_ref):
    c = c_ref[...]
    s = c * (1.0 / (1.0 + jnp.exp(-c)))
    s8 = jnp.broadcast_to(s, (8, D_MODEL))
    r = jnp.dot(s8, w_ref[0], preferred_element_type=F32, precision=lax.Precision.HIGHEST)
    for v in range(MOD_GROUP):
        o_ref[0, v] = r[0:1, v * D_MODEL:(v + 1) * D_MODEL] + b_ref[0, v]


def _modulation(c, w_mod, b_mod):
    depth = w_mod.shape[0]
    b4 = b_mod.reshape(depth, 6, 1, D_MODEL)
    return pl.pallas_call(
        _mod_kernel,
        grid=(depth, 6 // MOD_GROUP),
        in_specs=[
            pl.BlockSpec((1, D_MODEL), lambda l, j: (0, 0)),
            pl.BlockSpec((1, D_MODEL, MOD_GROUP * D_MODEL), lambda l, j: (l, 0, j)),
            pl.BlockSpec((1, MOD_GROUP, 1, D_MODEL), lambda l, j: (l, j, 0, 0)),
        ],
        out_specs=pl.BlockSpec((1, MOD_GROUP, 1, D_MODEL), lambda l, j: (l, j, 0, 0)),
        out_shape=jax.ShapeDtypeStruct((depth, 6, 1, D_MODEL), F32),
        compiler_params=_params(("arbitrary", "arbitrary")),
        name="modulation",
    )(c, w_mod, b4)


def _head_norm(x, g, bd):
    hi, lo = _split2(x * x)
    ss = _dot(hi, bd) + _dot(lo, bd)
    return x * lax.rsqrt(ss * (1.0 / HEAD_DIM) + EPS) * g


def _prep_kernel(x_ref, mod_ref, g1_ref, win_hbm, cqg_ref, wuq_ref, ckvg_ref, wuk_ref,
                 wuvt_ref, qg_ref, kg_ref, cs_ref,
                 fqg_ref, fkg_ref, fb_ref, mqg_ref, mkg_ref, dqg_ref, dkg_ref,
                 bd_ref, tri_ref, eq_ref, ek_ref, sel_ref,
                 qa_ref, ka_ref, va_ref, qf_ref, kf_ref, vf_ref, qm_ref, km_ref, vm_ref,
                 qd_ref, kd_ref, vd_ref, f_ref,
                 fcarry, kmean, wraw_ref, win_ref, wvt_ref, rope_scr, wsem, *, layer):
    i = pl.program_id(0)
    tm = x_ref.shape[0]

    @pl.when(i == 0)
    def _():
        fetch = pltpu.make_async_copy(win_hbm.at[pl.ds(layer * D_MODEL, D_MODEL), :], wraw_ref, wsem)
        fetch.start()
        fcarry[...] = jnp.zeros_like(fcarry)
        kmean[...] = jnp.zeros_like(kmean)
        win_ref[...] = jnp.zeros_like(win_ref)
        rope_scr[...] = jnp.zeros_like(rope_scr)
        rope_scr[0, :, 0:MLA_NOPE] = jnp.ones((tm, MLA_NOPE), F32)
        fetch.wait()
        for dst, src, width in IN_COLUMN_MAP:
            win_ref[:, dst:dst + width] = wraw_ref[:, src:src + width].astype(BF16)
        for slot, src in enumerate(IN_VALUE_COLUMNS):
            wvt_ref[slot] = wraw_ref[:, src:src + GROUP].T.astype(BF16)

    x = x_ref[...]
    y = x * lax.rsqrt(jnp.mean(x * x, axis=-1, keepdims=True) + EPS) * g1_ref[...]
    hb = (y * (1.0 + mod_ref[1]) + mod_ref[0]).astype(BF16)
    bd = bd_ref[...]
    lane = lax.broadcasted_iota(jnp.int32, (1, LANES), 1)
    lane_f = lane.astype(F32)
    head_of_lane = _head_of_lane()

    pa = _dot(hb, win_ref[:, 0:COLS_MLA])
    cq = pa[:, 0:MLA_Q_RANK]
    ckv = pa[:, MLA_Q_RANK:MLA_Q_RANK + MLA_KV_RANK]
    kr = pa[:, MLA_Q_RANK + MLA_KV_RANK:COLS_MLA]
    cqn = (cq * lax.rsqrt(jnp.mean(cq * cq, axis=-1, keepdims=True) + EPS) * cqg_ref[...]).astype(BF16)
    ckvn = (ckv * lax.rsqrt(jnp.mean(ckv * ckv, axis=-1, keepdims=True) + EPS) * ckvg_ref[...]).astype(BF16)
    q_all = _dot(cqn, wuq_ref[...])
    k_all = _dot(ckvn, wuk_ref[...])
    va_ref[...] = _dot_nt(wuvt_ref[...], ckvn).astype(BF16)
    half = MLA_ROPE // 2
    cos, sin = cs_ref[:, 0:half], cs_ref[:, half:MLA_ROPE]
    rope_scr[0, :, MLA_NOPE:MLA_NOPE + half] = cos
    rope_scr[0, :, MLA_NOPE + half:MLA_QK] = cos
    rope_scr[1, :, MLA_NOPE:MLA_NOPE + half] = -sin
    rope_scr[2, :, MLA_NOPE + half:MLA_QK] = sin
    rc, rs1, rs2 = rope_scr[0], rope_scr[1], rope_scr[2]

    def rope(t):
        return t * rc + pltpu.roll(t, LANES - MLA_ROPE // 2, 1) * rs1 + pltpu.roll(t, MLA_ROPE // 2, 1) * rs2

    for h in range(HEADS):
        q = q_all[:, h * LANES:(h + 1) * LANES]
        q = q * lax.rsqrt(jnp.sum(q * q, axis=-1, keepdims=True) * (1.0 / MLA_QK) + EPS) * qg_ref[...]
        qa_ref[h] = (rope(q) * (MLA_QK ** -0.5)).astype(BF16)
        k = k_all[:, h * LANES:(h + 1) * LANES] + kr
        k = k * lax.rsqrt(jnp.sum(k * k, axis=-1, keepdims=True) * (1.0 / MLA_QK) + EPS) * kg_ref[...]
        ka_ref[h] = rope(k).astype(BF16)

    pf = _dot(hb, win_ref[:, COLS_MLA:COLS_MLA + COLS_FOX])
    fqn = (_head_norm(pf[:, 0:GROUP], fqg_ref[...], bd) * (HEAD_DIM ** -0.5)).astype(BF16)
    fkn = _head_norm(pf[:, GROUP:2 * GROUP], fkg_ref[...], bd).astype(BF16)
    vf_ref[...] = _dot_nt(wvt_ref[0], hb).astype(BF16)
    z = pf[:, 2 * GROUP:2 * GROUP + LANES] + fb_ref[...]
    log_f = jnp.minimum(z, 0.0) - jnp.log(1.0 + jnp.exp(-jnp.abs(z)))
    tri = tri_ref[...]
    a1, a2, a3 = _split3(log_f)
    cum = fcarry[...] + (_dot(tri, a1) + _dot(tri, a2) + _dot(tri, a3))
    fcarry[...] = cum[tm - 1:tm, :]
    f_ref[...] = cum
    f1, f2, f3 = _split3(cum)
    xq = jnp.concatenate([fqn, f1, f2, f3], axis=1)
    xk = jnp.concatenate([fkn, f1, f2, f3], axis=1)
    ones_q = jnp.where((lane >= HEAD_DIM + 3) & (lane < HEAD_DIM + 6), 1.0, 0.0)
    ones_k = jnp.where((lane >= HEAD_DIM) & (lane < HEAD_DIM + 3), 1.0, 0.0)
    qf_all = _dot(xq, eq_ref[...])
    kf_all = _dot(xk, ek_ref[...])
    for h in range(HEADS):
        qf_ref[h] = (qf_all[:, h * LANES:(h + 1) * LANES] + ones_q).astype(BF16)
        kf_ref[h] = (kf_all[:, h * LANES:(h + 1) * LANES] + ones_k).astype(BF16)

    pm = _dot(hb, win_ref[:, COLS_MLA + COLS_FOX:COLS_MLA + COLS_FOX + COLS_MOBA])
    mqn = _head_norm(pm[:, 0:GROUP], mqg_ref[...], bd) * (HEAD_DIM ** -0.5)
    mkn = _head_norm(pm[:, GROUP:2 * GROUP], mkg_ref[...], bd)
    vm_ref[...] = _dot_nt(wvt_ref[1], hb).astype(BF16)
    mqb = mqn.astype(BF16)
    mkb = mkn.astype(BF16)
    blocks = tm // MOBA_BLOCK
    blk = lane - HEAD_DIM
    blk_f = blk.astype(F32)
    row_block = jnp.right_shift(lax.broadcasted_iota(jnp.int32, (tm, 1), 0),
                                MOBA_BLOCK.bit_length() - 1)
    own = i * blocks + row_block
    own_f = own.astype(F32)
    past = (blk >= 0) & (blk < own)
    for b in range(blocks):
        col_mean = jnp.mean(mkn[b * MOBA_BLOCK:(b + 1) * MOBA_BLOCK, :], axis=0, keepdims=True)
        for h in range(HEADS):
            kmean[pl.ds(h * LANES + HEAD_DIM + i * blocks + b, 1), :] = jnp.where(
                head_of_lane == h, col_mean, 0.0)
    q_hi, q_lo = _split2(mqn)
    km_hi, km_lo = _split2(kmean[...])
    gate_all = _dot_nt(q_hi, km_hi) + _dot_nt(q_hi, km_lo) + _dot_nt(q_lo, km_hi)
    qm_all = _dot(mqb, sel_ref[...])
    km_all = _dot(mkb, sel_ref[...])
    for h in range(HEADS):
        g = jnp.where(past, gate_all[:, h * LANES:(h + 1) * LANES], NEG)
        chosen = jnp.zeros((tm, LANES), F32)
        for _ in range(MOBA_TOPK):
            m = jnp.max(g, axis=-1, keepdims=True)
            first = jnp.min(jnp.where(g == m, lane_f, 1e9), axis=-1, keepdims=True)
            pick = (lane_f == first) & (m > NEG)
            chosen = jnp.where(pick, 1.0, chosen)
            g = jnp.where(pick, NEG, g)
        slope = 2.0 ** (-(2 * h + 2))
        keep = (chosen > 0.0) | (blk == own)
        bias = jnp.where(keep, (slope * MOBA_BLOCK) * (blk_f - own_f), NEG)
        bias = jnp.where(blk >= 0, bias, 0.0)
        qm_ref[h] = (qm_all[:, h * LANES:(h + 1) * LANES] + bias).astype(BF16)
        onehot = jnp.where(blk == own, 1.0, 0.0)
        km_ref[h] = (km_all[:, h * LANES:(h + 1) * LANES] + onehot).astype(BF16)

    pd = _dot(hb, win_ref[:, COLS_MLA + COLS_FOX + COLS_MOBA:COLS_IN])
    qd_ref[...] = _head_norm(pd[:, 0:GROUP], dqg_ref[...], bd) * (HEAD_DIM ** -0.5)
    kd_ref[...] = _head_norm(pd[:, GROUP:2 * GROUP], dkg_ref[...], bd)
    vd_ref[...] = pd[:, 2 * GROUP:3 * GROUP]


def _prep_constants(seq, tm):
    half = MLA_ROPE // 2
    inv = 1.0 / (ROPE_THETA ** (jnp.arange(half, dtype=F32) / half))
    ang = jnp.arange(seq, dtype=F32)[:, None] * inv[None, :]
    cs = jnp.concatenate([jnp.cos(ang), jnp.sin(ang)], axis=1)
    bd =np.kron(np.eye(HEADS, dtype=np.float32), np.ones((HEAD_DIM, HEAD_DIM), np.float32))
    tri = np.tril(np.ones((tm, tm), np.float32))
    tri_strict = np.tril(np.ones((POST_TILE, POST_TILE), np.float32), -1)
    sel = np.zeros((GROUP, HEADS * LANES), np.float32)
    eq = np.zeros((GROUP + 3 * LANES, HEADS * LANES), np.float32)
    ek = np.zeros((GROUP + 3 * LANES, HEADS * LANES), np.float32)
    for h in range(HEADS):
        for d in range(HEAD_DIM):
            sel[h * HEAD_DIM + d, h * LANES + d] = 1.0
        for piece in range(3):
            eq[GROUP + piece * LANES + h, h * LANES + HEAD_DIM + piece] = 1.0
            ek[GROUP + piece * LANES + h, h * LANES + HEAD_DIM + 3 + piece] = -1.0
    eq[:GROUP] = sel
    ek[:GROUP] = sel
    as_bf = lambda a: jnp.asarray(a, BF16)
    return dict(cs=cs, bd=as_bf(bd),
                tri=as_bf(tri), tri_strict=as_bf(tri_strict), sel=as_bf(sel), eq=as_bf(eq), ek=as_bf(ek))


def _prep(x2, mod_l, consts, p):
    seq = x2.shape[0]
    tm = TOKEN_TILE
    row = lambda n: pl.BlockSpec((tm, n), lambda i: (i, 0))
    heads = pl.BlockSpec((HEADS, tm, LANES), lambda i: (0, i, 0))
    in_arrays = [
        (x2, row(D_MODEL)), (mod_l, _full_spec(mod_l.shape)), (p["g1"], None),
        (p["w_in"], pl.BlockSpec(memory_space=pl.ANY)),
        (p["cq_g"], None), (p["w_uq"], None), (p["ckv_g"], None), (p["w_uk"], None), (p["w_uvt"], None),
        (p["q_g"], None), (p["k_g"], None),
        (consts["cs"], row(MLA_ROPE)),
        (p["fq_g"], None), (p["fk_g"], None), (p["f_b"], None), (p["mq_g"], None), (p["mk_g"], None),
        (p["dq_g"], None), (p["dk_g"], None),
        (consts["bd"], None), (consts["tri"], None), (consts["eq"], None), (consts["ek"], None),
        (consts["sel"], None),
    ]
    args = [a for a, _ in in_arrays]
    specs = [s if s is not None else _full_spec(a.shape) for a, s in in_arrays]
    hshape = jax.ShapeDtypeStruct((HEADS, seq, LANES), BF16)
    vshape = jax.ShapeDtypeStruct((GROUP, seq), BF16)
    dshape = jax.ShapeDtypeStruct((seq, GROUP), F32)
    vt = pl.BlockSpec((GROUP, tm), lambda i: (0, i))
    return pl.pallas_call(
        functools.partial(_prep_kernel, layer=p["layer"]),
        grid=(seq // tm,),
        in_specs=specs,
        out_specs=[heads, heads, vt] * 3 + [row(GROUP)] * 3 + [row(LANES)],
        out_shape=[hshape, hshape, vshape] * 3 + [dshape] * 3
                  + [jax.ShapeDtypeStruct((seq, LANES), F32)],
        scratch_shapes=[pltpu.VMEM((1, LANES), F32), pltpu.VMEM((HEADS * LANES, GROUP), F32),
                        pltpu.VMEM((D_MODEL, p["w_in"].shape[-1]), F32),
                        pltpu.VMEM((D_MODEL, COLS_IN), BF16),
                        pltpu.VMEM((len(IN_VALUE_COLUMNS), GROUP, D_MODEL), BF16),
                        pltpu.VMEM((3, tm, LANES), F32),
                        pltpu.SemaphoreType.DMA(())],
        compiler_params=_params(("arbitrary",)),
        name="prep",
    )(*args)


def _flash_kernel(jlo_ref, q_ref, k_ref, vt_ref, kbias_ref, o_ref, m_sc, acc_sc, sa_sc, sb_sc,
                  *, tile, use_kbias):
    i = pl.program_id(0)
    m_sc[...] = jnp.full_like(m_sc, -jnp.inf)
    acc_sc[...] = jnp.zeros_like(acc_sc)

    def score(hh, j, buf):
        start = pl.multiple_of(j * tile, tile)
        buf[hh] = _dot_nt(k_ref[hh, pl.ds(start, tile), :], q_ref[hh])

    def absorb(hh, j, buf, causal):
        start = pl.multiple_of(j * tile, tile)
        s = buf[hh]
        if use_kbias:
            s = s + jnp.concatenate([kbias_ref[hh]] * (tile // LANES), axis=1)
        if causal:
            key = lax.broadcasted_iota(jnp.int32, (tile, tile), 0)
            qry = lax.broadcasted_iota(jnp.int32, (tile, tile), 1)
            s = jnp.where(key <= qry, s, NEG)
        m_prev = m_sc[hh]
        m_cur = jnp.max(jnp.max(s.reshape(8, tile // 8, tile), axis=0), axis=0, keepdims=True)
        m_new = jnp.maximum(m_prev, m_cur)
        alpha = jnp.exp(m_prev - m_new)
        p = jnp.exp((s - m_new).astype(BF16))
        vt = vt_ref[hh * HEAD_DIM:(hh + 1) * HEAD_DIM, pl.ds(start, tile)]
        vt = jnp.concatenate([vt, jnp.ones((ONES_ROWS, tile), BF16)], axis=0)
        acc_sc[hh] = alpha * acc_sc[hh] + _dot(vt, p)
        m_sc[hh] = m_new

    first = jlo_ref[i]
    n_off = i - first
    for hh in range(HEADS):
        score(hh, first, sa_sc)

    def body(t, carry):
        j = first + 2 * t
        for hh in range(HEADS):
            score(hh, j + 1, sb_sc)
            absorb(hh, j, sa_sc, False)
        for hh in range(HEADS):
            score(hh, j + 2, sa_sc)
            absorb(hh, j + 1, sb_sc, False)
        return carry

    lax.fori_loop(0, n_off // 2, body, 0)

    @pl.when(n_off % 2 == 1)
    def _():
        for hh in range(HEADS):
            score(hh, i, sb_sc)
            absorb(hh, i - 1, sa_sc, False)
        for hh in range(HEADS):
            absorb(hh, i, sb_sc, True)

    @pl.when(n_off % 2 == 0)
    def _():
        for hh in range(HEADS):
            absorb(hh, i, sa_sc, True)

    o_t = jnp.concatenate([acc_sc[hh, 0:HEAD_DIM, :] / acc_sc[hh, HEAD_DIM:HEAD_DIM + 1, :]
                           for hh in range(HEADS)], axis=0)
    o_ref[...] = o_t.T.astype(o_ref.dtype)


def _flash(q, k, v_t, kbias, first_tile, use_kbias):
    seq = v_t.shape[1]
    tile = FLASH_TILE
    kern = functools.partial(_flash_kernel, tile=tile, use_kbias=use_kbias)
    resident = pl.Buffered(1)
    return pl.pallas_call(
        kern,
        grid_spec=pltpu.PrefetchScalarGridSpec(
            num_scalar_prefetch=1,
            grid=(seq // tile,),
            in_specs=[
                pl.BlockSpec((HEADS, tile, LANES), lambda i, f: (0, i, 0)),
                pl.BlockSpec((HEADS, seq, LANES), lambda i, f: (0, 0, 0), pipeline_mode=resident),
                pl.BlockSpec((GROUP, seq), lambda i, f: (0, 0), pipeline_mode=resident),
                pl.BlockSpec((HEADS, tile, LANES), lambda i, f: (0, 0, 0)),
            ],
            out_specs=pl.BlockSpec((tile, GROUP), lambda i, f: (i, 0)),
            scratch_shapes=[pltpu.VMEM((HEADS, 1, tile), F32),
                            pltpu.VMEM((HEADS, HEAD_DIM + ONES_ROWS, tile), F32),
                            pltpu.VMEM((HEADS, tile, tile), F32), pltpu.VMEM((HEADS, tile, tile), F32)],
        ),
        out_shape=jax.ShapeDtypeStruct((seq, GROUP), BF16),
        compiler_params=_params(("arbitrary",)),
        name="flash_kbias" if use_kbias else "flash",
    )(first_tile, q, k, v_t, kbias)


def _fox_first_tile(decay, qk_bound):
    seq = decay.shape[0]
    nq = seq // FLASH_TILE
    f = decay[:, :HEADS]
    f_first = f[0::FLASH_TILE]
    f_last = f[FLASH_TILE - 1::FLASH_TILE]
    gap = f_first[:, None, :] - f_last[None, :, :] + 2.0 * qk_bound
    jj = jnp.arange(nq, dtype=jnp.int32)
    needed = (gap >= FOX_SKIP_LOG) | (jj[None, :, None] >= jj[:, None, None])
    first = jnp.min(jnp.where(needed, jj[None, :, None], nq), axis=1)
    return jnp.min(first, axis=1).astype(jnp.int32)


def _dilated_kernel(q_ref, k_ref, v_ref, o_ref, kbuf, vbuf, acc_s, m_s, l_s):
    pair = pl.program_id(0)
    i = pl.program_id(1)
    T = q_ref.shape[0]

    @pl.when(i == 0)
    def _():
        kbuf[...] = jnp.zeros_like(kbuf)
        vbuf[...] = jnp.zeros_like(vbuf)

    kbuf[0:T, :] = kbuf[T:2 * T, :]
    vbuf[0:T, :] = vbuf[T:2 * T, :]
    kbuf[T:2 * T, :] = k_ref[...]
    vbuf[T:2 * T, :] = v_ref[...]

    ii = lax.broadcasted_iota(jnp.int32, (DIL_SPAN, 2 * DIL_SPAN), 0)
    jj = lax.broadcasted_iota(jnp.int32, (DIL_SPAN, 2 * DIL_SPAN), 1)
    dist = ii + DIL_SPAN - jj
    band = (dist >= 0) & (dist <= DIL_SPAN)
    dist_f = dist.astype(F32)
    upper = lax.broadcasted_iota(jnp.int32, (1, LANES), 1) >= HEAD_DIM

    for pi, (window, r) in enumerate(DIL_PATTERNS):
        assert window // r == DIL_SPAN
        sub = DIL_SPAN * r

        def body(idx, carry, r=r, sub=sub, pi=pi):
            n = idx // r
            rho = idx - n * r
            base = n * sub + rho
            q = q_ref[pl.ds(base, DIL_SPAN, stride=r), :]
            kc = kbuf[pl.ds(T + base - sub, 2 * DIL_SPAN, stride=r), :].astype(BF16)
            vc = vbuf[pl.ds(T + base - sub, 2 * DIL_SPAN, stride=r), :].astype(BF16)
            first_key = jnp.where((i == 0) & (n == 0), DIL_SPAN, 0)
            valid = band & (jj >= first_key)
            stats = []
            for hh in range(2):
                slope = jnp.where(pair == 0, 2.0 ** (-(2 * hh + 1)), 2.0 ** (-(2 * hh + 5)))
                qh = (jnp.where(upper, q, 0.0) if hh else jnp.where(upper, 0.0, q)).astype(BF16)
                s = _dot_nt(qh, kc) - (slope * r) * dist_f
                s = jnp.where(valid, s, NEG)
                m = jnp.max(s, axis=-1, keepdims=True)
                p = jnp.exp(s - m)
                l = jnp.sum(p, axis=-1, keepdims=True)
                stats.append((_dot(p.astype(BF16), vc), m, l))
            rows = pl.ds(pi * T + base, DIL_SPAN, stride=r)
            acc_s[rows, :] = jnp.where(upper, stats[1][0], stats[0][0])
            m_s[rows, :] = jnp.where(upper, stats[1][1], stats[0][1])
            l_s[rows, :] = jnp.where(upper, stats[1][2], stats[0][2])
            return carry

        lax.fori_loop(0, T // DIL_SPAN, body, 0, unroll=True)

    npat = len(DIL_PATTERNS)
    ms = [m_s[pi * T:(pi + 1) * T, :] for pi in range(npat)]
    m_top = functools.reduce(jnp.maximum, ms)
    num = jnp.zeros((T, LANES), F32)
    den = jnp.zeros((T, LANES), F32)
    for pi in range(npat):
        w = jnp.exp(ms[pi] - m_top)
        num = num + w * acc_s[pi * T:(pi + 1) * T, :]
        den = den + w * l_s[pi * T:(pi + 1) * T, :]
    o_ref[...] = (num / den).astype(o_ref.dtype)


def _dilated(q, k, v):
    seq = q.shape[0]
    T = DIL_TILE
    spec = pl.BlockSpec((T, LANES), lambda p, i: (i, p))
    npat = len(DIL_PATTERNS)
    return pl.pallas_call(
        _dilated_kernel,
        grid=(HEADS // 2, seq // T),
        in_specs=[spec, spec, spec],
        out_specs=spec,
        out_shape=jax.ShapeDtypeStruct((seq, GROUP), BF16),
        scratch_shapes=[pltpu.VMEM((2 * T, LANES), F32), pltpu.VMEM((2 * T, LANES), F32),
                        pltpu.VMEM((npat * T, LANES), F32), pltpu.VMEM((npat * T, LANES), F32),
                        pltpu.VMEM((npat * T, LANES), F32)],
        compiler_params=_params(("arbitrary", "arbitrary")),
        name="dilated",
    )(q, k, v)


def _post_kernel(x_ref, oa_ref, ob_ref, oc_ref, od_ref, wout_ref, mod_ref, g2_ref,
                 rw_ref, rb_ref, tri_ref,
                 x1_ref, h2_ref, eidx_ref, gate_ref, rank_ref, cnt_ref, carry):
    i = pl.program_id(0)
    tm = x_ref.shape[0]

    @pl.when(i == 0)
    def _():
        carry[...] = jnp.zeros_like(carry)

    o = (_dot(oa_ref[...], wout_ref[0]) + _dot(ob_ref[...], wout_ref[1])
         + _dot(oc_ref[...], wout_ref[2]) + _dot(od_ref[...], wout_ref[3]))
    x1 = x_ref[...] + mod_ref[2] * o
    x1_ref[...] = x1
    y = x1 * lax.rsqrt(jnp.mean(x1 * x1, axis=-1, keepdims=True) + EPS) * g2_ref[...]
    h2 = y * (1.0 + mod_ref[4]) + mod_ref[3]
    _store_token_tiles(h2_ref, h2)

    h_hi, h_lo = _split2(h2)
    hh = _dot(h_hi, rw_ref[...])
    logits = (hh[:, :LANES] + hh[:, LANES:] + _dot(h_lo, rw_ref[:, :LANES])
              + rb_ref[...])
    lane = lax.broadcasted_iota(jnp.int32, (tm, LANES), 1)
    lane_f = lane.astype(F32)
    g = logits
    chosen = jnp.zeros((tm, LANES), F32)
    vals, idxs = [], []
    for _ in range(TOP_K):
        m = jnp.max(g, axis=-1, keepdims=True)
        first = jnp.min(jnp.where(g == m, lane_f, 1e9), axis=-1, keepdims=True)
        pick = lane_f == first
        chosen = jnp.where(pick, 1.0, chosen)
        g = jnp.where(pick, -jnp.inf, g)
        vals.append(m)
        idxs.append(first)
    exps = [jnp.exp(v - vals[0]) for v in vals]
    den = exps[0] + exps[1] + exps[2] + exps[3]
    before = _dot(tri_ref[...], chosen.astype(BF16)) + carry[...]
    carry[...] = carry[...] + jnp.sum(chosen, axis=0, keepdims=True)
    cnt_ref[...] = carry[...]
    e_out = jnp.zeros((tm, LANES), F32)
    g_out = jnp.zeros((tm, LANES), F32)
    r_out = jnp.zeros((tm, LANES), F32)
    for k in range(TOP_K):
        rank_k = jnp.sum(jnp.where(lane_f == idxs[k], before, 0.0), axis=-1, keepdims=True)
        e_out = jnp.where(lane == k, idxs[k], e_out)
        g_out = jnp.where(lane == k, exps[k] / den, g_out)
        r_out = jnp.where(lane == k, rank_k, r_out)
    eidx_ref[...] = e_out.T[0:8, :].astype(jnp.int32)
    gate_ref[...] = g_out
    rank_ref[...] = r_out.T[0:8, :].astype(jnp.int32)


def _post(x2, oa, ob, oc, od, mod_l, consts, p):
    seq = x2.shape[0]
    tm = POST_TILE
    row = lambda n: pl.BlockSpec((tm, n), lambda i: (i, 0))
    full = [p["w_out"], mod_l, p["g2"], p["rw"], p["r_b"], consts["tri_strict"]]
    f32 = lambda n: jax.ShapeDtypeStruct((seq, n), F32)
    slots = pl.BlockSpec((8, tm), lambda i: (0, i))
    slots_shape = jax.ShapeDtypeStruct((8, seq), jnp.int32)
    return pl.pallas_call(
        _post_kernel,
        grid=(seq // tm,),
        in_specs=[row(D_MODEL)] + [row(GROUP)] * 4 + [_full_spec(a.shape) for a in full],
        out_specs=[row(D_MODEL), pl.BlockSpec((tm * CHUNKS, LANES), lambda i: (i, 0)),
                   slots, row(LANES), slots, _full_spec((1, LANES))],
        out_shape=[f32(D_MODEL), jax.ShapeDtypeStruct((seq * CHUNKS, LANES), F32),
                   slots_shape, f32(LANES), slots_shape, jax.ShapeDtypeStruct((1, LANES), F32)],
        scratch_shapes=[pltpu.VMEM((1, LANES), F32)],
        compiler_params=_params(("arbitrary",)),
        name="post",
    )(x2, oa, ob, oc, od, *full)


def _tile_copy(src, s, dst, d, sem):
    return pltpu.make_async_copy(src.at[pl.ds(pl.multiple_of(s * CHUNKS, CHUNKS), CHUNKS), :],
                                 dst.at[pl.ds(pl.multiple_of(d * CHUNKS, CHUNKS), CHUNKS), :], sem)


def _dispatch_kernel(dest_ref, padlo_ref, padn_ref, nu_ref, h_ref, xs_ref, zbuf, sem, zsem):
    i = pl.program_id(0)
    n = DISPATCH_TILE * TOP_K
    base = i * DISPATCH_TILE
    seq = pl.num_programs(0) * DISPATCH_TILE
    block_rows = EXPERT_ROWS * CHUNKS

    @pl.when(i == 0)
    def _():
        zbuf[...] = jnp.zeros_like(zbuf)

        def fill(wait):
            def go(copy):
                copy.wait() if wait else copy.start()

            def per_expert(e, carry):
                lo, cnt = padlo_ref[e], padn_ref[e]
                off = lo
                p = EXPERT_ROWS // 2
                while p >= 1:
                    rows = p * CHUNKS

                    @pl.when((cnt & p) != 0)
                    def _(off=off, rows=rows):
                        go(pltpu.make_async_copy(
                            zbuf.at[pl.ds(0, rows), :],
                            xs_ref.at[pl.ds(pl.multiple_of(off * CHUNKS, CHUNKS), rows), :], zsem))

                    off = off + (cnt & p)
                    p //= 2
                return carry

            lax.fori_loop(0, N_EXPERTS, per_expert, 0)

            def per_block(b, carry):
                go(pltpu.make_async_copy(
                    zbuf, xs_ref.at[pl.ds(pl.multiple_of(b * block_rows, block_rows), block_rows), :],
                    zsem))
                return carry

            lax.fori_loop(nu_ref[0], xs_ref.shape[0] // block_rows, per_block, 0)

        fill(False)
        fill(True)

    def issue(r, carry):
        for k in range(TOP_K):
            _tile_copy(h_ref, r, xs_ref, dest_ref[k * seq + base + r], sem).start(priority=k % 2)
        return carry

    lax.fori_loop(0, DISPATCH_TILE, issue, 0, unroll=4)
    rows = pl.ds(0, n * CHUNKS)
    pltpu.make_async_copy(xs_ref.at[rows, :], xs_ref.at[rows, :], sem).wait()


def _dispatch(dest, pad_lo, pad_n, n_used, h2_tiles, m_pad):
    seq = h2_tiles.shape[0] // CHUNKS
    return pl.pallas_call(
        _dispatch_kernel,
        grid_spec=pltpu.PrefetchScalarGridSpec(
            num_scalar_prefetch=4,
            grid=(seq // DISPATCH_TILE,),
            in_specs=[pl.BlockSpec((DISPATCH_TILE * CHUNKS, LANES), lambda i, *_: (i, 0))],
            out_specs=pl.BlockSpec(memory_space=pl.ANY),
            scratch_shapes=[pltpu.VMEM((EXPERT_ROWS * CHUNKS, LANES), F32),
                            pltpu.SemaphoreType.DMA(()), pltpu.SemaphoreType.DMA(())],
        ),
        out_shape=jax.ShapeDtypeStruct((m_pad * CHUNKS, LANES), F32),
        compiler_params=_params(("arbitrary",)),
        name="dispatch",
    )(dest, pad_lo, pad_n, n_used, h2_tiles)


def _expert_kernel(be_ref, nu_ref, nxt_ref, ord_ref, xs_ref, w1_hbm, b1_ref, w2_hbm, b2_ref, ys_ref,
                   w1f, w2f, w1b, w2b, sems):
    b = pl.program_id(0)
    e = be_ref[b]
    prev = be_ref[jnp.maximum(b - 1, 0)]
    fresh = ((b == 0) | (e != prev)) & (b < nu_ref[0])
    slot = ord_ref[b] % 2

    def fetch(expert, to_slot):
        return (pltpu.make_async_copy(w1_hbm.at[expert], w1f.at[to_slot], sems.at[0, to_slot]),
                pltpu.make_async_copy(w2_hbm.at[expert], w2f.at[to_slot], sems.at[1, to_slot]))

    @pl.when(b == 0)
    def _():
        for copy in fetch(e, slot):
            copy.start()

    @pl.when(fresh)
    def _():
        for copy in fetch(e, slot):
            copy.wait()
        w1b[...] = w1f[slot].astype(BF16)
        w2b[...] = w2f[slot].astype(BF16)

        @pl.when(nxt_ref[b] >= 0)
        def _():
            for copy in fetch(nxt_ref[b], 1 - slot):
                copy.start()

    @pl.when(b < nu_ref[0])
    def _():
        half = EXPERT_ROWS // 2
        gus = []
        for r in range(2):
            xb = jnp.concatenate(
                [_load_token_chunk(xs_ref, half, c, offset=r * half * CHUNKS) for c in range(CHUNKS)],
                axis=1).astype(BF16)
            gus.append(_dot(xb, w1b[...]) + b1_ref[0])
        for r in range(2):
            g = jnp.minimum(gus[r][:, :D_EXPERT], SWIGLU_LIMIT)
            u = jnp.clip(gus[r][:, D_EXPERT:], -SWIGLU_LIMIT, SWIGLU_LIMIT)
            y = (u + 1.0) * g * (1.0 / (1.0 + jnp.exp(-SWIGLU_ALPHA * g)))
            _store_token_tiles(ys_ref, _dot(y.astype(BF16), w2b[...]) + b2_ref[0],
                               offset=r * half * CHUNKS)

    @pl.when(b >= nu_ref[0])
    def _():
        ys_ref[...] = jnp.zeros_like(ys_ref)


def _experts(blk_expert, n_used, blk_next, blk_ord, xs, w1, b1, w2, b2):
    m_pad = xs.shape[0] // CHUNKS
    bm = EXPERT_ROWS
    n_all = w1.shape[0] * w1.shape[1]
    rows = lambda b, be, nu, *_: (jnp.minimum(b, nu[0] - 1), 0)
    ex = lambda b, be, nu, *_: (be[jnp.minimum(b, nu[0] - 1)], 0, 0)
    return pl.pallas_call(
        _expert_kernel,
        grid_spec=pltpu.PrefetchScalarGridSpec(
            num_scalar_prefetch=4,
            grid=(m_pad // bm,),
            in_specs=[
                pl.BlockSpec((bm * CHUNKS, LANES), rows),
                pl.BlockSpec(memory_space=pl.ANY),
                pl.BlockSpec((1, 1, 2 * D_EXPERT), ex),
                pl.BlockSpec(memory_space=pl.ANY),
                pl.BlockSpec((1, 1, D_MODEL), ex),
            ],
            out_specs=pl.BlockSpec((bm * CHUNKS, LANES), lambda b, *_: (b, 0)),
            scratch_shapes=[pltpu.VMEM((2, D_MODEL, 2 * D_EXPERT), F32),
                            pltpu.VMEM((2, D_EXPERT, D_MODEL), F32),
                            pltpu.VMEM((D_MODEL, 2 * D_EXPERT), BF16),
                            pltpu.VMEM((D_EXPERT, D_MODEL), BF16),
                            pltpu.SemaphoreType.DMA((2, 2))],
        ),
        out_shape=jax.ShapeDtypeStruct((m_pad * CHUNKS, LANES), F32),
        compiler_params=_params(("arbitrary",)),
        name="experts",
    )(blk_expert, n_used, blk_next, blk_ord, xs,
      w1.reshape(n_all, D_MODEL, 2 * D_EXPERT), b1.reshape(n_all, 1, -1),
      w2.reshape(n_all, D_EXPERT, D_MODEL), b2.reshape(n_all, 1, -1))


def _combine_kernel(dest_ref, ys_ref, x1_ref, gate_ref, mod_ref, o_ref, buf, sems):
    i = pl.program_id(0)
    tm = x1_ref.shape[0]
    n = tm * TOP_K
    slot = i % 2

    seq = pl.num_programs(0) * tm

    def gather(step, to_slot):
        base = step * tm

        def issue(r, carry):
            for k in range(TOP_K):
                _tile_copy(ys_ref, dest_ref[k * seq + base + r], buf, to_slot * n + k * tm + r,
                           sems.at[to_slot]).start(priority=k % 2)
            return carry

        lax.fori_loop(0, tm, issue, 0, unroll=8)

    @pl.when(i == 0)
    def _():
        gather(0, 0)

    @pl.when(i + 1 < pl.num_programs(0))
    def _():
        gather(i + 1, 1 - slot)

    mine = pl.ds(pl.multiple_of(slot * n * CHUNKS, n * CHUNKS), n * CHUNKS)
    pltpu.make_async_copy(ys_ref.at[pl.ds(0, n * CHUNKS), :], buf.at[mine, :], sems.at[slot]).wait()
    gates = gate_ref[...]
    g2 = mod_ref[5]
    for c in range(CHUNKS):
        cols = slice(c * LANES, (c + 1) * LANES)
        mix = jnp.zeros((tm, LANES), F32)
        for k in range(TOP_K):
            mix = mix + gates[:, k:k + 1] * _load_token_chunk(
                buf, tm, c, offset=(slot * n + k * tm) * CHUNKS)
        o_ref[:, cols] = x1_ref[:, cols] + g2[:, cols] * mix


def _combine(dest, ys, x1, gates, mod_l):
    seq = x1.shape[0]
    tm = COMBINE_TILE
    return pl.pallas_call(
        _combine_kernel,
        grid_spec=pltpu.PrefetchScalarGridSpec(
            num_scalar_prefetch=1,
            grid=(seq // tm,),
            in_specs=[
                pl.BlockSpec(memory_space=pl.ANY),
                pl.BlockSpec((tm, D_MODEL), lambda i, d: (i, 0)),
                pl.BlockSpec((tm, LANES), lambda i, d: (i, 0)),
                pl.BlockSpec(mod_l.shape, lambda i, d: (0, 0, 0)),
            ],
            out_specs=pl.BlockSpec((tm, D_MODEL), lambda i, d: (i, 0)),
            scratch_shapes=[pltpu.VMEM((2 * TOP_K * tm * CHUNKS, LANES), F32),
                            pltpu.SemaphoreType.DMA((2,))],
        ),
        out_shape=jax.ShapeDtypeStruct((seq, D_MODEL), F32),
        compiler_params=_params(("arbitrary",)),
        name="combine",
    )(dest, ys, x1, gates, mod_l)


def _moe(l, x1, h2_tiles, eidx, gates, rank, counts, mod_l, w1, b1, w2, b2):
    seq = x1.shape[0]
    bm = EXPERT_ROWS
    m_pad = seq * TOP_K + N_EXPERTS * bm
    cnt = counts[0, :N_EXPERTS].astype(jnp.int32)
    padded = (cnt + bm - 1) // bm * bm
    pad_end = jnp.cumsum(padded)
    pad_start = pad_end - padded
    onehot = eidx[:TOP_K, :, None] == jnp.arange(N_EXPERTS, dtype=jnp.int32)
    start_of = jnp.sum(jnp.where(onehot, pad_start, 0), axis=-1)
    dest = (start_of + rank[:TOP_K]).reshape(TOP_K * seq).astype(jnp.int32)
    nblk = m_pad // bm
    blk_start = jnp.arange(nblk, dtype=jnp.int32) * bm
    local = jnp.minimum(jnp.sum(pad_end[None, :] <= blk_start[:, None], axis=1), N_EXPERTS - 1)
    blk_expert = (local + l * N_EXPERTS).astype(jnp.int32)
    n_used = (pad_end[-1:] // bm).astype(jnp.int32)
    ids = jnp.arange(N_EXPERTS, dtype=jnp.int32)
    live = padded > 0
    ordinal = jnp.cumsum(live.astype(jnp.int32)) - 1
    later = live[None, :] & (ids[None, :] > ids[:, None])
    nxt = jnp.min(jnp.where(later, ids[None, :], N_EXPERTS), axis=1)
    nxt = jnp.where(nxt < N_EXPERTS, nxt + l * N_EXPERTS, -1)
    of_block = local[:, None] == ids[None, :]
    blk_next = jnp.sum(jnp.where(of_block, nxt[None, :], 0), axis=1).astype(jnp.int32)
    blk_ord = jnp.sum(jnp.where(of_block, ordinal[None, :], 0), axis=1).astype(jnp.int32)
    xs = _dispatch(dest, (pad_start + cnt).astype(jnp.int32), (padded - cnt).astype(jnp.int32),
                   n_used, h2_tiles, m_pad)
    ys = _experts(blk_expert, n_used, blk_next, blk_ord, xs, w1, b1, w2, b2)
    return _combine(dest, ys, x1, gates, mod_l)


def _pad_cols(a, n):
    return jnp.pad(a, ((0, 0), (0, n - a.shape[1])))


def _layer_params(l, w_in, mla_cq_g, mla_w_uq, mla_ckv_g, mla_w_ukv, mla_q_g, mla_k_g,
                  fox_q_g, fox_k_g, fox_b_f, moba_q_g, moba_k_g, dil_q_g, dil_k_g, w_out,
                  norm1_g, norm2_g, router_w, router_b):
    w_uq = jnp.pad(mla_w_uq[l].reshape(MLA_Q_RANK, HEADS, MLA_QK),
                   ((0, 0), (0, 0), (0, LANES - MLA_QK))).reshape(MLA_Q_RANK, HEADS * LANES)
    w_ukv = mla_w_ukv[l].reshape(MLA_KV_RANK, HEADS, MLA_NOPE + HEAD_DIM)
    w_uk = jnp.pad(w_ukv[:, :, :MLA_NOPE], ((0, 0), (0, 0), (0, LANES - MLA_NOPE)))
    w_uv = w_ukv[:, :, MLA_NOPE:]
    tile4 = lambda g: jnp.tile(g, HEADS)[None, :]
    rw = _pad_cols(router_w[l], LANES)
    rw_hi = rw.astype(BF16)
    rw_lo = (rw - rw_hi.astype(F32)).astype(BF16)
    r_b = jnp.concatenate([router_b[l], jnp.full((LANES - N_EXPERTS,), NEG, F32)])[None, :]
    return dict(
        g1=norm1_g[l][None, :], g2=norm2_g[l][None, :], layer=l,
        w_in=w_in.reshape(-1, w_in.shape[-1]),
        cq_g=mla_cq_g[l][None, :], w_uq=w_uq.astype(BF16), ckv_g=mla_ckv_g[l][None, :],
        w_uk=w_uk.reshape(MLA_KV_RANK, HEADS * LANES).astype(BF16),
        w_uvt=w_uv.reshape(MLA_KV_RANK, GROUP).T.astype(BF16),
        fox_bound=FOX_NORM_SLACK * HEAD_DIM ** 0.5 * jnp.max(jnp.abs(fox_q_g[l]))
        * jnp.max(jnp.abs(fox_k_g[l])),
        q_g=_pad_cols(mla_q_g[l][None, :], LANES), k_g=_pad_cols(mla_k_g[l][None, :], LANES),
        fq_g=tile4(fox_q_g[l]), fk_g=tile4(fox_k_g[l]), f_b=_pad_cols(fox_b_f[l][None, :], LANES),
        mq_g=tile4(moba_q_g[l]), mk_g=tile4(moba_k_g[l]), dq_g=tile4(dil_q_g[l]), dk_g=tile4(dil_k_g[l]),
        w_out=w_out[l].reshape(HEADS, GROUP, D_MODEL).astype(BF16),
        rw=jnp.concatenate([rw_hi, rw_lo], axis=1), r_b=r_b,
    )


def kernel(x, c, w_mod, b_mod, norm1_g, norm2_g, w_in, mla_cq_g, mla_w_uq, mla_ckv_g, mla_w_ukv, mla_q_g, mla_k_g, fox_q_g, fox_k_g, fox_b_f, moba_q_g, moba_k_g, dil_q_g, dil_k_g, w_out, router_w, router_b, exp_w1, exp_b1, exp_w2, exp_b2):
    batch, seq, d = x.shape
    assert batch == 1 and d == D_MODEL
    assert seq % DIL_TILE == 0 and seq // MOBA_BLOCK <= MOBA_MAX_BLOCKS
    depth = w_mod.shape[0]
    consts = _prep_constants(seq, TOKEN_TILE)
    mod = _modulation(c, w_mod, b_mod)
    slopes_c = 2.0 ** (-(2.0 * np.arange(HEADS) + 2.0))
    in_block = np.arange(FLASH_TILE) % MOBA_BLOCK
    kbias_c = jnp.asarray(np.broadcast_to(slopes_c[:, None, None] * in_block[None, :, None],
                                          (HEADS, FLASH_TILE, LANES)), F32)
    kbias_0 = jnp.zeros((HEADS, FLASH_TILE, LANES), F32)
    all_tiles = jnp.zeros((seq // FLASH_TILE,), jnp.int32)
    x2 = x.reshape(seq, d)
    for l in range(depth):
        p = _layer_params(l, w_in, mla_cq_g, mla_w_uq, mla_ckv_g, mla_w_ukv, mla_q_g, mla_k_g,
                          fox_q_g, fox_k_g, fox_b_f, moba_q_g, moba_k_g, dil_q_g, dil_k_g, w_out,
                          norm1_g, norm2_g, router_w, router_b)
        mod_l = mod[l]
        qa, ka, va, qf, kf, vf, qm, km, vm, qd, kd, vd, decay = _prep(x2, mod_l, consts, p)
        oa = _flash(qa, ka, va, kbias_0, all_tiles, False)
        ob = _flash(qf, kf, vf, kbias_0, _fox_first_tile(decay, p["fox_bound"]), False)
        oc = _flash(qm, km, vm, kbias_c, all_tiles, True)
        od = _dilated(qd, kd, vd)
        x1, h2, eidx, gates, rank, counts = _post(x2, oa, ob, oc, od, mod_l, consts, p)
        x2 = _moe(l, x1, h2, eidx, gates, rank, counts, mod_l, exp_w1, exp_b1, exp_w2, exp_b2)
    return x2.reshape(batch, seq, d)
```

```python
import functools

import numpy as np
import jax
import jax.numpy as jnp
from jax import lax
from jax.experimental import pallas as pl
from jax.experimental.pallas import tpu as pltpu

F32 = jnp.float32
BF16 = jnp.bfloat16

D_MODEL = 1024
HEAD_DIM = 64
HEADS = 4
GROUP = HEADS * HEAD_DIM
LANES = 128
CHUNKS = D_MODEL // LANES
MLA_Q_RANK = 256
MLA_KV_RANK = 128
MLA_NOPE = 64
MLA_ROPE = 32
MLA_QK = MLA_NOPE + MLA_ROPE
ROPE_THETA = 10000.0
MOBA_BLOCK = 256
MOBA_TOPK = 3
MOBA_MAX_BLOCKS = 64
DIL_PATTERNS = ((128, 1), (512, 4), (2048, 16))
DIL_SPAN = 128
DIL_TILE = 2048
N_EXPERTS = 32
TOP_K = 4
D_EXPERT = 1024
SWIGLU_LIMIT = 7.0
SWIGLU_ALPHA = 1.702
EPS = 1e-6
NEG = -1e30

FLASH_TILE = 512
ONES_ROWS = 16
FOX_SKIP_LOG = -106.0
FOX_NORM_SLACK = 1.02
TOKEN_TILE = 512
POST_TILE = 512
DISPATCH_TILE = 1024
COMBINE_TILE = 512
EXPERT_ROWS = 512
VMEM_LIMIT = 56 * 1024 * 1024

COLS_MLA = 512
COLS_FOX = 640
COLS_MOBA = 512
COLS_DIL = 768
COLS_IN = COLS_MLA + COLS_FOX + COLS_MOBA + COLS_DIL


def _in_column_map():
    sizes = [MLA_Q_RANK, MLA_KV_RANK, MLA_ROPE, GROUP, GROUP, GROUP, HEADS] + [GROUP] * 6
    src = np.concatenate([[0], np.cumsum(sizes)])
    cq, ckv, kr, fq, fk, fv, flog, mq, mk, mv, dq, dk, dv = range(13)
    layout = [(cq, 0), (ckv, MLA_Q_RANK), (kr, MLA_Q_RANK + MLA_KV_RANK + MLA_NOPE),
              (fq, COLS_MLA), (fk, COLS_MLA + GROUP), (flog, COLS_MLA + 2 * GROUP),
              (mq, COLS_MLA + COLS_FOX), (mk, COLS_MLA + COLS_FOX + GROUP),
              (dq, COLS_IN - 3 * GROUP), (dk, COLS_IN - 2 * GROUP), (dv, COLS_IN - GROUP)]
    moves = tuple((dst, int(src[piece]), sizes[piece]) for piece, dst in layout)
    return moves, (int(src[fv]), int(src[mv]))


IN_COLUMN_MAP, IN_VALUE_COLUMNS = _in_column_map()


def _dot(a, b):
    return jnp.dot(a, b, preferred_element_type=F32)


def _dot_nt(a, b):
    return lax.dot_general(a, b, (((1,), (1,)), ((), ())), preferred_element_type=F32)


def _split2(x):
    hi = x.astype(BF16)
    lo = (x - hi.astype(F32)).astype(BF16)
    return hi, lo


def _split3(x):
    a = x.astype(BF16)
    r = x - a.astype(F32)
    b = r.astype(BF16)
    c = (r - b.astype(F32)).astype(BF16)
    return a, b, c


def _head_of_lane():
    return jnp.right_shift(lax.broadcasted_iota(jnp.int32, (1, GROUP), 1), 6)


def _store_token_tiles(ref, x, offset=0):
    n = x.shape[0]
    for c in range(CHUNKS):
        ref[pl.ds(offset + c, n, stride=CHUNKS), :] = x[:, c * LANES:(c + 1) * LANES]


def _load_token_chunk(ref, n, c, offset=0):
    return ref[pl.ds(offset + c, n, stride=CHUNKS), :]


def _full_spec(shape):
    nd = len(shape)
    return pl.BlockSpec(shape, lambda *_: (0,) * nd)


def _params(sem):
    return pltpu.CompilerParams(dimension_semantics=sem, vmem_limit_bytes=VMEM_LIMIT)


MOD_GROUP = 3


def _mod_kernel(c_ref, w_ref, b_ref, o_ref):
    c = c_ref[...]
    s = c * (1.0 / (1.0 + jnp.exp(-c)))
    s8 = jnp.broadcast_to(s, (8, D_MODEL))
    r = jnp.dot(s8, w_ref[0], preferred_element_type=F32, precision=lax.Precision.HIGHEST)
    for v in range(MOD_GROUP):
        o_ref[0, v] = r[0:1, v * D_MODEL:(v + 1) * D_MODEL] + b_ref[0, v]


def _modulation(c, w_mod, b_mod):
    depth = w_mod.shape[0]
    b4 = b_mod.reshape(depth, 6, 1, D_MODEL)
    return pl.pallas_call(
        _mod_kernel,
        grid=(depth, 6 // MOD_GROUP),
        in_specs=[
            pl.BlockSpec((1, D_MODEL), lambda l, j: (0, 0)),
            pl.BlockSpec((1, D_MODEL, MOD_GROUP * D_MODEL), lambda l, j: (l, 0, j)),
            pl.BlockSpec((1, MOD_GROUP, 1, D_MODEL), lambda l, j: (l, j, 0, 0)),
        ],
        out_specs=pl.BlockSpec((1, MOD_GROUP, 1, D_MODEL), lambda l, j: (l, j, 0, 0)),
        out_shape=jax.ShapeDtypeStruct((depth, 6, 1, D_MODEL), F32),
        compiler_params=_params(("arbitrary", "arbitrary")),
        name="modulation",
    )(c, w_mod, b4)


def _head_norm(x, g, bd):
    hi, lo = _split2(x * x)
    ss = _dot(hi, bd) + _dot(lo, bd)
    return x * lax.rsqrt(ss * (1.0 / HEAD_DIM) + EPS) * g


def _prep_kernel(x_ref, mod_ref, g1_ref, win_hbm, cqg_ref, wuq_ref, ckvg_ref, wuk_ref,
                 wuvt_ref, qg_ref, kg_ref, cs_ref,
                 fqg_ref, fkg_ref, fb_ref, mqg_ref, mkg_ref, dqg_ref, dkg_ref,
                 bd_ref, tri_ref, eq_ref, ek_ref, sel_ref,
                 qa_ref, ka_ref, va_ref, qf_ref, kf_ref, vf_ref, qm_ref, km_ref, vm_ref,
                 qd_ref, kd_ref, vd_ref, f_ref,
                 fcarry, kmean, wraw_ref, win_ref, wvt_ref, rope_scr, wsem, *, layer):
    i = pl.program_id(0)
    tm = x_ref.shape[0]

    @pl.when(i == 0)
    def _():
        fetch = pltpu.make_async_copy(win_hbm.at[pl.ds(layer * D_MODEL, D_MODEL), :], wraw_ref, wsem)
        fetch.start()
        fcarry[...] = jnp.zeros_like(fcarry)
        kmean[...] = jnp.zeros_like(kmean)
        win_ref[...] = jnp.zeros_like(win_ref)
        rope_scr[...] = jnp.zeros_like(rope_scr)
        rope_scr[0, :, 0:MLA_NOPE] = jnp.ones((tm, MLA_NOPE), F32)
        fetch.wait()
        for dst, src, width in IN_COLUMN_MAP:
            win_ref[:, dst:dst + width] = wraw_ref[:, src:src + width].astype(BF16)
        for slot, src in enumerate(IN_VALUE_COLUMNS):
            wvt_ref[slot] = wraw_ref[:, src:src + GROUP].T.astype(BF16)

    x = x_ref[...]
    y = x * lax.rsqrt(jnp.mean(x * x, axis=-1, keepdims=True) + EPS) * g1_ref[...]
    hb = (y * (1.0 + mod_ref[1]) + mod_ref[0]).astype(BF16)
    bd = bd_ref[...]
    lane = lax.broadcasted_iota(jnp.int32, (1, LANES), 1)
    lane_f = lane.astype(F32)
    head_of_lane = _head_of_lane()

    pa = _dot(hb, win_ref[:, 0:COLS_MLA])
    cq = pa[:, 0:MLA_Q_RANK]
    ckv = pa[:, MLA_Q_RANK:MLA_Q_RANK + MLA_KV_RANK]
    kr = pa[:, MLA_Q_RANK + MLA_KV_RANK:COLS_MLA]
    cqn = (cq * lax.rsqrt(jnp.mean(cq * cq, axis=-1, keepdims=True) + EPS) * cqg_ref[...]).astype(BF16)
    ckvn = (ckv * lax.rsqrt(jnp.mean(ckv * ckv, axis=-1, keepdims=True) + EPS) * ckvg_ref[...]).astype(BF16)
    q_all = _dot(cqn, wuq_ref[...])
    k_all = _dot(ckvn, wuk_ref[...])
    va_ref[...] = _dot_nt(wuvt_ref[...], ckvn).astype(BF16)
    half = MLA_ROPE // 2
    cos, sin = cs_ref[:, 0:half], cs_ref[:, half:MLA_ROPE]
    rope_scr[0, :, MLA_NOPE:MLA_NOPE + half] = cos
    rope_scr[0, :, MLA_NOPE + half:MLA_QK] = cos
    rope_scr[1, :, MLA_NOPE:MLA_NOPE + half] = -sin
    rope_scr[2, :, MLA_NOPE + half:MLA_QK] = sin
    rc, rs1, rs2 = rope_scr[0], rope_scr[1], rope_scr[2]

    def rope(t):
        return t * rc + pltpu.roll(t, LANES - MLA_ROPE // 2, 1) * rs1 + pltpu.roll(t, MLA_ROPE // 2, 1) * rs2

    for h in range(HEADS):
        q = q_all[:, h * LANES:(h + 1) * LANES]
        q = q * lax.rsqrt(jnp.sum(q * q, axis=-1, keepdims=True) * (1.0 / MLA_QK) + EPS) * qg_ref[...]
        qa_ref[h] = (rope(q) * (MLA_QK ** -0.5)).astype(BF16)
        k = k_all[:, h * LANES:(h + 1) * LANES] + kr
        k = k * lax.rsqrt(jnp.sum(k * k, axis=-1, keepdims=True) * (1.0 / MLA_QK) + EPS) * kg_ref[...]
        ka_ref[h] = rope(k).astype(BF16)

    pf = _dot(hb, win_ref[:, COLS_MLA:COLS_MLA + COLS_FOX])
    fqn = (_head_norm(pf[:, 0:GROUP], fqg_ref[...], bd) * (HEAD_DIM ** -0.5)).astype(BF16)
    fkn = _head_norm(pf[:, GROUP:2 * GROUP], fkg_ref[...], bd).astype(BF16)
    vf_ref[...] = _dot_nt(wvt_ref[0], hb).astype(BF16)
    z = pf[:, 2 * GROUP:2 * GROUP + LANES] + fb_ref[...]
    log_f = jnp.minimum(z, 0.0) - jnp.log(1.0 + jnp.exp(-jnp.abs(z)))
    tri = tri_ref[...]
    a1, a2, a3 = _split3(log_f)
    cum = fcarry[...] + (_dot(tri, a1) + _dot(tri, a2) + _dot(tri, a3))
    fcarry[...] = cum[tm - 1:tm, :]
    f_ref[...] = cum
    f1, f2, f3 = _split3(cum)
    xq = jnp.concatenate([fqn, f1, f2, f3], axis=1)
    xk = jnp.concatenate([fkn, f1, f2, f3], axis=1)
    ones_q = jnp.where((lane >= HEAD_DIM + 3) & (lane < HEAD_DIM + 6), 1.0, 0.0)
    ones_k = jnp.where((lane >= HEAD_DIM) & (lane < HEAD_DIM + 3), 1.0, 0.0)
    qf_all = _dot(xq, eq_ref[...])
    kf_all = _dot(xk, ek_ref[...])
    for h in range(HEADS):
        qf_ref[h] = (qf_all[:, h * LANES:(h + 1) * LANES] + ones_q).astype(BF16)
        kf_ref[h] = (kf_all[:, h * LANES:(h + 1) * LANES] + ones_k).astype(BF16)

    pm = _dot(hb, win_ref[:, COLS_MLA + COLS_FOX:COLS_MLA + COLS_FOX + COLS_MOBA])
    mqn = _head_norm(pm[:, 0:GROUP], mqg_ref[...], bd) * (HEAD_DIM ** -0.5)
    mkn = _head_norm(pm[:, GROUP:2 * GROUP], mkg_ref[...], bd)
    vm_ref[...] = _dot_nt(wvt_ref[1], hb).astype(BF16)
    mqb = mqn.astype(BF16)
    mkb = mkn.astype(BF16)
    blocks = tm // MOBA_BLOCK
    blk = lane - HEAD_DIM
    blk_f = blk.astype(F32)
    row_block = jnp.right_shift(lax.broadcasted_iota(jnp.int32, (tm, 1), 0),
                                MOBA_BLOCK.bit_length() - 1)
    own = i * blocks + row_block
    own_f = own.astype(F32)
    past = (blk >= 0) & (blk < own)
    for b in range(blocks):
        col_mean = jnp.mean(mkn[b * MOBA_BLOCK:(b + 1) * MOBA_BLOCK, :], axis=0, keepdims=True)
        for h in range(HEADS):
            kmean[pl.ds(h * LANES + HEAD_DIM + i * blocks + b, 1), :] = jnp.where(
                head_of_lane == h, col_mean, 0.0)
    q_hi, q_lo = _split2(mqn)
    km_hi, km_lo = _split2(kmean[...])
    gate_all = _dot_nt(q_hi, km_hi) + _dot_nt(q_hi, km_lo) + _dot_nt(q_lo, km_hi)
    qm_all = _dot(mqb, sel_ref[...])
    km_all = _dot(mkb, sel_ref[...])
    for h in range(HEADS):
        g = jnp.where(past, gate_all[:, h * LANES:(h + 1) * LANES], NEG)
        chosen = jnp.zeros((tm, LANES), F32)
        for _ in range(MOBA_TOPK):
            m = jnp.max(g, axis=-1, keepdims=True)
            first = jnp.min(jnp.where(g == m, lane_f, 1e9), axis=-1, keepdims=True)
            pick = (lane_f == first) & (m > NEG)
            chosen = jnp.where(pick, 1.0, chosen)
            g = jnp.where(pick, NEG, g)
        slope = 2.0 ** (-(2 * h + 2))
        keep = (chosen > 0.0) | (blk == own)
        bias = jnp.where(keep, (slope * MOBA_BLOCK) * (blk_f - own_f), NEG)
        bias = jnp.where(blk >= 0, bias, 0.0)
        qm_ref[h] = (qm_all[:, h * LANES:(h + 1) * LANES] + bias).astype(BF16)
        onehot = jnp.where(blk == own, 1.0, 0.0)
        km_ref[h] = (km_all[:, h * LANES:(h + 1) * LANES] + onehot).astype(BF16)

    pd = _dot(hb, win_ref[:, COLS_MLA + COLS_FOX + COLS_MOBA:COLS_IN])
    qd_ref[...] = _head_norm(pd[:, 0:GROUP], dqg_ref[...], bd) * (HEAD_DIM ** -0.5)
    kd_ref[...] = _head_norm(pd[:, GROUP:2 * GROUP], dkg_ref[...], bd)
    vd_ref[...] = pd[:, 2 * GROUP:3 * GROUP]


def _prep_constants(seq, tm):
    half = MLA_ROPE // 2
    inv = 1.0 / (ROPE_THETA ** (jnp.arange(half, dtype=F32) / half))
    ang = jnp.arange(seq, dtype=F32)[:, None] * inv[None, :]
    cs = jnp.concatenate([jnp.cos(ang), jnp.sin(ang)], axis=1)
    bd =np.kron(np.eye(HEADS, dtype=np.float32), np.ones((HEAD_DIM, HEAD_DIM), np.float32))
    tri = np.tril(np.ones((tm, tm), np.float32))
    tri_strict = np.tril(np.ones((POST_TILE, POST_TILE), np.float32), -1)
    sel = np.zeros((GROUP, HEADS * LANES), np.float32)
    eq = np.zeros((GROUP + 3 * LANES, HEADS * LANES), np.float32)
    ek = np.zeros((GROUP + 3 * LANES, HEADS * LANES), np.float32)
    for h in range(HEADS):
        for d in range(HEAD_DIM):
            sel[h * HEAD_DIM + d, h * LANES + d] = 1.0
        for piece in range(3):
            eq[GROUP + piece * LANES + h, h * LANES + HEAD_DIM + piece] = 1.0
            ek[GROUP + piece * LANES + h, h * LANES + HEAD_DIM + 3 + piece] = -1.0
    eq[:GROUP] = sel
    ek[:GROUP] = sel
    as_bf = lambda a: jnp.asarray(a, BF16)
    return dict(cs=cs, bd=as_bf(bd),
                tri=as_bf(tri), tri_strict=as_bf(tri_strict), sel=as_bf(sel), eq=as_bf(eq), ek=as_bf(ek))


def _prep(x2, mod_l, consts, p):
    seq = x2.shape[0]
    tm = TOKEN_TILE
    row = lambda n: pl.BlockSpec((tm, n), lambda i: (i, 0))
    heads = pl.BlockSpec((HEADS, tm, LANES), lambda i: (0, i, 0))
    in_arrays = [
        (x2, row(D_MODEL)), (mod_l, _full_spec(mod_l.shape)), (p["g1"], None),
        (p["w_in"], pl.BlockSpec(memory_space=pl.ANY)),
        (p["cq_g"], None), (p["w_uq"], None), (p["ckv_g"], None), (p["w_uk"], None), (p["w_uvt"], None),
        (p["q_g"], None), (p["k_g"], None),
        (consts["cs"], row(MLA_ROPE)),
        (p["fq_g"], None), (p["fk_g"], None), (p["f_b"], None), (p["mq_g"], None), (p["mk_g"], None),
        (p["dq_g"], None), (p["dk_g"], None),
        (consts["bd"], None), (consts["tri"], None), (consts["eq"], None), (consts["ek"], None),
        (consts["sel"], None),
    ]
    args = [a for a, _ in in_arrays]
    specs = [s if s is not None else _full_spec(a.shape) for a, s in in_arrays]
    hshape = jax.ShapeDtypeStruct((HEADS, seq, LANES), BF16)
    vshape = jax.ShapeDtypeStruct((GROUP, seq), BF16)
    dshape = jax.ShapeDtypeStruct((seq, GROUP), F32)
    vt = pl.BlockSpec((GROUP, tm), lambda i: (0, i))
    return pl.pallas_call(
        functools.partial(_prep_kernel, layer=p["layer"]),
        grid=(seq // tm,),
        in_specs=specs,
        out_specs=[heads, heads, vt] * 3 + [row(GROUP)] * 3 + [row(LANES)],
        out_shape=[hshape, hshape, vshape] * 3 + [dshape] * 3
                  + [jax.ShapeDtypeStruct((seq, LANES), F32)],
        scratch_shapes=[pltpu.VMEM((1, LANES), F32), pltpu.VMEM((HEADS * LANES, GROUP), F32),
                        pltpu.VMEM((D_MODEL, p["w_in"].shape[-1]), F32),
                        pltpu.VMEM((D_MODEL, COLS_IN), BF16),
                        pltpu.VMEM((len(IN_VALUE_COLUMNS), GROUP, D_MODEL), BF16),
                        pltpu.VMEM((3, tm, LANES), F32),
                        pltpu.SemaphoreType.DMA(())],
        compiler_params=_params(("arbitrary",)),
        name="prep",
    )(*args)


def _flash_kernel(jlo_ref, q_ref, k_ref, vt_ref, kbias_ref, o_ref, m_sc, acc_sc, sa_sc, sb_sc,
                  *, tile, use_kbias):
    i = pl.program_id(0)
    m_sc[...] = jnp.full_like(m_sc, -jnp.inf)
    acc_sc[...] = jnp.zeros_like(acc_sc)

    def score(hh, j, buf):
        start = pl.multiple_of(j * tile, tile)
        buf[hh] = _dot_nt(k_ref[hh, pl.ds(start, tile), :], q_ref[hh])

    def absorb(hh, j, buf, causal):
        start = pl.multiple_of(j * tile, tile)
        s = buf[hh]
        if use_kbias:
            s = s + jnp.concatenate([kbias_ref[hh]] * (tile // LANES), axis=1)
        if causal:
            key = lax.broadcasted_iota(jnp.int32, (tile, tile), 0)
            qry = lax.broadcasted_iota(jnp.int32, (tile, tile), 1)
            s = jnp.where(key <= qry, s, NEG)
        m_prev = m_sc[hh]
        m_cur = jnp.max(jnp.max(s.reshape(8, tile // 8, tile), axis=0), axis=0, keepdims=True)
        m_new = jnp.maximum(m_prev, m_cur)
        alpha = jnp.exp(m_prev - m_new)
        p = jnp.exp((s - m_new).astype(BF16))
        vt = vt_ref[hh * HEAD_DIM:(hh + 1) * HEAD_DIM, pl.ds(start, tile)]
        vt = jnp.concatenate([vt, jnp.ones((ONES_ROWS, tile), BF16)], axis=0)
        acc_sc[hh] = alpha * acc_sc[hh] + _dot(vt, p)
        m_sc[hh] = m_new

    first = jlo_ref[i]
    n_off = i - first
    for hh in range(HEADS):
        score(hh, first, sa_sc)

    def body(t, carry):
        j = first + 2 * t
        for hh in range(HEADS):
            score(hh, j + 1, sb_sc)
            absorb(hh, j, sa_sc, False)
        for hh in range(HEADS):
            score(hh, j + 2, sa_sc)
            absorb(hh, j + 1, sb_sc, False)
        return carry

    lax.fori_loop(0, n_off // 2, body, 0)

    @pl.when(n_off % 2 == 1)
    def _():
        for hh in range(HEADS):
            score(hh, i, sb_sc)
            absorb(hh, i - 1, sa_sc, False)
        for hh in range(HEADS):
            absorb(hh, i, sb_sc, True)

    @pl.when(n_off % 2 == 0)
    def _():
        for hh in range(HEADS):
            absorb(hh, i, sa_sc, True)

    o_t = jnp.concatenate([acc_sc[hh, 0:HEAD_DIM, :] / acc_sc[hh, HEAD_DIM:HEAD_DIM + 1, :]
                           for hh in range(HEADS)], axis=0)
    o_ref[...] = o_t.T.astype(o_ref.dtype)


def _flash(q, k, v_t, kbias, first_tile, use_kbias):
    seq = v_t.shape[1]
    tile = FLASH_TILE
    kern = functools.partial(_flash_kernel, tile=tile, use_kbias=use_kbias)
    resident = pl.Buffered(1)
    return pl.pallas_call(
        kern,
        grid_spec=pltpu.PrefetchScalarGridSpec(
            num_scalar_prefetch=1,
            grid=(seq // tile,),
            in_specs=[
                pl.BlockSpec((HEADS, tile, LANES), lambda i, f: (0, i, 0)),
                pl.BlockSpec((HEADS, seq, LANES), lambda i, f: (0, 0, 0), pipeline_mode=resident),
                pl.BlockSpec((GROUP, seq), lambda i, f: (0, 0), pipeline_mode=resident),
                pl.BlockSpec((HEADS, tile, LANES), lambda i, f: (0, 0, 0)),
            ],
            out_specs=pl.BlockSpec((tile, GROUP), lambda i, f: (i, 0)),
            scratch_shapes=[pltpu.VMEM((HEADS, 1, tile), F32),
                            pltpu.VMEM((HEADS, HEAD_DIM + ONES_ROWS, tile), F32),
                            pltpu.VMEM((HEADS, tile, tile), F32), pltpu.VMEM((HEADS, tile, tile), F32)],
        ),
        out_shape=jax.ShapeDtypeStruct((seq, GROUP), BF16),
        compiler_params=_params(("arbitrary",)),
        name="flash_kbias" if use_kbias else "flash",
    )(first_tile, q, k, v_t, kbias)


def _fox_first_tile(decay, qk_bound):
    seq = decay.shape[0]
    nq = seq // FLASH_TILE
    f = decay[:, :HEADS]
    f_first = f[0::FLASH_TILE]
    f_last = f[FLASH_TILE - 1::FLASH_TILE]
    gap = f_first[:, None, :] - f_last[None, :, :] + 2.0 * qk_bound
    jj = jnp.arange(nq, dtype=jnp.int32)
    needed = (gap >= FOX_SKIP_LOG) | (jj[None, :, None] >= jj[:, None, None])
    first = jnp.min(jnp.where(needed, jj[None, :, None], nq), axis=1)
    return jnp.min(first, axis=1).astype(jnp.int32)


def _dilated_kernel(q_ref, k_ref, v_ref, o_ref, kbuf, vbuf, acc_s, m_s, l_s):
    pair = pl.program_id(0)
    i = pl.program_id(1)
    T = q_ref.shape[0]

    @pl.when(i == 0)
    def _():
        kbuf[...] = jnp.zeros_like(kbuf)
        vbuf[...] = jnp.zeros_like(vbuf)

    kbuf[0:T, :] = kbuf[T:2 * T, :]
    vbuf[0:T, :] = vbuf[T:2 * T, :]
    kbuf[T:2 * T, :] = k_ref[...]
    vbuf[T:2 * T, :] = v_ref[...]

    ii = lax.broadcasted_iota(jnp.int32, (DIL_SPAN, 2 * DIL_SPAN), 0)
    jj = lax.broadcasted_iota(jnp.int32, (DIL_SPAN, 2 * DIL_SPAN), 1)
    dist = ii + DIL_SPAN - jj
    band = (dist >= 0) & (dist <= DIL_SPAN)
    dist_f = dist.astype(F32)
    upper = lax.broadcasted_iota(jnp.int32, (1, LANES), 1) >= HEAD_DIM

    for pi, (window, r) in enumerate(DIL_PATTERNS):
        assert window // r == DIL_SPAN
        sub = DIL_SPAN * r

        def body(idx, carry, r=r, sub=sub, pi=pi):
            n = idx // r
            rho = idx - n * r
            base = n * sub + rho
            q = q_ref[pl.ds(base, DIL_SPAN, stride=r), :]
            kc = kbuf[pl.ds(T + base - sub, 2 * DIL_SPAN, stride=r), :].astype(BF16)
            vc = vbuf[pl.ds(T + base - sub, 2 * DIL_SPAN, stride=r), :].astype(BF16)
            first_key = jnp.where((i == 0) & (n == 0), DIL_SPAN, 0)
            valid = band & (jj >= first_key)
            stats = []
            for hh in range(2):
                slope = jnp.where(pair == 0, 2.0 ** (-(2 * hh + 1)), 2.0 ** (-(2 * hh + 5)))
                qh = (jnp.where(upper, q, 0.0) if hh else jnp.where(upper, 0.0, q)).astype(BF16)
                s = _dot_nt(qh, kc) - (slope * r) * dist_f
                s = jnp.where(valid, s, NEG)
                m = jnp.max(s, axis=-1, keepdims=True)
                p = jnp.exp(s - m)
                l = jnp.sum(p, axis=-1, keepdims=True)
                stats.append((_dot(p.astype(BF16), vc), m, l))
            rows = pl.ds(pi * T + base, DIL_SPAN, stride=r)
            acc_s[rows, :] = jnp.where(upper, stats[1][0], stats[0][0])
            m_s[rows, :] = jnp.where(upper, stats[1][1], stats[0][1])
            l_s[rows, :] = jnp.where(upper, stats[1][2], stats[0][2])
            return carry

        lax.fori_loop(0, T // DIL_SPAN, body, 0, unroll=True)

    npat = len(DIL_PATTERNS)
    ms = [m_s[pi * T:(pi + 1) * T, :] for pi in range(npat)]
    m_top = functools.reduce(jnp.maximum, ms)
    num = jnp.zeros((T, LANES), F32)
    den = jnp.zeros((T, LANES), F32)
    for pi in range(npat):
        w = jnp.exp(ms[pi] - m_top)
        num = num + w * acc_s[pi * T:(pi + 1) * T, :]
        den = den + w * l_s[pi * T:(pi + 1) * T, :]
    o_ref[...] = (num / den).astype(o_ref.dtype)


def _dilated(q, k, v):
    seq = q.shape[0]
    T = DIL_TILE
    spec = pl.BlockSpec((T, LANES), lambda p, i: (i, p))
    npat = len(DIL_PATTERNS)
    return pl.pallas_call(
        _dilated_kernel,
        grid=(HEADS // 2, seq // T),
        in_specs=[spec, spec, spec],
        out_specs=spec,
        out_shape=jax.ShapeDtypeStruct((seq, GROUP), BF16),
        scratch_shapes=[pltpu.VMEM((2 * T, LANES), F32), pltpu.VMEM((2 * T, LANES), F32),
                        pltpu.VMEM((npat * T, LANES), F32), pltpu.VMEM((npat * T, LANES), F32),
                        pltpu.VMEM((npat * T, LANES), F32)],
        compiler_params=_params(("arbitrary", "arbitrary")),
        name="dilated",
    )(q, k, v)


def _post_kernel(x_ref, oa_ref, ob_ref, oc_ref, od_ref, wout_ref, mod_ref, g2_ref,
                 rw_ref, rb_ref, tri_ref,
                 x1_ref, h2_ref, eidx_ref, gate_ref, rank_ref, cnt_ref, carry):
    i = pl.program_id(0)
    tm = x_ref.shape[0]

    @pl.when(i == 0)
    def _():
        carry[...] = jnp.zeros_like(carry)

    o = (_dot(oa_ref[...], wout_ref[0]) + _dot(ob_ref[...], wout_ref[1])
         + _dot(oc_ref[...], wout_ref[2]) + _dot(od_ref[...], wout_ref[3]))
    x1 = x_ref[...] + mod_ref[2] * o
    x1_ref[...] = x1
    y = x1 * lax.rsqrt(jnp.mean(x1 * x1, axis=-1, keepdims=True) + EPS) * g2_ref[...]
    h2 = y * (1.0 + mod_ref[4]) + mod_ref[3]
    _store_token_tiles(h2_ref, h2)

    h_hi, h_lo = _split2(h2)
    hh = _dot(h_hi, rw_ref[...])
    logits = (hh[:, :LANES] + hh[:, LANES:] + _dot(h_lo, rw_ref[:, :LANES])
              + rb_ref[...])
    lane = lax.broadcasted_iota(jnp.int32, (tm, LANES), 1)
    lane_f = lane.astype(F32)
    g = logits
    chosen = jnp.zeros((tm, LANES), F32)
    vals, idxs = [], []
    for _ in range(TOP_K):
        m = jnp.max(g, axis=-1, keepdims=True)
        first = jnp.min(jnp.where(g == m, lane_f, 1e9), axis=-1, keepdims=True)
        pick = lane_f == first
        chosen = jnp.where(pick, 1.0, chosen)
        g = jnp.where(pick, -jnp.inf, g)
        vals.append(m)
        idxs.append(first)
    exps = [jnp.exp(v - vals[0]) for v in vals]
    den = exps[0] + exps[1] + exps[2] + exps[3]
    before = _dot(tri_ref[...], chosen.astype(BF16)) + carry[...]
    carry[...] = carry[...] + jnp.sum(chosen, axis=0, keepdims=True)
    cnt_ref[...] = carry[...]
    e_out = jnp.zeros((tm, LANES), F32)
    g_out = jnp.zeros((tm, LANES), F32)
    r_out = jnp.zeros((tm, LANES), F32)
    for k in range(TOP_K):
        rank_k = jnp.sum(jnp.where(lane_f == idxs[k], before, 0.0), axis=-1, keepdims=True)
        e_out = jnp.where(lane == k, idxs[k], e_out)
        g_out = jnp.where(lane == k, exps[k] / den, g_out)
        r_out = jnp.where(lane == k, rank_k, r_out)
    eidx_ref[...] = e_out.T[0:8, :].astype(jnp.int32)
    gate_ref[...] = g_out
    rank_ref[...] = r_out.T[0:8, :].astype(jnp.int32)


def _post(x2, oa, ob, oc, od, mod_l, consts, p):
    seq = x2.shape[0]
    tm = POST_TILE
    row = lambda n: pl.BlockSpec((tm, n), lambda i: (i, 0))
    full = [p["w_out"], mod_l, p["g2"], p["rw"], p["r_b"], consts["tri_strict"]]
    f32 = lambda n: jax.ShapeDtypeStruct((seq, n), F32)
    slots = pl.BlockSpec((8, tm), lambda i: (0, i))
    slots_shape = jax.ShapeDtypeStruct((8, seq), jnp.int32)
    return pl.pallas_call(
        _post_kernel,
        grid=(seq // tm,),
        in_specs=[row(D_MODEL)] + [row(GROUP)] * 4 + [_full_spec(a.shape) for a in full],
        out_specs=[row(D_MODEL), pl.BlockSpec((tm * CHUNKS, LANES), lambda i: (i, 0)),
                   slots, row(LANES), slots, _full_spec((1, LANES))],
        out_shape=[f32(D_MODEL), jax.ShapeDtypeStruct((seq * CHUNKS, LANES), F32),
                   slots_shape, f32(LANES), slots_shape, jax.ShapeDtypeStruct((1, LANES), F32)],
        scratch_shapes=[pltpu.VMEM((1, LANES), F32)],
        compiler_params=_params(("arbitrary",)),
        name="post",
    )(x2, oa, ob, oc, od, *full)


def _tile_copy(src, s, dst, d, sem):
    return pltpu.make_async_copy(src.at[pl.ds(pl.multiple_of(s * CHUNKS, CHUNKS), CHUNKS), :],
                                 dst.at[pl.ds(pl.multiple_of(d * CHUNKS, CHUNKS), CHUNKS), :], sem)


def _dispatch_kernel(dest_ref, padlo_ref, padn_ref, nu_ref, h_ref, xs_ref, zbuf, sem, zsem):
    i = pl.program_id(0)
    n = DISPATCH_TILE * TOP_K
    base = i * DISPATCH_TILE
    seq = pl.num_programs(0) * DISPATCH_TILE
    block_rows = EXPERT_ROWS * CHUNKS

    @pl.when(i == 0)
    def _():
        zbuf[...] = jnp.zeros_like(zbuf)

        def fill(wait):
            def go(copy):
                copy.wait() if wait else copy.start()

            def per_expert(e, carry):
                lo, cnt = padlo_ref[e], padn_ref[e]
                off = lo
                p = EXPERT_ROWS // 2
                while p >= 1:
                    rows = p * CHUNKS

                    @pl.when((cnt & p) != 0)
                    def _(off=off, rows=rows):
                        go(pltpu.make_async_copy(
                            zbuf.at[pl.ds(0, rows), :],
                            xs_ref.at[pl.ds(pl.multiple_of(off * CHUNKS, CHUNKS), rows), :], zsem))

                    off = off + (cnt & p)
                    p //= 2
                return carry

            lax.fori_loop(0, N_EXPERTS, per_expert, 0)

            def per_block(b, carry):
                go(pltpu.make_async_copy(
                    zbuf, xs_ref.at[pl.ds(pl.multiple_of(b * block_rows, block_rows), block_rows), :],
                    zsem))
                return carry

            lax.fori_loop(nu_ref[0], xs_ref.shape[0] // block_rows, per_block, 0)

        fill(False)
        fill(True)

    def issue(r, carry):
        for k in range(TOP_K):
            _tile_copy(h_ref, r, xs_ref, dest_ref[k * seq + base + r], sem).start(priority=k % 2)
        return carry

    lax.fori_loop(0, DISPATCH_TILE, issue, 0, unroll=4)
    rows = pl.ds(0, n * CHUNKS)
    pltpu.make_async_copy(xs_ref.at[rows, :], xs_ref.at[rows, :], sem).wait()


def _dispatch(dest, pad_lo, pad_n, n_used, h2_tiles, m_pad):
    seq = h2_tiles.shape[0] // CHUNKS
    return pl.pallas_call(
        _dispatch_kernel,
        grid_spec=pltpu.PrefetchScalarGridSpec(
            num_scalar_prefetch=4,
            grid=(seq // DISPATCH_TILE,),
            in_specs=[pl.BlockSpec((DISPATCH_TILE * CHUNKS, LANES), lambda i, *_: (i, 0))],
            out_specs=pl.BlockSpec(memory_space=pl.ANY),
            scratch_shapes=[pltpu.VMEM((EXPERT_ROWS * CHUNKS, LANES), F32),
                            pltpu.SemaphoreType.DMA(()), pltpu.SemaphoreType.DMA(())],
        ),
        out_shape=jax.ShapeDtypeStruct((m_pad * CHUNKS, LANES), F32),
        compiler_params=_params(("arbitrary",)),
        name="dispatch",
    )(dest, pad_lo, pad_n, n_used, h2_tiles)


def _expert_kernel(be_ref, nu_ref, nxt_ref, ord_ref, xs_ref, w1_hbm, b1_ref, w2_hbm, b2_ref, ys_ref,
                   w1f, w2f, w1b, w2b, sems):
    b = pl.program_id(0)
    e = be_ref[b]
    prev = be_ref[jnp.maximum(b - 1, 0)]
    fresh = ((b == 0) | (e != prev)) & (b < nu_ref[0])
    slot = ord_ref[b] % 2

    def fetch(expert, to_slot):
        return (pltpu.make_async_copy(w1_hbm.at[expert], w1f.at[to_slot], sems.at[0, to_slot]),
                pltpu.make_async_copy(w2_hbm.at[expert], w2f.at[to_slot], sems.at[1, to_slot]))

    @pl.when(b == 0)
    def _():
        for copy in fetch(e, slot):
            copy.start()

    @pl.when(fresh)
    def _():
        @pl.when(nxt_ref[b] >= 0)
        def _():
            for copy in fetch(nxt_ref[b], 1 - slot):
                copy.start()

        for copy in fetch(e, slot):
            copy.wait()
        w1b[...] = w1f[slot].astype(BF16)
        w2b[...] = w2f[slot].astype(BF16)

    @pl.when(b < nu_ref[0])
    def _():
        half = EXPERT_ROWS // 2
        gus = []
        for r in range(2):
            xb = jnp.concatenate(
                [_load_token_chunk(xs_ref, half, c, offset=r * half * CHUNKS) for c in range(CHUNKS)],
                axis=1).astype(BF16)
            gus.append(_dot(xb, w1b[...]) + b1_ref[0])
        for r in range(2):
            g = jnp.minimum(gus[r][:, :D_EXPERT], SWIGLU_LIMIT)
            u = jnp.clip(gus[r][:, D_EXPERT:], -SWIGLU_LIMIT, SWIGLU_LIMIT)
            y = (u + 1.0) * g * (1.0 / (1.0 + jnp.exp(-SWIGLU_ALPHA * g)))
            _store_token_tiles(ys_ref, _dot(y.astype(BF16), w2b[...]) + b2_ref[0],
                               offset=r * half * CHUNKS)

    @pl.when(b >= nu_ref[0])
    def _():
        ys_ref[...] = jnp.zeros_like(ys_ref)


def _experts(blk_expert, n_used, blk_next, blk_ord, xs, w1, b1, w2, b2):
    m_pad = xs.shape[0] // CHUNKS
    bm = EXPERT_ROWS
    n_all = w1.shape[0] * w1.shape[1]
    rows = lambda b, be, nu, *_: (jnp.minimum(b, nu[0] - 1), 0)
    ex = lambda b, be, nu, *_: (be[jnp.minimum(b, nu[0] - 1)], 0, 0)
    return pl.pallas_call(
        _expert_kernel,
        grid_spec=pltpu.PrefetchScalarGridSpec(
            num_scalar_prefetch=4,
            grid=(m_pad // bm,),
            in_specs=[
                pl.BlockSpec((bm * CHUNKS, LANES), rows),
                pl.BlockSpec(memory_space=pl.ANY),
                pl.BlockSpec((1, 1, 2 * D_EXPERT), ex),
                pl.BlockSpec(memory_space=pl.ANY),
                pl.BlockSpec((1, 1, D_MODEL), ex),
            ],
            out_specs=pl.BlockSpec((bm * CHUNKS, LANES), lambda b, *_: (b, 0)),
            scratch_shapes=[pltpu.VMEM((2, D_MODEL, 2 * D_EXPERT), F32),
                            pltpu.VMEM((2, D_EXPERT, D_MODEL), F32),
                            pltpu.VMEM((D_MODEL, 2 * D_EXPERT), BF16),
                            pltpu.VMEM((D_EXPERT, D_MODEL), BF16),
                            pltpu.SemaphoreType.DMA((2, 2))],
        ),
        out_shape=jax.ShapeDtypeStruct((m_pad * CHUNKS, LANES), F32),
        compiler_params=_params(("arbitrary",)),
        name="experts",
    )(blk_expert, n_used, blk_next, blk_ord, xs,
      w1.reshape(n_all, D_MODEL, 2 * D_EXPERT), b1.reshape(n_all, 1, -1),
      w2.reshape(n_all, D_EXPERT, D_MODEL), b2.reshape(n_all, 1, -1))


def _combine_kernel(dest_ref, ys_ref, x1_ref, gate_ref, mod_ref, o_ref, buf, sems):
    i = pl.program_id(0)
    tm = x1_ref.shape[0]
    n = tm * TOP_K
    slot = i % 2

    seq = pl.num_programs(0) * tm

    def gather(step, to_slot):
        base = step * tm

        def issue(r, carry):
            for k in range(TOP_K):
                _tile_copy(ys_ref, dest_ref[k * seq + base + r], buf, to_slot * n + k * tm + r,
                           sems.at[to_slot]).start(priority=k % 2)
            return carry

        lax.fori_loop(0, tm, issue, 0, unroll=8)

    @pl.when(i == 0)
    def _():
        gather(0, 0)

    @pl.when(i + 1 < pl.num_programs(0))
    def _():
        gather(i + 1, 1 - slot)

    mine = pl.ds(pl.multiple_of(slot * n * CHUNKS, n * CHUNKS), n * CHUNKS)
    pltpu.make_async_copy(ys_ref.at[pl.ds(0, n * CHUNKS), :], buf.at[mine, :], sems.at[slot]).wait()
    gates = gate_ref[...]
    g2 = mod_ref[5]
    for c in range(CHUNKS):
        cols = slice(c * LANES, (c + 1) * LANES)
        mix = jnp.zeros((tm, LANES), F32)
        for k in range(TOP_K):
            mix = mix + gates[:, k:k + 1] * _load_token_chunk(
                buf, tm, c, offset=(slot * n + k * tm) * CHUNKS)
        o_ref[:, cols] = x1_ref[:, cols] + g2[:, cols] * mix


def _combine(dest, ys, x1, gates, mod_l):
    seq = x1.shape[0]
    tm = COMBINE_TILE
    return pl.pallas_call(
        _combine_kernel,
        grid_spec=pltpu.PrefetchScalarGridSpec(
            num_scalar_prefetch=1,
            grid=(seq // tm,),
            in_specs=[
                pl.BlockSpec(memory_space=pl.ANY),
                pl.BlockSpec((tm, D_MODEL), lambda i, d: (i, 0)),
                pl.BlockSpec((tm, LANES), lambda i, d: (i, 0)),
                pl.BlockSpec(mod_l.shape, lambda i, d: (0, 0, 0)),
            ],
            out_specs=pl.BlockSpec((tm, D_MODEL), lambda i, d: (i, 0)),
            scratch_shapes=[pltpu.VMEM((2 * TOP_K * tm * CHUNKS, LANES), F32),
                            pltpu.SemaphoreType.DMA((2,))],
        ),
        out_shape=jax.ShapeDtypeStruct((seq, D_MODEL), F32),
        compiler_params=_params(("arbitrary",)),
        name="combine",
    )(dest, ys, x1, gates, mod_l)


def _moe(l, x1, h2_tiles, eidx, gates, rank, counts, mod_l, w1, b1, w2, b2):
    seq = x1.shape[0]
    bm = EXPERT_ROWS
    m_pad = seq * TOP_K + N_EXPERTS * bm
    cnt = counts[0, :N_EXPERTS].astype(jnp.int32)
    padded = (cnt + bm - 1) // bm * bm
    pad_end = jnp.cumsum(padded)
    pad_start = pad_end - padded
    onehot = eidx[:TOP_K, :, None] == jnp.arange(N_EXPERTS, dtype=jnp.int32)
    start_of = jnp.sum(jnp.where(onehot, pad_start, 0), axis=-1)
    dest = (start_of + rank[:TOP_K]).reshape(TOP_K * seq).astype(jnp.int32)
    nblk = m_pad // bm
    blk_start = jnp.arange(nblk, dtype=jnp.int32) * bm
    local = jnp.minimum(jnp.sum(pad_end[None, :] <= blk_start[:, None], axis=1), N_EXPERTS - 1)
    blk_expert = (local + l * N_EXPERTS).astype(jnp.int32)
    n_used = (pad_end[-1:] // bm).astype(jnp.int32)
    ids = jnp.arange(N_EXPERTS, dtype=jnp.int32)
    live = padded > 0
    ordinal = jnp.cumsum(live.astype(jnp.int32)) - 1
    later = live[None, :] & (ids[None, :] > ids[:, None])
    nxt = jnp.min(jnp.where(later, ids[None, :], N_EXPERTS), axis=1)
    nxt = jnp.where(nxt < N_EXPERTS, nxt + l * N_EXPERTS, -1)
    of_block = local[:, None] == ids[None, :]
    blk_next = jnp.sum(jnp.where(of_block, nxt[None, :], 0), axis=1).astype(jnp.int32)
    blk_ord = jnp.sum(jnp.where(of_block, ordinal[None, :], 0), axis=1).astype(jnp.int32)
    xs = _dispatch(dest, (pad_start + cnt).astype(jnp.int32), (padded - cnt).astype(jnp.int32),
                   n_used, h2_tiles, m_pad)
    ys = _experts(blk_expert, n_used, blk_next, blk_ord, xs, w1, b1, w2, b2)
    return _combine(dest, ys, x1, gates, mod_l)


def _pad_cols(a, n):
    return jnp.pad(a, ((0, 0), (0, n - a.shape[1])))


def _layer_params(l, w_in, mla_cq_g, mla_w_uq, mla_ckv_g, mla_w_ukv, mla_q_g, mla_k_g,
                  fox_q_g, fox_k_g, fox_b_f, moba_q_g, moba_k_g, dil_q_g, dil_k_g, w_out,
                  norm1_g, norm2_g, router_w, router_b):
    w_uq = jnp.pad(mla_w_uq[l].reshape(MLA_Q_RANK, HEADS, MLA_QK),
                   ((0, 0), (0, 0), (0, LANES - MLA_QK))).reshape(MLA_Q_RANK, HEADS * LANES)
    w_ukv = mla_w_ukv[l].reshape(MLA_KV_RANK, HEADS, MLA_NOPE + HEAD_DIM)
    w_uk = jnp.pad(w_ukv[:, :, :MLA_NOPE], ((0, 0), (0, 0), (0, LANES - MLA_NOPE)))
    w_uv = w_ukv[:, :, MLA_NOPE:]
    tile4 = lambda g: jnp.tile(g, HEADS)[None, :]
    rw = _pad_cols(router_w[l], LANES)
    rw_hi = rw.astype(BF16)
    rw_lo = (rw - rw_hi.astype(F32)).astype(BF16)
    r_b = jnp.concatenate([router_b[l], jnp.full((LANES - N_EXPERTS,), NEG, F32)])[None, :]
    return dict(
        g1=norm1_g[l][None, :], g2=norm2_g[l][None, :], layer=l,
        w_in=w_in.reshape(-1, w_in.shape[-1]),
        cq_g=mla_cq_g[l][None, :], w_uq=w_uq.astype(BF16), ckv_g=mla_ckv_g[l][None, :],
        w_uk=w_uk.reshape(MLA_KV_RANK, HEADS * LANES).astype(BF16),
        w_uvt=w_uv.reshape(MLA_KV_RANK, GROUP).T.astype(BF16),
        fox_bound=FOX_NORM_SLACK * HEAD_DIM ** 0.5 * jnp.max(jnp.abs(fox_q_g[l]))
        * jnp.max(jnp.abs(fox_k_g[l])),
        q_g=_pad_cols(mla_q_g[l][None, :], LANES), k_g=_pad_cols(mla_k_g[l][None, :], LANES),
        fq_g=tile4(fox_q_g[l]), fk_g=tile4(fox_k_g[l]), f_b=_pad_cols(fox_b_f[l][None, :], LANES),
        mq_g=tile4(moba_q_g[l]), mk_g=tile4(moba_k_g[l]), dq_g=tile4(dil_q_g[l]), dk_g=tile4(dil_k_g[l]),
        w_out=w_out[l].reshape(HEADS, GROUP, D_MODEL).astype(BF16),
        rw=jnp.concatenate([rw_hi, rw_lo], axis=1), r_b=r_b,
    )


def kernel(x, c, w_mod, b_mod, norm1_g, norm2_g, w_in, mla_cq_g, mla_w_uq, mla_ckv_g, mla_w_ukv, mla_q_g, mla_k_g, fox_q_g, fox_k_g, fox_b_f, moba_q_g, moba_k_g, dil_q_g, dil_k_g, w_out, router_w, router_b, exp_w1, exp_b1, exp_w2, exp_b2):
    batch, seq, d = x.shape
    assert batch == 1 and d == D_MODEL
    assert seq % DIL_TILE == 0 and seq // MOBA_BLOCK <= MOBA_MAX_BLOCKS
    depth = w_mod.shape[0]
    consts = _prep_constants(seq, TOKEN_TILE)
    mod = _modulation(c, w_mod, b_mod)
    slopes_c = 2.0 ** (-(2.0 * np.arange(HEADS) + 2.0))
    in_block = np.arange(FLASH_TILE) % MOBA_BLOCK
    kbias_c = jnp.asarray(np.broadcast_to(slopes_c[:, None, None] * in_block[None, :, None],
                                          (HEADS, FLASH_TILE, LANES)), F32)
    kbias_0 = jnp.zeros((HEADS, FLASH_TILE, LANES), F32)
    all_tiles = jnp.zeros((seq // FLASH_TILE,), jnp.int32)
    x2 = x.reshape(seq, d)
    for l in range(depth):
        p = _layer_params(l, w_in, mla_cq_g, mla_w_uq, mla_ckv_g, mla_w_ukv, mla_q_g, mla_k_g,
                          fox_q_g, fox_k_g, fox_b_f, moba_q_g, moba_k_g, dil_q_g, dil_k_g, w_out,
                          norm1_g, norm2_g, router_w, router_b)
        mod_l = mod[l]
        qa, ka, va, qf, kf, vf, qm, km, vm, qd, kd, vd, decay = _prep(x2, mod_l, consts, p)
        oa = _flash(qa, ka, va, kbias_0, all_tiles, False)
        ob = _flash(qf, kf, vf, kbias_0, _fox_first_tile(decay, p["fox_bound"]), False)
        oc = _flash(qm, km, vm, kbias_c, all_tiles, True)
        od = _dilated(qd, kd, vd)
        x1, h2, eidx, gates, rank, counts = _post(x2, oa, ob, oc, od, mod_l, consts, p)
        x2 = _moe(l, x1, h2, eidx, gates, rank, counts, mod_l, exp_w1, exp_b1, exp_w2, exp_b2)
    return x2.reshape(batch, seq, d)
```
